```python
import jax
import jax.numpy as jnp
from jax import lax
import numpy as np

D_MODEL = 2048
BATCH = 4
SEQ = 4096
DEPTH = 2

GRID_W = 64
CTX_LEN = 256
EPS = 1e-6
ROPE_BASE = 10000.0
ROT_DIM = 64
D_FF = 5504
N_MOD = 9
N_BRANCH = 4
BRANCH_W = D_MODEL // N_BRANCH

CHUNK = 128
GM_GROUPS = 4
GM_GC = BRANCH_W // GM_GROUPS

CONV_W = 31

MLA_HEADS = 4
MLA_Q_RANK = 512
MLA_KV_RANK = 256
MLA_NOPE = 128
MLA_ROPE = ROT_DIM
MLA_V = BRANCH_W // MLA_HEADS
MLA_QBLOCK = 128

SWA_HD = ROT_DIM
SWA_HEADS = BRANCH_W // SWA_HD
SWA_KV_HEADS = 2
SWA_GROUP = SWA_HEADS // SWA_KV_HEADS
WINDOW = 128
SWA_BLOCK = 128

N_A = 2 * BRANCH_W
N_B = 2 * BRANCH_W
N_C = MLA_Q_RANK + MLA_KV_RANK + MLA_ROPE
N_D = (SWA_HEADS + 2 * SWA_KV_HEADS) * SWA_HD
N_G = N_BRANCH * D_MODEL
N_IN = N_A + N_B + N_C + N_D + N_G
IN_SPLITS = (N_A, N_A + N_B, N_A + N_B + N_C, N_A + N_B + N_C + N_D)

kernel_name = 'hybrid_gated_mixer_dit_block'


def rmsnorm(x, g):
    xf = x.astype(jnp.float32)
    y = xf * lax.rsqrt(jnp.mean(xf * xf, axis=-1, keepdims=True) + EPS)
    return (y * g.astype(jnp.float32)).astype(x.dtype)


def layernorm(x, g, b):
    xf = x.astype(jnp.float32)
    mu = jnp.mean(xf, axis=-1, keepdims=True)
    var = jnp.mean(jnp.square(xf - mu), axis=-1, keepdims=True)
    y = (xf - mu) * lax.rsqrt(var + EPS) * g.astype(jnp.float32) + b.astype(jnp.float32)
    return y.astype(x.dtype)


def modulation(cond, w_mod, b_mod):
    m = jax.nn.silu(cond) @ w_mod + b_mod
    return m.reshape(cond.shape[:-1] + (N_MOD, D_MODEL))


def pick(m, k):
    return m[..., k, :][..., None, :]


def modulated_norm(x, g, m, j):
    return rmsnorm(x, g) * (1 + pick(m, 3 * j + 1)) + pick(m, 3 * j)


def ffn_sublayer(x, m, j, g_pre, g_post, w_gate, w_up, w_down):
    h = modulated_norm(x, g_pre, m, j)
    y = (jax.nn.silu(h @ w_gate) * (h @ w_up)) @ w_down
    return x + 0.5 * pick(m, 3 * j + 2) * rmsnorm(y, g_post)


def axial_rope_tables(rows, rot_dim):
    half = rot_dim // 2
    freqs = 1.0 / (ROPE_BASE ** (jnp.arange(0, half, 2, dtype=jnp.float32) / half))
    t_row = jnp.repeat(jnp.arange(rows, dtype=jnp.float32), GRID_W)
    t_col = jnp.tile(jnp.arange(GRID_W, dtype=jnp.float32), rows)

    def cos_sin(pos):
        ang = pos[:, None] * freqs[None, :]
        ang = jnp.concatenate([ang, ang], axis=-1)
        return jnp.cos(ang), jnp.sin(ang)

    cr, sr = cos_sin(t_row)
    cc, sc = cos_sin(t_col)
    return cr, sr, cc, sc


def _rotate(x, cos, sin):
    h = x.shape[-1] // 2
    rot = jnp.concatenate([-x[..., h:], x[..., :h]], axis=-1)
    return x * cos + rot * sin


def apply_axial_rope(x, tables):
    cr, sr, cc, sc = [t[None, :, None, :].astype(x.dtype) for t in tables]
    half = x.shape[-1] // 2
    return jnp.concatenate([_rotate(x[..., :half], cr, sr), _rotate(x[..., half:], cc, sc)], axis=-1)


def chunk_gmlp(z, ln_g, ln_b, ws, bs):
    bn, n, _ = z.shape
    z = jax.nn.gelu(z, approximate=False)
    u, v = jnp.split(z, 2, axis=-1)
    v = layernorm(v, ln_g, ln_b).reshape(bn, n // CHUNK, CHUNK, GM_GROUPS, GM_GC)
    sv = jnp.einsum('gpq,bnqgc->bnpgc', ws, v) + bs.T[None, None, :, :, None]
    return u * sv.reshape(bn, n, BRANCH_W)


def conformer_conv(z, dw, db, ln_g, ln_b):
    a, gt = jnp.split(z, 2, axis=-1)
    h = a * jax.nn.sigmoid(gt)
    h = lax.conv_general_dilated(
        h, dw[:, None, :], window_strides=(1,),
        padding=((CONV_W // 2, CONV_W // 2),),
        dimension_numbers=('NWC', 'WIO', 'NWC'),
        feature_group_count=h.shape[-1]) + db
    return jax.nn.silu(layernorm(h, ln_g, ln_b))


def mla_qkv(z, q_norm, kv_norm, w_uq, w_ukv):
    bn, n, _ = z.shape
    cq, ckv, k_rope = jnp.split(z, [MLA_Q_RANK, MLA_Q_RANK + MLA_KV_RANK], axis=-1)
    q = (rmsnorm(cq, q_norm) @ w_uq).reshape(bn, n, MLA_HEADS, MLA_NOPE + MLA_ROPE)
    kv = (rmsnorm(ckv, kv_norm) @ w_ukv).reshape(bn, n, MLA_HEADS, MLA_NOPE + MLA_V)
    return q, kv[..., :MLA_NOPE], k_rope[:, :, None, :], kv[..., MLA_NOPE:]


def mla_keys(k_nope, k_rope):
    return jnp.concatenate([k_nope, jnp.broadcast_to(k_rope, k_nope.shape[:-1] + (MLA_ROPE,))], axis=-1)


def block_attention(q, k, v):
    bn, n, h, dk = q.shape
    nb = n // MLA_QBLOCK
    scale = dk ** -0.5
    qb = jnp.moveaxis(q.reshape(bn, nb, MLA_QBLOCK, h, dk), 1, 0)

    def one_block(qblk):
        s = jnp.einsum('bqhd,bkhd->bhqk', qblk, k, preferred_element_type=jnp.float32) * scale
        p = jax.nn.softmax(s, axis=-1).astype(v.dtype)
        return jnp.einsum('bhqk,bkhd->bqhd', p, v)

    o = lax.map(one_block, qb)
    return jnp.moveaxis(o, 0, 1).reshape(bn, n, h * v.shape[-1])


def swa_qkv(z):
    bn, n, _ = z.shape
    q, k, v = jnp.split(z, [SWA_HEADS * SWA_HD, (SWA_HEADS + SWA_KV_HEADS) * SWA_HD], axis=-1)
    return (q.reshape(bn, n, SWA_HEADS, SWA_HD),
            k.reshape(bn, n, SWA_KV_HEADS, SWA_HD),
            v.reshape(bn, n, SWA_KV_HEADS, SWA_HD))


def window_attention(q, k, v, k_ctx, v_ctx, sink):
    bn, n, kvh, grp, hd = q.shape
    nb = n // SWA_BLOCK
    n_ctx = k_ctx.shape[1]
    scale = hd ** -0.5
    qb = q.reshape(bn, nb, SWA_BLOCK, kvh, grp, hd)

    def band(t):
        tb = jnp.pad(t.reshape(bn, nb, SWA_BLOCK, kvh, hd), ((0, 0), (1, 1), (0, 0), (0, 0), (0, 0)))
        return jnp.concatenate([tb[:, :-2], tb[:, 1:-1], tb[:, 2:]], axis=2)

    kb, vb = band(k), band(v)
    blk = jnp.arange(nb)[:, None] * SWA_BLOCK
    q_pos = blk + jnp.arange(SWA_BLOCK)[None, :]
    k_pos = blk - SWA_BLOCK + jnp.arange(3 * SWA_BLOCK)[None, :]
    dist = k_pos[:, None, :] - q_pos[:, :, None]
    valid = (jnp.abs(dist) <= WINDOW) & (k_pos[:, None, :] >= 0) & (k_pos[:, None, :] < n)
    s_loc = jnp.einsum('bnqkgd,bnjkd->bnkgqj', qb, kb, preferred_element_type=jnp.float32) * scale
    s_loc = jnp.where(valid[None, :, None, None], s_loc, -jnp.inf)
    s_ctx = jnp.einsum('bnqkgd,bjkd->bnkgqj', qb, k_ctx, preferred_element_type=jnp.float32) * scale
    s_sink = jnp.broadcast_to(sink.astype(jnp.float32).reshape(1, 1, kvh, grp, 1, 1), s_ctx.shape[:-1] + (1,))
    p = jax.nn.softmax(jnp.concatenate([s_sink, s_ctx, s_loc], axis=-1), axis=-1).astype(v.dtype)
    o = (jnp.einsum('bnkgqj,bjkd->bnqkgd', p[..., 1:1 + n_ctx], v_ctx)
         + jnp.einsum('bnkgqj,bnjkd->bnqkgd', p[..., 1 + n_ctx:], vb))
    return o.reshape(bn, n, kvh * grp * hd)


def sink_attention_dense(q, k, v, sink):
    bn, n, kvh, grp, hd = q.shape
    s = jnp.einsum('bqkgd,bjkd->bkgqj', q, k, preferred_element_type=jnp.float32) * (hd ** -0.5)
    s_sink = jnp.broadcast_to(sink.astype(jnp.float32).reshape(1, kvh, grp, 1, 1), s.shape[:-1] + (1,))
    p = jax.nn.softmax(jnp.concatenate([s_sink, s], axis=-1), axis=-1)[..., 1:].astype(v.dtype)
    return jnp.einsum('bkgqj,bjkd->bqkgd', p, v).reshape(bn, n, kvh * grp * hd)


def gated_merge(branches, zg, w_branch, w_out):
    acc = None
    for b, y in enumerate(branches):
        term = jax.nn.sigmoid(zg[..., b * D_MODEL:(b + 1) * D_MODEL]) * (y @ w_branch[b])
        acc = term if acc is None else acc + term
    return acc @ w_out


def token_mix(h_lat, h_ctx, rope, need_ctx, w_in, b_in, gm_ln_g, gm_ln_b, gm_ws, gm_bs,
              cv_dw, cv_db, cv_ln_g, cv_ln_b, mla_q_norm, mla_kv_norm, mla_w_uq, mla_w_ukv,
              swa_sink, w_branch, w_out):
    za_l, zb_l, zc_l, zd_l, zg_l = jnp.split(h_lat @ w_in + b_in, IN_SPLITS, axis=-1)
    if need_ctx:
        za_c, zb_c, zc_c, zd_c, zg_c = jnp.split(h_ctx @ w_in + b_in, IN_SPLITS, axis=-1)
    else:
        lo, hi = IN_SPLITS[1], IN_SPLITS[3]
        zc_c, zd_c = jnp.split(h_ctx @ w_in[:, lo:hi] + b_in[lo:hi], [N_C], axis=-1)

    q_l, kn_l, kr_l, v_l = mla_qkv(zc_l, mla_q_norm, mla_kv_norm, mla_w_uq, mla_w_ukv)
    q_l = jnp.concatenate([q_l[..., :MLA_NOPE], apply_axial_rope(q_l[..., MLA_NOPE:], rope)], axis=-1)
    k_l = mla_keys(kn_l, apply_axial_rope(kr_l, rope))
    q_c, kn_c, kr_c, v_c = mla_qkv(zc_c, mla_q_norm, mla_kv_norm, mla_w_uq, mla_w_ukv)
    k_c = mla_keys(kn_c, kr_c)
    yc_l = block_attention(q_l, jnp.concatenate([k_c, k_l], axis=1), jnp.concatenate([v_c, v_l], axis=1))

    qd_l, kd_l, vd_l = swa_qkv(zd_l)
    qd_l = apply_axial_rope(qd_l, rope)
    kd_l = apply_axial_rope(kd_l, rope)
    qd_c, kd_c, vd_c = swa_qkv(zd_c)
    bn, n = qd_l.shape[:2]
    yd_l = window_attention(qd_l.reshape(bn, n, SWA_KV_HEADS, SWA_GROUP, SWA_HD), kd_l, vd_l, kd_c, vd_c, swa_sink)

    ya_l = chunk_gmlp(za_l, gm_ln_g, gm_ln_b, gm_ws, gm_bs)
    yb_l = conformer_conv(zb_l, cv_dw, cv_db, cv_ln_g, cv_ln_b)
    y_lat = gated_merge((ya_l, yb_l, yc_l, yd_l), zg_l, w_branch, w_out)
    if not need_ctx:
        return y_lat, None

    nc = qd_c.shape[1]
    ya_c = chunk_gmlp(za_c, gm_ln_g, gm_ln_b, gm_ws, gm_bs)
    yb_c = conformer_conv(zb_c, cv_dw, cv_db, cv_ln_g, cv_ln_b)
    yc_c = block_attention(q_c, k_c, v_c)
    yd_c = sink_attention_dense(qd_c.reshape(bn, nc, SWA_KV_HEADS, SWA_GROUP, SWA_HD), kd_c, vd_c, swa_sink)
    y_ctx = gated_merge((ya_c, yb_c, yc_c, yd_c), zg_c, w_branch, w_out)
    return y_lat, y_ctx


def setup_inputs(seed: int = 0) -> dict:
    key = jax.random.key(seed)
    keys = iter(jax.random.split(key, 40))

    def nrm(shape, s):
        return s * jax.random.normal(next(keys), shape, jnp.float32)

    def gain(shape):
        return 1.0 + nrm(shape, 0.02)

    L = DEPTH
    return {
        'x': nrm((BATCH, SEQ, D_MODEL), 1.0),
        'c': nrm((BATCH, D_MODEL), 1.0),
        'ctx': nrm((BATCH, CTX_LEN, D_MODEL), 1.0),
        'c_ctx': nrm((D_MODEL,), 1.0),
        'w_mod': nrm((L, D_MODEL, N_MOD * D_MODEL), 0.5 * D_MODEL ** -0.5),
        'b_mod': nrm((L, N_MOD * D_MODEL), 0.02),
        'norm_pre': gain((L, 3, D_MODEL)),
        'norm_post': gain((L, 3, D_MODEL)),
        'w_ff_gate': nrm((L, 2, D_MODEL, D_FF), D_MODEL ** -0.5),
        'w_ff_up': nrm((L, 2, D_MODEL, D_FF), D_MODEL ** -0.5),
        'w_ff_down': nrm((L, 2, D_FF, D_MODEL), D_FF ** -0.5),
        'w_in': nrm((L, D_MODEL, N_IN), D_MODEL ** -0.5),
        'b_in': nrm((L, N_IN), 0.02),
        'gm_ln_g': gain((L, BRANCH_W)),
        'gm_ln_b': nrm((L, BRANCH_W), 0.02),
        'gm_ws': nrm((L, GM_GROUPS, CHUNK, CHUNK), CHUNK ** -0.5),
        'gm_bs': gain((L, GM_GROUPS, CHUNK)),
        'cv_dw': nrm((L, CONV_W, BRANCH_W), CONV_W ** -0.5),
        'cv_db': nrm((L, BRANCH_W), 0.02),
        'cv_ln_g': gain((L, BRANCH_W)),
        'cv_ln_b': nrm((L, BRANCH_W), 0.02),
        'mla_q_norm': gain((L, MLA_Q_RANK)),
        'mla_kv_norm': gain((L, MLA_KV_RANK)),
        'mla_w_uq': nrm((L, MLA_Q_RANK, MLA_HEADS * (MLA_NOPE + MLA_ROPE)), MLA_Q_RANK ** -0.5),
        'mla_w_ukv': nrm((L, MLA_KV_RANK, MLA_HEADS * (MLA_NOPE + MLA_V)), MLA_KV_RANK ** -0.5),
        'swa_sink': nrm((L, SWA_HEADS), 0.5),
        'w_branch': nrm((L, N_BRANCH, BRANCH_W, D_MODEL), BRANCH_W ** -0.5),
        'w_out': nrm((L, D_MODEL, D_MODEL), D_MODEL ** -0.5),
    }


def reference(x, c, ctx, c_ctx, w_mod, b_mod, norm_pre, norm_post, w_ff_gate, w_ff_up, w_ff_down,
              w_in, b_in, gm_ln_g, gm_ln_b, gm_ws, gm_bs, cv_dw, cv_db, cv_ln_g, cv_ln_b,
              mla_q_norm, mla_kv_norm, mla_w_uq, mla_w_ukv, swa_sink, w_branch, w_out):
    n_lat = x.shape[1]
    ROWS = n_lat // GRID_W
    rope = axial_rope_tables(ROWS, ROT_DIM)
    for l in range(DEPTH):
        last = l == DEPTH - 1
        m_lat = modulation(c, w_mod[l], b_mod[l])
        m_ctx = modulation(c_ctx, w_mod[l], b_mod[l])

        x = ffn_sublayer(x, m_lat, 0, norm_pre[l, 0], norm_post[l, 0], w_ff_gate[l, 0], w_ff_up[l, 0], w_ff_down[l, 0])
        ctx = ffn_sublayer(ctx, m_ctx, 0, norm_pre[l, 0], norm_post[l, 0], w_ff_gate[l, 0], w_ff_up[l, 0], w_ff_down[l, 0])

        h_lat = modulated_norm(x, norm_pre[l, 1], m_lat, 1)
        h_ctx = modulated_norm(ctx, norm_pre[l, 1], m_ctx, 1)
        y_lat, y_ctx = token_mix(h_lat, h_ctx, rope, not last, w_in[l], b_in[l], gm_ln_g[l], gm_ln_b[l],
                                 gm_ws[l], gm_bs[l], cv_dw[l], cv_db[l], cv_ln_g[l], cv_ln_b[l],
                                 mla_q_norm[l], mla_kv_norm[l], mla_w_uq[l], mla_w_ukv[l],
                                 swa_sink[l], w_branch[l], w_out[l])
        x = x + pick(m_lat, 5) * rmsnorm(y_lat, norm_post[l, 1])

        x = ffn_sublayer(x, m_lat, 2, norm_pre[l, 2], norm_post[l, 2], w_ff_gate[l, 1], w_ff_up[l, 1], w_ff_down[l, 1])
        if not last:
            ctx = ctx + pick(m_ctx, 5) * rmsnorm(y_ctx, norm_post[l, 1])
            ctx = ffn_sublayer(ctx, m_ctx, 2, norm_pre[l, 2], norm_post[l, 2], w_ff_gate[l, 1], w_ff_up[l, 1], w_ff_down[l, 1])
    return x
```

```python
import functools
import math

import jax
import jax.numpy as jnp
from jax import lax
from jax.experimental import pallas as pl
from jax.experimental.pallas import tpu as pltpu

F32 = jnp.float32
BF16 = jnp.bfloat16

EPS = 1e-6
ROPE_BASE = 10000.0
GRID_W = 64
N_MOD = 9
COND_ROWS = 8
LANES = 128
ROT_DIM = 64
ROT_QUARTER = ROT_DIM // 4
CONV_W = 31
CONV_HALO = 16
CHUNK = 128
GM_GROUPS = 4
BRANCH_W = 512
MLA_HEADS = 4
MLA_NOPE = 128
MLA_V = 128
MLA_QK = 256
MLA_SCALE = (MLA_NOPE + ROT_DIM) ** -0.5
SWA_HEADS = 8
SWA_KV = 2
SWA_GROUP = SWA_HEADS // SWA_KV
SWA_BLOCK = 128
WINDOW = 128
SWA_SCALE = ROT_DIM ** -0.5
D_FF_PAD = 5632
FF_CHUNK = 512
VMEM_LIMIT = 56 * 1024 * 1024


def _cparams(sem):
    return pltpu.CompilerParams(dimension_semantics=sem, vmem_limit_bytes=VMEM_LIMIT)


def _rms(x):
    return x * lax.rsqrt(jnp.mean(x * x, axis=-1, keepdims=True) + EPS)


def _layernorm(x, g, b):
    mu = jnp.mean(x, axis=-1, keepdims=True)
    xc = x - mu
    var = jnp.mean(xc * xc, axis=-1, keepdims=True)
    return xc * lax.rsqrt(var + EPS) * g + b


def _silu(x):
    return x * jax.nn.sigmoid(x)


def _dot(a, b):
    return jnp.dot(a, b, preferred_element_type=F32)


def _dot_nt(a, b):
    return lax.dot_general(a, b, (((1,), (1,)), ((), ())), preferred_element_type=F32)


def _mod_kernel(cond_ref, w_ref, b_ref, o_ref):
    s = _silu(cond_ref[...]).astype(BF16)
    o_ref[...] = _dot(s, w_ref[...].astype(BF16)) + b_ref[...]


def _modulation(cond, w_mod, b_mod, tn=1024):
    n_layers, d, n = w_mod.shape
    return pl.pallas_call(
        _mod_kernel,
        grid=(n_layers, n // tn),
        in_specs=[
            pl.BlockSpec((COND_ROWS, d), lambda l, j: (0, 0)),
            pl.BlockSpec((None, d, tn), lambda l, j: (l, 0, j)),
            pl.BlockSpec((None, 1, tn), lambda l, j: (l, 0, j)),
        ],
        out_specs=pl.BlockSpec((None, COND_ROWS, tn), lambda l, j: (l, 0, j)),
        out_shape=jax.ShapeDtypeStruct((n_layers, COND_ROWS, n), F32),
        compiler_params=_cparams(("parallel", "parallel")),
    )(cond, w_mod, b_mod.reshape(n_layers, 1, n))


def _ffn_kernel(x_ref, shift_ref, scale_ref, gate_ref, gpre_ref, gpost_ref,
                wg_ref, wu_ref, wd_ref, o_ref, h_ref, acc_ref):
    f = pl.program_id(1)

    @pl.when(f == 0)
    def _():
        y = _rms(x_ref[...]) * gpre_ref[...]
        h_ref[...] = (y * (1.0 + scale_ref[...]) + shift_ref[...]).astype(BF16)
        acc_ref[...] = jnp.zeros_like(acc_ref)

    h = h_ref[...]
    g = _dot(h, wg_ref[...])
    u = _dot(h, wu_ref[...])
    a = (_silu(g) * u).astype(BF16)
    acc_ref[...] += _dot(a, wd_ref[...])

    @pl.when(f == pl.num_programs(1) - 1)
    def _():
        r = _rms(acc_ref[...]) * gpost_ref[...]
        o_ref[...] = x_ref[...] + 0.5 * gate_ref[...] * r


def _ffn(x, rows_out, mod, j, row_of_tile, gpre, gpost, wg, wu, wd, tm):
    d = x.shape[1]
    f_pad = wg.shape[1]
    tf = FF_CHUNK

    def mspec(k):
        return pl.BlockSpec((None, None, 1, d), lambda i, f: (row_of_tile(i), k, 0, 0))

    vec = pl.BlockSpec((1, d), lambda i, f: (0, 0))
    return pl.pallas_call(
        _ffn_kernel,
        grid=(rows_out // tm, f_pad // tf),
        in_specs=[
            pl.BlockSpec((tm, d), lambda i, f: (i, 0)),
            mspec(3 * j), mspec(3 * j + 1), mspec(3 * j + 2),
            vec, vec,
            pl.BlockSpec((d, tf), lambda i, f: (0, f)),
            pl.BlockSpec((d, tf), lambda i, f: (0, f)),
            pl.BlockSpec((tf, d), lambda i, f: (f, 0)),
        ],
        out_specs=pl.BlockSpec((tm, d), lambda i, f: (i, 0)),
        out_shape=jax.ShapeDtypeStruct((rows_out, d), F32),
        scratch_shapes=[pltpu.VMEM((tm, d), BF16), pltpu.VMEM((tm, d), F32)],
        compiler_params=_cparams(("parallel", "arbitrary")),
    )(x, mod, mod, mod, gpre, gpost, wg, wu, wd)


def _inproj_kernel(x_ref, shift_ref, scale_ref, gpre_ref, w_ref, b_ref,
                   h_ref, za_ref, zb_ref, zc_ref, zd_ref):
    y = _rms(x_ref[...]) * gpre_ref[...]
    h = (y * (1.0 + scale_ref[...]) + shift_ref[...]).astype(BF16)
    h_ref[...] = h
    off = 0
    for o_ref in (za_ref, zb_ref, zc_ref, zd_ref):
        n = o_ref.shape[-1]
        o_ref[...] = _dot(h, w_ref[:, off:off + n]) + b_ref[:, off:off + n]
        off += n


def _inproj(x, mod, row_of_tile, gpre, w, b, widths, tm):
    rows, d = x.shape
    n = w.shape[1]

    def mspec(k):
        return pl.BlockSpec((None, None, 1, d), lambda i: (row_of_tile(i), k, 0, 0))

    outs = [jax.ShapeDtypeStruct((rows, d), BF16)] + [jax.ShapeDtypeStruct((rows, wd), F32) for wd in widths]
    ospecs = [pl.BlockSpec((tm, d), lambda i: (i, 0))] + [pl.BlockSpec((tm, wd), lambda i: (i, 0)) for wd in widths]
    return pl.pallas_call(
        _inproj_kernel,
        grid=(rows // tm,),
        in_specs=[
            pl.BlockSpec((tm, d), lambda i: (i, 0)),
            mspec(3), mspec(4),
            pl.BlockSpec((1, d), lambda i: (0, 0)),
            pl.BlockSpec((d, n), lambda i: (0, 0), pipeline_mode=pl.Buffered(1)),
            pl.BlockSpec((1, n), lambda i: (0, 0)),
        ],
        out_specs=ospecs,
        out_shape=outs,
        compiler_params=_cparams(("parallel",)),
    )(x, mod, mod, gpre, w, b)


def _ab_kernel(za_ref, zb_ref, zprev_ref, znext_ref, lng_ref, lnb_ref, ws_ref, bs_ref,
               dw_ref, db_ref, clng_ref, clnb_ref, ya_ref, yb_ref, ext_ref,
               *, ts, n_lat_tiles, lat_tiles_per_seq, ctx_tiles_per_seq):
    i = pl.program_id(0)
    w = BRANCH_W

    za = za_ref[...]
    g = 0.5 * za * (1.0 + lax.erf(za * (2.0 ** -0.5)))
    vln = _layernorm(g[:, w:], lng_ref[...], lnb_ref[...]).astype(BF16)
    for c in range(ts // CHUNK):
        rows = slice(c * CHUNK, (c + 1) * CHUNK)
        for gi in range(GM_GROUPS):
            cols = slice(gi * CHUNK, (gi + 1) * CHUNK)
            sv = _dot(ws_ref[gi], vln[rows, cols]) + bs_ref[gi]
            ya_ref[rows, cols] = (g[rows, cols] * sv).astype(BF16)

    def glu(z):
        return z[:, :w] * jax.nn.sigmoid(z[:, w:])

    in_lat = i < n_lat_tiles
    pos = jnp.where(in_lat, i % lat_tiles_per_seq, (i - n_lat_tiles) % ctx_tiles_per_seq)
    last = jnp.where(in_lat, lat_tiles_per_seq - 1, ctx_tiles_per_seq - 1)
    ext_ref[0:CONV_HALO, :] = jnp.where(pos != 0, glu(zprev_ref[...]), 0.0)
    ext_ref[CONV_HALO:CONV_HALO + ts, :] = glu(zb_ref[...])
    ext_ref[CONV_HALO + ts:, :] = jnp.where(pos != last, glu(znext_ref[...]), 0.0)
    first_tap = CONV_HALO - CONV_W // 2
    acc = jnp.zeros((ts, w), F32) + db_ref[...]
    for k in range(CONV_W):
        acc = acc + dw_ref[k:k + 1, :] * ext_ref[first_tap + k:first_tap + k + ts, :]
    yb_ref[...] = _silu(_layernorm(acc, clng_ref[...], clnb_ref[...])).astype(BF16)


def _mix_ab(za, zb, lng, lnb, ws, bs, dw, db, clng, clnb, n_lat_rows, seq_lat, seq_ctx, ts):
    rows = za.shape[0]
    w = BRANCH_W
    hb = ts // CONV_HALO
    n_halo = rows // CONV_HALO
    kern = functools.partial(_ab_kernel, ts=ts, n_lat_tiles=n_lat_rows // ts,
                             lat_tiles_per_seq=seq_lat // ts, ctx_tiles_per_seq=seq_ctx // ts)
    vec = pl.BlockSpec((1, w), lambda i: (0, 0))
    return pl.pallas_call(
        kern,
        grid=(rows // ts,),
        in_specs=[
            pl.BlockSpec((ts, 2 * w), lambda i: (i, 0)),
            pl.BlockSpec((ts, 2 * w), lambda i: (i, 0)),
            pl.BlockSpec((CONV_HALO, 2 * w), lambda i: (jnp.maximum(i * hb - 1, 0), 0)),
            pl.BlockSpec((CONV_HALO, 2 * w), lambda i: (jnp.minimum((i + 1) * hb, n_halo - 1), 0)),
            vec, vec,
            pl.BlockSpec((GM_GROUPS, CHUNK, CHUNK), lambda i: (0, 0, 0)),
            pl.BlockSpec((GM_GROUPS, CHUNK, 1), lambda i: (0, 0, 0)),
            pl.BlockSpec((CONV_W, w), lambda i: (0, 0)),
            vec, vec, vec,
        ],
        out_specs=[pl.BlockSpec((ts, w), lambda i: (i, 0)), pl.BlockSpec((ts, w), lambda i: (i, 0))],
        out_shape=[jax.ShapeDtypeStruct((rows, w), BF16), jax.ShapeDtypeStruct((rows, w), BF16)],
        scratch_shapes=[pltpu.VMEM((ts + 2 * CONV_HALO, w), F32)],
        compiler_params=_cparams(("parallel",)),
    )(za, zb, zb, zb, lng, lnb, ws, bs, dw, db, clng, clnb)


def _qkv_kernel(zc_ref, zd_ref, cos_ref, sina_ref, sinb_ref, qn_ref, kvn_ref, wuq_ref, wukv_ref,
                qc_ref, kc_ref, vc_ref, qd_ref, kd_ref, vd_ref):
    cos, sina, sinb = cos_ref[...], sina_ref[...], sinb_ref[...]
    low = lax.broadcasted_iota(jnp.int32, cos.shape, 1) < ROT_DIM

    def rope(slab):
        return (slab * cos + pltpu.roll(slab, LANES - ROT_QUARTER, 1) * sina
                + pltpu.roll(slab, ROT_QUARTER, 1) * sinb)

    def swap(slab):
        return pltpu.roll(slab, ROT_DIM, 1)

    zc = zc_ref[...]
    nq = MLA_HEADS * MLA_NOPE
    q = _dot((_rms(zc[:, :512]) * qn_ref[...]).astype(BF16), wuq_ref[...]) * MLA_SCALE
    kv = _dot((_rms(zc[:, 512:768]) * kvn_ref[...]).astype(BF16), wukv_ref[...])
    kr = rope(zc[:, 768:896])
    for pair in range(MLA_HEADS // 2):
        qr = rope(q[:, nq + pair * LANES:nq + (pair + 1) * LANES]).astype(BF16)
        for h in (2 * pair, 2 * pair + 1):
            qc_ref[h, :, 0:MLA_NOPE] = q[:, h * MLA_NOPE:(h + 1) * MLA_NOPE].astype(BF16)
            qc_ref[h, :, MLA_NOPE:] = qr
            kc_ref[h, :, 0:MLA_NOPE] = kv[:, h * MLA_NOPE:(h + 1) * MLA_NOPE].astype(BF16)
            keep = low if h % 2 == 0 else jnp.logical_not(low)
            kc_ref[h, :, MLA_NOPE:] = jnp.where(keep, kr, 0.0).astype(BF16)
            vc_ref[h] = kv[:, nq + h * MLA_V:nq + (h + 1) * MLA_V].astype(BF16)

    zd = zd_ref[...]
    for pair in range(SWA_HEADS // 2):
        s = rope(zd[:, pair * LANES:(pair + 1) * LANES]) * SWA_SCALE
        qd_ref[2 * pair] = jnp.where(low, s, 0.0).astype(BF16)
        qd_ref[2 * pair + 1] = jnp.where(low, swap(s), 0.0).astype(BF16)
    k = rope(zd[:, 512:640])
    kd_ref[0] = jnp.where(low, k, 0.0).astype(BF16)
    kd_ref[1] = jnp.where(low, swap(k), 0.0).astype(BF16)
    v = zd[:, 640:768]
    vd_ref[0] = jnp.where(low, v, 0.0).astype(BF16)
    vd_ref[1] = jnp.where(low, swap(v), 0.0).astype(BF16)


def _qkv(zc, zd, cos, sina, sinb, qn, kvn, wuq, wukv, tm):
    rows = zc.shape[0]

    def rowspec(wd):
        return pl.BlockSpec((tm, wd), lambda i: (i, 0))

    def full(a):
        return pl.BlockSpec(a.shape, lambda i: (0,) * a.ndim)

    def headspec(nh, wd):
        return pl.BlockSpec((nh, tm, wd), lambda i: (0, i, 0))

    shapes = [(MLA_HEADS, MLA_QK), (MLA_HEADS, MLA_QK), (MLA_HEADS, MLA_V),
              (SWA_HEADS, LANES), (SWA_KV, LANES), (SWA_KV, LANES)]
    return pl.pallas_call(
        _qkv_kernel,
        grid=(rows // tm,),
        in_specs=[rowspec(zc.shape[1]), rowspec(zd.shape[1]), rowspec(LANES), rowspec(LANES), rowspec(LANES),
                  full(qn), full(kvn), full(wuq), full(wukv)],
        out_specs=[headspec(nh, wd) for nh, wd in shapes],
        out_shape=[jax.ShapeDtypeStruct((nh, rows, wd), BF16) for nh, wd in shapes],
        compiler_params=_cparams(("parallel",)),
    )(zc, zd, cos, sina, sinb, qn, kvn, wuq, wukv)


def _mla_kernel(*refs, has_lat):
    if has_lat:
        q_ref, kl_ref, vl_ref, kx_ref, vx_ref, o_ref = refs
    else:
        q_ref, kx_ref, vx_ref, o_ref = refs
    q = q_ref[...]
    sx = _dot_nt(q, kx_ref[...])
    m = jnp.max(sx, axis=-1, keepdims=True)
    if has_lat:
        sl = _dot_nt(q, kl_ref[...])
        m = jnp.maximum(m, jnp.max(sl, axis=-1, keepdims=True))
    px = jnp.exp(sx - m)
    den = jnp.sum(px, axis=-1, keepdims=True)
    o = _dot(px.astype(BF16), vx_ref[...])
    if has_lat:
        pl_ = jnp.exp(sl - m)
        den = den + jnp.sum(pl_, axis=-1, keepdims=True)
        o = o + _dot(pl_.astype(BF16), vl_ref[...])
    o_ref[...] = (o / den).astype(BF16)


def _mla_attention(qc, kc, vc, batch, seq_lat, seq_ctx, lat_queries, tq):
    n_lat_rows = batch * seq_lat
    ctx_blk0 = n_lat_rows // seq_ctx
    if lat_queries:
        nq, q0 = seq_lat // tq, 0
    else:
        nq, q0 = seq_ctx // tq, n_lat_rows // tq
    qspec = pl.BlockSpec((None, tq, MLA_QK), lambda b, h, i: (h, q0 + b * nq + i, 0))
    kx = pl.BlockSpec((None, seq_ctx, MLA_QK), lambda b, h, i: (h, ctx_blk0 + b, 0))
    vx = pl.BlockSpec((None, seq_ctx, MLA_V), lambda b, h, i: (h, ctx_blk0 + b, 0))
    if lat_queries:
        kl = pl.BlockSpec((None, seq_lat, MLA_QK), lambda b, h, i: (h, b, 0))
        vl = pl.BlockSpec((None, seq_lat, MLA_V), lambda b, h, i: (h, b, 0))
        in_specs, args = [qspec, kl, vl, kx, vx], (qc, kc, vc, kc, vc)
    else:
        in_specs, args = [qspec, kx, vx], (qc, kc, vc)
    return pl.pallas_call(
        functools.partial(_mla_kernel, has_lat=lat_queries),
        grid=(batch, MLA_HEADS, nq),
        in_specs=in_specs,
        out_specs=pl.BlockSpec((tq, MLA_V), lambda b, h, i: (b * nq + i, h)),
        out_shape=jax.ShapeDtypeStruct((batch * nq * tq, MLA_HEADS * MLA_V), BF16),
        compiler_params=_cparams(("parallel", "parallel", "parallel")),
    )(*args)


def _swa_kernel(*refs, has_lat, seq_lat):
    if has_lat:
        q_ref, kl_ref, vl_ref, kx_ref, vx_ref, sink_ref, o_ref = refs
    else:
        q_ref, kx_ref, vx_ref, sink_ref, o_ref = refs
    n = pl.program_id(1)
    rows = SWA_GROUP * SWA_BLOCK
    span = 3 * SWA_BLOCK
    for kv in range(SWA_KV):
        q = q_ref[kv * SWA_GROUP:(kv + 1) * SWA_GROUP].reshape(rows, LANES)
        sink = sink_ref[kv]
        sx = _dot_nt(q, kx_ref[kv])
        m = jnp.maximum(sink, jnp.max(sx, axis=-1, keepdims=True))
        if has_lat:
            start = pl.multiple_of(jnp.clip((n - 1) * SWA_BLOCK, 0, seq_lat - span), SWA_BLOCK)
            sl = _dot_nt(q, kl_ref[kv, pl.ds(start, span), :])
            qpos = n * SWA_BLOCK + (lax.broadcasted_iota(jnp.int32, sl.shape, 0) & (SWA_BLOCK - 1))
            kpos = start + lax.broadcasted_iota(jnp.int32, sl.shape, 1)
            sl = jnp.where(jnp.abs(kpos - qpos) <= WINDOW, sl, -jnp.inf)
            m = jnp.maximum(m, jnp.max(sl, axis=-1, keepdims=True))
        px = jnp.exp(sx - m)
        den = jnp.exp(sink - m) + jnp.sum(px, axis=-1, keepdims=True)
        o = _dot(px.astype(BF16), vx_ref[kv])
        if has_lat:
            pl_ = jnp.exp(sl - m)
            den = den + jnp.sum(pl_, axis=-1, keepdims=True)
            o = o + _dot(pl_.astype(BF16), vl_ref[kv, pl.ds(start, span), :])
        o = (o / den).reshape(SWA_GROUP, SWA_BLOCK, LANES)
        for pair in range(SWA_GROUP // 2):
            slab = o[2 * pair] + pltpu.roll(o[2 * pair + 1], ROT_DIM, 1)
            c0 = (kv * (SWA_GROUP // 2) + pair) * LANES
            o_ref[:, c0:c0 + LANES] = slab.astype(BF16)


def _swa_attention(qd, kd, vd, sink_col, batch, seq_lat, seq_ctx, lat_queries):
    n_lat_rows = batch * seq_lat
    ctx_blk0 = n_lat_rows // seq_ctx
    if lat_queries:
        nb, q0 = seq_lat // SWA_BLOCK, 0
    else:
        nb, q0 = seq_ctx // SWA_BLOCK, n_lat_rows // SWA_BLOCK
    qspec = pl.BlockSpec((SWA_HEADS, SWA_BLOCK, LANES), lambda b, i: (0, q0 + b * nb + i, 0))
    kx = pl.BlockSpec((SWA_KV, seq_ctx, LANES), lambda b, i: (0, ctx_blk0 + b, 0))
    sk = pl.BlockSpec(sink_col.shape, lambda b, i: (0, 0, 0))
    if lat_queries:
        kl = pl.BlockSpec((SWA_KV, seq_lat, LANES), lambda b, i: (0, b, 0))
        in_specs, args = [qspec, kl, kl, kx, kx, sk], (qd, kd, vd, kd, vd, sink_col)
    else:
        in_specs, args = [qspec, kx, kx, sk], (qd, kd, vd, sink_col)
    return pl.pallas_call(
        functools.partial(_swa_kernel, has_lat=lat_queries, seq_lat=seq_lat),
        grid=(batch, nb),
        in_specs=in_specs,
        out_specs=pl.BlockSpec((SWA_BLOCK, BRANCH_W), lambda b, i: (b * nb + i, 0)),
        out_shape=jax.ShapeDtypeStruct((batch * nb * SWA_BLOCK, BRANCH_W), BF16),
        compiler_params=_cparams(("parallel", "parallel")),
    )(*args)


def _gate_kernel(h_ref, ya_ref, yb_ref, yc_ref, yd_ref, wg0, wg1, wg2, wg3, bg_ref,
                 wb0, wb1, wb2, wb3, o_ref):
    h = h_ref[...]
    acc = None
    for b, (y_ref, wg_ref, wb_ref) in enumerate(zip((ya_ref, yb_ref, yc_ref, yd_ref),
                                                    (wg0, wg1, wg2, wg3), (wb0, wb1, wb2, wb3))):
        gate = jax.nn.sigmoid(_dot(h, wg_ref[...]) + bg_ref[b])
        term = gate * _dot(y_ref[...], wb_ref[...])
        acc = term if acc is None else acc + term
    o_ref[...] = acc.astype(BF16)


def _gated_merge(h, ys, wg, bg, wbr, rows, tm, tn):
    d = h.shape[1]
    ncol = d // tn
    wg_specs = [pl.BlockSpec((d, tn), lambda j, i, b=b: (0, b * ncol + j)) for b in range(4)]
    wb_specs = [pl.BlockSpec((None, BRANCH_W, tn), lambda j, i, b=b: (b, 0, j)) for b in range(4)]
    yspec = pl.BlockSpec((tm, BRANCH_W), lambda j, i: (i, 0))
    return pl.pallas_call(
        _gate_kernel,
        grid=(ncol, rows // tm),
        in_specs=[pl.BlockSpec((tm, d), lambda j, i: (i, 0)), yspec, yspec, yspec, yspec,
                  *wg_specs, pl.BlockSpec((4, 1, tn), lambda j, i: (0, 0, j)), *wb_specs],
        out_specs=pl.BlockSpec((tm, tn), lambda j, i: (i, j)),
        out_shape=jax.ShapeDtypeStruct((rows, d), BF16),
        compiler_params=_cparams(("parallel", "parallel")),
    )(h, *ys, wg, wg, wg, wg, bg, wbr, wbr, wbr, wbr)


def _outproj_kernel(a_ref, x_ref, gate_ref, gpost_ref, w_ref, o_ref):
    y = _dot(a_ref[...], w_ref[...])
    o_ref[...] = x_ref[...] + gate_ref[...] * (_rms(y) * gpost_ref[...])


def _outproj(a, x, mod, row_of_tile, gpost, w, tm):
    rows, d = a.shape
    return pl.pallas_call(
        _outproj_kernel,
        grid=(rows // tm,),
        in_specs=[
            pl.BlockSpec((tm, d), lambda i: (i, 0)),
            pl.BlockSpec((tm, d), lambda i: (i, 0)),
            pl.BlockSpec((None, None, 1, d), lambda i: (row_of_tile(i), 5, 0, 0)),
            pl.BlockSpec((1, d), lambda i: (0, 0)),
            pl.BlockSpec((d, d), lambda i: (0, 0), pipeline_mode=pl.Buffered(1)),
        ],
        out_specs=pl.BlockSpec((tm, d), lambda i: (i, 0)),
        out_shape=jax.ShapeDtypeStruct((rows, d), F32),
        compiler_params=_cparams(("parallel",)),
    )(a, x, mod, gpost, w)


def _rope_tables(seq_lat, n_ctx_rows, batch):
    half = ROT_DIM // 2
    freqs = 1.0 / (ROPE_BASE ** (jnp.arange(0, half, 2, dtype=F32) / half))
    t = jnp.arange(seq_lat)
    ang_r = (t // GRID_W).astype(F32)[:, None] * freqs[None, :]
    ang_c = (t % GRID_W).astype(F32)[:, None] * freqs[None, :]
    ang = jnp.concatenate([ang_r, ang_r, ang_c, ang_c], axis=-1)
    ang = jnp.tile(ang, (batch, LANES // ROT_DIM))
    cos, sin = jnp.cos(ang), jnp.sin(ang)
    first = (jnp.arange(LANES) % half) < ROT_QUARTER
    sina = jnp.where(first, -sin, 0.0)
    sinb = jnp.where(first, 0.0, sin)
    ones = jnp.ones((n_ctx_rows, LANES), F32)
    zeros = jnp.zeros((n_ctx_rows, LANES), F32)
    return (jnp.concatenate([cos, ones]), jnp.concatenate([sina, zeros]), jnp.concatenate([sinb, zeros]))


def _pack_layer(l, w_ff_gate, w_ff_up, w_ff_down, w_in, b_in, gm_ws, gm_bs, mla_w_uq, mla_w_ukv,
                swa_sink, w_branch, w_out):
    d = w_in.shape[1]
    d_ff = w_ff_gate.shape[-1]
    pad_f = D_FF_PAD - d_ff
    p = {}
    p['wg'] = [jnp.pad(w_ff_gate[l, s].astype(BF16), ((0, 0), (0, pad_f))) for s in range(2)]
    p['wu'] = [jnp.pad(w_ff_up[l, s].astype(BF16), ((0, 0), (0, pad_f))) for s in range(2)]
    p['wd'] = [jnp.pad(w_ff_down[l, s].astype(BF16), ((0, pad_f), (0, 0))) for s in range(2)]
    wi, bi = w_in[l], b_in[l]
    c_hi = 2048 + 832
    kr = slice(c_hi - ROT_DIM, c_hi)
    p['w_abcd'] = jnp.concatenate([wi[:, :c_hi], wi[:, kr], wi[:, c_hi:c_hi + 768]], axis=1).astype(BF16)
    p['b_abcd'] = jnp.concatenate([bi[:c_hi], bi[kr], bi[c_hi:c_hi + 768]])[None, :]
    p['w_gates'] = wi[:, c_hi + 768:].astype(BF16)
    p['b_gates'] = bi[c_hi + 768:].reshape(4, 1, d)
    p['gm_ws'] = gm_ws[l].astype(BF16)
    p['gm_bs'] = gm_bs[l][:, :, None]
    uq = mla_w_uq[l].reshape(-1, MLA_HEADS, MLA_NOPE + ROT_DIM)
    p['wuq'] = jnp.concatenate([uq[:, :, :MLA_NOPE].reshape(-1, MLA_HEADS * MLA_NOPE),
                                uq[:, :, MLA_NOPE:].reshape(-1, MLA_HEADS * ROT_DIM)], axis=1).astype(BF16)
    ukv = mla_w_ukv[l].reshape(-1, MLA_HEADS, MLA_NOPE + MLA_V)
    p['wukv'] = jnp.concatenate([ukv[:, :, :MLA_NOPE].reshape(-1, MLA_HEADS * MLA_NOPE),
                                 ukv[:, :, MLA_NOPE:].reshape(-1, MLA_HEADS * MLA_V)], axis=1).astype(BF16)
    p['sink'] = jnp.repeat(swa_sink[l].reshape(SWA_KV, SWA_GROUP), SWA_BLOCK, axis=1)[:, :, None]
    p['w_branch'] = w_branch[l].astype(BF16)
    p['w_out'] = w_out[l].astype(BF16)
    return p


def kernel(x, c, ctx, c_ctx, w_mod, b_mod, norm_pre, norm_post, w_ff_gate, w_ff_up, w_ff_down, w_in, b_in, gm_ln_g, gm_ln_b, gm_ws, gm_bs, cv_dw, cv_db, cv_ln_g, cv_ln_b, mla_q_norm, mla_kv_norm, mla_w_uq, mla_w_ukv, swa_sink, w_branch, w_out):
    batch, seq_lat, d = x.shape
    seq_ctx = ctx.shape[1]
    depth = w_mod.shape[0]
    n_lat = batch * seq_lat
    n_ctx = batch * seq_ctx
    assert batch + 1 <= COND_ROWS and seq_lat % GRID_W == 0
    tm = math.gcd(512, math.gcd(seq_lat, n_ctx))
    ts = math.gcd(256, math.gcd(seq_lat, seq_ctx))
    tq = math.gcd(256, math.gcd(seq_lat, seq_ctx))
    assert tm % SWA_BLOCK == 0 and ts % CHUNK == 0 and seq_ctx % SWA_BLOCK == 0 and seq_lat >= 3 * SWA_BLOCK

    lat_tiles, tiles_per_batch = n_lat // tm, seq_lat // tm

    def row_of_tile(i):
        return jnp.where(i < lat_tiles, i // tiles_per_batch, batch)

    rows_all = jnp.concatenate([x.reshape(n_lat, d), ctx.reshape(n_ctx, d)], axis=0)
    cond = jnp.zeros((COND_ROWS, d), F32).at[:batch].set(c).at[batch].set(c_ctx)
    mod = _modulation(cond, w_mod, b_mod).reshape(depth, COND_ROWS, N_MOD, 1, d)
    cos, sina, sinb = _rope_tables(seq_lat, n_ctx, batch)

    xs = rows_all
    for l in range(depth):
        last = l == depth - 1
        p = _pack_layer(l, w_ff_gate, w_ff_up, w_ff_down, w_in, b_in, gm_ws, gm_bs, mla_w_uq, mla_w_ukv,
                        swa_sink, w_branch, w_out)
        m = mod[l]
        npre, npost = norm_pre[l][:, None, :], norm_post[l][:, None, :]

        xs = _ffn(xs, xs.shape[0], m, 0, row_of_tile, npre[0], npost[0], p['wg'][0], p['wu'][0], p['wd'][0], tm)

        h, za, zb, zc, zd = _inproj(xs, m, row_of_tile, npre[1], p['w_abcd'], p['b_abcd'],
                                    (1024, 1024, 896, 768), tm)
        ya, yb = _mix_ab(za, zb, gm_ln_g[l][None], gm_ln_b[l][None], p['gm_ws'], p['gm_bs'],
                         cv_dw[l], cv_db[l][None], cv_ln_g[l][None], cv_ln_b[l][None],
                         n_lat, seq_lat, seq_ctx, ts)
        qc, kc, vc, qd, kd, vd = _qkv(zc, zd, cos, sina, sinb, mla_q_norm[l][None], mla_kv_norm[l][None],
                                      p['wuq'], p['wukv'], tm)
        yc = _mla_attention(qc, kc, vc, batch, seq_lat, seq_ctx, True, tq)
        yd = _swa_attention(qd, kd, vd, p['sink'], batch, seq_lat, seq_ctx, True)
        rows_mix = n_lat
        if not last:
            yc = jnp.concatenate([yc, _mla_attention(qc, kc, vc, batch, seq_lat, seq_ctx, False, tq)], axis=0)
            yd = jnp.concatenate([yd, _swa_attention(qd, kd, vd, p['sink'], batch, seq_lat, seq_ctx, False)], axis=0)
            rows_mix = n_lat + n_ctx
        merged = _gated_merge(h, (ya, yb, yc, yd), p['w_gates'], p['b_gates'], p['w_branch'], rows_mix, tm, 512)
        xs = _outproj(merged, xs, m, row_of_tile, npost[1], p['w_out'], tm)

        xs = _ffn(xs, rows_mix, m, 2, row_of_tile, npre[2], npost[2], p['wg'][1], p['wu'][1], p['wd'][1], tm)
    return xs[:n_lat].reshape(batch, seq_lat, d)
```

```python
import functools
import math

import jax
import jax.numpy as jnp
from jax import lax
from jax.experimental import pallas as pl
from jax.experimental.pallas import tpu as pltpu

F32 = jnp.float32
BF16 = jnp.bfloat16

EPS = 1e-6
ROPE_BASE = 10000.0
GRID_W = 64
N_MOD = 9
COND_ROWS = 8
LANES = 128
ROT_DIM = 64
ROT_QUARTER = ROT_DIM // 4
CONV_W = 31
CONV_HALO = 16
CHUNK = 128
GM_GROUPS = 4
BRANCH_W = 512
MLA_HEADS = 4
MLA_NOPE = 128
MLA_V = 128
MLA_QK = 256
MLA_SCALE = (MLA_NOPE + ROT_DIM) ** -0.5
SWA_HEADS = 8
SWA_KV = 2
SWA_GROUP = SWA_HEADS // SWA_KV
SWA_BLOCK = 128
WINDOW = 128
SWA_SCALE = ROT_DIM ** -0.5
D_FF_PAD = 5632
FF_CHUNK = 512
VMEM_LIMIT = 56 * 1024 * 1024


def _cparams(sem):
    return pltpu.CompilerParams(dimension_semantics=sem, vmem_limit_bytes=VMEM_LIMIT)


def _rms(x):
    return x * lax.rsqrt(jnp.mean(x * x, axis=-1, keepdims=True) + EPS)


def _layernorm(x, g, b):
    mu = jnp.mean(x, axis=-1, keepdims=True)
    xc = x - mu
    var = jnp.mean(xc * xc, axis=-1, keepdims=True)
    return xc * lax.rsqrt(var + EPS) * g + b


def _silu(x):
    return x * jax.nn.sigmoid(x)


def _dot(a, b):
    return jnp.dot(a, b, preferred_element_type=F32)


def _dot_nt(a, b):
    return lax.dot_general(a, b, (((1,), (1,)), ((), ())), preferred_element_type=F32)


def _cast_pad_cols_kernel(w_ref, o_ref):
    n = w_ref.shape[1]
    o_ref[:, :n] = w_ref[...].astype(BF16)
    o_ref[:, n:] = jnp.zeros((o_ref.shape[0], o_ref.shape[1] - n), BF16)


def _cast_pad_cols(w, n_pad, tr=256):
    m, n = w.shape
    return pl.pallas_call(
        _cast_pad_cols_kernel,
        grid=(m // tr,),
        in_specs=[pl.BlockSpec((tr, n), lambda i: (i, 0))],
        out_specs=pl.BlockSpec((tr, n_pad), lambda i: (i, 0)),
        out_shape=jax.ShapeDtypeStruct((m, n_pad), BF16),
        compiler_params=_cparams(("parallel",)),
    )(w)


def _cast_pad_rows_kernel(w_ref, o_ref, *, rows_valid):
    tr = w_ref.shape[0]
    row = pl.program_id(1) * tr + lax.broadcasted_iota(jnp.int32, w_ref.shape, 0)
    o_ref[...] = jnp.where(row < rows_valid, w_ref[...], 0.0).astype(BF16)


def _cast_pad_rows(w, m_pad, tr=512):
    g, m, n = w.shape
    return pl.pallas_call(
        functools.partial(_cast_pad_rows_kernel, rows_valid=m),
        grid=(g, m_pad // tr),
        in_specs=[pl.BlockSpec((None, tr, n), lambda i, j: (i, j, 0))],
        out_specs=pl.BlockSpec((None, tr, n), lambda i, j: (i, j, 0)),
        out_shape=jax.ShapeDtypeStruct((g, m_pad, n), BF16),
        compiler_params=_cparams(("parallel", "parallel")),
    )(w)


def _mod_kernel(cond_ref, w_ref, b_ref, o_ref):
    s = _silu(cond_ref[...]).astype(BF16)
    o_ref[...] = _dot(s, w_ref[...].astype(BF16)) + b_ref[...]


def _modulation(cond, w_mod, b_mod, tn=1024):
    n_layers, d, n = w_mod.shape
    return pl.pallas_call(
        _mod_kernel,
        grid=(n_layers, n // tn),
        in_specs=[
            pl.BlockSpec((COND_ROWS, d), lambda l, j: (0, 0)),
            pl.BlockSpec((None, d, tn), lambda l, j: (l, 0, j)),
            pl.BlockSpec((None, 1, tn), lambda l, j: (l, 0, j)),
        ],
        out_specs=pl.BlockSpec((None, COND_ROWS, tn), lambda l, j: (l, 0, j)),
        out_shape=jax.ShapeDtypeStruct((n_layers, COND_ROWS, n), F32),
        compiler_params=_cparams(("parallel", "parallel")),
    )(cond, w_mod, b_mod.reshape(n_layers, 1, n))


def _ffn_kernel(x_ref, shift_ref, scale_ref, gate_ref, gpre_ref, gpost_ref,
                wg_ref, wu_ref, wd_ref, o_ref, h_ref, acc_ref):
    f = pl.program_id(1)

    @pl.when(f == 0)
    def _():
        y = _rms(x_ref[...]) * gpre_ref[...]
        h_ref[...] = (y * (1.0 + scale_ref[...]) + shift_ref[...]).astype(BF16)
        acc_ref[...] = jnp.zeros_like(acc_ref)

    h = h_ref[...]
    g = _dot(h, wg_ref[...])
    u = _dot(h, wu_ref[...])
    a = (_silu(g) * u).astype(BF16)
    acc_ref[...] += _dot(a, wd_ref[...])

    @pl.when(f == pl.num_programs(1) - 1)
    def _():
        r = _rms(acc_ref[...]) * gpost_ref[...]
        o_ref[...] = x_ref[...] + 0.5 * gate_ref[...] * r


def _ffn(x, rows_out, mod, j, row_of_tile, gpre, gpost, wg, wu, wd, l, s, tm):
    d = x.shape[1]
    f_pad = wg.shape[-1]
    tf = FF_CHUNK

    def mspec(k):
        return pl.BlockSpec((None, None, 1, d), lambda i, f: (row_of_tile(i), k, 0, 0))

    vec = pl.BlockSpec((1, d), lambda i, f: (0, 0))
    return pl.pallas_call(
        _ffn_kernel,
        grid=(rows_out // tm, f_pad // tf),
        in_specs=[
            pl.BlockSpec((tm, d), lambda i, f: (i, 0)),
            mspec(3 * j), mspec(3 * j + 1), mspec(3 * j + 2),
            vec, vec,
            pl.BlockSpec((None, None, d, tf), lambda i, f: (l, s, 0, f)),
            pl.BlockSpec((None, None, d, tf), lambda i, f: (l, s, 0, f)),
            pl.BlockSpec((None, None, tf, d), lambda i, f: (l, s, f, 0)),
        ],
        out_specs=pl.BlockSpec((tm, d), lambda i, f: (i, 0)),
        out_shape=jax.ShapeDtypeStruct((rows_out, d), F32),
        scratch_shapes=[pltpu.VMEM((tm, d), BF16), pltpu.VMEM((tm, d), F32)],
        compiler_params=_cparams(("parallel", "arbitrary")),
    )(x, mod, mod, mod, gpre, gpost, wg, wu, wd)


def _inproj_kernel(x_ref, shift_ref, scale_ref, gpre_ref, w_ref, b_ref,
                   h_ref, za_ref, zb_ref, zc_ref, zd_ref):
    y = _rms(x_ref[...]) * gpre_ref[...]
    h = (y * (1.0 + scale_ref[...]) + shift_ref[...]).astype(BF16)
    h_ref[...] = h
    off = 0
    for o_ref in (za_ref, zb_ref, zc_ref, zd_ref):
        n = o_ref.shape[-1]
        o_ref[...] = _dot(h, w_ref[:, off:off + n]) + b_ref[:, off:off + n]
        off += n


def _inproj(x, mod, row_of_tile, gpre, w, b, widths, tm):
    rows, d = x.shape
    n = w.shape[1]

    def mspec(k):
        return pl.BlockSpec((None, None, 1, d), lambda i: (row_of_tile(i), k, 0, 0))

    outs = [jax.ShapeDtypeStruct((rows, d), BF16)] + [jax.ShapeDtypeStruct((rows, wd), F32) for wd in widths]
    ospecs = [pl.BlockSpec((tm, d), lambda i: (i, 0))] + [pl.BlockSpec((tm, wd), lambda i: (i, 0)) for wd in widths]
    return pl.pallas_call(
        _inproj_kernel,
        grid=(rows // tm,),
        in_specs=[
            pl.BlockSpec((tm, d), lambda i: (i, 0)),
            mspec(3), mspec(4),
            pl.BlockSpec((1, d), lambda i: (0, 0)),
            pl.BlockSpec((d, n), lambda i: (0, 0), pipeline_mode=pl.Buffered(1)),
            pl.BlockSpec((1, n), lambda i: (0, 0)),
        ],
        out_specs=ospecs,
        out_shape=outs,
        compiler_params=_cparams(("parallel",)),
    )(x, mod, mod, gpre, w, b)


def _ab_kernel(za_ref, zb_ref, zprev_ref, znext_ref, lng_ref, lnb_ref, ws_ref, bs_ref,
               dw_ref, db_ref, clng_ref, clnb_ref, ya_ref, yb_ref, ext_ref,
               *, ts, n_lat_tiles, lat_tiles_per_seq, ctx_tiles_per_seq):
    i = pl.program_id(0)
    w = BRANCH_W

    za = za_ref[...]
    g = 0.5 * za * (1.0 + lax.erf(za * (2.0 ** -0.5)))
    vln = _layernorm(g[:, w:], lng_ref[...], lnb_ref[...]).astype(BF16)
    for c in range(ts // CHUNK):
        rows = slice(c * CHUNK, (c + 1) * CHUNK)
        for gi in range(GM_GROUPS):
            cols = slice(gi * CHUNK, (gi + 1) * CHUNK)
            sv = _dot(ws_ref[gi], vln[rows, cols]) + bs_ref[gi]
            ya_ref[rows, cols] = (g[rows, cols] * sv).astype(BF16)

    def glu(z):
        return z[:, :w] * jax.nn.sigmoid(z[:, w:])

    in_lat = i < n_lat_tiles
    pos = jnp.where(in_lat, i % lat_tiles_per_seq, (i - n_lat_tiles) % ctx_tiles_per_seq)
    last = jnp.where(in_lat, lat_tiles_per_seq - 1, ctx_tiles_per_seq - 1)
    ext_ref[0:CONV_HALO, :] = jnp.where(pos != 0, glu(zprev_ref[...]), 0.0)
    ext_ref[CONV_HALO:CONV_HALO + ts, :] = glu(zb_ref[...])
    ext_ref[CONV_HALO + ts:, :] = jnp.where(pos != last, glu(znext_ref[...]), 0.0)
    first_tap = CONV_HALO - CONV_W // 2
    acc = jnp.zeros((ts, w), F32) + db_ref[...]
    for k in range(CONV_W):
        acc = acc + dw_ref[k:k + 1, :] * ext_ref[first_tap + k:first_tap + k + ts, :]
    yb_ref[...] = _silu(_layernorm(acc, clng_ref[...], clnb_ref[...])).astype(BF16)


def _mix_ab(za, zb, lng, lnb, ws, bs, dw, db, clng, clnb, n_lat_rows, seq_lat, seq_ctx, ts):
    rows = za.shape[0]
    w = BRANCH_W
    hb = ts // CONV_HALO
    n_halo = rows // CONV_HALO
    kern = functools.partial(_ab_kernel, ts=ts, n_lat_tiles=n_lat_rows // ts,
                             lat_tiles_per_seq=seq_lat // ts, ctx_tiles_per_seq=seq_ctx // ts)
    vec = pl.BlockSpec((1, w), lambda i: (0, 0))
    return pl.pallas_call(
        kern,
        grid=(rows // ts,),
        in_specs=[
            pl.BlockSpec((ts, 2 * w), lambda i: (i, 0)),
            pl.BlockSpec((ts, 2 * w), lambda i: (i, 0)),
            pl.BlockSpec((CONV_HALO, 2 * w), lambda i: (jnp.maximum(i * hb - 1, 0), 0)),
            pl.BlockSpec((CONV_HALO, 2 * w), lambda i: (jnp.minimum((i + 1) * hb, n_halo - 1), 0)),
            vec, vec,
            pl.BlockSpec((GM_GROUPS, CHUNK, CHUNK), lambda i: (0, 0, 0)),
            pl.BlockSpec((GM_GROUPS, CHUNK, 1), lambda i: (0, 0, 0)),
            pl.BlockSpec((CONV_W, w), lambda i: (0, 0)),
            vec, vec, vec,
        ],
        out_specs=[pl.BlockSpec((ts, w), lambda i: (i, 0)), pl.BlockSpec((ts, w), lambda i: (i, 0))],
        out_shape=[jax.ShapeDtypeStruct((rows, w), BF16), jax.ShapeDtypeStruct((rows, w), BF16)],
        scratch_shapes=[pltpu.VMEM((ts + 2 * CONV_HALO, w), F32)],
        compiler_params=_cparams(("parallel",)),
    )(za, zb, zb, zb, lng, lnb, ws, bs, dw, db, clng, clnb)


def _qkv_kernel(zc_ref, zd_ref, cos_ref, sina_ref, sinb_ref, qn_ref, kvn_ref, wuq_ref, wukv_ref,
                qc_ref, kc_ref, vc_ref, qd_ref, kd_ref, vd_ref):
    cos, sina, sinb = cos_ref[...], sina_ref[...], sinb_ref[...]
    low = lax.broadcasted_iota(jnp.int32, cos.shape, 1) < ROT_DIM

    def rope(slab):
        return (slab * cos + pltpu.roll(slab, LANES - ROT_QUARTER, 1) * sina
                + pltpu.roll(slab, ROT_QUARTER, 1) * sinb)

    def swap(slab):
        return pltpu.roll(slab, ROT_DIM, 1)

    zc = zc_ref[...]
    nq = MLA_HEADS * MLA_NOPE
    q = _dot((_rms(zc[:, :512]) * qn_ref[...]).astype(BF16), wuq_ref[...]) * MLA_SCALE
    kv = _dot((_rms(zc[:, 512:768]) * kvn_ref[...]).astype(BF16), wukv_ref[...])
    kr = rope(zc[:, 768:896])
    for pair in range(MLA_HEADS // 2):
        qr = rope(q[:, nq + pair * LANES:nq + (pair + 1) * LANES]).astype(BF16)
        for h in (2 * pair, 2 * pair + 1):
            qc_ref[h, :, 0:MLA_NOPE] = q[:, h * MLA_NOPE:(h + 1) * MLA_NOPE].astype(BF16)
            qc_ref[h, :, MLA_NOPE:] = qr
            kc_ref[h, :, 0:MLA_NOPE] = kv[:, h * MLA_NOPE:(h + 1) * MLA_NOPE].astype(BF16)
            keep = low if h % 2 == 0 else jnp.logical_not(low)
            kc_ref[h, :, MLA_NOPE:] = jnp.where(keep, kr, 0.0).astype(BF16)
            vc_ref[h, :, 0:MLA_V] = kv[:, nq + h * MLA_V:nq + (h + 1) * MLA_V].astype(BF16)
            vc_ref[h, :, MLA_V:] = jnp.ones((zc.shape[0], MLA_V), BF16)

    zd = zd_ref[...]
    for pair in range(SWA_HEADS // 2):
        s = rope(zd[:, pair * LANES:(pair + 1) * LANES]) * SWA_SCALE
        qd_ref[2 * pair] = jnp.where(low, s, 0.0).astype(BF16)
        qd_ref[2 * pair + 1] = jnp.where(low, swap(s), 0.0).astype(BF16)
    k = rope(zd[:, 512:640])
    kd_ref[0] = jnp.where(low, k, 0.0).astype(BF16)
    kd_ref[1] = jnp.where(low, swap(k), 0.0).astype(BF16)
    v = zd[:, 640:768]
    vd_ref[0] = jnp.where(low, v, 1.0).astype(BF16)
    vd_ref[1] = jnp.where(low, swap(v), 1.0).astype(BF16)


def _qkv(zc, zd, cos, sina, sinb, qn, kvn, wuq, wukv, tm):
    rows = zc.shape[0]

    def rowspec(wd):
        return pl.BlockSpec((tm, wd), lambda i: (i, 0))

    def full(a):
        return pl.BlockSpec(a.shape, lambda i: (0,) * a.ndim)

    def headspec(nh, wd):
        return pl.BlockSpec((nh, tm, wd), lambda i: (0, i, 0))

    shapes = [(MLA_HEADS, MLA_QK), (MLA_HEADS, MLA_QK), (MLA_HEADS, 2 * MLA_V),
              (SWA_HEADS, LANES), (SWA_KV, LANES), (SWA_KV, LANES)]
    return pl.pallas_call(
        _qkv_kernel,
        grid=(rows // tm,),
        in_specs=[rowspec(zc.shape[1]), rowspec(zd.shape[1]), rowspec(LANES), rowspec(LANES), rowspec(LANES),
                  full(qn), full(kvn), full(wuq), full(wukv)],
        out_specs=[headspec(nh, wd) for nh, wd in shapes],
        out_shape=[jax.ShapeDtypeStruct((nh, rows, wd), BF16) for nh, wd in shapes],
        compiler_params=_cparams(("parallel",)),
    )(zc, zd, cos, sina, sinb, qn, kvn, wuq, wukv)


def _mla_kernel(*refs, has_lat):
    if has_lat:
        q_ref, kl_ref, vl_ref, kx_ref, vx_ref, o_ref = refs
    else:
        q_ref, kx_ref, vx_ref, o_ref = refs
    q = q_ref[...]
    sx = _dot_nt(q, kx_ref[...])
    m = jnp.max(sx, axis=-1, keepdims=True)
    if has_lat:
        sl = _dot_nt(q, kl_ref[...])
        m = jnp.maximum(m, jnp.max(sl, axis=-1, keepdims=True))
    o = _dot(jnp.exp(sx - m).astype(BF16), vx_ref[...])
    if has_lat:
        o = o + _dot(jnp.exp(sl - m).astype(BF16), vl_ref[...])
    o_ref[...] = (o[:, :MLA_V] / o[:, MLA_V:]).astype(BF16)


def _mla_attention(qc, kc, vc, batch, seq_lat, seq_ctx, lat_queries):
    n_lat_rows = batch * seq_lat
    ctx_blk0 = n_lat_rows // seq_ctx
    tq = math.gcd(512, seq_lat if lat_queries else seq_ctx)
    if lat_queries:
        nq, q0 = seq_lat // tq, 0
    else:
        nq, q0 = seq_ctx // tq, n_lat_rows // tq
    qspec = pl.BlockSpec((None, tq, MLA_QK), lambda b, h, i: (h, q0 + b * nq + i, 0))
    kx = pl.BlockSpec((None, seq_ctx, MLA_QK), lambda b, h, i: (h, ctx_blk0 + b, 0))
    vx = pl.BlockSpec((None, seq_ctx, 2 * MLA_V), lambda b, h, i: (h, ctx_blk0 + b, 0))
    if lat_queries:
        kl = pl.BlockSpec((None, seq_lat, MLA_QK), lambda b, h, i: (h, b, 0))
        vl = pl.BlockSpec((None, seq_lat, 2 * MLA_V), lambda b, h, i: (h, b, 0))
        in_specs, args = [qspec, kl, vl, kx, vx], (qc, kc, vc, kc, vc)
    else:
        in_specs, args = [qspec, kx, vx], (qc, kc, vc)
    return pl.pallas_call(
        functools.partial(_mla_kernel, has_lat=lat_queries),
        grid=(batch, MLA_HEADS, nq),
        in_specs=in_specs,
        out_specs=pl.BlockSpec((tq, MLA_V), lambda b, h, i: (b * nq + i, h)),
        out_shape=jax.ShapeDtypeStruct((batch * nq * tq, MLA_HEADS * MLA_V), BF16),
        compiler_params=_cparams(("parallel", "parallel", "parallel")),
    )(*args)


def _swa_kernel(*refs, has_lat, seq_lat, blocks):
    if has_lat:
        q_ref, kl_ref, vl_ref, kx_ref, vx_ref, sink_ref, o_ref = refs
    else:
        q_ref, kx_ref, vx_ref, sink_ref, o_ref = refs
    rows = SWA_GROUP * SWA_BLOCK
    span = 3 * SWA_BLOCK
    low = lax.broadcasted_iota(jnp.int32, (rows, LANES), 1) < ROT_DIM
    for blk in range(blocks):
        n = pl.program_id(1) * blocks + blk
        r0 = blk * SWA_BLOCK
        for kv in range(SWA_KV):
            q = q_ref[kv * SWA_GROUP:(kv + 1) * SWA_GROUP, r0:r0 + SWA_BLOCK, :].reshape(rows, LANES)
            sink = sink_ref[kv]
            s = _dot_nt(q, kx_ref[kv])
            v = vx_ref[kv]
            if has_lat:
                start = pl.multiple_of(jnp.clip((n - 1) * SWA_BLOCK, 0, seq_lat - span), SWA_BLOCK)
                sl = _dot_nt(q, kl_ref[kv, pl.ds(start, span), :])
                qpos = n * SWA_BLOCK + (lax.broadcasted_iota(jnp.int32, sl.shape, 0) & (SWA_BLOCK - 1))
                kpos = start + lax.broadcasted_iota(jnp.int32, sl.shape, 1)
                s = jnp.concatenate([s, jnp.where(jnp.abs(kpos - qpos) <= WINDOW, sl, -jnp.inf)], axis=1)
                v = jnp.concatenate([v, vl_ref[kv, pl.ds(start, span), :]], axis=0)
            m = jnp.maximum(sink, jnp.max(s, axis=-1, keepdims=True))
            o = _dot(jnp.exp(s - m).astype(BF16), v)
            den = pltpu.roll(o, ROT_DIM, 1) + jnp.exp(sink - m)
            o = jnp.where(low, o / den, 0.0).reshape(SWA_GROUP, SWA_BLOCK, LANES)
            for pair in range(SWA_GROUP // 2):
                slab = o[2 * pair] + pltpu.roll(o[2 * pair + 1], ROT_DIM, 1)
                c0 = (kv * (SWA_GROUP // 2) + pair) * LANES
                o_ref[r0:r0 + SWA_BLOCK, c0:c0 + LANES] = slab.astype(BF16)


def _swa_attention(qd, kd, vd, sink_col, batch, seq_lat, seq_ctx, lat_queries):
    n_lat_rows = batch * seq_lat
    ctx_blk0 = n_lat_rows // seq_ctx
    seq_q = seq_lat if lat_queries else seq_ctx
    blocks = 2 if seq_q % (2 * SWA_BLOCK) == 0 and n_lat_rows % (2 * SWA_BLOCK) == 0 else 1
    tq = blocks * SWA_BLOCK
    nb = seq_q // tq
    q0 = 0 if lat_queries else n_lat_rows // tq
    qspec = pl.BlockSpec((SWA_HEADS, tq, LANES), lambda b, i: (0, q0 + b * nb + i, 0))
    kx = pl.BlockSpec((SWA_KV, seq_ctx, LANES), lambda b, i: (0, ctx_blk0 + b, 0))
    sk = pl.BlockSpec(sink_col.shape, lambda b, i: (0, 0, 0))
    if lat_queries:
        kl = pl.BlockSpec((SWA_KV, seq_lat, LANES), lambda b, i: (0, b, 0))
        in_specs, args = [qspec, kl, kl, kx, kx, sk], (qd, kd, vd, kd, vd, sink_col)
    else:
        in_specs, args = [qspec, kx, kx, sk], (qd, kd, vd, sink_col)
    return pl.pallas_call(
        functools.partial(_swa_kernel, has_lat=lat_queries, seq_lat=seq_lat, blocks=blocks),
        grid=(batch, nb),
        in_specs=in_specs,
        out_specs=pl.BlockSpec((tq, BRANCH_W), lambda b, i: (b * nb + i, 0)),
        out_shape=jax.ShapeDtypeStruct((batch * nb * tq, BRANCH_W), BF16),
        compiler_params=_cparams(("parallel", "parallel")),
    )(*args)


def _gate_kernel(h_ref, ya_ref, yb_ref, yc_ref, yd_ref, wg0, wg1, wg2, wg3, bg_ref,
                 wb0, wb1, wb2, wb3, o_ref):
    h = h_ref[...]
    acc = None
    for b, (y_ref, wg_ref, wb_ref) in enumerate(zip((ya_ref, yb_ref, yc_ref, yd_ref),
                                                    (wg0, wg1, wg2, wg3), (wb0, wb1, wb2, wb3))):
        gate = jax.nn.sigmoid(_dot(h, wg_ref[...]) + bg_ref[b])
        term = gate * _dot(y_ref[...], wb_ref[...])
        acc = term if acc is None else acc + term
    o_ref[...] = acc.astype(BF16)


def _gated_merge(h, ys, wg, bg, wbr, rows, tm, tn):
    d = h.shape[1]
    ncol = d // tn
    wg_specs = [pl.BlockSpec((d, tn), lambda j, i, b=b: (0, b * ncol + j)) for b in range(4)]
    wb_specs = [pl.BlockSpec((None, BRANCH_W, tn), lambda j, i, b=b: (b, 0, j)) for b in range(4)]
    yspec = pl.BlockSpec((tm, BRANCH_W), lambda j, i: (i, 0))
    return pl.pallas_call(
        _gate_kernel,
        grid=(ncol, rows // tm),
        in_specs=[pl.BlockSpec((tm, d), lambda j, i: (i, 0)), yspec, yspec, yspec, yspec,
                  *wg_specs, pl.BlockSpec((4, 1, tn), lambda j, i: (0, 0, j)), *wb_specs],
        out_specs=pl.BlockSpec((tm, tn), lambda j, i: (i, j)),
        out_shape=jax.ShapeDtypeStruct((rows, d), BF16),
        compiler_params=_cparams(("parallel", "parallel")),
    )(h, *ys, wg, wg, wg, wg, bg, wbr, wbr, wbr, wbr)


def _outproj_kernel(a_ref, x_ref, gate_ref, gpost_ref, w_ref, o_ref):
    y = _dot(a_ref[...], w_ref[...])
    o_ref[...] = x_ref[...] + gate_ref[...] * (_rms(y) * gpost_ref[...])


def _outproj(a, x, mod, row_of_tile, gpost, w, tm):
    rows, d = a.shape
    return pl.pallas_call(
        _outproj_kernel,
        grid=(rows // tm,),
        in_specs=[
            pl.BlockSpec((tm, d), lambda i: (i, 0)),
            pl.BlockSpec((tm, d), lambda i: (i, 0)),
            pl.BlockSpec((None, None, 1, d), lambda i: (row_of_tile(i), 5, 0, 0)),
            pl.BlockSpec((1, d), lambda i: (0, 0)),
            pl.BlockSpec((d, d), lambda i: (0, 0), pipeline_mode=pl.Buffered(1)),
        ],
        out_specs=pl.BlockSpec((tm, d), lambda i: (i, 0)),
        out_shape=jax.ShapeDtypeStruct((rows, d), F32),
        compiler_params=_cparams(("parallel",)),
    )(a, x, mod, gpost, w)


def _rope_tables(seq_lat, n_ctx_rows, batch):
    half = ROT_DIM // 2
    freqs = 1.0 / (ROPE_BASE ** (jnp.arange(0, half, 2, dtype=F32) / half))
    t = jnp.arange(seq_lat)
    ang_r = (t // GRID_W).astype(F32)[:, None] * freqs[None, :]
    ang_c = (t % GRID_W).astype(F32)[:, None] * freqs[None, :]
    ang = jnp.concatenate([ang_r, ang_r, ang_c, ang_c], axis=-1)
    ang = jnp.tile(ang, (batch, LANES // ROT_DIM))
    cos, sin = jnp.cos(ang), jnp.sin(ang)
    first = (jnp.arange(LANES) % half) < ROT_QUARTER
    sina = jnp.where(first, -sin, 0.0)
    sinb = jnp.where(first, 0.0, sin)
    ones = jnp.ones((n_ctx_rows, LANES), F32)
    zeros = jnp.zeros((n_ctx_rows, LANES), F32)
    return (jnp.concatenate([cos, ones]), jnp.concatenate([sina, zeros]), jnp.concatenate([sinb, zeros]))


def _pack_layer(l, w_in, b_in, gm_ws, gm_bs, mla_w_uq, mla_w_ukv, swa_sink, w_branch, w_out):
    d = w_in.shape[1]
    p = {}
    wi, bi = w_in[l], b_in[l]
    c_hi = 2048 + 832
    kr = slice(c_hi - ROT_DIM, c_hi)
    p['w_abcd'] = jnp.concatenate([wi[:, :c_hi], wi[:, kr], wi[:, c_hi:c_hi + 768]], axis=1).astype(BF16)
    p['b_abcd'] = jnp.concatenate([bi[:c_hi], bi[kr], bi[c_hi:c_hi + 768]])[None, :]
    p['w_gates'] = wi[:, c_hi + 768:].astype(BF16)
    p['b_gates'] = bi[c_hi + 768:].reshape(4, 1, d)
    p['gm_ws'] = gm_ws[l].astype(BF16)
    p['gm_bs'] = gm_bs[l][:, :, None]
    uq = mla_w_uq[l].reshape(-1, MLA_HEADS, MLA_NOPE + ROT_DIM)
    p['wuq'] = jnp.concatenate([uq[:, :, :MLA_NOPE].reshape(-1, MLA_HEADS * MLA_NOPE),
                                uq[:, :, MLA_NOPE:].reshape(-1, MLA_HEADS * ROT_DIM)], axis=1).astype(BF16)
    ukv = mla_w_ukv[l].reshape(-1, MLA_HEADS, MLA_NOPE + MLA_V)
    p['wukv'] = jnp.concatenate([ukv[:, :, :MLA_NOPE].reshape(-1, MLA_HEADS * MLA_NOPE),
                                 ukv[:, :, MLA_NOPE:].reshape(-1, MLA_HEADS * MLA_V)], axis=1).astype(BF16)
    p['sink'] = jnp.repeat(swa_sink[l].reshape(SWA_KV, SWA_GROUP), SWA_BLOCK, axis=1)[:, :, None]
    p['w_branch'] = w_branch[l].astype(BF16)
    p['w_out'] = w_out[l].astype(BF16)
    return p


def kernel(x, c, ctx, c_ctx, w_mod, b_mod, norm_pre, norm_post, w_ff_gate, w_ff_up, w_ff_down, w_in, b_in, gm_ln_g, gm_ln_b, gm_ws, gm_bs, cv_dw, cv_db, cv_ln_g, cv_ln_b, mla_q_norm, mla_kv_norm, mla_w_uq, mla_w_ukv, swa_sink, w_branch, w_out):
    batch, seq_lat, d = x.shape
    seq_ctx = ctx.shape[1]
    depth = w_mod.shape[0]
    n_lat = batch * seq_lat
    n_ctx = batch * seq_ctx
    assert batch + 1 <= COND_ROWS and seq_lat % GRID_W == 0
    tm = math.gcd(512, math.gcd(seq_lat, n_ctx))
    ts = math.gcd(256, math.gcd(seq_lat, seq_ctx))
    assert tm % SWA_BLOCK == 0 and ts % CHUNK == 0 and seq_ctx % SWA_BLOCK == 0 and seq_lat >= 3 * SWA_BLOCK

    lat_tiles, tiles_per_batch = n_lat // tm, seq_lat // tm

    def row_of_tile(i):
        return jnp.where(i < lat_tiles, i // tiles_per_batch, batch)

    rows_all = jnp.concatenate([x.reshape(n_lat, d), ctx.reshape(n_ctx, d)], axis=0)
    cond = jnp.zeros((COND_ROWS, d), F32).at[:batch].set(c).at[batch].set(c_ctx)
    mod = _modulation(cond, w_mod, b_mod).reshape(depth, COND_ROWS, N_MOD, 1, d)
    cos, sina, sinb = _rope_tables(seq_lat, n_ctx, batch)

    d_ff = w_ff_gate.shape[-1]
    wg_all = _cast_pad_cols(w_ff_gate.reshape(-1, d_ff), D_FF_PAD).reshape(depth, 2, d, D_FF_PAD)
    wu_all = _cast_pad_cols(w_ff_up.reshape(-1, d_ff), D_FF_PAD).reshape(depth, 2, d, D_FF_PAD)
    wd_all = _cast_pad_rows(w_ff_down.reshape(-1, d_ff, d), D_FF_PAD).reshape(depth, 2, D_FF_PAD, d)

    xs = rows_all
    for l in range(depth):
        last = l == depth - 1
        p = _pack_layer(l, w_in, b_in, gm_ws, gm_bs, mla_w_uq, mla_w_ukv, swa_sink, w_branch, w_out)
        m = mod[l]
        npre, npost = norm_pre[l][:, None, :], norm_post[l][:, None, :]

        xs = _ffn(xs, xs.shape[0], m, 0, row_of_tile, npre[0], npost[0], wg_all, wu_all, wd_all, l, 0, tm)

        h, za, zb, zc, zd = _inproj(xs, m, row_of_tile, npre[1], p['w_abcd'], p['b_abcd'],
                                    (1024, 1024, 896, 768), tm)
        ya, yb = _mix_ab(za, zb, gm_ln_g[l][None], gm_ln_b[l][None], p['gm_ws'], p['gm_bs'],
                         cv_dw[l], cv_db[l][None], cv_ln_g[l][None], cv_ln_b[l][None],
                         n_lat, seq_lat, seq_ctx, ts)
        qc, kc, vc, qd, kd, vd = _qkv(zc, zd, cos, sina, sinb, mla_q_norm[l][None], mla_kv_norm[l][None],
                                      p['wuq'], p['wukv'], tm)
        yc = _mla_attention(qc, kc, vc, batch, seq_lat, seq_ctx, True)
        yd = _swa_attention(qd, kd, vd, p['sink'], batch, seq_lat, seq_ctx, True)
        rows_mix = n_lat
        if not last:
            yc = jnp.concatenate([yc, _mla_attention(qc, kc, vc, batch, seq_lat, seq_ctx, False)], axis=0)
            yd = jnp.concatenate([yd, _swa_attention(qd, kd, vd, p['sink'], batch, seq_lat, seq_ctx, False)], axis=0)
            rows_mix = n_lat + n_ctx
        merged = _gated_merge(h, (ya, yb, yc, yd), p['w_gates'], p['b_gates'], p['w_branch'], rows_mix, tm, 512)
        xs = _outproj(merged, xs, m, row_of_tile, npost[1], p['w_out'], tm)

        xs = _ffn(xs, rows_mix, m, 2, row_of_tile, npre[2], npost[2], wg_all, wu_all, wd_all, l, 1, tm)
    return xs[:n_lat].reshape(batch, seq_lat, d)
```

```python
import functools
import math

import jax
import jax.numpy as jnp
from jax import lax
from jax.experimental import pallas as pl
from jax.experimental.pallas import tpu as pltpu

F32 = jnp.float32
BF16 = jnp.bfloat16

EPS = 1e-6
ROPE_BASE = 10000.0
GRID_W = 64
N_MOD = 9
COND_ROWS = 8
LANES = 128
SUBLANES = 8
ROT_DIM = 64
ROT_QUARTER = ROT_DIM // 4
CONV_W = 31
CONV_HALO = 16
CHUNK = 128
GM_GROUPS = 4
BRANCH_W = 512
MLA_HEADS = 4
MLA_NOPE = 128
MLA_V = 128
MLA_QK = 256
MLA_SUM_ROWS = 16
MLA_SCALE = (MLA_NOPE + ROT_DIM) ** -0.5
SWA_HEADS = 8
SWA_KV = 2
SWA_GROUP = SWA_HEADS // SWA_KV
SWA_BLOCK = 128
WINDOW = 128
SWA_SCALE = ROT_DIM ** -0.5
D_FF_PAD = 5632
FF_CHUNK = 512
VMEM_LIMIT = 56 * 1024 * 1024
VMEM_LIMIT_FFN = 62 * 1024 * 1024


def _cparams(sem, vmem_limit=VMEM_LIMIT):
    return pltpu.CompilerParams(dimension_semantics=sem, vmem_limit_bytes=vmem_limit)


def _rms(x):
    return x * lax.rsqrt(jnp.mean(x * x, axis=-1, keepdims=True) + EPS)


def _layernorm(x, g, b):
    mu = jnp.mean(x, axis=-1, keepdims=True)
    xc = x - mu
    var = jnp.mean(xc * xc, axis=-1, keepdims=True)
    return xc * lax.rsqrt(var + EPS) * g + b


def _silu(x):
    return x * jax.nn.sigmoid(x)


def _dot(a, b):
    return jnp.dot(a, b, preferred_element_type=F32)


def _dot_nt(a, b):
    return lax.dot_general(a, b, (((1,), (1,)), ((), ())), preferred_element_type=F32)


def _cast_pad_cols_kernel(w_ref, o_ref):
    n = w_ref.shape[1]
    o_ref[:, :n] = w_ref[...].astype(BF16)
    o_ref[:, n:] = jnp.zeros((o_ref.shape[0], o_ref.shape[1] - n), BF16)


def _cast_pad_cols(w, n_pad, tr=256):
    m, n = w.shape
    return pl.pallas_call(
        _cast_pad_cols_kernel,
        grid=(m // tr,),
        in_specs=[pl.BlockSpec((tr, n), lambda i: (i, 0))],
        out_specs=pl.BlockSpec((tr, n_pad), lambda i: (i, 0)),
        out_shape=jax.ShapeDtypeStruct((m, n_pad), BF16),
        compiler_params=_cparams(("parallel",)),
    )(w)


def _cast_pad_rows_kernel(w_ref, o_ref, *, rows_valid):
    tr = w_ref.shape[0]
    row = pl.program_id(1) * tr + lax.broadcasted_iota(jnp.int32, w_ref.shape, 0)
    o_ref[...] = jnp.where(row < rows_valid, w_ref[...], 0.0).astype(BF16)


def _cast_pad_rows(w, m_pad, tr=512):
    g, m, n = w.shape
    return pl.pallas_call(
        functools.partial(_cast_pad_rows_kernel, rows_valid=m),
        grid=(g, m_pad // tr),
        in_specs=[pl.BlockSpec((None, tr, n), lambda i, j: (i, j, 0))],
        out_specs=pl.BlockSpec((None, tr, n), lambda i, j: (i, j, 0)),
        out_shape=jax.ShapeDtypeStruct((g, m_pad, n), BF16),
        compiler_params=_cparams(("parallel", "parallel")),
    )(w)


def _mod_kernel(cond_ref, w_ref, b_ref, o_ref):
    s = _silu(cond_ref[...]).astype(BF16)
    o_ref[...] = _dot(s, w_ref[...].astype(BF16)) + b_ref[...]


def _modulation(cond, w_mod, b_mod, tn=1024):
    n_layers, d, n = w_mod.shape
    return pl.pallas_call(
        _mod_kernel,
        grid=(n_layers, n // tn),
        in_specs=[
            pl.BlockSpec((COND_ROWS, d), lambda l, j: (0, 0)),
            pl.BlockSpec((None, d, tn), lambda l, j: (l, 0, j)),
            pl.BlockSpec((None, 1, tn), lambda l, j: (l, 0, j)),
        ],
        out_specs=pl.BlockSpec((None, COND_ROWS, tn), lambda l, j: (l, 0, j)),
        out_shape=jax.ShapeDtypeStruct((n_layers, COND_ROWS, n), F32),
        compiler_params=_cparams(("parallel", "parallel")),
    )(cond, w_mod, b_mod.reshape(n_layers, 1, n))


def _ffn_kernel(x_ref, shift_ref, scale_ref, gate_ref, gpre_ref, gpost_ref,
                wg_ref, wu_ref, wd_ref, o_ref, h_ref):
    f = pl.program_id(1)

    @pl.when(f == 0)
    def _():
        y = _rms(x_ref[...]) * gpre_ref[...]
        h_ref[...] = (y * (1.0 + scale_ref[...]) + shift_ref[...]).astype(BF16)
        o_ref[...] = jnp.zeros_like(o_ref)

    h = h_ref[...]
    g = _dot(h, wg_ref[...])
    u = _dot(h, wu_ref[...])
    a = (_silu(g) * u).astype(BF16)
    o_ref[...] += _dot(a, wd_ref[...])

    @pl.when(f == pl.num_programs(1) - 1)
    def _():
        r = _rms(o_ref[...]) * gpost_ref[...]
        o_ref[...] = x_ref[...] + 0.5 * gate_ref[...] * r


def _ffn(x, rows_out, mod, j, row_of_tile, gpre, gpost, wg, wu, wd, l, s, tm):
    d = x.shape[1]
    f_pad = wg.shape[-1]
    tf = FF_CHUNK

    def mspec(k):
        return pl.BlockSpec((None, None, 1, d), lambda i, f: (row_of_tile(i), k, 0, 0))

    vec = pl.BlockSpec((1, d), lambda i, f: (0, 0))
    return pl.pallas_call(
        _ffn_kernel,
        grid=(rows_out // tm, f_pad // tf),
        in_specs=[
            pl.BlockSpec((tm, d), lambda i, f: (i, 0)),
            mspec(3 * j), mspec(3 * j + 1), mspec(3 * j + 2),
            vec, vec,
            pl.BlockSpec((None, None, d, tf), lambda i, f: (l, s, 0, f)),
            pl.BlockSpec((None, None, d, tf), lambda i, f: (l, s, 0, f)),
            pl.BlockSpec((None, None, tf, d), lambda i, f: (l, s, f, 0)),
        ],
        out_specs=pl.BlockSpec((tm, d), lambda i, f: (i, 0)),
        out_shape=jax.ShapeDtypeStruct((rows_out, d), F32),
        scratch_shapes=[pltpu.VMEM((tm, d), BF16)],
        compiler_params=_cparams(("parallel", "arbitrary"), VMEM_LIMIT_FFN),
    )(x, mod, mod, mod, gpre, gpost, wg, wu, wd)


def _inproj_kernel(x_ref, shift_ref, scale_ref, gpre_ref, w_ref, b_ref,
                   h_ref, za_ref, zb_ref, zc_ref, zd_ref):
    y = _rms(x_ref[...]) * gpre_ref[...]
    h = (y * (1.0 + scale_ref[...]) + shift_ref[...]).astype(BF16)
    h_ref[...] = h
    off = 0
    for o_ref in (za_ref, zb_ref, zc_ref, zd_ref):
        n = o_ref.shape[-1]
        o_ref[...] = _dot(h, w_ref[:, off:off + n]) + b_ref[:, off:off + n]
        off += n


def _inproj(x, mod, row_of_tile, gpre, w, b, widths, tm):
    rows, d = x.shape
    n = w.shape[1]

    def mspec(k):
        return pl.BlockSpec((None, None, 1, d), lambda i: (row_of_tile(i), k, 0, 0))

    outs = [jax.ShapeDtypeStruct((rows, d), BF16)] + [jax.ShapeDtypeStruct((rows, wd), F32) for wd in widths]
    ospecs = [pl.BlockSpec((tm, d), lambda i: (i, 0))] + [pl.BlockSpec((tm, wd), lambda i: (i, 0)) for wd in widths]
    return pl.pallas_call(
        _inproj_kernel,
        grid=(rows // tm,),
        in_specs=[
            pl.BlockSpec((tm, d), lambda i: (i, 0)),
            mspec(3), mspec(4),
            pl.BlockSpec((1, d), lambda i: (0, 0)),
            pl.BlockSpec((d, n), lambda i: (0, 0), pipeline_mode=pl.Buffered(1)),
            pl.BlockSpec((1, n), lambda i: (0, 0)),
        ],
        out_specs=ospecs,
        out_shape=outs,
        compiler_params=_cparams(("parallel",)),
    )(x, mod, mod, gpre, w, b)


def _ab_kernel(za_ref, zb_ref, zprev_ref, znext_ref, lng_ref, lnb_ref, ws_ref, bs_ref,
               dw_ref, db_ref, clng_ref, clnb_ref, ya_ref, yb_ref, ext_ref, shift_ref,
               *, ts, n_lat_tiles, lat_tiles_per_seq, ctx_tiles_per_seq):
    i = pl.program_id(0)
    w = BRANCH_W

    za = za_ref[...]
    g = 0.5 * za * (1.0 + lax.erf(za * (2.0 ** -0.5)))
    vln = _layernorm(g[:, w:], lng_ref[...], lnb_ref[...]).astype(BF16)
    for c in range(ts // CHUNK):
        rows = slice(c * CHUNK, (c + 1) * CHUNK)
        for gi in range(GM_GROUPS):
            cols = slice(gi * CHUNK, (gi + 1) * CHUNK)
            sv = _dot(ws_ref[gi], vln[rows, cols]) + bs_ref[gi]
            ya_ref[rows, cols] = (g[rows, cols] * sv).astype(BF16)

    def glu(z):
        return z[:, :w] * jax.nn.sigmoid(z[:, w:])

    in_lat = i < n_lat_tiles
    pos = jnp.where(in_lat, i % lat_tiles_per_seq, (i - n_lat_tiles) % ctx_tiles_per_seq)
    last = jnp.where(in_lat, lat_tiles_per_seq - 1, ctx_tiles_per_seq - 1)
    ext_ref[0:CONV_HALO, :] = jnp.where(pos != 0, glu(zprev_ref[...]), 0.0)
    ext_ref[CONV_HALO:CONV_HALO + ts, :] = glu(zb_ref[...])
    ext_ref[CONV_HALO + ts:, :] = jnp.where(pos != last, glu(znext_ref[...]), 0.0)
    span = ts + 2 * CONV_HALO - SUBLANES
    for r in range(1, SUBLANES):
        shift_ref[r - 1, 0:span, :] = ext_ref[r:r + span, :]
    first_tap = CONV_HALO - CONV_W // 2
    acc = jnp.zeros((ts, w), F32) + db_ref[...]
    for k in range(CONV_W):
        a, r = divmod(first_tap + k, SUBLANES)
        win = ext_ref[a * SUBLANES:a * SUBLANES + ts, :] if r == 0 else shift_ref[r - 1, a * SUBLANES:a * SUBLANES + ts, :]
        acc = acc + dw_ref[k:k + 1, :] * win
    yb_ref[...] = _silu(_layernorm(acc, clng_ref[...], clnb_ref[...])).astype(BF16)


def _mix_ab(za, zb, lng, lnb, ws, bs, dw, db, clng, clnb, n_lat_rows, seq_lat, seq_ctx, ts):
    rows = za.shape[0]
    w = BRANCH_W
    hb = ts // CONV_HALO
    n_halo = rows // CONV_HALO
    kern = functools.partial(_ab_kernel, ts=ts, n_lat_tiles=n_lat_rows // ts,
                             lat_tiles_per_seq=seq_lat // ts, ctx_tiles_per_seq=seq_ctx // ts)
    vec = pl.BlockSpec((1, w), lambda i: (0, 0))
    return pl.pallas_call(
        kern,
        grid=(rows // ts,),
        in_specs=[
            pl.BlockSpec((ts, 2 * w), lambda i: (i, 0)),
            pl.BlockSpec((ts, 2 * w), lambda i: (i, 0)),
            pl.BlockSpec((CONV_HALO, 2 * w), lambda i: (jnp.maximum(i * hb - 1, 0), 0)),
            pl.BlockSpec((CONV_HALO, 2 * w), lambda i: (jnp.minimum((i + 1) * hb, n_halo - 1), 0)),
            vec, vec,
            pl.BlockSpec((GM_GROUPS, CHUNK, CHUNK), lambda i: (0, 0, 0)),
            pl.BlockSpec((GM_GROUPS, CHUNK, 1), lambda i: (0, 0, 0)),
            pl.BlockSpec((CONV_W, w), lambda i: (0, 0)),
            vec, vec, vec,
        ],
        out_specs=[pl.BlockSpec((ts, w), lambda i: (i, 0)), pl.BlockSpec((ts, w), lambda i: (i, 0))],
        out_shape=[jax.ShapeDtypeStruct((rows, w), BF16), jax.ShapeDtypeStruct((rows, w), BF16)],
        scratch_shapes=[pltpu.VMEM((ts + 2 * CONV_HALO, w), F32),
                        pltpu.VMEM((SUBLANES - 1, ts + 2 * CONV_HALO - SUBLANES, w), F32)],
        compiler_params=_cparams(("parallel",)),
    )(za, zb, zb, zb, lng, lnb, ws, bs, dw, db, clng, clnb)


def _qkv_kernel(zc_ref, zd_ref, cos_ref, sina_ref, sinb_ref, qn_ref, kvn_ref, wuq_ref, wukv_ref,
                qc_ref, kc_ref, vc_ref, qd_ref, kd_ref, vd_ref):
    cos, sina, sinb = cos_ref[...], sina_ref[...], sinb_ref[...]
    low = lax.broadcasted_iota(jnp.int32, cos.shape, 1) < ROT_DIM

    def rope(slab):
        return (slab * cos + pltpu.roll(slab, LANES - ROT_QUARTER, 1) * sina
                + pltpu.roll(slab, ROT_QUARTER, 1) * sinb)

    def swap(slab):
        return pltpu.roll(slab, ROT_DIM, 1)

    zc = zc_ref[...]
    nq = MLA_HEADS * MLA_NOPE
    q = _dot((_rms(zc[:, :512]) * qn_ref[...]).astype(BF16), wuq_ref[...]) * MLA_SCALE
    kv = _dot((_rms(zc[:, 512:768]) * kvn_ref[...]).astype(BF16), wukv_ref[...])
    kr = rope(zc[:, 768:896])
    ones = jnp.ones((MLA_SUM_ROWS, zc.shape[0]), BF16)
    for pair in range(MLA_HEADS // 2):
        qr = rope(q[:, nq + pair * LANES:nq + (pair + 1) * LANES]).T.astype(BF16)
        for h in (2 * pair, 2 * pair + 1):
            qc_ref[h, 0:MLA_NOPE, :] = q[:, h * MLA_NOPE:(h + 1) * MLA_NOPE].T.astype(BF16)
            qc_ref[h, MLA_NOPE:, :] = qr
            kc_ref[h, :, 0:MLA_NOPE] = kv[:, h * MLA_NOPE:(h + 1) * MLA_NOPE].astype(BF16)
            keep = low if h % 2 == 0 else jnp.logical_not(low)
            kc_ref[h, :, MLA_NOPE:] = jnp.where(keep, kr, 0.0).astype(BF16)
            vc_ref[h, 0:MLA_V, :] = kv[:, nq + h * MLA_V:nq + (h + 1) * MLA_V].T.astype(BF16)
            vc_ref[h, MLA_V:, :] = ones

    zd = zd_ref[...]
    for pair in range(SWA_HEADS // 2):
        s = rope(zd[:, pair * LANES:(pair + 1) * LANES]) * SWA_SCALE
        qd_ref[2 * pair] = jnp.where(low, s, 0.0).astype(BF16)
        qd_ref[2 * pair + 1] = jnp.where(low, swap(s), 0.0).astype(BF16)
    k = rope(zd[:, 512:640])
    kd_ref[0] = jnp.where(low, k, 0.0).astype(BF16)
    kd_ref[1] = jnp.where(low, swap(k), 0.0).astype(BF16)
    v = zd[:, 640:768]
    vd_ref[0] = jnp.where(low, v, 1.0).astype(BF16)
    vd_ref[1] = jnp.where(low, swap(v), 1.0).astype(BF16)


def _qkv(zc, zd, cos, sina, sinb, qn, kvn, wuq, wukv, tm):
    rows = zc.shape[0]

    def rowspec(wd):
        return pl.BlockSpec((tm, wd), lambda i: (i, 0))

    def full(a):
        return pl.BlockSpec(a.shape, lambda i: (0,) * a.ndim)

    def headspec(nh, wd):
        return pl.BlockSpec((nh, tm, wd), lambda i: (0, i, 0))

    def headspec_t(nh, wd):
        return pl.BlockSpec((nh, wd, tm), lambda i: (0, 0, i))

    row_major = [(MLA_HEADS, MLA_QK), (SWA_HEADS, LANES), (SWA_KV, LANES), (SWA_KV, LANES)]
    feat_major = [(MLA_HEADS, MLA_QK), (MLA_HEADS, MLA_V + MLA_SUM_ROWS)]
    kc, qd, kd, vd = [jax.ShapeDtypeStruct((nh, rows, wd), BF16) for nh, wd in row_major]
    qc, vc = [jax.ShapeDtypeStruct((nh, wd, rows), BF16) for nh, wd in feat_major]
    return pl.pallas_call(
        _qkv_kernel,
        grid=(rows // tm,),
        in_specs=[rowspec(zc.shape[1]), rowspec(zd.shape[1]), rowspec(LANES), rowspec(LANES), rowspec(LANES),
                  full(qn), full(kvn), full(wuq), full(wukv)],
        out_specs=[headspec_t(*feat_major[0]), headspec(*row_major[0]), headspec_t(*feat_major[1]),
                   headspec(*row_major[1]), headspec(*row_major[2]), headspec(*row_major[3])],
        out_shape=[qc, kc, vc, qd, kd, vd],
        compiler_params=_cparams(("parallel",)),
    )(zc, zd, cos, sina, sinb, qn, kvn, wuq, wukv)


def _mla_kernel(*refs, has_lat):
    if has_lat:
        qt_ref, kl_ref, vlt_ref, kx_ref, vxt_ref, o_ref = refs
    else:
        qt_ref, kx_ref, vxt_ref, o_ref = refs
    qt = qt_ref[...]
    sx = _dot(kx_ref[...], qt)
    m = jnp.max(sx, axis=0, keepdims=True)
    if has_lat:
        sl = _dot(kl_ref[...], qt)
        m = jnp.maximum(m, jnp.max(sl, axis=0, keepdims=True))
    ot = _dot(vxt_ref[...], jnp.exp(sx - m).astype(BF16))
    if has_lat:
        ot = ot + _dot(vlt_ref[...], jnp.exp(sl - m).astype(BF16))
    o_ref[...] = (ot[:MLA_V] / ot[MLA_V:MLA_V + 1]).T.astype(BF16)


def _mla_attention(qc, kc, vc, batch, seq_lat, seq_ctx, lat_queries):
    n_lat_rows = batch * seq_lat
    ctx_blk0 = n_lat_rows // seq_ctx
    tq = math.gcd(512, seq_lat if lat_queries else seq_ctx)
    if lat_queries:
        nq, q0 = seq_lat // tq, 0
    else:
        nq, q0 = seq_ctx // tq, n_lat_rows // tq
    vrows = vc.shape[1]
    qspec = pl.BlockSpec((None, MLA_QK, tq), lambda b, h, i: (h, 0, q0 + b * nq + i))
    kx = pl.BlockSpec((None, seq_ctx, MLA_QK), lambda b, h, i: (h, ctx_blk0 + b, 0))
    vx = pl.BlockSpec((None, vrows, seq_ctx), lambda b, h, i: (h, 0, ctx_blk0 + b))
    if lat_queries:
        kl = pl.BlockSpec((None, seq_lat, MLA_QK), lambda b, h, i: (h, b, 0))
        vl = pl.BlockSpec((None, vrows, seq_lat), lambda b, h, i: (h, 0, b))
        in_specs, args = [qspec, kl, vl, kx, vx], (qc, kc, vc, kc, vc)
    else:
        in_specs, args = [qspec, kx, vx], (qc, kc, vc)
    return pl.pallas_call(
        functools.partial(_mla_kernel, has_lat=lat_queries),
        grid=(batch, MLA_HEADS, nq),
        in_specs=in_specs,
        out_specs=pl.BlockSpec((tq, MLA_V), lambda b, h, i: (b * nq + i, h)),
        out_shape=jax.ShapeDtypeStruct((batch * nq * tq, MLA_HEADS * MLA_V), BF16),
        compiler_params=_cparams(("parallel", "parallel", "parallel")),
    )(*args)


def _swa_kernel(*refs, has_lat, seq_lat, blocks):
    if has_lat:
        q_ref, kl_ref, vl_ref, kx_ref, vx_ref, sink_ref, o_ref = refs
    else:
        q_ref, kx_ref, vx_ref, sink_ref, o_ref = refs
    rows = SWA_GROUP * SWA_BLOCK
    span = 3 * SWA_BLOCK
    low = lax.broadcasted_iota(jnp.int32, (rows, LANES), 1) < ROT_DIM
    for blk in range(blocks):
        n = pl.program_id(1) * blocks + blk
        r0 = blk * SWA_BLOCK
        for kv in range(SWA_KV):
            q = q_ref[kv * SWA_GROUP:(kv + 1) * SWA_GROUP, r0:r0 + SWA_BLOCK, :].reshape(rows, LANES)
            sink = sink_ref[kv]
            s = _dot_nt(q, kx_ref[kv])
            v = vx_ref[kv]
            if has_lat:
                start = pl.multiple_of(jnp.clip((n - 1) * SWA_BLOCK, 0, seq_lat - span), SWA_BLOCK)
                sl = _dot_nt(q, kl_ref[kv, pl.ds(start, span), :])
                qpos = n * SWA_BLOCK + (lax.broadcasted_iota(jnp.int32, sl.shape, 0) & (SWA_BLOCK - 1))
                kpos = start + lax.broadcasted_iota(jnp.int32, sl.shape, 1)
                s = jnp.concatenate([s, jnp.where(jnp.abs(kpos - qpos) <= WINDOW, sl, -jnp.inf)], axis=1)
                v = jnp.concatenate([v, vl_ref[kv, pl.ds(start, span), :]], axis=0)
            m = jnp.maximum(sink, jnp.max(s, axis=-1, keepdims=True))
            o = _dot(jnp.exp(s - m).astype(BF16), v)
            den = pltpu.roll(o, ROT_DIM, 1) + jnp.exp(sink - m)
            o = jnp.where(low, o / den, 0.0).reshape(SWA_GROUP, SWA_BLOCK, LANES)
            for pair in range(SWA_GROUP // 2):
                slab = o[2 * pair] + pltpu.roll(o[2 * pair + 1], ROT_DIM, 1)
                c0 = (kv * (SWA_GROUP // 2) + pair) * LANES
                o_ref[r0:r0 + SWA_BLOCK, c0:c0 + LANES] = slab.astype(BF16)


def _swa_attention(qd, kd, vd, sink_col, batch, seq_lat, seq_ctx, lat_queries):
    n_lat_rows = batch * seq_lat
    ctx_blk0 = n_lat_rows // seq_ctx
    seq_q = seq_lat if lat_queries else seq_ctx
    blocks = 2 if seq_q % (2 * SWA_BLOCK) == 0 and n_lat_rows % (2 * SWA_BLOCK) == 0 else 1
    tq = blocks * SWA_BLOCK
    nb = seq_q // tq
    q0 = 0 if lat_queries else n_lat_rows // tq
    qspec = pl.BlockSpec((SWA_HEADS, tq, LANES), lambda b, i: (0, q0 + b * nb + i, 0))
    kx = pl.BlockSpec((SWA_KV, seq_ctx, LANES), lambda b, i: (0, ctx_blk0 + b, 0))
    sk = pl.BlockSpec(sink_col.shape, lambda b, i: (0, 0, 0))
    if lat_queries:
        kl = pl.BlockSpec((SWA_KV, seq_lat, LANES), lambda b, i: (0, b, 0))
        in_specs, args = [qspec, kl, kl, kx, kx, sk], (qd, kd, vd, kd, vd, sink_col)
    else:
        in_specs, args = [qspec, kx, kx, sk], (qd, kd, vd, sink_col)
    return pl.pallas_call(
        functools.partial(_swa_kernel, has_lat=lat_queries, seq_lat=seq_lat, blocks=blocks),
        grid=(batch, nb),
        in_specs=in_specs,
        out_specs=pl.BlockSpec((tq, BRANCH_W), lambda b, i: (b * nb + i, 0)),
        out_shape=jax.ShapeDtypeStruct((batch * nb * tq, BRANCH_W), BF16),
        compiler_params=_cparams(("parallel", "parallel")),
    )(*args)


def _gate_kernel(h_ref, ya_ref, yb_ref, yc_ref, yd_ref, wg0, wg1, wg2, wg3, bg_ref,
                 wb0, wb1, wb2, wb3, o_ref):
    h = h_ref[...]
    acc = None
    for b, (y_ref, wg_ref, wb_ref) in enumerate(zip((ya_ref, yb_ref, yc_ref, yd_ref),
                                                    (wg0, wg1, wg2, wg3), (wb0, wb1, wb2, wb3))):
        gate = jax.nn.sigmoid(_dot(h, wg_ref[...]) + bg_ref[b])
        term = gate * _dot(y_ref[...], wb_ref[...])
        acc = term if acc is None else acc + term
    o_ref[...] = acc.astype(BF16)


def _gated_merge(h, ys, wg, bg, wbr, rows, tm, tn):
    d = h.shape[1]
    ncol = d // tn
    wg_specs = [pl.BlockSpec((d, tn), lambda j, i, b=b: (0, b * ncol + j)) for b in range(4)]
    wb_specs = [pl.BlockSpec((None, BRANCH_W, tn), lambda j, i, b=b: (b, 0, j)) for b in range(4)]
    yspec = pl.BlockSpec((tm, BRANCH_W), lambda j, i: (i, 0))
    return pl.pallas_call(
        _gate_kernel,
        grid=(ncol, rows // tm),
        in_specs=[pl.BlockSpec((tm, d), lambda j, i: (i, 0)), yspec, yspec, yspec, yspec,
                  *wg_specs, pl.BlockSpec((4, 1, tn), lambda j, i: (0, 0, j)), *wb_specs],
        out_specs=pl.BlockSpec((tm, tn), lambda j, i: (i, j)),
        out_shape=jax.ShapeDtypeStruct((rows, d), BF16),
        compiler_params=_cparams(("parallel", "parallel")),
    )(h, *ys, wg, wg, wg, wg, bg, wbr, wbr, wbr, wbr)


def _outproj_kernel(a_ref, x_ref, gate_ref, gpost_ref, w_ref, o_ref):
    y = _dot(a_ref[...], w_ref[...])
    o_ref[...] = x_ref[...] + gate_ref[...] * (_rms(y) * gpost_ref[...])


def _outproj(a, x, mod, row_of_tile, gpost, w, tm):
    rows, d = a.shape
    return pl.pallas_call(
        _outproj_kernel,
        grid=(rows // tm,),
        in_specs=[
            pl.BlockSpec((tm, d), lambda i: (i, 0)),
            pl.BlockSpec((tm, d), lambda i: (i, 0)),
            pl.BlockSpec((None, None, 1, d), lambda i: (row_of_tile(i), 5, 0, 0)),
            pl.BlockSpec((1, d), lambda i: (0, 0)),
            pl.BlockSpec((d, d), lambda i: (0, 0), pipeline_mode=pl.Buffered(1)),
        ],
        out_specs=pl.BlockSpec((tm, d), lambda i: (i, 0)),
        out_shape=jax.ShapeDtypeStruct((rows, d), F32),
        compiler_params=_cparams(("parallel",)),
    )(a, x, mod, gpost, w)


def _rope_tables(seq_lat, n_ctx_rows, batch):
    half = ROT_DIM // 2
    freqs = 1.0 / (ROPE_BASE ** (jnp.arange(0, half, 2, dtype=F32) / half))
    t = jnp.arange(seq_lat)
    ang_r = (t // GRID_W).astype(F32)[:, None] * freqs[None, :]
    ang_c = (t % GRID_W).astype(F32)[:, None] * freqs[None, :]
    ang = jnp.concatenate([ang_r, ang_r, ang_c, ang_c], axis=-1)
    ang = jnp.tile(ang, (batch, LANES // ROT_DIM))
    cos, sin = jnp.cos(ang), jnp.sin(ang)
    first = (jnp.arange(LANES) % half) < ROT_QUARTER
    sina = jnp.where(first, -sin, 0.0)
    sinb = jnp.where(first, 0.0, sin)
    ones = jnp.ones((n_ctx_rows, LANES), F32)
    zeros = jnp.zeros((n_ctx_rows, LANES), F32)
    return (jnp.concatenate([cos, ones]), jnp.concatenate([sina, zeros]), jnp.concatenate([sinb, zeros]))


def _pack_layer(l, w_in, b_in, gm_ws, gm_bs, mla_w_uq, mla_w_ukv, swa_sink, w_branch, w_out):
    d = w_in.shape[1]
    p = {}
    wi, bi = w_in[l], b_in[l]
    c_hi = 2048 + 832
    kr = slice(c_hi - ROT_DIM, c_hi)
    p['w_abcd'] = jnp.concatenate([wi[:, :c_hi], wi[:, kr], wi[:, c_hi:c_hi + 768]], axis=1).astype(BF16)
    p['b_abcd'] = jnp.concatenate([bi[:c_hi], bi[kr], bi[c_hi:c_hi + 768]])[None, :]
    p['w_gates'] = wi[:, c_hi + 768:].astype(BF16)
    p['b_gates'] = bi[c_hi + 768:].reshape(4, 1, d)
    p['gm_ws'] = gm_ws[l].astype(BF16)
    p['gm_bs'] = gm_bs[l][:, :, None]
    uq = mla_w_uq[l].reshape(-1, MLA_HEADS, MLA_NOPE + ROT_DIM)
    p['wuq'] = jnp.concatenate([uq[:, :, :MLA_NOPE].reshape(-1, MLA_HEADS * MLA_NOPE),
                                uq[:, :, MLA_NOPE:].reshape(-1, MLA_HEADS * ROT_DIM)], axis=1).astype(BF16)
    ukv = mla_w_ukv[l].reshape(-1, MLA_HEADS, MLA_NOPE + MLA_V)
    p['wukv'] = jnp.concatenate([ukv[:, :, :MLA_NOPE].reshape(-1, MLA_HEADS * MLA_NOPE),
                                 ukv[:, :, MLA_NOPE:].reshape(-1, MLA_HEADS * MLA_V)], axis=1).astype(BF16)
    p['sink'] = jnp.repeat(swa_sink[l].reshape(SWA_KV, SWA_GROUP), SWA_BLOCK, axis=1)[:, :, None]
    p['w_branch'] = w_branch[l].astype(BF16)
    p['w_out'] = w_out[l].astype(BF16)
    return p


def kernel(x, c, ctx, c_ctx, w_mod, b_mod, norm_pre, norm_post, w_ff_gate, w_ff_up, w_ff_down, w_in, b_in, gm_ln_g, gm_ln_b, gm_ws, gm_bs, cv_dw, cv_db, cv_ln_g, cv_ln_b, mla_q_norm, mla_kv_norm, mla_w_uq, mla_w_ukv, swa_sink, w_branch, w_out):
    batch, seq_lat, d = x.shape
    seq_ctx = ctx.shape[1]
    depth = w_mod.shape[0]
    n_lat = batch * seq_lat
    n_ctx = batch * seq_ctx
    assert batch + 1 <= COND_ROWS and seq_lat % GRID_W == 0
    tm = math.gcd(512, math.gcd(seq_lat, n_ctx))
    ts = math.gcd(256, math.gcd(seq_lat, seq_ctx))
    assert tm % SWA_BLOCK == 0 and ts % CHUNK == 0 and seq_ctx % SWA_BLOCK == 0 and seq_lat >= 3 * SWA_BLOCK

    tm_ffn = math.gcd(1024, math.gcd(seq_lat, n_ctx))

    def cond_row(tile):
        return lambda i: jnp.where(i < n_lat // tile, i // (seq_lat // tile), batch)

    row_of_tile, row_of_ffn_tile = cond_row(tm), cond_row(tm_ffn)

    rows_all = jnp.concatenate([x.reshape(n_lat, d), ctx.reshape(n_ctx, d)], axis=0)
    cond = jnp.zeros((COND_ROWS, d), F32).at[:batch].set(c).at[batch].set(c_ctx)
    mod = _modulation(cond, w_mod, b_mod).reshape(depth, COND_ROWS, N_MOD, 1, d)
    cos, sina, sinb = _rope_tables(seq_lat, n_ctx, batch)

    d_ff = w_ff_gate.shape[-1]
    wg_all = _cast_pad_cols(w_ff_gate.reshape(-1, d_ff), D_FF_PAD).reshape(depth, 2, d, D_FF_PAD)
    wu_all = _cast_pad_cols(w_ff_up.reshape(-1, d_ff), D_FF_PAD).reshape(depth, 2, d, D_FF_PAD)
    wd_all = _cast_pad_rows(w_ff_down.reshape(-1, d_ff, d), D_FF_PAD).reshape(depth, 2, D_FF_PAD, d)

    xs = rows_all
    for l in range(depth):
        last = l == depth - 1
        p = _pack_layer(l, w_in, b_in, gm_ws, gm_bs, mla_w_uq, mla_w_ukv, swa_sink, w_branch, w_out)
        m = mod[l]
        npre, npost = norm_pre[l][:, None, :], norm_post[l][:, None, :]

        xs = _ffn(xs, xs.shape[0], m, 0, row_of_ffn_tile, npre[0], npost[0], wg_all, wu_all, wd_all, l, 0, tm_ffn)

        h, za, zb, zc, zd = _inproj(xs, m, row_of_tile, npre[1], p['w_abcd'], p['b_abcd'],
                                    (1024, 1024, 896, 768), tm)
        ya, yb = _mix_ab(za, zb, gm_ln_g[l][None], gm_ln_b[l][None], p['gm_ws'], p['gm_bs'],
                         cv_dw[l], cv_db[l][None], cv_ln_g[l][None], cv_ln_b[l][None],
                         n_lat, seq_lat, seq_ctx, ts)
        qc, kc, vc, qd, kd, vd = _qkv(zc, zd, cos, sina, sinb, mla_q_norm[l][None], mla_kv_norm[l][None],
                                      p['wuq'], p['wukv'], tm)
        yc = _mla_attention(qc, kc, vc, batch, seq_lat, seq_ctx, True)
        yd = _swa_attention(qd, kd, vd, p['sink'], batch, seq_lat, seq_ctx, True)
        rows_mix = n_lat
        if not last:
            yc = jnp.concatenate([yc, _mla_attention(qc, kc, vc, batch, seq_lat, seq_ctx, False)], axis=0)
            yd = jnp.concatenate([yd, _swa_attention(qd, kd, vd, p['sink'], batch, seq_lat, seq_ctx, False)], axis=0)
            rows_mix = n_lat + n_ctx
        merged = _gated_merge(h, (ya, yb, yc, yd), p['w_gates'], p['b_gates'], p['w_branch'], rows_mix, tm, 512)
        xs = _outproj(merged, xs, m, row_of_tile, npost[1], p['w_out'], tm)

        xs = _ffn(xs, rows_mix, m, 2, row_of_ffn_tile, npre[2], npost[2], wg_all, wu_all, wd_all, l, 1, tm_ffn)
    return xs[:n_lat].reshape(batch, seq_lat, d)
```

```python
import functools
import math

import jax
import jax.numpy as jnp
from jax import lax
from jax.experimental import pallas as pl
from jax.experimental.pallas import tpu as pltpu

F32 = jnp.float32
BF16 = jnp.bfloat16

EPS = 1e-6
ROPE_BASE = 10000.0
GRID_W = 64
N_MOD = 9
COND_ROWS = 8
LANES = 128
SUBLANES = 8
ROT_DIM = 64
ROT_QUARTER = ROT_DIM // 4
CONV_W = 31
CONV_HALO = 16
CHUNK = 128
GM_GROUPS = 4
BRANCH_W = 512
MLA_HEADS = 4
MLA_NOPE = 128
MLA_V = 128
MLA_QK = 256
MLA_SUM_ROWS = 16
MLA_GROUP_Q = 512
MLA_GROUPS = 2
LOG2_E = math.log2(math.e)
MLA_SCALE = (MLA_NOPE + ROT_DIM) ** -0.5 * LOG2_E
SWA_HEADS = 8
SWA_KV = 2
SWA_GROUP = SWA_HEADS // SWA_KV
SWA_BLOCK = 128
SWA_WINDOW_BLOCKS = 4
SWA_SUM_ROWS = 16
WINDOW = 128
SWA_SCALE = ROT_DIM ** -0.5 * LOG2_E
D_FF_PAD = 5632
FF_CHUNK = 512
ROW_BLOCK = 256
VMEM_LIMIT = 56 * 1024 * 1024
VMEM_LIMIT_FFN = 62 * 1024 * 1024


def _cparams(sem, vmem_limit=VMEM_LIMIT):
    return pltpu.CompilerParams(dimension_semantics=sem, vmem_limit_bytes=vmem_limit)


def _inv_rms(x):
    return lax.rsqrt(jnp.mean(x * x, axis=-1, keepdims=True) + EPS)


def _rms(x):
    return x * _inv_rms(x)


def _layernorm(x, g, b):
    mu = jnp.mean(x, axis=-1, keepdims=True)
    xc = x - mu
    var = jnp.mean(xc * xc, axis=-1, keepdims=True)
    return xc * lax.rsqrt(var + EPS) * g + b


def _silu(x):
    return x * jax.nn.sigmoid(x)


def _for_row_blocks(n_rows, body):
    def step(r, carry):
        body(pl.ds(pl.multiple_of(r * ROW_BLOCK, ROW_BLOCK), ROW_BLOCK))
        return carry

    lax.fori_loop(0, n_rows // ROW_BLOCK, step, 0)


def _dot(a, b):
    return jnp.dot(a, b, preferred_element_type=F32)


def _dot_nt(a, b):
    return lax.dot_general(a, b, (((1,), (1,)), ((), ())), preferred_element_type=F32)


def _cast_pad_cols_kernel(w_ref, o_ref):
    n = w_ref.shape[1]
    o_ref[:, :n] = w_ref[...].astype(BF16)
    o_ref[:, n:] = jnp.zeros((o_ref.shape[0], o_ref.shape[1] - n), BF16)


def _cast_pad_cols(w, n_pad, tr=256):
    m, n = w.shape
    return pl.pallas_call(
        _cast_pad_cols_kernel,
        grid=(m // tr,),
        in_specs=[pl.BlockSpec((tr, n), lambda i: (i, 0))],
        out_specs=pl.BlockSpec((tr, n_pad), lambda i: (i, 0)),
        out_shape=jax.ShapeDtypeStruct((m, n_pad), BF16),
        compiler_params=_cparams(("parallel",)),
    )(w)


def _cast_pad_rows_kernel(w_ref, o_ref, *, rows_valid):
    tr = w_ref.shape[0]
    row = pl.program_id(1) * tr + lax.broadcasted_iota(jnp.int32, w_ref.shape, 0)
    o_ref[...] = jnp.where(row < rows_valid, w_ref[...], 0.0).astype(BF16)


def _cast_pad_rows(w, m_pad, tr=512):
    g, m, n = w.shape
    return pl.pallas_call(
        functools.partial(_cast_pad_rows_kernel, rows_valid=m),
        grid=(g, m_pad // tr),
        in_specs=[pl.BlockSpec((None, tr, n), lambda i, j: (i, j, 0))],
        out_specs=pl.BlockSpec((None, tr, n), lambda i, j: (i, j, 0)),
        out_shape=jax.ShapeDtypeStruct((g, m_pad, n), BF16),
        compiler_params=_cparams(("parallel", "parallel")),
    )(w)


def _mod_kernel(cond_ref, w_ref, b_ref, o_ref):
    s = _silu(cond_ref[...]).astype(BF16)
    o_ref[...] = _dot(s, w_ref[...].astype(BF16)) + b_ref[...]


def _modulation(cond, w_mod, b_mod, tn=1024):
    n_layers, d, n = w_mod.shape
    return pl.pallas_call(
        _mod_kernel,
        grid=(n_layers, n // tn),
        in_specs=[
            pl.BlockSpec((COND_ROWS, d), lambda l, j: (0, 0)),
            pl.BlockSpec((None, d, tn), lambda l, j: (l, 0, j)),
            pl.BlockSpec((None, 1, tn), lambda l, j: (l, 0, j)),
        ],
        out_specs=pl.BlockSpec((None, COND_ROWS, tn), lambda l, j: (l, 0, j)),
        out_shape=jax.ShapeDtypeStruct((n_layers, COND_ROWS, n), F32),
        compiler_params=_cparams(("parallel", "parallel")),
    )(cond, w_mod, b_mod.reshape(n_layers, 1, n))


def _ffn_kernel(x_ref, shift_ref, scale_ref, gate_ref, gpre_ref, gpost_ref,
                wg_ref, wu_ref, wd_ref, o_ref, h_ref, r_ref):
    f = pl.program_id(1)

    @pl.when(f == 0)
    def _():
        gain = gpre_ref[...] * (1.0 + scale_ref[...])
        shift = shift_ref[...]

        def norms(rows):
            r_ref[rows, :] = _inv_rms(x_ref[rows, :])

        def block(rows):
            h_ref[rows, :] = (x_ref[rows, :] * r_ref[rows, :] * gain + shift).astype(BF16)
            o_ref[rows, :] = jnp.zeros((ROW_BLOCK, o_ref.shape[1]), F32)

        _for_row_blocks(x_ref.shape[0], norms)
        _for_row_blocks(x_ref.shape[0], block)

    h = h_ref[...]
    g = _dot(h, wg_ref[...])
    u = _dot(h, wu_ref[...])
    a = (_silu(g) * u).astype(BF16)
    o_ref[...] += _dot(a, wd_ref[...])

    @pl.when(f == pl.num_programs(1) - 1)
    def _():
        gain = 0.5 * gate_ref[...] * gpost_ref[...]

        def norms(rows):
            r_ref[rows, :] = _inv_rms(o_ref[rows, :])

        def block(rows):
            o_ref[rows, :] = x_ref[rows, :] + o_ref[rows, :] * r_ref[rows, :] * gain

        _for_row_blocks(x_ref.shape[0], norms)
        _for_row_blocks(x_ref.shape[0], block)


def _ffn(x, rows_out, mod, j, row_of_tile, gpre, gpost, wg, wu, wd, l, s, tm):
    d = x.shape[1]
    f_pad = wg.shape[-1]
    tf = FF_CHUNK

    def mspec(k):
        return pl.BlockSpec((None, None, 1, d), lambda i, f: (row_of_tile(i), k, 0, 0))

    vec = pl.BlockSpec((1, d), lambda i, f: (0, 0))
    return pl.pallas_call(
        _ffn_kernel,
        grid=(rows_out // tm, f_pad // tf),
        in_specs=[
            pl.BlockSpec((tm, d), lambda i, f: (i, 0)),
            mspec(3 * j), mspec(3 * j + 1), mspec(3 * j + 2),
            vec, vec,
            pl.BlockSpec((None, None, d, tf), lambda i, f: (l, s, 0, f)),
            pl.BlockSpec((None, None, d, tf), lambda i, f: (l, s, 0, f)),
            pl.BlockSpec((None, None, tf, d), lambda i, f: (l, s, f, 0)),
        ],
        out_specs=pl.BlockSpec((tm, d), lambda i, f: (i, 0)),
        out_shape=jax.ShapeDtypeStruct((rows_out, d), F32),
        scratch_shapes=[pltpu.VMEM((tm, d), BF16), pltpu.VMEM((tm, 1), F32)],
        compiler_params=_cparams(("parallel", "arbitrary"), VMEM_LIMIT_FFN),
    )(x, mod, mod, mod, gpre, gpost, wg, wu, wd)


def _inproj_kernel(x_ref, shift_ref, scale_ref, gpre_ref, w_ref, b_ref,
                   h_ref, za_ref, zb_ref, zc_ref, zd_ref, r_ref):
    gain = gpre_ref[...] * (1.0 + scale_ref[...])
    shift = shift_ref[...]

    def norms(rows):
        r_ref[rows, :] = _inv_rms(x_ref[rows, :])

    def block(rows):
        h_ref[rows, :] = (x_ref[rows, :] * r_ref[rows, :] * gain + shift).astype(BF16)

    _for_row_blocks(x_ref.shape[0], norms)
    _for_row_blocks(x_ref.shape[0], block)
    h = h_ref[...]
    off = 0
    for o_ref in (za_ref, zb_ref, zc_ref, zd_ref):
        n = o_ref.shape[-1]
        o_ref[...] = _dot(h, w_ref[:, off:off + n]) + b_ref[:, off:off + n]
        off += n


def _inproj(x, mod, row_of_tile, gpre, w, b, widths, tm):
    rows, d = x.shape
    n = w.shape[1]

    def mspec(k):
        return pl.BlockSpec((None, None, 1, d), lambda i: (row_of_tile(i), k, 0, 0))

    outs = [jax.ShapeDtypeStruct((rows, d), BF16)] + [jax.ShapeDtypeStruct((rows, wd), F32) for wd in widths]
    ospecs = [pl.BlockSpec((tm, d), lambda i: (i, 0))] + [pl.BlockSpec((tm, wd), lambda i: (i, 0)) for wd in widths]
    return pl.pallas_call(
        _inproj_kernel,
        grid=(rows // tm,),
        in_specs=[
            pl.BlockSpec((tm, d), lambda i: (i, 0)),
            mspec(3), mspec(4),
            pl.BlockSpec((1, d), lambda i: (0, 0)),
            pl.BlockSpec((d, n), lambda i: (0, 0), pipeline_mode=pl.Buffered(1)),
            pl.BlockSpec((1, n), lambda i: (0, 0)),
        ],
        out_specs=ospecs,
        out_shape=outs,
        scratch_shapes=[pltpu.VMEM((tm, 1), F32)],
        compiler_params=_cparams(("parallel",)),
    )(x, mod, mod, gpre, w, b)


def _ab_kernel(za_ref, zb_ref, zprev_ref, znext_ref, lng_ref, lnb_ref, ws_ref, bs_ref,
               dw_ref, db_ref, clng_ref, clnb_ref, ya_ref, yb_ref, ext_ref, shift_ref,
               *, ts, n_lat_tiles, lat_tiles_per_seq, ctx_tiles_per_seq):
    i = pl.program_id(0)
    w = BRANCH_W

    za = za_ref[...]
    g = 0.5 * za * (1.0 + lax.erf(za * (2.0 ** -0.5)))
    vln = _layernorm(g[:, w:], lng_ref[...], lnb_ref[...]).astype(BF16)
    for c in range(ts // CHUNK):
        rows = slice(c * CHUNK, (c + 1) * CHUNK)
        for gi in range(GM_GROUPS):
            cols = slice(gi * CHUNK, (gi + 1) * CHUNK)
            sv = _dot(ws_ref[gi], vln[rows, cols]) + bs_ref[gi]
            ya_ref[rows, cols] = (g[rows, cols] * sv).astype(BF16)

    def glu(z):
        return z[:, :w] * jax.nn.sigmoid(z[:, w:])

    in_lat = i < n_lat_tiles
    pos = jnp.where(in_lat, i % lat_tiles_per_seq, (i - n_lat_tiles) % ctx_tiles_per_seq)
    last = jnp.where(in_lat, lat_tiles_per_seq - 1, ctx_tiles_per_seq - 1)
    ext_ref[0:CONV_HALO, :] = jnp.where(pos != 0, glu(zprev_ref[...]), 0.0)
    ext_ref[CONV_HALO:CONV_HALO + ts, :] = glu(zb_ref[...])
    ext_ref[CONV_HALO + ts:, :] = jnp.where(pos != last, glu(znext_ref[...]), 0.0)
    span = ts + 2 * CONV_HALO - SUBLANES
    for r in range(1, SUBLANES):
        shift_ref[r - 1, 0:span, :] = ext_ref[r:r + span, :]
    first_tap = CONV_HALO - CONV_W // 2
    acc = jnp.zeros((ts, w), F32) + db_ref[...]
    for k in range(CONV_W):
        a, r = divmod(first_tap + k, SUBLANES)
        win = ext_ref[a * SUBLANES:a * SUBLANES + ts, :] if r == 0 else shift_ref[r - 1, a * SUBLANES:a * SUBLANES + ts, :]
        acc = acc + dw_ref[k:k + 1, :] * win
    yb_ref[...] = _silu(_layernorm(acc, clng_ref[...], clnb_ref[...])).astype(BF16)


def _mix_ab(za, zb, lng, lnb, ws, bs, dw, db, clng, clnb, n_lat_rows, seq_lat, seq_ctx, ts):
    rows = za.shape[0]
    w = BRANCH_W
    hb = ts // CONV_HALO
    n_halo = rows // CONV_HALO
    kern = functools.partial(_ab_kernel, ts=ts, n_lat_tiles=n_lat_rows // ts,
                             lat_tiles_per_seq=seq_lat // ts, ctx_tiles_per_seq=seq_ctx // ts)
    vec = pl.BlockSpec((1, w), lambda i: (0, 0))
    return pl.pallas_call(
        kern,
        grid=(rows // ts,),
        in_specs=[
            pl.BlockSpec((ts, 2 * w), lambda i: (i, 0)),
            pl.BlockSpec((ts, 2 * w), lambda i: (i, 0)),
            pl.BlockSpec((CONV_HALO, 2 * w), lambda i: (jnp.maximum(i * hb - 1, 0), 0)),
            pl.BlockSpec((CONV_HALO, 2 * w), lambda i: (jnp.minimum((i + 1) * hb, n_halo - 1), 0)),
            vec, vec,
            pl.BlockSpec((GM_GROUPS, CHUNK, CHUNK), lambda i: (0, 0, 0)),
            pl.BlockSpec((GM_GROUPS, CHUNK, 1), lambda i: (0, 0, 0)),
            pl.BlockSpec((CONV_W, w), lambda i: (0, 0)),
            vec, vec, vec,
        ],
        out_specs=[pl.BlockSpec((ts, w), lambda i: (i, 0)), pl.BlockSpec((ts, w), lambda i: (i, 0))],
        out_shape=[jax.ShapeDtypeStruct((rows, w), BF16), jax.ShapeDtypeStruct((rows, w), BF16)],
        scratch_shapes=[pltpu.VMEM((ts + 2 * CONV_HALO, w), F32),
                        pltpu.VMEM((SUBLANES - 1, ts + 2 * CONV_HALO - SUBLANES, w), F32)],
        compiler_params=_cparams(("parallel",)),
    )(za, zb, zb, zb, lng, lnb, ws, bs, dw, db, clng, clnb)


def _qkv_kernel(zc_ref, zd_ref, cos_ref, sina_ref, sinb_ref, qn_ref, kvn_ref, wuq_ref, wukv_ref,
                qc_ref, kc_ref, vc_ref, qd_ref, kd_ref, vd_ref):
    cos, sina, sinb = cos_ref[...], sina_ref[...], sinb_ref[...]
    low = lax.broadcasted_iota(jnp.int32, cos.shape, 1) < ROT_DIM

    def rope(slab):
        return (slab * cos + pltpu.roll(slab, LANES - ROT_QUARTER, 1) * sina
                + pltpu.roll(slab, ROT_QUARTER, 1) * sinb)

    zc = zc_ref[...]
    nq = MLA_HEADS * MLA_NOPE
    q = _dot((_rms(zc[:, :512]) * qn_ref[...]).astype(BF16), wuq_ref[...]) * MLA_SCALE
    kv = _dot((_rms(zc[:, 512:768]) * kvn_ref[...]).astype(BF16), wukv_ref[...])
    kr = rope(zc[:, 768:896])
    ones = jnp.ones((MLA_SUM_ROWS, zc.shape[0]), BF16)
    for pair in range(MLA_HEADS // 2):
        qr = rope(q[:, nq + pair * LANES:nq + (pair + 1) * LANES]).T.astype(BF16)
        for h in (2 * pair, 2 * pair + 1):
            qc_ref[h, 0:MLA_NOPE, :] = q[:, h * MLA_NOPE:(h + 1) * MLA_NOPE].T.astype(BF16)
            qc_ref[h, MLA_NOPE:, :] = qr
            kc_ref[h, :, 0:MLA_NOPE] = kv[:, h * MLA_NOPE:(h + 1) * MLA_NOPE].astype(BF16)
            keep = low if h % 2 == 0 else jnp.logical_not(low)
            kc_ref[h, :, MLA_NOPE:] = jnp.where(keep, kr, 0.0).astype(BF16)
            vc_ref[h, 0:MLA_V, :] = kv[:, nq + h * MLA_V:nq + (h + 1) * MLA_V].T.astype(BF16)
            vc_ref[h, MLA_V:, :] = ones

    zd = zd_ref[...]
    for pair in range(SWA_HEADS // 2):
        qd_ref[pair] = (rope(zd[:, pair * LANES:(pair + 1) * LANES]) * SWA_SCALE).T.astype(BF16)
    kd_ref[...] = rope(zd[:, 512:640]).astype(BF16)
    vd_ref[...] = zd[:, 640:768].T.astype(BF16)


def _qkv(zc, zd, cos, sina, sinb, qn, kvn, wuq, wukv, tm):
    rows = zc.shape[0]

    def rowspec(wd):
        return pl.BlockSpec((tm, wd), lambda i: (i, 0))

    def full(a):
        return pl.BlockSpec(a.shape, lambda i: (0,) * a.ndim)

    def headspec(nh, wd):
        return pl.BlockSpec((nh, tm, wd), lambda i: (0, i, 0))

    def headspec_t(nh, wd):
        return pl.BlockSpec((nh, wd, tm), lambda i: (0, 0, i))

    kc = jax.ShapeDtypeStruct((MLA_HEADS, rows, MLA_QK), BF16)
    qc = jax.ShapeDtypeStruct((MLA_HEADS, MLA_QK, rows), BF16)
    vc = jax.ShapeDtypeStruct((MLA_HEADS, MLA_V + MLA_SUM_ROWS, rows), BF16)
    qd = jax.ShapeDtypeStruct((SWA_HEADS // 2, LANES, rows), BF16)
    kd = jax.ShapeDtypeStruct((rows, LANES), BF16)
    vd = jax.ShapeDtypeStruct((LANES, rows), BF16)
    return pl.pallas_call(
        _qkv_kernel,
        grid=(rows // tm,),
        in_specs=[rowspec(zc.shape[1]), rowspec(zd.shape[1]), rowspec(LANES), rowspec(LANES), rowspec(LANES),
                  full(qn), full(kvn), full(wuq), full(wukv)],
        out_specs=[headspec_t(MLA_HEADS, MLA_QK), headspec(MLA_HEADS, MLA_QK),
                   headspec_t(MLA_HEADS, MLA_V + MLA_SUM_ROWS), headspec_t(SWA_HEADS // 2, LANES),
                   rowspec(LANES), pl.BlockSpec((LANES, tm), lambda i: (0, i))],
        out_shape=[qc, kc, vc, qd, kd, vd],
        compiler_params=_cparams(("parallel",)),
    )(zc, zd, cos, sina, sinb, qn, kvn, wuq, wukv)


def _mla_kernel(*refs, has_lat, groups):
    if has_lat:
        qt_ref, kl_ref, vlt_ref, kx_ref, vxt_ref, o_ref = refs
    else:
        qt_ref, kx_ref, vxt_ref, o_ref = refs
    cols = qt_ref.shape[1] // groups
    scores = []
    for g in range(groups):
        qt = qt_ref[:, g * cols:(g + 1) * cols]
        scores.append((_dot(kx_ref[...], qt), _dot(kl_ref[...], qt) if has_lat else None))
    for g, (sx, sl) in enumerate(scores):
        m = jnp.max(sx, axis=0, keepdims=True)
        if has_lat:
            m = jnp.maximum(m, jnp.max(sl, axis=0, keepdims=True))
        ot = _dot(vxt_ref[...], jnp.exp2(sx - m).astype(BF16))
        if has_lat:
            ot = ot + _dot(vlt_ref[...], jnp.exp2(sl - m).astype(BF16))
        o_ref[g * cols:(g + 1) * cols, :] = (ot[:MLA_V] / ot[MLA_V:MLA_V + 1]).T.astype(BF16)


def _mla_attention(qc, kc, vc, batch, seq_lat, seq_ctx, lat_queries):
    n_lat_rows = batch * seq_lat
    ctx_blk0 = n_lat_rows // seq_ctx
    tq = math.gcd(MLA_GROUPS * MLA_GROUP_Q, seq_lat if lat_queries else seq_ctx)
    if lat_queries:
        nq, q0 = seq_lat // tq, 0
    else:
        nq, q0 = seq_ctx // tq, n_lat_rows // tq
    vrows = vc.shape[1]
    qspec = pl.BlockSpec((None, MLA_QK, tq), lambda b, h, i: (h, 0, q0 + b * nq + i))
    kx = pl.BlockSpec((None, seq_ctx, MLA_QK), lambda b, h, i: (h, ctx_blk0 + b, 0))
    vx = pl.BlockSpec((None, vrows, seq_ctx), lambda b, h, i: (h, 0, ctx_blk0 + b))
    if lat_queries:
        kl = pl.BlockSpec((None, seq_lat, MLA_QK), lambda b, h, i: (h, b, 0))
        vl = pl.BlockSpec((None, vrows, seq_lat), lambda b, h, i: (h, 0, b))
        in_specs, args = [qspec, kl, vl, kx, vx], (qc, kc, vc, kc, vc)
    else:
        in_specs, args = [qspec, kx, vx], (qc, kc, vc)
    return pl.pallas_call(
        functools.partial(_mla_kernel, has_lat=lat_queries, groups=max(1, tq // MLA_GROUP_Q)),
        grid=(batch, MLA_HEADS, nq),
        in_specs=in_specs,
        out_specs=pl.BlockSpec((tq, MLA_V), lambda b, h, i: (b * nq + i, h)),
        out_shape=jax.ShapeDtypeStruct((batch * nq * tq, MLA_HEADS * MLA_V), BF16),
        compiler_params=_cparams(("parallel", "parallel", "parallel")),
    )(*args)


def _swa_kernel(*refs, has_lat, seq_lat):
    if has_lat:
        qt_ref, kx_ref, vxt_ref, sink_ref = refs[:4]
        kw_refs, vw_refs, o_ref = refs[4:4 + SWA_WINDOW_BLOCKS], refs[4 + SWA_WINDOW_BLOCKS:-1], refs[-1]
    else:
        qt_ref, kx_ref, vxt_ref, sink_ref, o_ref = refs
    blocks = qt_ref.shape[2] // SWA_BLOCK
    half = LANES // 2
    ncol = blocks * SWA_GROUP * SWA_BLOCK
    zeros = jnp.zeros((half, ncol), BF16)
    if has_lat:
        k_win = jnp.concatenate([r[...] for r in kw_refs], axis=0)
        v_win = jnp.concatenate([r[...] for r in vw_refs], axis=1)
        first = (pl.program_id(1) * blocks - 1) * SWA_BLOCK
        shape = (k_win.shape[0], ncol)
        col = lax.broadcasted_iota(jnp.int32, shape, 1)
        qpos = first + SWA_BLOCK + (col // (SWA_GROUP * SWA_BLOCK)) * SWA_BLOCK + (col & (SWA_BLOCK - 1))
        kpos = first + lax.broadcasted_iota(jnp.int32, shape, 0)
        valid = (jnp.abs(kpos - qpos) <= WINDOW) & (kpos >= 0) & (kpos < seq_lat)
    for kv in range(SWA_KV):
        rows = slice(kv * half, (kv + 1) * half)
        parts = []
        for blk in range(blocks):
            for g in range(SWA_GROUP):
                h = kv * SWA_GROUP + g
                parts.append(qt_ref[h // 2, (h % 2) * half:(h % 2 + 1) * half, blk * SWA_BLOCK:(blk + 1) * SWA_BLOCK])
        qt = jnp.concatenate(parts, axis=1)
        qt = jnp.concatenate([qt, zeros] if kv == 0 else [zeros, qt], axis=0)
        sink = sink_ref[kv] * LOG2_E
        sx = _dot(kx_ref[...], qt)
        m = jnp.maximum(sink, jnp.max(sx, axis=0, keepdims=True))
        if has_lat:
            sl = jnp.where(valid, _dot(k_win, qt), -jnp.inf)
            m = jnp.maximum(m, jnp.max(sl, axis=0, keepdims=True))
        ones = jnp.ones((SWA_SUM_ROWS, 1), BF16)
        vx = vxt_ref[rows, :]
        ot = _dot(jnp.concatenate([vx, jnp.broadcast_to(ones, (SWA_SUM_ROWS, vx.shape[1]))], axis=0),
                  jnp.exp2(sx - m).astype(BF16))
        if has_lat:
            vl = v_win[rows, :]
            ot = ot + _dot(jnp.concatenate([vl, jnp.broadcast_to(ones, (SWA_SUM_ROWS, vl.shape[1]))], axis=0),
                           jnp.exp2(sl - m).astype(BF16))
        o = ot[:half] / (ot[half:half + 1] + jnp.exp2(sink - m))
        for blk in range(blocks):
            for pair in range(SWA_GROUP // 2):
                c0 = (blk * SWA_GROUP + 2 * pair) * SWA_BLOCK
                slab_t = jnp.concatenate([o[:, c0:c0 + SWA_BLOCK], o[:, c0 + SWA_BLOCK:c0 + 2 * SWA_BLOCK]], axis=0)
                lane0 = (kv * (SWA_GROUP // 2) + pair) * LANES
                o_ref[blk * SWA_BLOCK:(blk + 1) * SWA_BLOCK, lane0:lane0 + LANES] = slab_t.T.astype(BF16)


def _swa_attention(qd, kd, vd, sink, batch, seq_lat, seq_ctx, lat_queries):
    n_lat_rows = batch * seq_lat
    ctx_blk0 = n_lat_rows // seq_ctx
    seq_q = seq_lat if lat_queries else seq_ctx
    blocks = 2 if seq_q % (2 * SWA_BLOCK) == 0 and n_lat_rows % (2 * SWA_BLOCK) == 0 else 1
    tq = blocks * SWA_BLOCK
    nb = seq_q // tq
    q0 = 0 if lat_queries else n_lat_rows // tq
    kblocks = seq_lat // SWA_BLOCK
    sink_row = jnp.tile(jnp.repeat(sink.reshape(SWA_KV, SWA_GROUP), SWA_BLOCK, axis=1), (1, blocks))[:, None, :]
    in_specs = [pl.BlockSpec((SWA_HEADS // 2, LANES, tq), lambda b, i: (0, 0, q0 + b * nb + i)),
                pl.BlockSpec((seq_ctx, LANES), lambda b, i: (ctx_blk0 + b, 0)),
                pl.BlockSpec((LANES, seq_ctx), lambda b, i: (0, ctx_blk0 + b)),
                pl.BlockSpec(sink_row.shape, lambda b, i: (0, 0, 0))]
    args = [qd, kd, vd, sink_row]
    if lat_queries:
        assert blocks + 2 == SWA_WINDOW_BLOCKS

        def win(j):
            return lambda b, i: b * kblocks + jnp.clip(i * blocks - 1 + j, 0, kblocks - 1)

        in_specs += [pl.BlockSpec((SWA_BLOCK, LANES), lambda b, i, j=j: (win(j)(b, i), 0))
                     for j in range(SWA_WINDOW_BLOCKS)]
        in_specs += [pl.BlockSpec((LANES, SWA_BLOCK), lambda b, i, j=j: (0, win(j)(b, i)))
                     for j in range(SWA_WINDOW_BLOCKS)]
        args += [kd] * SWA_WINDOW_BLOCKS + [vd] * SWA_WINDOW_BLOCKS
    return pl.pallas_call(
        functools.partial(_swa_kernel, has_lat=lat_queries, seq_lat=seq_lat),
        grid=(batch, nb),
        in_specs=in_specs,
        out_specs=pl.BlockSpec((tq, BRANCH_W), lambda b, i: (b * nb + i, 0)),
        out_shape=jax.ShapeDtypeStruct((batch * nb * tq, BRANCH_W), BF16),
        compiler_params=_cparams(("parallel", "parallel")),
    )(*args)


def _gate_kernel(h_ref, ya_ref, yb_ref, yc_ref, yd_ref, wg0, wg1, wg2, wg3, bg_ref,
                 wb0, wb1, wb2, wb3, o_ref):
    h = h_ref[...]
    acc = None
    for b, (y_ref, wg_ref, wb_ref) in enumerate(zip((ya_ref, yb_ref, yc_ref, yd_ref),
                                                    (wg0, wg1, wg2, wg3), (wb0, wb1, wb2, wb3))):
        gate = jax.nn.sigmoid(_dot(h, wg_ref[...]) + bg_ref[b])
        term = gate * _dot(y_ref[...], wb_ref[...])
        acc = term if acc is None else acc + term
    o_ref[...] = acc.astype(BF16)


def _gated_merge(h, ys, wg, bg, wbr, rows, tm, tn):
    d = h.shape[1]
    ncol = d // tn
    wg_specs = [pl.BlockSpec((d, tn), lambda j, i, b=b: (0, b * ncol + j)) for b in range(4)]
    wb_specs = [pl.BlockSpec((None, BRANCH_W, tn), lambda j, i, b=b: (b, 0, j)) for b in range(4)]
    yspec = pl.BlockSpec((tm, BRANCH_W), lambda j, i: (i, 0))
    return pl.pallas_call(
        _gate_kernel,
        grid=(ncol, rows // tm),
        in_specs=[pl.BlockSpec((tm, d), lambda j, i: (i, 0)), yspec, yspec, yspec, yspec,
                  *wg_specs, pl.BlockSpec((4, 1, tn), lambda j, i: (0, 0, j)), *wb_specs],
        out_specs=pl.BlockSpec((tm, tn), lambda j, i: (i, j)),
        out_shape=jax.ShapeDtypeStruct((rows, d), BF16),
        compiler_params=_cparams(("parallel", "parallel")),
    )(h, *ys, wg, wg, wg, wg, bg, wbr, wbr, wbr, wbr)


def _outproj_kernel(a_ref, x_ref, gate_ref, gpost_ref, w_ref, o_ref, r_ref):
    o_ref[...] = _dot(a_ref[...], w_ref[...])
    gain = gate_ref[...] * gpost_ref[...]

    def norms(rows):
        r_ref[rows, :] = _inv_rms(o_ref[rows, :])

    def block(rows):
        o_ref[rows, :] = x_ref[rows, :] + o_ref[rows, :] * r_ref[rows, :] * gain

    _for_row_blocks(x_ref.shape[0], norms)
    _for_row_blocks(x_ref.shape[0], block)


def _outproj(a, x, mod, row_of_tile, gpost, w, tm):
    rows, d = a.shape
    return pl.pallas_call(
        _outproj_kernel,
        grid=(rows // tm,),
        in_specs=[
            pl.BlockSpec((tm, d), lambda i: (i, 0)),
            pl.BlockSpec((tm, d), lambda i: (i, 0)),
            pl.BlockSpec((None, None, 1, d), lambda i: (row_of_tile(i), 5, 0, 0)),
            pl.BlockSpec((1, d), lambda i: (0, 0)),
            pl.BlockSpec((d, d), lambda i: (0, 0), pipeline_mode=pl.Buffered(1)),
        ],
        out_specs=pl.BlockSpec((tm, d), lambda i: (i, 0)),
        out_shape=jax.ShapeDtypeStruct((rows, d), F32),
        scratch_shapes=[pltpu.VMEM((tm, 1), F32)],
        compiler_params=_cparams(("parallel",)),
    )(a, x, mod, gpost, w)


def _rope_tables(seq_lat, n_ctx_rows, batch):
    half = ROT_DIM // 2
    freqs = 1.0 / (ROPE_BASE ** (jnp.arange(0, half, 2, dtype=F32) / half))
    t = jnp.arange(seq_lat)
    ang_r = (t // GRID_W).astype(F32)[:, None] * freqs[None, :]
    ang_c = (t % GRID_W).astype(F32)[:, None] * freqs[None, :]
    ang = jnp.concatenate([ang_r, ang_r, ang_c, ang_c], axis=-1)
    ang = jnp.tile(ang, (batch, LANES // ROT_DIM))
    cos, sin = jnp.cos(ang), jnp.sin(ang)
    first = (jnp.arange(LANES) % half) < ROT_QUARTER
    sina = jnp.where(first, -sin, 0.0)
    sinb = jnp.where(first, 0.0, sin)
    ones = jnp.ones((n_ctx_rows, LANES), F32)
    zeros = jnp.zeros((n_ctx_rows, LANES), F32)
    return (jnp.concatenate([cos, ones]), jnp.concatenate([sina, zeros]), jnp.concatenate([sinb, zeros]))


def _pack_layer(l, w_in, b_in, gm_ws, gm_bs, mla_w_uq, mla_w_ukv, swa_sink, w_branch, w_out):
    d = w_in.shape[1]
    p = {}
    wi, bi = w_in[l], b_in[l]
    c_hi = 2048 + 832
    kr = slice(c_hi - ROT_DIM, c_hi)
    p['w_abcd'] = jnp.concatenate([wi[:, :c_hi], wi[:, kr], wi[:, c_hi:c_hi + 768]], axis=1).astype(BF16)
    p['b_abcd'] = jnp.concatenate([bi[:c_hi], bi[kr], bi[c_hi:c_hi + 768]])[None, :]
    p['w_gates'] = wi[:, c_hi + 768:].astype(BF16)
    p['b_gates'] = bi[c_hi + 768:].reshape(4, 1, d)
    p['gm_ws'] = gm_ws[l].astype(BF16)
    p['gm_bs'] = gm_bs[l][:, :, None]
    uq = mla_w_uq[l].reshape(-1, MLA_HEADS, MLA_NOPE + ROT_DIM)
    p['wuq'] = jnp.concatenate([uq[:, :, :MLA_NOPE].reshape(-1, MLA_HEADS * MLA_NOPE),
                                uq[:, :, MLA_NOPE:].reshape(-1, MLA_HEADS * ROT_DIM)], axis=1).astype(BF16)
    ukv = mla_w_ukv[l].reshape(-1, MLA_HEADS, MLA_NOPE + MLA_V)
    p['wukv'] = jnp.concatenate([ukv[:, :, :MLA_NOPE].reshape(-1, MLA_HEADS * MLA_NOPE),
                                 ukv[:, :, MLA_NOPE:].reshape(-1, MLA_HEADS * MLA_V)], axis=1).astype(BF16)
    p['sink'] = swa_sink[l]
    p['w_branch'] = w_branch[l].astype(BF16)
    p['w_out'] = w_out[l].astype(BF16)
    return p


def kernel(x, c, ctx, c_ctx, w_mod, b_mod, norm_pre, norm_post, w_ff_gate, w_ff_up, w_ff_down, w_in, b_in, gm_ln_g, gm_ln_b, gm_ws, gm_bs, cv_dw, cv_db, cv_ln_g, cv_ln_b, mla_q_norm, mla_kv_norm, mla_w_uq, mla_w_ukv, swa_sink, w_branch, w_out):
    batch, seq_lat, d = x.shape
    seq_ctx = ctx.shape[1]
    depth = w_mod.shape[0]
    n_lat = batch * seq_lat
    n_ctx = batch * seq_ctx
    assert batch + 1 <= COND_ROWS and seq_lat % GRID_W == 0
    tm = math.gcd(512, math.gcd(seq_lat, n_ctx))
    ts = math.gcd(256, math.gcd(seq_lat, seq_ctx))
    assert tm % SWA_BLOCK == 0 and ts % CHUNK == 0 and seq_ctx % SWA_BLOCK == 0 and seq_lat >= 3 * SWA_BLOCK

    tm_ffn = math.gcd(1024, math.gcd(seq_lat, n_ctx))

    def cond_row(tile):
        return lambda i: jnp.where(i < n_lat // tile, i // (seq_lat // tile), batch)

    row_of_tile, row_of_ffn_tile = cond_row(tm), cond_row(tm_ffn)

    rows_all = jnp.concatenate([x.reshape(n_lat, d), ctx.reshape(n_ctx, d)], axis=0)
    cond = jnp.zeros((COND_ROWS, d), F32).at[:batch].set(c).at[batch].set(c_ctx)
    mod = _modulation(cond, w_mod, b_mod).reshape(depth, COND_ROWS, N_MOD, 1, d)
    cos, sina, sinb = _rope_tables(seq_lat, n_ctx, batch)

    d_ff = w_ff_gate.shape[-1]
    wg_all = _cast_pad_cols(w_ff_gate.reshape(-1, d_ff), D_FF_PAD).reshape(depth, 2, d, D_FF_PAD)
    wu_all = _cast_pad_cols(w_ff_up.reshape(-1, d_ff), D_FF_PAD).reshape(depth, 2, d, D_FF_PAD)
    wd_all = _cast_pad_rows(w_ff_down.reshape(-1, d_ff, d), D_FF_PAD).reshape(depth, 2, D_FF_PAD, d)

    xs = rows_all
    for l in range(depth):
        last = l == depth - 1
        p = _pack_layer(l, w_in, b_in, gm_ws, gm_bs, mla_w_uq, mla_w_ukv, swa_sink, w_branch, w_out)
        m = mod[l]
        npre, npost = norm_pre[l][:, None, :], norm_post[l][:, None, :]

        xs = _ffn(xs, xs.shape[0], m, 0, row_of_ffn_tile, npre[0], npost[0], wg_all, wu_all, wd_all, l, 0, tm_ffn)

        h, za, zb, zc, zd = _inproj(xs, m, row_of_tile, npre[1], p['w_abcd'], p['b_abcd'],
                                    (1024, 1024, 896, 768), tm)
        ya, yb = _mix_ab(za, zb, gm_ln_g[l][None], gm_ln_b[l][None], p['gm_ws'], p['gm_bs'],
                         cv_dw[l], cv_db[l][None], cv_ln_g[l][None], cv_ln_b[l][None],
                         n_lat, seq_lat, seq_ctx, ts)
        qc, kc, vc, qd, kd, vd = _qkv(zc, zd, cos, sina, sinb, mla_q_norm[l][None], mla_kv_norm[l][None],
                                      p['wuq'], p['wukv'], tm)
        yc = _mla_attention(qc, kc, vc, batch, seq_lat, seq_ctx, True)
        yd = _swa_attention(qd, kd, vd, p['sink'], batch, seq_lat, seq_ctx, True)
        rows_mix = n_lat
        if not last:
            yc = jnp.concatenate([yc, _mla_attention(qc, kc, vc, batch, seq_lat, seq_ctx, False)], axis=0)
            yd = jnp.concatenate([yd, _swa_attention(qd, kd, vd, p['sink'], batch, seq_lat, seq_ctx, False)], axis=0)
            rows_mix = n_lat + n_ctx
        merged = _gated_merge(h, (ya, yb, yc, yd), p['w_gates'], p['b_gates'], p['w_branch'], rows_mix, tm, 512)
        xs = _outproj(merged, xs, m, row_of_tile, npost[1], p['w_out'], tm)

        xs = _ffn(xs, rows_mix, m, 2, row_of_ffn_tile, npre[2], npost[2], wg_all, wu_all, wd_all, l, 1, tm_ffn)
    return xs[:n_lat].reshape(batch, seq_lat, d)
```

```python
import functools
import math

import jax
import jax.numpy as jnp
from jax import lax
from jax.experimental import pallas as pl
from jax.experimental.pallas import tpu as pltpu

F32 = jnp.float32
BF16 = jnp.bfloat16

EPS = 1e-6
ROPE_BASE = 10000.0
GRID_W = 64
N_MOD = 9
COND_ROWS = 8
LANES = 128
SUBLANES = 8
ROT_DIM = 64
ROT_QUARTER = ROT_DIM // 4
CONV_W = 31
CONV_HALO = 16
CHUNK = 128
GM_GROUPS = 4
BRANCH_W = 512
MLA_HEADS = 4
MLA_NOPE = 128
MLA_V = 128
MLA_QK = 256
MLA_SUM_ROWS = 16
MLA_GROUP_Q = 512
MLA_GROUPS = 2
LOG2_E = math.log2(math.e)
MLA_SCALE = (MLA_NOPE + ROT_DIM) ** -0.5 * LOG2_E
SWA_HEADS = 8
SWA_KV = 2
SWA_GROUP = SWA_HEADS // SWA_KV
SWA_BLOCK = 128
SWA_WINDOW_BLOCKS = 4
SWA_SUM_ROWS = 16
WINDOW = 128
SWA_SCALE = ROT_DIM ** -0.5 * LOG2_E
D_FF_PAD = 5632
FF_CHUNK = 512
ROW_BLOCK = 256
VMEM_LIMIT = 56 * 1024 * 1024
VMEM_LIMIT_FFN = 62 * 1024 * 1024


def _cparams(sem, vmem_limit=VMEM_LIMIT):
    return pltpu.CompilerParams(dimension_semantics=sem, vmem_limit_bytes=vmem_limit)


def _inv_rms(x):
    return lax.rsqrt(jnp.mean(x * x, axis=-1, keepdims=True) + EPS)


def _rms(x):
    return x * _inv_rms(x)


def _layernorm(x, g, b):
    mu = jnp.mean(x, axis=-1, keepdims=True)
    xc = x - mu
    var = jnp.mean(xc * xc, axis=-1, keepdims=True)
    return xc * lax.rsqrt(var + EPS) * g + b


def _silu(x):
    return x * jax.nn.sigmoid(x)


def _written_into(kernel_fn, in_specs, args, into):
    if into is None:
        return kernel_fn, in_specs, args, {}

    def kern(dst_ref, *refs):
        del dst_ref
        kernel_fn(*refs)

    return kern, [pl.BlockSpec(memory_space=pl.ANY)] + list(in_specs), [into] + list(args), {0: 0}


def _for_row_blocks(n_rows, body):
    def step(r, carry):
        body(pl.ds(pl.multiple_of(r * ROW_BLOCK, ROW_BLOCK), ROW_BLOCK))
        return carry

    lax.fori_loop(0, n_rows // ROW_BLOCK, step, 0)


def _dot(a, b):
    return jnp.dot(a, b, preferred_element_type=F32)


def _dot_nt(a, b):
    return lax.dot_general(a, b, (((1,), (1,)), ((), ())), preferred_element_type=F32)


def _cast_pad_cols_kernel(w_ref, o_ref):
    n = w_ref.shape[1]
    o_ref[:, :n] = w_ref[...].astype(BF16)
    o_ref[:, n:] = jnp.zeros((o_ref.shape[0], o_ref.shape[1] - n), BF16)


def _cast_pad_cols(w, n_pad, tr=256):
    m, n = w.shape
    return pl.pallas_call(
        _cast_pad_cols_kernel,
        grid=(m // tr,),
        in_specs=[pl.BlockSpec((tr, n), lambda i: (i, 0))],
        out_specs=pl.BlockSpec((tr, n_pad), lambda i: (i, 0)),
        out_shape=jax.ShapeDtypeStruct((m, n_pad), BF16),
        compiler_params=_cparams(("parallel",)),
    )(w)


def _cast_pad_rows_kernel(w_ref, o_ref, *, rows_valid):
    tr = w_ref.shape[0]
    row = pl.program_id(1) * tr + lax.broadcasted_iota(jnp.int32, w_ref.shape, 0)
    o_ref[...] = jnp.where(row < rows_valid, w_ref[...], 0.0).astype(BF16)


def _cast_pad_rows(w, m_pad, tr=512):
    g, m, n = w.shape
    return pl.pallas_call(
        functools.partial(_cast_pad_rows_kernel, rows_valid=m),
        grid=(g, m_pad // tr),
        in_specs=[pl.BlockSpec((None, tr, n), lambda i, j: (i, j, 0))],
        out_specs=pl.BlockSpec((None, tr, n), lambda i, j: (i, j, 0)),
        out_shape=jax.ShapeDtypeStruct((g, m_pad, n), BF16),
        compiler_params=_cparams(("parallel", "parallel")),
    )(w)


def _mod_kernel(cond_ref, w_ref, b_ref, o_ref):
    s = _silu(cond_ref[...]).astype(BF16)
    o_ref[...] = _dot(s, w_ref[...].astype(BF16)) + b_ref[...]


def _modulation(cond, w_mod, b_mod, tn=1024):
    n_layers, d, n = w_mod.shape
    return pl.pallas_call(
        _mod_kernel,
        grid=(n_layers, n // tn),
        in_specs=[
            pl.BlockSpec((COND_ROWS, d), lambda l, j: (0, 0)),
            pl.BlockSpec((None, d, tn), lambda l, j: (l, 0, j)),
            pl.BlockSpec((None, 1, tn), lambda l, j: (l, 0, j)),
        ],
        out_specs=pl.BlockSpec((None, COND_ROWS, tn), lambda l, j: (l, 0, j)),
        out_shape=jax.ShapeDtypeStruct((n_layers, COND_ROWS, n), F32),
        compiler_params=_cparams(("parallel", "parallel")),
    )(cond, w_mod, b_mod.reshape(n_layers, 1, n))


def _ffn_kernel(x_ref, shift_ref, scale_ref, gate_ref, gpre_ref, gpost_ref,
                wg_ref, wu_ref, wd_ref, o_ref, h_ref, r_ref):
    f = pl.program_id(1)

    @pl.when(f == 0)
    def _():
        gain = gpre_ref[...] * (1.0 + scale_ref[...])
        shift = shift_ref[...]

        def norms(rows):
            r_ref[rows, :] = _inv_rms(x_ref[rows, :])

        def block(rows):
            h_ref[rows, :] = (x_ref[rows, :] * r_ref[rows, :] * gain + shift).astype(BF16)
            o_ref[rows, :] = jnp.zeros((ROW_BLOCK, o_ref.shape[1]), F32)

        _for_row_blocks(x_ref.shape[0], norms)
        _for_row_blocks(x_ref.shape[0], block)

    h = h_ref[...]
    g = _dot(h, wg_ref[...])
    u = _dot(h, wu_ref[...])
    a = (_silu(g) * u).astype(BF16)
    o_ref[...] += _dot(a, wd_ref[...])

    @pl.when(f == pl.num_programs(1) - 1)
    def _():
        gain = 0.5 * gate_ref[...] * gpost_ref[...]

        def norms(rows):
            r_ref[rows, :] = _inv_rms(o_ref[rows, :])

        def block(rows):
            o_ref[rows, :] = x_ref[rows, :] + o_ref[rows, :] * r_ref[rows, :] * gain

        _for_row_blocks(x_ref.shape[0], norms)
        _for_row_blocks(x_ref.shape[0], block)


def _ffn(x, rows, mod, j, row_of_tile, gpre, gpost, wg, wu, wd, l, s, tm, out_rows=None, out_tile0=0, into=None):
    d = x.shape[1]
    f_pad = wg.shape[-1]
    tf = FF_CHUNK
    out_rows = rows if out_rows is None else out_rows

    def mspec(k):
        return pl.BlockSpec((None, None, 1, d), lambda i, f: (row_of_tile(i), k, 0, 0))

    vec = pl.BlockSpec((1, d), lambda i, f: (0, 0))
    in_specs = [
        pl.BlockSpec((tm, d), lambda i, f: (i, 0)),
        mspec(3 * j), mspec(3 * j + 1), mspec(3 * j + 2),
        vec, vec,
        pl.BlockSpec((None, None, d, tf), lambda i, f: (l, s, 0, f)),
        pl.BlockSpec((None, None, d, tf), lambda i, f: (l, s, 0, f)),
        pl.BlockSpec((None, None, tf, d), lambda i, f: (l, s, f, 0)),
    ]
    args = [x, mod, mod, mod, gpre, gpost, wg, wu, wd]
    kern, in_specs, args, aliases = _written_into(_ffn_kernel, in_specs, args, into)
    return pl.pallas_call(
        kern,
        grid=(rows // tm, f_pad // tf),
        in_specs=in_specs,
        out_specs=pl.BlockSpec((tm, d), lambda i, f: (out_tile0 + i, 0)),
        out_shape=jax.ShapeDtypeStruct((out_rows, d), F32),
        scratch_shapes=[pltpu.VMEM((tm, d), BF16), pltpu.VMEM((tm, 1), F32)],
        input_output_aliases=aliases,
        compiler_params=_cparams(("parallel", "arbitrary"), VMEM_LIMIT_FFN),
    )(*args)


def _inproj_kernel(x_ref, shift_ref, scale_ref, gpre_ref, w_ref, b_ref,
                   h_ref, za_ref, zb_ref, zc_ref, zd_ref):
    y = _rms(x_ref[...]) * gpre_ref[...]
    h = (y * (1.0 + scale_ref[...]) + shift_ref[...]).astype(BF16)
    h_ref[...] = h
    off = 0
    for o_ref in (za_ref, zb_ref, zc_ref, zd_ref):
        n = o_ref.shape[-1]
        o_ref[...] = _dot(h, w_ref[:, off:off + n]) + b_ref[:, off:off + n]
        off += n


def _inproj(x, mod, row_of_tile, gpre, w, b, widths, tm):
    rows, d = x.shape
    n = w.shape[1]

    def mspec(k):
        return pl.BlockSpec((None, None, 1, d), lambda i: (row_of_tile(i), k, 0, 0))

    outs = [jax.ShapeDtypeStruct((rows, d), BF16)] + [jax.ShapeDtypeStruct((rows, wd), F32) for wd in widths]
    ospecs = [pl.BlockSpec((tm, d), lambda i: (i, 0))] + [pl.BlockSpec((tm, wd), lambda i: (i, 0)) for wd in widths]
    return pl.pallas_call(
        _inproj_kernel,
        grid=(rows // tm,),
        in_specs=[
            pl.BlockSpec((tm, d), lambda i: (i, 0)),
            mspec(3), mspec(4),
            pl.BlockSpec((1, d), lambda i: (0, 0)),
            pl.BlockSpec((d, n), lambda i: (0, 0), pipeline_mode=pl.Buffered(1)),
            pl.BlockSpec((1, n), lambda i: (0, 0)),
        ],
        out_specs=ospecs,
        out_shape=outs,
        compiler_params=_cparams(("parallel",)),
    )(x, mod, mod, gpre, w, b)


def _ab_kernel(za_ref, zb_ref, zprev_ref, znext_ref, lng_ref, lnb_ref, ws_ref, bs_ref,
               dw_ref, db_ref, clng_ref, clnb_ref, ya_ref, yb_ref, ext_ref, shift_ref,
               *, ts, n_lat_tiles, lat_tiles_per_seq, ctx_tiles_per_seq):
    i = pl.program_id(0)
    w = BRANCH_W

    za = za_ref[...]
    g = 0.5 * za * (1.0 + lax.erf(za * (2.0 ** -0.5)))
    vln = _layernorm(g[:, w:], lng_ref[...], lnb_ref[...]).astype(BF16)
    for c in range(ts // CHUNK):
        rows = slice(c * CHUNK, (c + 1) * CHUNK)
        for gi in range(GM_GROUPS):
            cols = slice(gi * CHUNK, (gi + 1) * CHUNK)
            sv = _dot(ws_ref[gi], vln[rows, cols]) + bs_ref[gi]
            ya_ref[rows, cols] = (g[rows, cols] * sv).astype(BF16)

    def glu(z):
        return z[:, :w] * jax.nn.sigmoid(z[:, w:])

    in_lat = i < n_lat_tiles
    pos = jnp.where(in_lat, i % lat_tiles_per_seq, (i - n_lat_tiles) % ctx_tiles_per_seq)
    last = jnp.where(in_lat, lat_tiles_per_seq - 1, ctx_tiles_per_seq - 1)
    ext_ref[0:CONV_HALO, :] = jnp.where(pos != 0, glu(zprev_ref[...]), 0.0)
    ext_ref[CONV_HALO:CONV_HALO + ts, :] = glu(zb_ref[...])
    ext_ref[CONV_HALO + ts:, :] = jnp.where(pos != last, glu(znext_ref[...]), 0.0)
    span = ts + 2 * CONV_HALO - SUBLANES
    for r in range(1, SUBLANES):
        shift_ref[r - 1, 0:span, :] = ext_ref[r:r + span, :]
    first_tap = CONV_HALO - CONV_W // 2
    acc = jnp.zeros((ts, w), F32) + db_ref[...]
    for k in range(CONV_W):
        a, r = divmod(first_tap + k, SUBLANES)
        win = ext_ref[a * SUBLANES:a * SUBLANES + ts, :] if r == 0 else shift_ref[r - 1, a * SUBLANES:a * SUBLANES + ts, :]
        acc = acc + dw_ref[k:k + 1, :] * win
    yb_ref[...] = _silu(_layernorm(acc, clng_ref[...], clnb_ref[...])).astype(BF16)


def _mix_ab(za, zb, lng, lnb, ws, bs, dw, db, clng, clnb, n_lat_rows, seq_lat, seq_ctx, ts):
    rows = za.shape[0]
    w = BRANCH_W
    hb = ts // CONV_HALO
    n_halo = rows // CONV_HALO
    kern = functools.partial(_ab_kernel, ts=ts, n_lat_tiles=n_lat_rows // ts,
                             lat_tiles_per_seq=seq_lat // ts, ctx_tiles_per_seq=seq_ctx // ts)
    vec = pl.BlockSpec((1, w), lambda i: (0, 0))
    return pl.pallas_call(
        kern,
        grid=(rows // ts,),
        in_specs=[
            pl.BlockSpec((ts, 2 * w), lambda i: (i, 0)),
            pl.BlockSpec((ts, 2 * w), lambda i: (i, 0)),
            pl.BlockSpec((CONV_HALO, 2 * w), lambda i: (jnp.maximum(i * hb - 1, 0), 0)),
            pl.BlockSpec((CONV_HALO, 2 * w), lambda i: (jnp.minimum((i + 1) * hb, n_halo - 1), 0)),
            vec, vec,
            pl.BlockSpec((GM_GROUPS, CHUNK, CHUNK), lambda i: (0, 0, 0)),
            pl.BlockSpec((GM_GROUPS, CHUNK, 1), lambda i: (0, 0, 0)),
            pl.BlockSpec((CONV_W, w), lambda i: (0, 0)),
            vec, vec, vec,
        ],
        out_specs=[pl.BlockSpec((ts, w), lambda i: (i, 0)), pl.BlockSpec((ts, w), lambda i: (i, 0))],
        out_shape=[jax.ShapeDtypeStruct((rows, w), BF16), jax.ShapeDtypeStruct((rows, w), BF16)],
        scratch_shapes=[pltpu.VMEM((ts + 2 * CONV_HALO, w), F32),
                        pltpu.VMEM((SUBLANES - 1, ts + 2 * CONV_HALO - SUBLANES, w), F32)],
        compiler_params=_cparams(("parallel",)),
    )(za, zb, zb, zb, lng, lnb, ws, bs, dw, db, clng, clnb)


def _qkv_kernel(zc_ref, zd_ref, cos_ref, sina_ref, sinb_ref, qn_ref, kvn_ref, wuq_ref, wukv_ref,
                qc_ref, kc_ref, vc_ref, qd_ref, kd_ref, vd_ref):
    cos, sina, sinb = cos_ref[...], sina_ref[...], sinb_ref[...]
    low = lax.broadcasted_iota(jnp.int32, cos.shape, 1) < ROT_DIM

    def rope(slab):
        return (slab * cos + pltpu.roll(slab, LANES - ROT_QUARTER, 1) * sina
                + pltpu.roll(slab, ROT_QUARTER, 1) * sinb)

    zc = zc_ref[...]
    nq = MLA_HEADS * MLA_NOPE
    q = _dot((_rms(zc[:, :512]) * qn_ref[...]).astype(BF16), wuq_ref[...]) * MLA_SCALE
    kv = _dot((_rms(zc[:, 512:768]) * kvn_ref[...]).astype(BF16), wukv_ref[...])
    kr = rope(zc[:, 768:896])
    ones = jnp.ones((MLA_SUM_ROWS, zc.shape[0]), BF16)
    for pair in range(MLA_HEADS // 2):
        qr = rope(q[:, nq + pair * LANES:nq + (pair + 1) * LANES]).T.astype(BF16)
        for h in (2 * pair, 2 * pair + 1):
            qc_ref[h, 0:MLA_NOPE, :] = q[:, h * MLA_NOPE:(h + 1) * MLA_NOPE].T.astype(BF16)
            qc_ref[h, MLA_NOPE:, :] = qr
            kc_ref[h, :, 0:MLA_NOPE] = kv[:, h * MLA_NOPE:(h + 1) * MLA_NOPE].astype(BF16)
            keep = low if h % 2 == 0 else jnp.logical_not(low)
            kc_ref[h, :, MLA_NOPE:] = jnp.where(keep, kr, 0.0).astype(BF16)
            vc_ref[h, 0:MLA_V, :] = kv[:, nq + h * MLA_V:nq + (h + 1) * MLA_V].T.astype(BF16)
            vc_ref[h, MLA_V:, :] = ones

    zd = zd_ref[...]
    for pair in range(SWA_HEADS // 2):
        qd_ref[pair] = (rope(zd[:, pair * LANES:(pair + 1) * LANES]) * SWA_SCALE).T.astype(BF16)
    kd_ref[...] = rope(zd[:, 512:640]).astype(BF16)
    vd_ref[...] = zd[:, 640:768].T.astype(BF16)


def _qkv(zc, zd, cos, sina, sinb, qn, kvn, wuq, wukv, tm):
    rows = zc.shape[0]

    def rowspec(wd):
        return pl.BlockSpec((tm, wd), lambda i: (i, 0))

    def full(a):
        return pl.BlockSpec(a.shape, lambda i: (0,) * a.ndim)

    def headspec(nh, wd):
        return pl.BlockSpec((nh, tm, wd), lambda i: (0, i, 0))

    def headspec_t(nh, wd):
        return pl.BlockSpec((nh, wd, tm), lambda i: (0, 0, i))

    kc = jax.ShapeDtypeStruct((MLA_HEADS, rows, MLA_QK), BF16)
    qc = jax.ShapeDtypeStruct((MLA_HEADS, MLA_QK, rows), BF16)
    vc = jax.ShapeDtypeStruct((MLA_HEADS, MLA_V + MLA_SUM_ROWS, rows), BF16)
    qd = jax.ShapeDtypeStruct((SWA_HEADS // 2, LANES, rows), BF16)
    kd = jax.ShapeDtypeStruct((rows, LANES), BF16)
    vd = jax.ShapeDtypeStruct((LANES, rows), BF16)
    return pl.pallas_call(
        _qkv_kernel,
        grid=(rows // tm,),
        in_specs=[rowspec(zc.shape[1]), rowspec(zd.shape[1]), rowspec(LANES), rowspec(LANES), rowspec(LANES),
                  full(qn), full(kvn), full(wuq), full(wukv)],
        out_specs=[headspec_t(MLA_HEADS, MLA_QK), headspec(MLA_HEADS, MLA_QK),
                   headspec_t(MLA_HEADS, MLA_V + MLA_SUM_ROWS), headspec_t(SWA_HEADS // 2, LANES),
                   rowspec(LANES), pl.BlockSpec((LANES, tm), lambda i: (0, i))],
        out_shape=[qc, kc, vc, qd, kd, vd],
        compiler_params=_cparams(("parallel",)),
    )(zc, zd, cos, sina, sinb, qn, kvn, wuq, wukv)


def _mla_kernel(*refs, has_lat, groups):
    if has_lat:
        qt_ref, kl_ref, vlt_ref, kx_ref, vxt_ref, o_ref = refs
    else:
        qt_ref, kx_ref, vxt_ref, o_ref = refs
    cols = qt_ref.shape[1] // groups
    scores = []
    for g in range(groups):
        qt = qt_ref[:, g * cols:(g + 1) * cols]
        scores.append((_dot(kx_ref[...], qt), _dot(kl_ref[...], qt) if has_lat else None))
    for g, (sx, sl) in enumerate(scores):
        m = jnp.max(sx, axis=0, keepdims=True)
        if has_lat:
            m = jnp.maximum(m, jnp.max(sl, axis=0, keepdims=True))
        ot = _dot(vxt_ref[...], jnp.exp2(sx - m).astype(BF16))
        if has_lat:
            ot = ot + _dot(vlt_ref[...], jnp.exp2(sl - m).astype(BF16))
        o_ref[g * cols:(g + 1) * cols, :] = (ot[:MLA_V] / ot[MLA_V:MLA_V + 1]).T.astype(BF16)


def _mla_attention(qc, kc, vc, batch, seq_lat, seq_ctx, lat_queries, out_rows, into=None):
    n_lat_rows = batch * seq_lat
    ctx_blk0 = n_lat_rows // seq_ctx
    tq = math.gcd(MLA_GROUPS * MLA_GROUP_Q, seq_lat if lat_queries else seq_ctx)
    if lat_queries:
        nq, q0 = seq_lat // tq, 0
    else:
        nq, q0 = seq_ctx // tq, n_lat_rows // tq
    vrows = vc.shape[1]
    qspec = pl.BlockSpec((None, MLA_QK, tq), lambda b, h, i: (h, 0, q0 + b * nq + i))
    kx = pl.BlockSpec((None, seq_ctx, MLA_QK), lambda b, h, i: (h, ctx_blk0 + b, 0))
    vx = pl.BlockSpec((None, vrows, seq_ctx), lambda b, h, i: (h, 0, ctx_blk0 + b))
    if lat_queries:
        kl = pl.BlockSpec((None, seq_lat, MLA_QK), lambda b, h, i: (h, b, 0))
        vl = pl.BlockSpec((None, vrows, seq_lat), lambda b, h, i: (h, 0, b))
        in_specs, args = [qspec, kl, vl, kx, vx], (qc, kc, vc, kc, vc)
    else:
        in_specs, args = [qspec, kx, vx], (qc, kc, vc)
    kern = functools.partial(_mla_kernel, has_lat=lat_queries, groups=max(1, tq // MLA_GROUP_Q))
    kern, in_specs, args, aliases = _written_into(kern, in_specs, args, into)
    return pl.pallas_call(
        kern,
        grid=(batch, MLA_HEADS, nq),
        in_specs=in_specs,
        out_specs=pl.BlockSpec((tq, MLA_V), lambda b, h, i: (q0 + b * nq + i, h)),
        out_shape=jax.ShapeDtypeStruct((out_rows, MLA_HEADS * MLA_V), BF16),
        input_output_aliases=aliases,
        compiler_params=_cparams(("parallel", "parallel", "parallel")),
    )(*args)


def _swa_kernel(*refs, has_lat):
    if has_lat:
        qt_ref, kx_ref, vxt_ref, sink_ref, bias_ref = refs[:5]
        kw_refs, vw_refs, o_ref = refs[5:5 + SWA_WINDOW_BLOCKS], refs[5 + SWA_WINDOW_BLOCKS:-1], refs[-1]
    else:
        qt_ref, kx_ref, vxt_ref, sink_ref, o_ref = refs
    blocks = qt_ref.shape[2] // SWA_BLOCK
    half = LANES // 2
    ncol = blocks * SWA_GROUP * SWA_BLOCK
    zeros = jnp.zeros((half, ncol), BF16)
    if has_lat:
        k_win = jnp.concatenate([r[...] for r in kw_refs], axis=0)
        v_win = jnp.concatenate([r[...] for r in vw_refs], axis=1)
    for kv in range(SWA_KV):
        rows = slice(kv * half, (kv + 1) * half)
        parts = []
        for blk in range(blocks):
            for g in range(SWA_GROUP):
                h = kv * SWA_GROUP + g
                parts.append(qt_ref[h // 2, (h % 2) * half:(h % 2 + 1) * half, blk * SWA_BLOCK:(blk + 1) * SWA_BLOCK])
        qt = jnp.concatenate(parts, axis=1)
        qt = jnp.concatenate([qt, zeros] if kv == 0 else [zeros, qt], axis=0)
        sink = sink_ref[kv] * LOG2_E
        sx = _dot(kx_ref[...], qt)
        m = jnp.maximum(sink, jnp.max(sx, axis=0, keepdims=True))
        if has_lat:
            sl = _dot(k_win, qt) + bias_ref[...]
            m = jnp.maximum(m, jnp.max(sl, axis=0, keepdims=True))
        ones = jnp.ones((SWA_SUM_ROWS, 1), BF16)
        vx = vxt_ref[rows, :]
        ot = _dot(jnp.concatenate([vx, jnp.broadcast_to(ones, (SWA_SUM_ROWS, vx.shape[1]))], axis=0),
                  jnp.exp2(sx - m).astype(BF16))
        if has_lat:
            vl = v_win[rows, :]
            ot = ot + _dot(jnp.concatenate([vl, jnp.broadcast_to(ones, (SWA_SUM_ROWS, vl.shape[1]))], axis=0),
                           jnp.exp2(sl - m).astype(BF16))
        o = ot[:half] / (ot[half:half + 1] + jnp.exp2(sink - m))
        for blk in range(blocks):
            for pair in range(SWA_GROUP // 2):
                c0 = (blk * SWA_GROUP + 2 * pair) * SWA_BLOCK
                slab_t = jnp.concatenate([o[:, c0:c0 + SWA_BLOCK], o[:, c0 + SWA_BLOCK:c0 + 2 * SWA_BLOCK]], axis=0)
                lane0 = (kv * (SWA_GROUP // 2) + pair) * LANES
                o_ref[blk * SWA_BLOCK:(blk + 1) * SWA_BLOCK, lane0:lane0 + LANES] = slab_t.T.astype(BF16)


def _swa_window_bias(blocks):
    nkeys = SWA_WINDOW_BLOCKS * SWA_BLOCK
    ncol = blocks * SWA_GROUP * SWA_BLOCK
    key = jnp.arange(nkeys)[:, None] - SWA_BLOCK
    col = jnp.arange(ncol)[None, :]
    qry = (col // (SWA_GROUP * SWA_BLOCK)) * SWA_BLOCK + (col % SWA_BLOCK)
    inside = jnp.abs(key - qry) <= WINDOW
    after_start = key >= 0
    before_end = key < blocks * SWA_BLOCK
    variants = [inside, inside & after_start, inside & before_end, inside & after_start & before_end]
    return jnp.where(jnp.stack(variants), 0.0, -jnp.inf).astype(F32)


def _swa_attention(qd, kd, vd, sink, batch, seq_lat, seq_ctx, lat_queries, out_rows, into=None):
    n_lat_rows = batch * seq_lat
    ctx_blk0 = n_lat_rows // seq_ctx
    seq_q = seq_lat if lat_queries else seq_ctx
    blocks = 2 if seq_q % (2 * SWA_BLOCK) == 0 and n_lat_rows % (2 * SWA_BLOCK) == 0 else 1
    tq = blocks * SWA_BLOCK
    nb = seq_q // tq
    q0 = 0 if lat_queries else n_lat_rows // tq
    kblocks = seq_lat // SWA_BLOCK
    sink_row = jnp.tile(jnp.repeat(sink.reshape(SWA_KV, SWA_GROUP), SWA_BLOCK, axis=1), (1, blocks))[:, None, :]
    in_specs = [pl.BlockSpec((SWA_HEADS // 2, LANES, tq), lambda b, i: (0, 0, q0 + b * nb + i)),
                pl.BlockSpec((seq_ctx, LANES), lambda b, i: (ctx_blk0 + b, 0)),
                pl.BlockSpec((LANES, seq_ctx), lambda b, i: (0, ctx_blk0 + b)),
                pl.BlockSpec(sink_row.shape, lambda b, i: (0, 0, 0))]
    args = [qd, kd, vd, sink_row]
    if lat_queries:
        assert blocks + 2 == SWA_WINDOW_BLOCKS
        bias = _swa_window_bias(blocks)
        in_specs.append(pl.BlockSpec((None,) + bias.shape[1:],
                                     lambda b, i: ((i == 0).astype(jnp.int32) + 2 * (i == nb - 1).astype(jnp.int32), 0, 0)))
        args.append(bias)

        def win(j):
            return lambda b, i: b * kblocks + jnp.clip(i * blocks - 1 + j, 0, kblocks - 1)

        in_specs += [pl.BlockSpec((SWA_BLOCK, LANES), lambda b, i, j=j: (win(j)(b, i), 0))
                     for j in range(SWA_WINDOW_BLOCKS)]
        in_specs += [pl.BlockSpec((LANES, SWA_BLOCK), lambda b, i, j=j: (0, win(j)(b, i)))
                     for j in range(SWA_WINDOW_BLOCKS)]
        args += [kd] * SWA_WINDOW_BLOCKS + [vd] * SWA_WINDOW_BLOCKS
    kern, in_specs, args, aliases = _written_into(functools.partial(_swa_kernel, has_lat=lat_queries),
                                                  in_specs, args, into)
    return pl.pallas_call(
        kern,
        grid=(batch, nb),
        in_specs=in_specs,
        out_specs=pl.BlockSpec((tq, BRANCH_W), lambda b, i: (q0 + b * nb + i, 0)),
        out_shape=jax.ShapeDtypeStruct((out_rows, BRANCH_W), BF16),
        input_output_aliases=aliases,
        compiler_params=_cparams(("parallel", "parallel")),
    )(*args)


def _gate_kernel(h_ref, ya_ref, yb_ref, yc_ref, yd_ref, wg0, wg1, wg2, wg3, bg_ref,
                 wb0, wb1, wb2, wb3, o_ref):
    h = h_ref[...]
    acc = None
    for b, (y_ref, wg_ref, wb_ref) in enumerate(zip((ya_ref, yb_ref, yc_ref, yd_ref),
                                                    (wg0, wg1, wg2, wg3), (wb0, wb1, wb2, wb3))):
        gate = jax.nn.sigmoid(_dot(h, wg_ref[...]) + bg_ref[b])
        term = gate * _dot(y_ref[...], wb_ref[...])
        acc = term if acc is None else acc + term
    o_ref[...] = acc.astype(BF16)


def _gated_merge(h, ys, wg, bg, wbr, rows, tm, tn):
    d = h.shape[1]
    ncol = d // tn
    wg_specs = [pl.BlockSpec((d, tn), lambda j, i, b=b: (0, b * ncol + j)) for b in range(4)]
    wb_specs = [pl.BlockSpec((None, BRANCH_W, tn), lambda j, i, b=b: (b, 0, j)) for b in range(4)]
    yspec = pl.BlockSpec((tm, BRANCH_W), lambda j, i: (i, 0))
    return pl.pallas_call(
        _gate_kernel,
        grid=(ncol, rows // tm),
        in_specs=[pl.BlockSpec((tm, d), lambda j, i: (i, 0)), yspec, yspec, yspec, yspec,
                  *wg_specs, pl.BlockSpec((4, 1, tn), lambda j, i: (0, 0, j)), *wb_specs],
        out_specs=pl.BlockSpec((tm, tn), lambda j, i: (i, j)),
        out_shape=jax.ShapeDtypeStruct((rows, d), BF16),
        compiler_params=_cparams(("parallel", "parallel")),
    )(h, *ys, wg, wg, wg, wg, bg, wbr, wbr, wbr, wbr)


def _outproj_kernel(a_ref, x_ref, gate_ref, gpost_ref, w_ref, o_ref):
    y = _dot(a_ref[...], w_ref[...])
    o_ref[...] = x_ref[...] + gate_ref[...] * (_rms(y) * gpost_ref[...])


def _outproj(a, x, mod, row_of_tile, gpost, w, tm):
    rows, d = a.shape
    return pl.pallas_call(
        _outproj_kernel,
        grid=(rows // tm,),
        in_specs=[
            pl.BlockSpec((tm, d), lambda i: (i, 0)),
            pl.BlockSpec((tm, d), lambda i: (i, 0)),
            pl.BlockSpec((None, None, 1, d), lambda i: (row_of_tile(i), 5, 0, 0)),
            pl.BlockSpec((1, d), lambda i: (0, 0)),
            pl.BlockSpec((d, d), lambda i: (0, 0), pipeline_mode=pl.Buffered(1)),
        ],
        out_specs=pl.BlockSpec((tm, d), lambda i: (i, 0)),
        out_shape=jax.ShapeDtypeStruct((rows, d), F32),
        compiler_params=_cparams(("parallel",)),
    )(a, x, mod, gpost, w)


def _rope_tables(seq_lat, n_ctx_rows, batch):
    half = ROT_DIM // 2
    freqs = 1.0 / (ROPE_BASE ** (jnp.arange(0, half, 2, dtype=F32) / half))
    t = jnp.arange(seq_lat)
    ang_r = (t // GRID_W).astype(F32)[:, None] * freqs[None, :]
    ang_c = (t % GRID_W).astype(F32)[:, None] * freqs[None, :]
    ang = jnp.concatenate([ang_r, ang_r, ang_c, ang_c], axis=-1)
    ang = jnp.tile(ang, (batch, LANES // ROT_DIM))
    cos, sin = jnp.cos(ang), jnp.sin(ang)
    first = (jnp.arange(LANES) % half) < ROT_QUARTER
    sina = jnp.where(first, -sin, 0.0)
    sinb = jnp.where(first, 0.0, sin)
    ones = jnp.ones((n_ctx_rows, LANES), F32)
    zeros = jnp.zeros((n_ctx_rows, LANES), F32)
    return (jnp.concatenate([cos, ones]), jnp.concatenate([sina, zeros]), jnp.concatenate([sinb, zeros]))


def _pack_layer(l, w_in, b_in, gm_ws, gm_bs, mla_w_uq, mla_w_ukv, swa_sink, w_branch, w_out):
    d = w_in.shape[1]
    p = {}
    wi, bi = w_in[l], b_in[l]
    c_hi = 2048 + 832
    kr = slice(c_hi - ROT_DIM, c_hi)
    p['w_abcd'] = jnp.concatenate([wi[:, :c_hi], wi[:, kr], wi[:, c_hi:c_hi + 768]], axis=1).astype(BF16)
    p['b_abcd'] = jnp.concatenate([bi[:c_hi], bi[kr], bi[c_hi:c_hi + 768]])[None, :]
    p['w_gates'] = wi[:, c_hi + 768:].astype(BF16)
    p['b_gates'] = bi[c_hi + 768:].reshape(4, 1, d)
    p['gm_ws'] = gm_ws[l].astype(BF16)
    p['gm_bs'] = gm_bs[l][:, :, None]
    uq = mla_w_uq[l].reshape(-1, MLA_HEADS, MLA_NOPE + ROT_DIM)
    p['wuq'] = jnp.concatenate([uq[:, :, :MLA_NOPE].reshape(-1, MLA_HEADS * MLA_NOPE),
                                uq[:, :, MLA_NOPE:].reshape(-1, MLA_HEADS * ROT_DIM)], axis=1).astype(BF16)
    ukv = mla_w_ukv[l].reshape(-1, MLA_HEADS, MLA_NOPE + MLA_V)
    p['wukv'] = jnp.concatenate([ukv[:, :, :MLA_NOPE].reshape(-1, MLA_HEADS * MLA_NOPE),
                                 ukv[:, :, MLA_NOPE:].reshape(-1, MLA_HEADS * MLA_V)], axis=1).astype(BF16)
    p['sink'] = swa_sink[l]
    p['w_branch'] = w_branch[l].astype(BF16)
    p['w_out'] = w_out[l].astype(BF16)
    return p


def kernel(x, c, ctx, c_ctx, w_mod, b_mod, norm_pre, norm_post, w_ff_gate, w_ff_up, w_ff_down, w_in, b_in, gm_ln_g, gm_ln_b, gm_ws, gm_bs, cv_dw, cv_db, cv_ln_g, cv_ln_b, mla_q_norm, mla_kv_norm, mla_w_uq, mla_w_ukv, swa_sink, w_branch, w_out):
    batch, seq_lat, d = x.shape
    seq_ctx = ctx.shape[1]
    depth = w_mod.shape[0]
    n_lat = batch * seq_lat
    n_ctx = batch * seq_ctx
    assert batch + 1 <= COND_ROWS and seq_lat % GRID_W == 0
    tm = math.gcd(512, math.gcd(seq_lat, n_ctx))
    ts = math.gcd(256, math.gcd(seq_lat, seq_ctx))
    assert tm % SWA_BLOCK == 0 and ts % CHUNK == 0 and seq_ctx % SWA_BLOCK == 0 and seq_lat >= 3 * SWA_BLOCK

    tm_ffn = math.gcd(1024, math.gcd(seq_lat, n_ctx))

    def cond_row(tile):
        return lambda i: jnp.where(i < n_lat // tile, i // (seq_lat // tile), batch)

    row_of_tile, row_of_ffn_tile = cond_row(tm), cond_row(tm_ffn)

    cond = jnp.zeros((COND_ROWS, d), F32).at[:batch].set(c).at[batch].set(c_ctx)
    mod = _modulation(cond, w_mod, b_mod).reshape(depth, COND_ROWS, N_MOD, 1, d)
    cos, sina, sinb = _rope_tables(seq_lat, n_ctx, batch)

    d_ff = w_ff_gate.shape[-1]
    wg_all = _cast_pad_cols(w_ff_gate.reshape(-1, d_ff), D_FF_PAD).reshape(depth, 2, d, D_FF_PAD)
    wu_all = _cast_pad_cols(w_ff_up.reshape(-1, d_ff), D_FF_PAD).reshape(depth, 2, d, D_FF_PAD)
    wd_all = _cast_pad_rows(w_ff_down.reshape(-1, d_ff, d), D_FF_PAD).reshape(depth, 2, D_FF_PAD, d)

    n_all = n_lat + n_ctx
    xs = None
    for l in range(depth):
        last = l == depth - 1
        p = _pack_layer(l, w_in, b_in, gm_ws, gm_bs, mla_w_uq, mla_w_ukv, swa_sink, w_branch, w_out)
        m = mod[l]
        npre, npost = norm_pre[l][:, None, :], norm_post[l][:, None, :]

        ffn1 = (m, 0, row_of_ffn_tile, npre[0], npost[0], wg_all, wu_all, wd_all, l, 0, tm_ffn)
        if xs is None:
            xs = _ffn(x.reshape(n_lat, d), n_lat, *ffn1, out_rows=n_all)
            ctx_ffn1 = (m, 0, lambda i: batch) + ffn1[3:]
            xs = _ffn(ctx.reshape(n_ctx, d), n_ctx, *ctx_ffn1, out_rows=n_all, out_tile0=n_lat // tm_ffn, into=xs)
        else:
            xs = _ffn(xs, n_all, *ffn1)

        h, za, zb, zc, zd = _inproj(xs, m, row_of_tile, npre[1], p['w_abcd'], p['b_abcd'],
                                    (1024, 1024, 896, 768), tm)
        ya, yb = _mix_ab(za, zb, gm_ln_g[l][None], gm_ln_b[l][None], p['gm_ws'], p['gm_bs'],
                         cv_dw[l], cv_db[l][None], cv_ln_g[l][None], cv_ln_b[l][None],
                         n_lat, seq_lat, seq_ctx, ts)
        qc, kc, vc, qd, kd, vd = _qkv(zc, zd, cos, sina, sinb, mla_q_norm[l][None], mla_kv_norm[l][None],
                                      p['wuq'], p['wukv'], tm)
        rows_mix = n_lat if last else n_all
        attn = (batch, seq_lat, seq_ctx)
        yc = _mla_attention(qc, kc, vc, *attn, True, rows_mix)
        yd = _swa_attention(qd, kd, vd, p['sink'], *attn, True, rows_mix)
        if not last:
            yc = _mla_attention(qc, kc, vc, *attn, False, rows_mix, into=yc)
            yd = _swa_attention(qd, kd, vd, p['sink'], *attn, False, rows_mix, into=yd)
        merged = _gated_merge(h, (ya, yb, yc, yd), p['w_gates'], p['b_gates'], p['w_branch'], rows_mix, tm_ffn, 512)
        xs = _outproj(merged, xs, m, row_of_tile, npost[1], p['w_out'], tm)

        xs = _ffn(xs, rows_mix, m, 2, row_of_ffn_tile, npre[2], npost[2], wg_all, wu_all, wd_all, l, 1, tm_ffn)
    return xs[:n_lat].reshape(batch, seq_lat, d)
```

```python
import functools
import math

import jax
import jax.numpy as jnp
from jax import lax
from jax.experimental import pallas as pl
from jax.experimental.pallas import tpu as pltpu

F32 = jnp.float32
BF16 = jnp.bfloat16

EPS = 1e-6
ROPE_BASE = 10000.0
GRID_W = 64
N_MOD = 9
COND_ROWS = 8
LANES = 128
SUBLANES = 8
ROT_DIM = 64
ROT_QUARTER = ROT_DIM // 4
CONV_W = 31
CONV_HALO = 16
CHUNK = 128
GM_GROUPS = 4
BRANCH_W = 512
MLA_HEADS = 4
MLA_NOPE = 128
MLA_V = 128
MLA_QK = 256
MLA_SUM_ROWS = 16
MLA_GROUP_Q = 512
MLA_GROUPS = 4
LOG2_E = math.log2(math.e)
MLA_SCALE = (MLA_NOPE + ROT_DIM) ** -0.5 * LOG2_E
SWA_HEADS = 8
SWA_KV = 2
SWA_GROUP = SWA_HEADS // SWA_KV
SWA_BLOCK = 128
SWA_WINDOW_BLOCKS = 4
SWA_SUM_ROWS = 16
WINDOW = 128
SWA_SCALE = ROT_DIM ** -0.5 * LOG2_E
IN_WIDTHS = (1024, 1024, 896, 768)
D_FF_PAD = 5632
FF_CHUNK = 512
ROW_BLOCK = 256
VMEM_LIMIT = 56 * 1024 * 1024
VMEM_LIMIT_FFN = 62 * 1024 * 1024


def _cparams(sem, vmem_limit=VMEM_LIMIT):
    return pltpu.CompilerParams(dimension_semantics=sem, vmem_limit_bytes=vmem_limit)


def _inv_rms(x):
    return lax.rsqrt(jnp.mean(x * x, axis=-1, keepdims=True) + EPS)


def _rms(x):
    return x * _inv_rms(x)


def _layernorm(x, g, b):
    mu = jnp.mean(x, axis=-1, keepdims=True)
    xc = x - mu
    var = jnp.mean(xc * xc, axis=-1, keepdims=True)
    return xc * lax.rsqrt(var + EPS) * g + b


def _silu(x):
    return x * jax.nn.sigmoid(x)


def _written_into(kernel_fn, in_specs, args, into):
    if into is None:
        return kernel_fn, in_specs, args, {}

    def kern(dst_ref, *refs):
        del dst_ref
        kernel_fn(*refs)

    return kern, [pl.BlockSpec(memory_space=pl.ANY)] + list(in_specs), [into] + list(args), {0: 0}


def _for_row_blocks(n_rows, body):
    def step(r, carry):
        body(pl.ds(pl.multiple_of(r * ROW_BLOCK, ROW_BLOCK), ROW_BLOCK))
        return carry

    lax.fori_loop(0, n_rows // ROW_BLOCK, step, 0)


def _dot(a, b):
    return jnp.dot(a, b, preferred_element_type=F32)


def _dot_nt(a, b):
    return lax.dot_general(a, b, (((1,), (1,)), ((), ())), preferred_element_type=F32)


def _cast_pad_cols_kernel(w_ref, o_ref):
    n = w_ref.shape[1]
    o_ref[:, :n] = w_ref[...].astype(BF16)
    o_ref[:, n:] = jnp.zeros((o_ref.shape[0], o_ref.shape[1] - n), BF16)


def _cast_pad_cols(w, n_pad, tr=256):
    m, n = w.shape
    return pl.pallas_call(
        _cast_pad_cols_kernel,
        grid=(m // tr,),
        in_specs=[pl.BlockSpec((tr, n), lambda i: (i, 0))],
        out_specs=pl.BlockSpec((tr, n_pad), lambda i: (i, 0)),
        out_shape=jax.ShapeDtypeStruct((m, n_pad), BF16),
        compiler_params=_cparams(("parallel",)),
    )(w)


def _cast_pad_rows_kernel(w_ref, o_ref, *, rows_valid):
    tr = w_ref.shape[0]
    row = pl.program_id(1) * tr + lax.broadcasted_iota(jnp.int32, w_ref.shape, 0)
    o_ref[...] = jnp.where(row < rows_valid, w_ref[...], 0.0).astype(BF16)


def _cast_pad_rows(w, m_pad, tr=512):
    g, m, n = w.shape
    return pl.pallas_call(
        functools.partial(_cast_pad_rows_kernel, rows_valid=m),
        grid=(g, m_pad // tr),
        in_specs=[pl.BlockSpec((None, tr, n), lambda i, j: (i, j, 0))],
        out_specs=pl.BlockSpec((None, tr, n), lambda i, j: (i, j, 0)),
        out_shape=jax.ShapeDtypeStruct((g, m_pad, n), BF16),
        compiler_params=_cparams(("parallel", "parallel")),
    )(w)


def _mod_kernel(cond_ref, w_ref, b_ref, o_ref):
    k = pl.program_id(1)

    @pl.when(k == 0)
    def _():
        o_ref[...] = jnp.broadcast_to(b_ref[...], o_ref.shape)

    s = _silu(cond_ref[...]).astype(BF16)
    o_ref[...] += _dot(s, w_ref[...].astype(BF16))


def _modulation(cond, w_mod, b_mod, tk=128):
    n_layers, d, n = w_mod.shape
    return pl.pallas_call(
        _mod_kernel,
        grid=(n_layers, d // tk),
        in_specs=[
            pl.BlockSpec((COND_ROWS, tk), lambda l, k: (0, k)),
            pl.BlockSpec((None, tk, n), lambda l, k: (l, k, 0)),
            pl.BlockSpec((None, 1, n), lambda l, k: (l, 0, 0)),
        ],
        out_specs=pl.BlockSpec((None, COND_ROWS, n), lambda l, k: (l, 0, 0)),
        out_shape=jax.ShapeDtypeStruct((n_layers, COND_ROWS, n), F32),
        compiler_params=_cparams(("parallel", "arbitrary")),
    )(cond, w_mod, b_mod.reshape(n_layers, 1, n))


def _ffn_kernel(x_ref, shift_ref, scale_ref, gate_ref, gpre_ref, gpost_ref,
                wg_ref, wu_ref, wd_ref, o_ref, h_ref, r_ref):
    f = pl.program_id(1)

    @pl.when(f == 0)
    def _():
        gain = gpre_ref[...] * (1.0 + scale_ref[...])
        shift = shift_ref[...]

        def norms(rows):
            r_ref[rows, :] = _inv_rms(x_ref[rows, :])

        def block(rows):
            h_ref[rows, :] = (x_ref[rows, :] * r_ref[rows, :] * gain + shift).astype(BF16)
            o_ref[rows, :] = jnp.zeros((ROW_BLOCK, o_ref.shape[1]), F32)

        _for_row_blocks(x_ref.shape[0], norms)
        _for_row_blocks(x_ref.shape[0], block)

    h = h_ref[...]
    g = _dot(h, wg_ref[...])
    u = _dot(h, wu_ref[...])
    a = (_silu(g) * u).astype(BF16)
    o_ref[...] += _dot(a, wd_ref[...])

    @pl.when(f == pl.num_programs(1) - 1)
    def _():
        gain = 0.5 * gate_ref[...] * gpost_ref[...]

        def norms(rows):
            r_ref[rows, :] = _inv_rms(o_ref[rows, :])

        def block(rows):
            o_ref[rows, :] = x_ref[rows, :] + o_ref[rows, :] * r_ref[rows, :] * gain

        _for_row_blocks(x_ref.shape[0], norms)
        _for_row_blocks(x_ref.shape[0], block)


def _ffn(x, rows, mod, j, row_of_tile, gpre, gpost, wg, wu, wd, l, s, tm, out_rows=None, out_tile0=0, into=None):
    d = x.shape[1]
    f_pad = wg.shape[-1]
    tf = FF_CHUNK
    out_rows = rows if out_rows is None else out_rows

    def mspec(k):
        return pl.BlockSpec((None, None, 1, d), lambda i, f: (row_of_tile(i), k, 0, 0))

    vec = pl.BlockSpec((1, d), lambda i, f: (0, 0))
    in_specs = [
        pl.BlockSpec((tm, d), lambda i, f: (i, 0)),
        mspec(3 * j), mspec(3 * j + 1), mspec(3 * j + 2),
        vec, vec,
        pl.BlockSpec((None, None, d, tf), lambda i, f: (l, s, 0, f)),
        pl.BlockSpec((None, None, d, tf), lambda i, f: (l, s, 0, f)),
        pl.BlockSpec((None, None, tf, d), lambda i, f: (l, s, f, 0)),
    ]
    args = [x, mod, mod, mod, gpre, gpost, wg, wu, wd]
    kern, in_specs, args, aliases = _written_into(_ffn_kernel, in_specs, args, into)
    return pl.pallas_call(
        kern,
        grid=(rows // tm, f_pad // tf),
        in_specs=in_specs,
        out_specs=pl.BlockSpec((tm, d), lambda i, f: (out_tile0 + i, 0)),
        out_shape=jax.ShapeDtypeStruct((out_rows, d), F32),
        scratch_shapes=[pltpu.VMEM((tm, d), BF16), pltpu.VMEM((tm, 1), F32)],
        input_output_aliases=aliases,
        compiler_params=_cparams(("parallel", "arbitrary"), VMEM_LIMIT_FFN),
    )(*args)


def _inproj_kernel(x_ref, shift_ref, scale_ref, gpre_ref, w_ref, b_ref,
                   cos_ref, sina_ref, sinb_ref, qn_ref, kvn_ref, wuq_ref, wukv_ref,
                   h_ref, za_ref, zb_ref, qc_ref, kc_ref, vc_ref, qd_ref, kd_ref, vd_ref):
    y = _rms(x_ref[...]) * gpre_ref[...]
    h = (y * (1.0 + scale_ref[...]) + shift_ref[...]).astype(BF16)
    h_ref[...] = h

    def proj(lo, hi):
        return _dot(h, w_ref[:, lo:hi]) + b_ref[:, lo:hi]

    na, nb, nc, nd = IN_WIDTHS
    _qkv_prep(proj(na + nb, na + nb + nc), proj(na + nb + nc, na + nb + nc + nd),
              cos_ref, sina_ref, sinb_ref, qn_ref, kvn_ref, wuq_ref, wukv_ref,
              qc_ref, kc_ref, vc_ref, qd_ref, kd_ref, vd_ref)
    za_ref[...] = proj(0, na)
    zb_ref[...] = proj(na, na + nb)


def _inproj(x, mod, row_of_tile, gpre, w, b, cos, sina, sinb, qn, kvn, wuq, wukv, tm):
    rows, d = x.shape
    n = w.shape[1]

    def mspec(k):
        return pl.BlockSpec((None, None, 1, d), lambda i: (row_of_tile(i), k, 0, 0))

    def rowspec(wd):
        return pl.BlockSpec((tm, wd), lambda i: (i, 0))

    def full(a):
        return pl.BlockSpec(a.shape, lambda i: (0,) * a.ndim)

    def headspec(nh, wd):
        return pl.BlockSpec((nh, tm, wd), lambda i: (0, i, 0))

    def headspec_t(nh, wd):
        return pl.BlockSpec((nh, wd, tm), lambda i: (0, 0, i))

    kc = jax.ShapeDtypeStruct((MLA_HEADS, rows, MLA_QK), BF16)
    qc = jax.ShapeDtypeStruct((MLA_HEADS, MLA_QK, rows), BF16)
    vc = jax.ShapeDtypeStruct((MLA_HEADS, MLA_V + MLA_SUM_ROWS, rows), BF16)
    qd = jax.ShapeDtypeStruct((SWA_HEADS // 2, LANES, rows), BF16)
    kd = jax.ShapeDtypeStruct((rows, LANES), BF16)
    vd = jax.ShapeDtypeStruct((LANES, rows), BF16)
    outs = [jax.ShapeDtypeStruct((rows, d), BF16)] + [jax.ShapeDtypeStruct((rows, wd), F32) for wd in IN_WIDTHS[:2]]
    ospecs = [rowspec(d), rowspec(IN_WIDTHS[0]), rowspec(IN_WIDTHS[1])]
    return pl.pallas_call(
        _inproj_kernel,
        grid=(rows // tm,),
        in_specs=[
            rowspec(d),
            mspec(3), mspec(4),
            pl.BlockSpec((1, d), lambda i: (0, 0)),
            pl.BlockSpec((d, n), lambda i: (0, 0), pipeline_mode=pl.Buffered(1)),
            pl.BlockSpec((1, n), lambda i: (0, 0)),
            rowspec(LANES), rowspec(LANES), rowspec(LANES), full(qn), full(kvn), full(wuq), full(wukv),
        ],
        out_specs=ospecs + [headspec_t(MLA_HEADS, MLA_QK), headspec(MLA_HEADS, MLA_QK),
                            headspec_t(MLA_HEADS, MLA_V + MLA_SUM_ROWS), headspec_t(SWA_HEADS // 2, LANES),
                            rowspec(LANES), pl.BlockSpec((LANES, tm), lambda i: (0, i))],
        out_shape=outs + [qc, kc, vc, qd, kd, vd],
        compiler_params=_cparams(("parallel",)),
    )(x, mod, mod, gpre, w, b, cos, sina, sinb, qn, kvn, wuq, wukv)


def _ab_kernel(za_ref, zb_ref, zprev_ref, znext_ref, lng_ref, lnb_ref, ws_ref, bs_ref,
               dw_ref, db_ref, clng_ref, clnb_ref, ya_ref, yb_ref, ext_ref, shift_ref,
               *, ts, n_lat_tiles, lat_tiles_per_seq, ctx_tiles_per_seq):
    i = pl.program_id(0)
    w = BRANCH_W

    za = za_ref[...]
    g = 0.5 * za * (1.0 + lax.erf(za * (2.0 ** -0.5)))
    vln = _layernorm(g[:, w:], lng_ref[...], lnb_ref[...]).astype(BF16)
    for c in range(ts // CHUNK):
        rows = slice(c * CHUNK, (c + 1) * CHUNK)
        for gi in range(GM_GROUPS):
            cols = slice(gi * CHUNK, (gi + 1) * CHUNK)
            sv = _dot(ws_ref[gi], vln[rows, cols]) + bs_ref[gi]
            ya_ref[rows, cols] = (g[rows, cols] * sv).astype(BF16)

    def glu(z):
        return z[:, :w] * jax.nn.sigmoid(z[:, w:])

    in_lat = i < n_lat_tiles
    pos = jnp.where(in_lat, i % lat_tiles_per_seq, (i - n_lat_tiles) % ctx_tiles_per_seq)
    last = jnp.where(in_lat, lat_tiles_per_seq - 1, ctx_tiles_per_seq - 1)
    ext_ref[0:CONV_HALO, :] = jnp.where(pos != 0, glu(zprev_ref[...]), 0.0)
    ext_ref[CONV_HALO:CONV_HALO + ts, :] = glu(zb_ref[...])
    ext_ref[CONV_HALO + ts:, :] = jnp.where(pos != last, glu(znext_ref[...]), 0.0)
    span = ts + 2 * CONV_HALO - SUBLANES
    for r in range(1, SUBLANES):
        shift_ref[r - 1, 0:span, :] = ext_ref[r:r + span, :]
    first_tap = CONV_HALO - CONV_W // 2
    acc = jnp.zeros((ts, w), F32) + db_ref[...]
    for k in range(CONV_W):
        a, r = divmod(first_tap + k, SUBLANES)
        win = ext_ref[a * SUBLANES:a * SUBLANES + ts, :] if r == 0 else shift_ref[r - 1, a * SUBLANES:a * SUBLANES + ts, :]
        acc = acc + dw_ref[k:k + 1, :] * win
    yb_ref[...] = _silu(_layernorm(acc, clng_ref[...], clnb_ref[...])).astype(BF16)


def _mix_ab(za, zb, lng, lnb, ws, bs, dw, db, clng, clnb, n_lat_rows, seq_lat, seq_ctx, ts):
    rows = za.shape[0]
    w = BRANCH_W
    hb = ts // CONV_HALO
    n_halo = rows // CONV_HALO
    kern = functools.partial(_ab_kernel, ts=ts, n_lat_tiles=n_lat_rows // ts,
                             lat_tiles_per_seq=seq_lat // ts, ctx_tiles_per_seq=seq_ctx // ts)
    vec = pl.BlockSpec((1, w), lambda i: (0, 0))
    return pl.pallas_call(
        kern,
        grid=(rows // ts,),
        in_specs=[
            pl.BlockSpec((ts, 2 * w), lambda i: (i, 0)),
            pl.BlockSpec((ts, 2 * w), lambda i: (i, 0)),
            pl.BlockSpec((CONV_HALO, 2 * w), lambda i: (jnp.maximum(i * hb - 1, 0), 0)),
            pl.BlockSpec((CONV_HALO, 2 * w), lambda i: (jnp.minimum((i + 1) * hb, n_halo - 1), 0)),
            vec, vec,
            pl.BlockSpec((GM_GROUPS, CHUNK, CHUNK), lambda i: (0, 0, 0)),
            pl.BlockSpec((GM_GROUPS, CHUNK, 1), lambda i: (0, 0, 0)),
            pl.BlockSpec((CONV_W, w), lambda i: (0, 0)),
            vec, vec, vec,
        ],
        out_specs=[pl.BlockSpec((ts, w), lambda i: (i, 0)), pl.BlockSpec((ts, w), lambda i: (i, 0))],
        out_shape=[jax.ShapeDtypeStruct((rows, w), BF16), jax.ShapeDtypeStruct((rows, w), BF16)],
        scratch_shapes=[pltpu.VMEM((ts + 2 * CONV_HALO, w), F32),
                        pltpu.VMEM((SUBLANES - 1, ts + 2 * CONV_HALO - SUBLANES, w), F32)],
        compiler_params=_cparams(("parallel",)),
    )(za, zb, zb, zb, lng, lnb, ws, bs, dw, db, clng, clnb)


def _qkv_prep(zc, zd, cos_ref, sina_ref, sinb_ref, qn_ref, kvn_ref, wuq_ref, wukv_ref,
              qc_ref, kc_ref, vc_ref, qd_ref, kd_ref, vd_ref):
    cos, sina, sinb = cos_ref[...], sina_ref[...], sinb_ref[...]
    low = lax.broadcasted_iota(jnp.int32, cos.shape, 1) < ROT_DIM

    def rope(slab):
        return (slab * cos + pltpu.roll(slab, LANES - ROT_QUARTER, 1) * sina
                + pltpu.roll(slab, ROT_QUARTER, 1) * sinb)

    nq = MLA_HEADS * MLA_NOPE
    q = _dot((_rms(zc[:, :512]) * qn_ref[...]).astype(BF16), wuq_ref[...]) * MLA_SCALE
    kv = _dot((_rms(zc[:, 512:768]) * kvn_ref[...]).astype(BF16), wukv_ref[...])
    kr = rope(zc[:, 768:896])
    ones = jnp.ones((MLA_SUM_ROWS, zc.shape[0]), BF16)
    for pair in range(MLA_HEADS // 2):
        qr = rope(q[:, nq + pair * LANES:nq + (pair + 1) * LANES]).T.astype(BF16)
        for h in (2 * pair, 2 * pair + 1):
            qc_ref[h, 0:MLA_NOPE, :] = q[:, h * MLA_NOPE:(h + 1) * MLA_NOPE].T.astype(BF16)
            qc_ref[h, MLA_NOPE:, :] = qr
            kc_ref[h, :, 0:MLA_NOPE] = kv[:, h * MLA_NOPE:(h + 1) * MLA_NOPE].astype(BF16)
            keep = low if h % 2 == 0 else jnp.logical_not(low)
            kc_ref[h, :, MLA_NOPE:] = jnp.where(keep, kr, 0.0).astype(BF16)
            vc_ref[h, 0:MLA_V, :] = kv[:, nq + h * MLA_V:nq + (h + 1) * MLA_V].T.astype(BF16)
            vc_ref[h, MLA_V:, :] = ones

    for pair in range(SWA_HEADS // 2):
        qd_ref[pair] = (rope(zd[:, pair * LANES:(pair + 1) * LANES]) * SWA_SCALE).T.astype(BF16)
    kd_ref[...] = rope(zd[:, 512:640]).astype(BF16)
    vd_ref[...] = zd[:, 640:768].T.astype(BF16)


def _mla_kernel(*refs, has_lat, groups):
    if has_lat:
        qt_ref, kl_ref, vlt_ref, kx_ref, vxt_ref, o_ref = refs
    else:
        qt_ref, kx_ref, vxt_ref, o_ref = refs
    cols = qt_ref.shape[1] // groups
    scores = []
    for g in range(groups):
        qt = qt_ref[:, g * cols:(g + 1) * cols]
        scores.append((_dot(kx_ref[...], qt), _dot(kl_ref[...], qt) if has_lat else None))
    for g, (sx, sl) in enumerate(scores):
        m = jnp.max(sx, axis=0, keepdims=True)
        if has_lat:
            m = jnp.maximum(m, jnp.max(sl, axis=0, keepdims=True))
        ot = _dot(vxt_ref[...], jnp.exp2(sx - m).astype(BF16))
        if has_lat:
            ot = ot + _dot(vlt_ref[...], jnp.exp2(sl - m).astype(BF16))
        o_ref[g * cols:(g + 1) * cols, :] = (ot[:MLA_V] / ot[MLA_V:MLA_V + 1]).T.astype(BF16)


def _mla_attention(qc, kc, vc, batch, seq_lat, seq_ctx, lat_queries, out_rows, into=None):
    n_lat_rows = batch * seq_lat
    ctx_blk0 = n_lat_rows // seq_ctx
    tq = math.gcd(MLA_GROUPS * MLA_GROUP_Q, seq_lat if lat_queries else seq_ctx)
    if lat_queries:
        nq, q0 = seq_lat // tq, 0
    else:
        nq, q0 = seq_ctx // tq, n_lat_rows // tq
    vrows = vc.shape[1]
    qspec = pl.BlockSpec((None, MLA_QK, tq), lambda b, h, i: (h, 0, q0 + b * nq + i))
    kx = pl.BlockSpec((None, seq_ctx, MLA_QK), lambda b, h, i: (h, ctx_blk0 + b, 0))
    vx = pl.BlockSpec((None, vrows, seq_ctx), lambda b, h, i: (h, 0, ctx_blk0 + b))
    if lat_queries:
        kl = pl.BlockSpec((None, seq_lat, MLA_QK), lambda b, h, i: (h, b, 0))
        vl = pl.BlockSpec((None, vrows, seq_lat), lambda b, h, i: (h, 0, b))
        in_specs, args = [qspec, kl, vl, kx, vx], (qc, kc, vc, kc, vc)
    else:
        in_specs, args = [qspec, kx, vx], (qc, kc, vc)
    kern = functools.partial(_mla_kernel, has_lat=lat_queries, groups=max(1, tq // MLA_GROUP_Q))
    kern, in_specs, args, aliases = _written_into(kern, in_specs, args, into)
    return pl.pallas_call(
        kern,
        grid=(batch, MLA_HEADS, nq),
        in_specs=in_specs,
        out_specs=pl.BlockSpec((tq, MLA_V), lambda b, h, i: (q0 + b * nq + i, h)),
        out_shape=jax.ShapeDtypeStruct((out_rows, MLA_HEADS * MLA_V), BF16),
        input_output_aliases=aliases,
        compiler_params=_cparams(("parallel", "parallel", "parallel")),
    )(*args)


def _swa_kernel(*refs, has_lat):
    if has_lat:
        qt_ref, kx_ref, vxt_ref, sink_ref, bias_ref = refs[:5]
        kw_refs, vw_refs, o_ref = refs[5:5 + SWA_WINDOW_BLOCKS], refs[5 + SWA_WINDOW_BLOCKS:-1], refs[-1]
    else:
        qt_ref, kx_ref, vxt_ref, sink_ref, o_ref = refs
    blocks = qt_ref.shape[2] // SWA_BLOCK
    half = LANES // 2
    ncol = blocks * SWA_GROUP * SWA_BLOCK
    zeros = jnp.zeros((half, ncol), BF16)
    if has_lat:
        k_win = jnp.concatenate([r[...] for r in kw_refs], axis=0)
        v_win = jnp.concatenate([r[...] for r in vw_refs], axis=1)
    scores = []
    for kv in range(SWA_KV):
        parts = []
        for blk in range(blocks):
            for g in range(SWA_GROUP):
                h = kv * SWA_GROUP + g
                parts.append(qt_ref[h // 2, (h % 2) * half:(h % 2 + 1) * half, blk * SWA_BLOCK:(blk + 1) * SWA_BLOCK])
        qt = jnp.concatenate(parts, axis=1)
        qt = jnp.concatenate([qt, zeros] if kv == 0 else [zeros, qt], axis=0)
        scores.append((_dot(kx_ref[...], qt), _dot(k_win, qt) if has_lat else None))
    for kv, (sx, sl) in enumerate(scores):
        rows = slice(kv * half, (kv + 1) * half)
        sink = sink_ref[kv] * LOG2_E
        m = jnp.maximum(sink, jnp.max(sx, axis=0, keepdims=True))
        if has_lat:
            sl = sl + bias_ref[...]
            m = jnp.maximum(m, jnp.max(sl, axis=0, keepdims=True))
        ones = jnp.ones((SWA_SUM_ROWS, 1), BF16)
        vx = vxt_ref[rows, :]
        ot = _dot(jnp.concatenate([vx, jnp.broadcast_to(ones, (SWA_SUM_ROWS, vx.shape[1]))], axis=0),
                  jnp.exp2(sx - m).astype(BF16))
        if has_lat:
            vl = v_win[rows, :]
            ot = ot + _dot(jnp.concatenate([vl, jnp.broadcast_to(ones, (SWA_SUM_ROWS, vl.shape[1]))], axis=0),
                           jnp.exp2(sl - m).astype(BF16))
        o = ot[:half] / (ot[half:half + 1] + jnp.exp2(sink - m))
        for blk in range(blocks):
            for pair in range(SWA_GROUP // 2):
                c0 = (blk * SWA_GROUP + 2 * pair) * SWA_BLOCK
                slab_t = jnp.concatenate([o[:, c0:c0 + SWA_BLOCK], o[:, c0 + SWA_BLOCK:c0 + 2 * SWA_BLOCK]], axis=0)
                lane0 = (kv * (SWA_GROUP // 2) + pair) * LANES
                o_ref[blk * SWA_BLOCK:(blk + 1) * SWA_BLOCK, lane0:lane0 + LANES] = slab_t.T.astype(BF16)


def _swa_window_bias(blocks):
    nkeys = SWA_WINDOW_BLOCKS * SWA_BLOCK
    ncol = blocks * SWA_GROUP * SWA_BLOCK
    key = jnp.arange(nkeys)[:, None] - SWA_BLOCK
    col = jnp.arange(ncol)[None, :]
    qry = (col // (SWA_GROUP * SWA_BLOCK)) * SWA_BLOCK + (col % SWA_BLOCK)
    inside = jnp.abs(key - qry) <= WINDOW
    after_start = key >= 0
    before_end = key < blocks * SWA_BLOCK
    variants = [inside, inside & after_start, inside & before_end, inside & after_start & before_end]
    return jnp.where(jnp.stack(variants), 0.0, -jnp.inf).astype(F32)


def _swa_attention(qd, kd, vd, sink, batch, seq_lat, seq_ctx, lat_queries, out_rows, into=None):
    n_lat_rows = batch * seq_lat
    ctx_blk0 = n_lat_rows // seq_ctx
    seq_q = seq_lat if lat_queries else seq_ctx
    blocks = 2 if seq_q % (2 * SWA_BLOCK) == 0 and n_lat_rows % (2 * SWA_BLOCK) == 0 else 1
    tq = blocks * SWA_BLOCK
    nb = seq_q // tq
    q0 = 0 if lat_queries else n_lat_rows // tq
    kblocks = seq_lat // SWA_BLOCK
    sink_row = jnp.tile(jnp.repeat(sink.reshape(SWA_KV, SWA_GROUP), SWA_BLOCK, axis=1), (1, blocks))[:, None, :]
    in_specs = [pl.BlockSpec((SWA_HEADS // 2, LANES, tq), lambda b, i: (0, 0, q0 + b * nb + i)),
                pl.BlockSpec((seq_ctx, LANES), lambda b, i: (ctx_blk0 + b, 0)),
                pl.BlockSpec((LANES, seq_ctx), lambda b, i: (0, ctx_blk0 + b)),
                pl.BlockSpec(sink_row.shape, lambda b, i: (0, 0, 0))]
    args = [qd, kd, vd, sink_row]
    if lat_queries:
        assert blocks + 2 == SWA_WINDOW_BLOCKS
        bias = _swa_window_bias(blocks)
        in_specs.append(pl.BlockSpec((None,) + bias.shape[1:],
                                     lambda b, i: ((i == 0).astype(jnp.int32) + 2 * (i == nb - 1).astype(jnp.int32), 0, 0)))
        args.append(bias)

        def win(j):
            return lambda b, i: b * kblocks + jnp.clip(i * blocks - 1 + j, 0, kblocks - 1)

        in_specs += [pl.BlockSpec((SWA_BLOCK, LANES), lambda b, i, j=j: (win(j)(b, i), 0))
                     for j in range(SWA_WINDOW_BLOCKS)]
        in_specs += [pl.BlockSpec((LANES, SWA_BLOCK), lambda b, i, j=j: (0, win(j)(b, i)))
                     for j in range(SWA_WINDOW_BLOCKS)]
        args += [kd] * SWA_WINDOW_BLOCKS + [vd] * SWA_WINDOW_BLOCKS
    kern, in_specs, args, aliases = _written_into(functools.partial(_swa_kernel, has_lat=lat_queries),
                                                  in_specs, args, into)
    return pl.pallas_call(
        kern,
        grid=(batch, nb),
        in_specs=in_specs,
        out_specs=pl.BlockSpec((tq, BRANCH_W), lambda b, i: (q0 + b * nb + i, 0)),
        out_shape=jax.ShapeDtypeStruct((out_rows, BRANCH_W), BF16),
        input_output_aliases=aliases,
        compiler_params=_cparams(("parallel", "parallel")),
    )(*args)


def _gate_kernel(h_ref, ya_ref, yb_ref, yc_ref, yd_ref, wg0, wg1, wg2, wg3, bg_ref,
                 wb0, wb1, wb2, wb3, o_ref):
    h = h_ref[...]
    acc = None
    for b, (y_ref, wg_ref, wb_ref) in enumerate(zip((ya_ref, yb_ref, yc_ref, yd_ref),
                                                    (wg0, wg1, wg2, wg3), (wb0, wb1, wb2, wb3))):
        gate = jax.nn.sigmoid(_dot(h, wg_ref[...]) + bg_ref[b])
        term = gate * _dot(y_ref[...], wb_ref[...])
        acc = term if acc is None else acc + term
    o_ref[...] = acc.astype(BF16)


def _gated_merge(h, ys, wg, bg, wbr, rows, tm, tn):
    d = h.shape[1]
    ncol = d // tn
    wg_specs = [pl.BlockSpec((d, tn), lambda j, i, b=b: (0, b * ncol + j)) for b in range(4)]
    wb_specs = [pl.BlockSpec((None, BRANCH_W, tn), lambda j, i, b=b: (b, 0, j)) for b in range(4)]
    yspec = pl.BlockSpec((tm, BRANCH_W), lambda j, i: (i, 0))
    return pl.pallas_call(
        _gate_kernel,
        grid=(ncol, rows // tm),
        in_specs=[pl.BlockSpec((tm, d), lambda j, i: (i, 0)), yspec, yspec, yspec, yspec,
                  *wg_specs, pl.BlockSpec((4, 1, tn), lambda j, i: (0, 0, j)), *wb_specs],
        out_specs=pl.BlockSpec((tm, tn), lambda j, i: (i, j)),
        out_shape=jax.ShapeDtypeStruct((rows, d), BF16),
        compiler_params=_cparams(("parallel", "parallel")),
    )(h, *ys, wg, wg, wg, wg, bg, wbr, wbr, wbr, wbr)


def _outproj_kernel(a_ref, x_ref, gate_ref, gpost_ref, w_ref, o_ref):
    y = _dot(a_ref[...], w_ref[...])
    o_ref[...] = x_ref[...] + gate_ref[...] * (_rms(y) * gpost_ref[...])


def _outproj(a, x, mod, row_of_tile, gpost, w, tm):
    rows, d = a.shape
    return pl.pallas_call(
        _outproj_kernel,
        grid=(rows // tm,),
        in_specs=[
            pl.BlockSpec((tm, d), lambda i: (i, 0)),
            pl.BlockSpec((tm, d), lambda i: (i, 0)),
            pl.BlockSpec((None, None, 1, d), lambda i: (row_of_tile(i), 5, 0, 0)),
            pl.BlockSpec((1, d), lambda i: (0, 0)),
            pl.BlockSpec((d, d), lambda i: (0, 0), pipeline_mode=pl.Buffered(1)),
        ],
        out_specs=pl.BlockSpec((tm, d), lambda i: (i, 0)),
        out_shape=jax.ShapeDtypeStruct((rows, d), F32),
        compiler_params=_cparams(("parallel",)),
    )(a, x, mod, gpost, w)


def _rope_tables(seq_lat, n_ctx_rows, batch):
    half = ROT_DIM // 2
    freqs = 1.0 / (ROPE_BASE ** (jnp.arange(0, half, 2, dtype=F32) / half))
    t = jnp.arange(seq_lat)
    ang_r = (t // GRID_W).astype(F32)[:, None] * freqs[None, :]
    ang_c = (t % GRID_W).astype(F32)[:, None] * freqs[None, :]
    ang = jnp.concatenate([ang_r, ang_r, ang_c, ang_c], axis=-1)
    ang = jnp.tile(ang, (batch, LANES // ROT_DIM))
    cos, sin = jnp.cos(ang), jnp.sin(ang)
    first = (jnp.arange(LANES) % half) < ROT_QUARTER
    sina = jnp.where(first, -sin, 0.0)
    sinb = jnp.where(first, 0.0, sin)
    ones = jnp.ones((n_ctx_rows, LANES), F32)
    zeros = jnp.zeros((n_ctx_rows, LANES), F32)
    return (jnp.concatenate([cos, ones]), jnp.concatenate([sina, zeros]), jnp.concatenate([sinb, zeros]))


def _pack_layer(l, w_in, b_in, gm_ws, gm_bs, mla_w_uq, mla_w_ukv, swa_sink, w_branch, w_out):
    d = w_in.shape[1]
    p = {}
    wi, bi = w_in[l], b_in[l]
    c_hi = 2048 + 832
    kr = slice(c_hi - ROT_DIM, c_hi)
    p['w_abcd'] = jnp.concatenate([wi[:, :c_hi], wi[:, kr], wi[:, c_hi:c_hi + 768]], axis=1).astype(BF16)
    p['b_abcd'] = jnp.concatenate([bi[:c_hi], bi[kr], bi[c_hi:c_hi + 768]])[None, :]
    p['w_gates'] = wi[:, c_hi + 768:].astype(BF16)
    p['b_gates'] = bi[c_hi + 768:].reshape(4, 1, d)
    p['gm_ws'] = gm_ws[l].astype(BF16)
    p['gm_bs'] = gm_bs[l][:, :, None]
    uq = mla_w_uq[l].reshape(-1, MLA_HEADS, MLA_NOPE + ROT_DIM)
    p['wuq'] = jnp.concatenate([uq[:, :, :MLA_NOPE].reshape(-1, MLA_HEADS * MLA_NOPE),
                                uq[:, :, MLA_NOPE:].reshape(-1, MLA_HEADS * ROT_DIM)], axis=1).astype(BF16)
    ukv = mla_w_ukv[l].reshape(-1, MLA_HEADS, MLA_NOPE + MLA_V)
    p['wukv'] = jnp.concatenate([ukv[:, :, :MLA_NOPE].reshape(-1, MLA_HEADS * MLA_NOPE),
                                 ukv[:, :, MLA_NOPE:].reshape(-1, MLA_HEADS * MLA_V)], axis=1).astype(BF16)
    p['sink'] = swa_sink[l]
    p['w_branch'] = w_branch[l].astype(BF16)
    p['w_out'] = w_out[l].astype(BF16)
    return p


def kernel(x, c, ctx, c_ctx, w_mod, b_mod, norm_pre, norm_post, w_ff_gate, w_ff_up, w_ff_down, w_in, b_in, gm_ln_g, gm_ln_b, gm_ws, gm_bs, cv_dw, cv_db, cv_ln_g, cv_ln_b, mla_q_norm, mla_kv_norm, mla_w_uq, mla_w_ukv, swa_sink, w_branch, w_out):
    batch, seq_lat, d = x.shape
    seq_ctx = ctx.shape[1]
    depth = w_mod.shape[0]
    n_lat = batch * seq_lat
    n_ctx = batch * seq_ctx
    assert batch + 1 <= COND_ROWS and seq_lat % GRID_W == 0
    tm = math.gcd(512, math.gcd(seq_lat, n_ctx))
    ts = math.gcd(256, math.gcd(seq_lat, seq_ctx))
    assert tm % SWA_BLOCK == 0 and ts % CHUNK == 0 and seq_ctx % SWA_BLOCK == 0 and seq_lat >= 3 * SWA_BLOCK

    tm_ffn = math.gcd(1024, math.gcd(seq_lat, n_ctx))

    def cond_row(tile):
        return lambda i: jnp.where(i < n_lat // tile, i // (seq_lat // tile), batch)

    row_of_tile, row_of_ffn_tile = cond_row(tm), cond_row(tm_ffn)

    cond = jnp.zeros((COND_ROWS, d), F32).at[:batch].set(c).at[batch].set(c_ctx)
    mod = _modulation(cond, w_mod, b_mod).reshape(depth, COND_ROWS, N_MOD, 1, d)
    cos, sina, sinb = _rope_tables(seq_lat, n_ctx, batch)

    d_ff = w_ff_gate.shape[-1]
    wg_all = _cast_pad_cols(w_ff_gate.reshape(-1, d_ff), D_FF_PAD).reshape(depth, 2, d, D_FF_PAD)
    wu_all = _cast_pad_cols(w_ff_up.reshape(-1, d_ff), D_FF_PAD).reshape(depth, 2, d, D_FF_PAD)
    wd_all = _cast_pad_rows(w_ff_down.reshape(-1, d_ff, d), D_FF_PAD).reshape(depth, 2, D_FF_PAD, d)

    n_all = n_lat + n_ctx
    xs = None
    for l in range(depth):
        last = l == depth - 1
        p = _pack_layer(l, w_in, b_in, gm_ws, gm_bs, mla_w_uq, mla_w_ukv, swa_sink, w_branch, w_out)
        m = mod[l]
        npre, npost = norm_pre[l][:, None, :], norm_post[l][:, None, :]

        ffn1 = (m, 0, row_of_ffn_tile, npre[0], npost[0], wg_all, wu_all, wd_all, l, 0, tm_ffn)
        if xs is None:
            xs = _ffn(x.reshape(n_lat, d), n_lat, *ffn1, out_rows=n_all)
            ctx_ffn1 = (m, 0, lambda i: batch) + ffn1[3:]
            xs = _ffn(ctx.reshape(n_ctx, d), n_ctx, *ctx_ffn1, out_rows=n_all, out_tile0=n_lat // tm_ffn, into=xs)
        else:
            xs = _ffn(xs, n_all, *ffn1)

        h, za, zb, qc, kc, vc, qd, kd, vd = _inproj(
            xs, m, row_of_tile, npre[1], p['w_abcd'], p['b_abcd'], cos, sina, sinb,
            mla_q_norm[l][None], mla_kv_norm[l][None], p['wuq'], p['wukv'], tm)
        ya, yb = _mix_ab(za, zb, gm_ln_g[l][None], gm_ln_b[l][None], p['gm_ws'], p['gm_bs'],
                         cv_dw[l], cv_db[l][None], cv_ln_g[l][None], cv_ln_b[l][None],
                         n_lat, seq_lat, seq_ctx, ts)
        rows_mix = n_lat if last else n_all
        attn = (batch, seq_lat, seq_ctx)
        yc = _mla_attention(qc, kc, vc, *attn, True, rows_mix)
        yd = _swa_attention(qd, kd, vd, p['sink'], *attn, True, rows_mix)
        if not last:
            yc = _mla_attention(qc, kc, vc, *attn, False, rows_mix, into=yc)
            yd = _swa_attention(qd, kd, vd, p['sink'], *attn, False, rows_mix, into=yd)
        merged = _gated_merge(h, (ya, yb, yc, yd), p['w_gates'], p['b_gates'], p['w_branch'], rows_mix, tm_ffn, 512)
        xs = _outproj(merged, xs, m, row_of_tile, npost[1], p['w_out'], tm)

        xs = _ffn(xs, rows_mix, m, 2, row_of_ffn_tile, npre[2], npost[2], wg_all, wu_all, wd_all, l, 1, tm_ffn)
    return xs[:n_lat].reshape(batch, seq_lat, d)
```

```python
import functools
import math

import jax
import jax.numpy as jnp
from jax import lax
from jax.experimental import pallas as pl
from jax.experimental.pallas import tpu as pltpu

F32 = jnp.float32
BF16 = jnp.bfloat16

EPS = 1e-6
ROPE_BASE = 10000.0
GRID_W = 64
N_MOD = 9
COND_ROWS = 8
MOD_STREAMS = 4
LANES = 128
SUBLANES = 8
BF16_ROWS = 16
ROT_DIM = 64
ROT_QUARTER = ROT_DIM // 4
CONV_W = 31
CONV_HALO = 16
CHUNK = 128
GM_GROUPS = 4
BRANCH_W = 512
MLA_HEADS = 4
MLA_NOPE = 128
MLA_V = 128
MLA_QK = 256
MLA_SUM_ROWS = 16
MLA_GROUP_Q = 512
MLA_GROUPS = 4
LOG2_E = math.log2(math.e)
MLA_SCALE = (MLA_NOPE + ROT_DIM) ** -0.5 * LOG2_E
SWA_HEADS = 8
SWA_KV = 2
SWA_GROUP = SWA_HEADS // SWA_KV
SWA_BLOCK = 128
SWA_WINDOW_BLOCKS = 4
SWA_SUM_ROWS = 16
WINDOW = 128
SWA_SCALE = ROT_DIM ** -0.5 * LOG2_E
IN_WIDTHS = (1024, 1024, 896, 768)
D_FF_PAD = 5632
FF_CHUNK = 512
ROW_BLOCK = 256
VMEM_LIMIT = 56 * 1024 * 1024
VMEM_LIMIT_FFN = 62 * 1024 * 1024


def _cparams(sem, vmem_limit=VMEM_LIMIT):
    return pltpu.CompilerParams(dimension_semantics=sem, vmem_limit_bytes=vmem_limit)


def _inv_rms(x):
    return lax.rsqrt(jnp.mean(x * x, axis=-1, keepdims=True) + EPS)


def _rms(x):
    return x * _inv_rms(x)


def _layernorm(x, g, b):
    mu = jnp.mean(x, axis=-1, keepdims=True)
    xc = x - mu
    var = jnp.mean(xc * xc, axis=-1, keepdims=True)
    return xc * lax.rsqrt(var + EPS) * g + b


def _silu(x):
    return x * jax.nn.sigmoid(x)


def _written_into(kernel_fn, in_specs, args, into):
    if into is None:
        return kernel_fn, in_specs, args, {}

    def kern(dst_ref, *refs):
        del dst_ref
        kernel_fn(*refs)

    return kern, [pl.BlockSpec(memory_space=pl.ANY)] + list(in_specs), [into] + list(args), {0: 0}


def _for_row_blocks(n_rows, body):
    def step(r, carry):
        body(pl.ds(pl.multiple_of(r * ROW_BLOCK, ROW_BLOCK), ROW_BLOCK))
        return carry

    lax.fori_loop(0, n_rows // ROW_BLOCK, step, 0)


def _dot(a, b):
    return jnp.dot(a, b, preferred_element_type=F32)


def _dot_nt(a, b):
    return lax.dot_general(a, b, (((1,), (1,)), ((), ())), preferred_element_type=F32)


def _cast_pad_cols_kernel(w_ref, o_ref):
    n = w_ref.shape[1]
    o_ref[:, :n] = w_ref[...].astype(BF16)
    o_ref[:, n:] = jnp.zeros((o_ref.shape[0], o_ref.shape[1] - n), BF16)


def _cast_pad_cols(w, l, s, n_pad, tr=256):
    m, n = w.shape[2:]
    return pl.pallas_call(
        _cast_pad_cols_kernel,
        grid=(m // tr,),
        in_specs=[pl.BlockSpec((None, None, tr, n), lambda i: (l, s, i, 0))],
        out_specs=pl.BlockSpec((tr, n_pad), lambda i: (i, 0)),
        out_shape=jax.ShapeDtypeStruct((m, n_pad), BF16),
        compiler_params=_cparams(("parallel",)),
    )(w)


def _cast_pad_rows_kernel(w_ref, o_ref, *, rows_valid):
    tr = w_ref.shape[0]
    row = pl.program_id(0) * tr + lax.broadcasted_iota(jnp.int32, w_ref.shape, 0)
    o_ref[...] = jnp.where(row < rows_valid, w_ref[...], 0.0).astype(BF16)


def _cast_pad_rows(w, l, s, m_pad, tr=512):
    m, n = w.shape[2:]
    return pl.pallas_call(
        functools.partial(_cast_pad_rows_kernel, rows_valid=m),
        grid=(m_pad // tr,),
        in_specs=[pl.BlockSpec((None, None, tr, n), lambda j: (l, s, j, 0))],
        out_specs=pl.BlockSpec((tr, n), lambda j: (j, 0)),
        out_shape=jax.ShapeDtypeStruct((m_pad, n), BF16),
        compiler_params=_cparams(("parallel",)),
    )(w)


def _mod_kernel(cond_ref, b_ref, *refs):
    w_refs, o_ref = refs[:-1], refs[-1]
    k = pl.program_id(1)

    @pl.when(k == 0)
    def _():
        o_ref[...] = jnp.broadcast_to(b_ref[...], o_ref.shape)

    s = _silu(cond_ref[...]).astype(BF16)
    part = o_ref.shape[1] // len(w_refs)
    for j, w_ref in enumerate(w_refs):
        o_ref[:, j * part:(j + 1) * part] += _dot(s, w_ref[...].astype(BF16))


def _modulation(cond, w_mod, b_mod, tk=256):
    n_layers, d, n = w_mod.shape
    part = n // MOD_STREAMS
    w_specs = [pl.BlockSpec((None, tk, part), lambda l, k, j=j: (l, k, j)) for j in range(MOD_STREAMS)]
    return pl.pallas_call(
        _mod_kernel,
        grid=(n_layers, d // tk),
        in_specs=[
            pl.BlockSpec((COND_ROWS, tk), lambda l, k: (0, k)),
            pl.BlockSpec((None, 1, n), lambda l, k: (l, 0, 0)),
            *w_specs,
        ],
        out_specs=pl.BlockSpec((None, COND_ROWS, n), lambda l, k: (l, 0, 0)),
        out_shape=jax.ShapeDtypeStruct((n_layers, COND_ROWS, n), F32),
        compiler_params=_cparams(("parallel", "arbitrary")),
    )(cond, b_mod.reshape(n_layers, 1, n), *([w_mod] * MOD_STREAMS))


def _ffn_kernel(*refs, cast_next, d_ff):
    x_ref, shift_ref, scale_ref, gate_ref, gpre_ref, gpost_ref, wg_ref, wu_ref, wd_ref = refs[:9]
    if cast_next:
        wgn_ref, wun_ref, wdn_ref, o_ref, wgo_ref, wuo_ref, wdo_ref, h_ref, r_ref = refs[9:]
    else:
        o_ref, h_ref, r_ref = refs[9:]
    f = pl.program_id(1)

    @pl.when(f == 0)
    def _():
        gain = gpre_ref[...] * (1.0 + scale_ref[...])
        shift = shift_ref[...]

        def norms(rows):
            r_ref[rows, :] = _inv_rms(x_ref[rows, :])

        def block(rows):
            h_ref[rows, :] = (x_ref[rows, :] * r_ref[rows, :] * gain + shift).astype(BF16)
            o_ref[rows, :] = jnp.zeros((ROW_BLOCK, o_ref.shape[1]), F32)

        _for_row_blocks(x_ref.shape[0], norms)
        _for_row_blocks(x_ref.shape[0], block)

    h = h_ref[...]
    g = _dot(h, wg_ref[...])
    u = _dot(h, wu_ref[...])
    a = (_silu(g) * u).astype(BF16)
    o_ref[...] += _dot(a, wd_ref[...])

    if cast_next:
        for src_ref, dst_ref in ((wgn_ref, wgo_ref), (wun_ref, wuo_ref)):
            dst_ref[:, :d_ff] = src_ref[...].astype(BF16)
            dst_ref[:, d_ff:] = jnp.zeros((dst_ref.shape[0], dst_ref.shape[1] - d_ff), BF16)
        rd = wdo_ref.shape[0]
        step = pl.program_id(0) * pl.num_programs(1) + f
        row = jnp.minimum(step, cast_next - 1) * rd + lax.broadcasted_iota(jnp.int32, wdo_ref.shape, 0)
        wdo_ref[...] = jnp.where(row < d_ff, wdn_ref[...], 0.0).astype(BF16)

    @pl.when(f == pl.num_programs(1) - 1)
    def _():
        gain = 0.5 * gate_ref[...] * gpost_ref[...]

        def norms(rows):
            r_ref[rows, :] = _inv_rms(o_ref[rows, :])

        def block(rows):
            o_ref[rows, :] = x_ref[rows, :] + o_ref[rows, :] * r_ref[rows, :] * gain

        _for_row_blocks(x_ref.shape[0], norms)
        _for_row_blocks(x_ref.shape[0], block)


def _slab_rows(total, n_steps):
    slab = BF16_ROWS
    while total % slab or total // slab > n_steps:
        slab *= 2
        assert slab <= total
    return slab


def _ffn(x, rows, mod, j, row_of_tile, gpre, gpost, wts, tm, out_rows=None, out_tile0=0, into=None, cast_next=None):
    wg, wu, wd = wts
    d = x.shape[1]
    f_pad = wg.shape[-1]
    tf = FF_CHUNK
    nf = f_pad // tf
    out_rows = rows if out_rows is None else out_rows

    def mspec(k):
        return pl.BlockSpec((None, None, 1, d), lambda i, f: (row_of_tile(i), k, 0, 0))

    vec = pl.BlockSpec((1, d), lambda i, f: (0, 0))
    in_specs = [
        pl.BlockSpec((tm, d), lambda i, f: (i, 0)),
        mspec(3 * j), mspec(3 * j + 1), mspec(3 * j + 2),
        vec, vec,
        pl.BlockSpec((d, tf), lambda i, f: (0, f)),
        pl.BlockSpec((d, tf), lambda i, f: (0, f)),
        pl.BlockSpec((tf, d), lambda i, f: (f, 0)),
    ]
    args = [x, mod, mod, mod, gpre, gpost, wg, wu, wd]
    out_specs = [pl.BlockSpec((tm, d), lambda i, f: (out_tile0 + i, 0))]
    out_shape = [jax.ShapeDtypeStruct((out_rows, d), F32)]
    down_slabs, d_ff = 0, f_pad
    if cast_next is not None:
        wg32, wu32, wd32, l2, s2 = cast_next
        d_ff = wg32.shape[-1]
        n_steps = (rows // tm) * nf
        rg, rd = _slab_rows(d, n_steps), _slab_rows(f_pad, n_steps)
        up_slabs, down_slabs, down_src_slabs = d // rg, f_pad // rd, pl.cdiv(d_ff, rd)

        def slab(count):
            return lambda i, f: jnp.minimum(i * nf + f, count - 1)

        in_specs += [pl.BlockSpec((None, None, rg, d_ff), lambda i, f: (l2, s2, slab(up_slabs)(i, f), 0))] * 2
        in_specs += [pl.BlockSpec((None, None, rd, d), lambda i, f: (l2, s2, slab(down_src_slabs)(i, f), 0))]
        args += [wg32, wu32, wd32]
        out_specs += [pl.BlockSpec((rg, f_pad), lambda i, f: (slab(up_slabs)(i, f), 0))] * 2
        out_specs += [pl.BlockSpec((rd, d), lambda i, f: (slab(down_slabs)(i, f), 0))]
        out_shape += [jax.ShapeDtypeStruct((d, f_pad), BF16)] * 2 + [jax.ShapeDtypeStruct((f_pad, d), BF16)]
    kern = functools.partial(_ffn_kernel, cast_next=down_slabs, d_ff=d_ff)
    kern, in_specs, args, aliases = _written_into(kern, in_specs, args, into)
    outs = pl.pallas_call(
        kern,
        grid=(rows // tm, nf),
        in_specs=in_specs,
        out_specs=out_specs,
        out_shape=out_shape,
        scratch_shapes=[pltpu.VMEM((tm, d), BF16), pltpu.VMEM((tm, 1), F32)],
        input_output_aliases=aliases,
        compiler_params=_cparams(("parallel", "arbitrary"), VMEM_LIMIT_FFN),
    )(*args)
    return outs[0] if cast_next is None else (outs[0], tuple(outs[1:]))


def _inproj_kernel(x_ref, shift_ref, scale_ref, gpre_ref, w_ref, b_ref,
                   cos_ref, sina_ref, sinb_ref, qn_ref, kvn_ref, wuq_ref, wukv_ref,
                   lng_ref, lnb_ref, ws_ref, bs_ref,
                   h_ref, ya_ref, hb_ref, qc_ref, kc_ref, vc_ref, qd_ref, kd_ref, vd_ref):
    y = _rms(x_ref[...]) * gpre_ref[...]
    h = (y * (1.0 + scale_ref[...]) + shift_ref[...]).astype(BF16)
    h_ref[...] = h

    def proj(lo, hi):
        return _dot(h, w_ref[:, lo:hi]) + b_ref[:, lo:hi]

    na, nb, nc, nd = IN_WIDTHS
    _qkv_prep(proj(na + nb, na + nb + nc), proj(na + nb + nc, na + nb + nc + nd),
              cos_ref, sina_ref, sinb_ref, qn_ref, kvn_ref, wuq_ref, wukv_ref,
              qc_ref, kc_ref, vc_ref, qd_ref, kd_ref, vd_ref)
    w = BRANCH_W
    zb = proj(na, na + nb)
    hb_ref[...] = zb[:, :w] * jax.nn.sigmoid(zb[:, w:])
    za = proj(0, na)
    g = 0.5 * za * (1.0 + lax.erf(za * (2.0 ** -0.5)))
    vln = _layernorm(g[:, w:], lng_ref[...], lnb_ref[...]).astype(BF16)
    for c in range(za.shape[0] // CHUNK):
        rows = slice(c * CHUNK, (c + 1) * CHUNK)
        for gi in range(GM_GROUPS):
            cols = slice(gi * CHUNK, (gi + 1) * CHUNK)
            sv = _dot(ws_ref[gi], vln[rows, cols]) + bs_ref[gi]
            ya_ref[rows, cols] = (g[rows, cols] * sv).astype(BF16)


def _inproj(x, mod, row_of_tile, gpre, w, b, cos, sina, sinb, qn, kvn, wuq, wukv, lng, lnb, ws, bs, tm):
    rows, d = x.shape
    n = w.shape[1]

    def mspec(k):
        return pl.BlockSpec((None, None, 1, d), lambda i: (row_of_tile(i), k, 0, 0))

    def rowspec(wd):
        return pl.BlockSpec((tm, wd), lambda i: (i, 0))

    def full(a):
        return pl.BlockSpec(a.shape, lambda i: (0,) * a.ndim)

    def headspec(nh, wd):
        return pl.BlockSpec((nh, tm, wd), lambda i: (0, i, 0))

    def headspec_t(nh, wd):
        return pl.BlockSpec((nh, wd, tm), lambda i: (0, 0, i))

    kc = jax.ShapeDtypeStruct((MLA_HEADS, rows, MLA_QK), BF16)
    qc = jax.ShapeDtypeStruct((MLA_HEADS, MLA_QK, rows), BF16)
    vc = jax.ShapeDtypeStruct((MLA_HEADS, MLA_V + MLA_SUM_ROWS, rows), BF16)
    qd = jax.ShapeDtypeStruct((SWA_HEADS // 2, LANES, rows), BF16)
    kd = jax.ShapeDtypeStruct((rows, LANES), BF16)
    vd = jax.ShapeDtypeStruct((LANES, rows), BF16)
    outs = [jax.ShapeDtypeStruct((rows, d), BF16), jax.ShapeDtypeStruct((rows, BRANCH_W), BF16),
            jax.ShapeDtypeStruct((rows, BRANCH_W), F32)]
    ospecs = [rowspec(d), rowspec(BRANCH_W), rowspec(BRANCH_W)]
    return pl.pallas_call(
        _inproj_kernel,
        grid=(rows // tm,),
        in_specs=[
            rowspec(d),
            mspec(3), mspec(4),
            pl.BlockSpec((1, d), lambda i: (0, 0)),
            pl.BlockSpec((d, n), lambda i: (0, 0), pipeline_mode=pl.Buffered(1)),
            pl.BlockSpec((1, n), lambda i: (0, 0)),
            rowspec(LANES), rowspec(LANES), rowspec(LANES), full(qn), full(kvn), full(wuq), full(wukv),
            full(lng), full(lnb), full(ws), full(bs),
        ],
        out_specs=ospecs + [headspec_t(MLA_HEADS, MLA_QK), headspec(MLA_HEADS, MLA_QK),
                            headspec_t(MLA_HEADS, MLA_V + MLA_SUM_ROWS), headspec_t(SWA_HEADS // 2, LANES),
                            rowspec(LANES), pl.BlockSpec((LANES, tm), lambda i: (0, i))],
        out_shape=outs + [qc, kc, vc, qd, kd, vd],
        compiler_params=_cparams(("parallel",)),
    )(x, mod, mod, gpre, w, b, cos, sina, sinb, qn, kvn, wuq, wukv, lng, lnb, ws, bs)


def _conv_kernel(h_ref, hprev_ref, hnext_ref, dw_ref, db_ref, lng_ref, lnb_ref, yb_ref, ext_ref, shift_ref,
                 *, ts, n_lat_tiles, lat_tiles_per_seq, ctx_tiles_per_seq):
    i = pl.program_id(0)
    in_lat = i < n_lat_tiles
    pos = jnp.where(in_lat, i % lat_tiles_per_seq, (i - n_lat_tiles) % ctx_tiles_per_seq)
    last = jnp.where(in_lat, lat_tiles_per_seq - 1, ctx_tiles_per_seq - 1)
    ext_ref[0:CONV_HALO, :] = jnp.where(pos != 0, hprev_ref[...], 0.0)
    ext_ref[CONV_HALO:CONV_HALO + ts, :] = h_ref[...]
    ext_ref[CONV_HALO + ts:, :] = jnp.where(pos != last, hnext_ref[...], 0.0)
    span = ts + 2 * CONV_HALO - SUBLANES
    for r in range(1, SUBLANES):
        shift_ref[r - 1, 0:span, :] = ext_ref[r:r + span, :]
    first_tap = CONV_HALO - CONV_W // 2
    acc = jnp.zeros((ts, BRANCH_W), F32) + db_ref[...]
    for k in range(CONV_W):
        a, r = divmod(first_tap + k, SUBLANES)
        win = ext_ref[a * SUBLANES:a * SUBLANES + ts, :] if r == 0 else shift_ref[r - 1, a * SUBLANES:a * SUBLANES + ts, :]
        acc = acc + dw_ref[k:k + 1, :] * win
    yb_ref[...] = _silu(_layernorm(acc, lng_ref[...], lnb_ref[...])).astype(BF16)


def _conv_mixer(hb, dw, db, lng, lnb, n_lat_rows, seq_lat, seq_ctx, ts):
    rows, w = hb.shape
    halo_blocks = ts // CONV_HALO
    n_halo = rows // CONV_HALO
    kern = functools.partial(_conv_kernel, ts=ts, n_lat_tiles=n_lat_rows // ts,
                             lat_tiles_per_seq=seq_lat // ts, ctx_tiles_per_seq=seq_ctx // ts)
    vec = pl.BlockSpec((1, w), lambda i: (0, 0))
    return pl.pallas_call(
        kern,
        grid=(rows // ts,),
        in_specs=[
            pl.BlockSpec((ts, w), lambda i: (i, 0)),
            pl.BlockSpec((CONV_HALO, w), lambda i: (jnp.maximum(i * halo_blocks - 1, 0), 0)),
            pl.BlockSpec((CONV_HALO, w), lambda i: (jnp.minimum((i + 1) * halo_blocks, n_halo - 1), 0)),
            pl.BlockSpec((CONV_W, w), lambda i: (0, 0)),
            vec, vec, vec,
        ],
        out_specs=pl.BlockSpec((ts, w), lambda i: (i, 0)),
        out_shape=jax.ShapeDtypeStruct((rows, w), BF16),
        scratch_shapes=[pltpu.VMEM((ts + 2 * CONV_HALO, w), F32),
                        pltpu.VMEM((SUBLANES - 1, ts + 2 * CONV_HALO - SUBLANES, w), F32)],
        compiler_params=_cparams(("parallel",)),
    )(hb, hb, hb, dw, db, lng, lnb)


def _qkv_prep(zc, zd, cos_ref, sina_ref, sinb_ref, qn_ref, kvn_ref, wuq_ref, wukv_ref,
              qc_ref, kc_ref, vc_ref, qd_ref, kd_ref, vd_ref):
    cos, sina, sinb = cos_ref[...], sina_ref[...], sinb_ref[...]
    low = lax.broadcasted_iota(jnp.int32, cos.shape, 1) < ROT_DIM

    def rope(slab):
        return (slab * cos + pltpu.roll(slab, LANES - ROT_QUARTER, 1) * sina
                + pltpu.roll(slab, ROT_QUARTER, 1) * sinb)

    nq = MLA_HEADS * MLA_NOPE
    q = _dot((_rms(zc[:, :512]) * qn_ref[...]).astype(BF16), wuq_ref[...]) * MLA_SCALE
    kv = _dot((_rms(zc[:, 512:768]) * kvn_ref[...]).astype(BF16), wukv_ref[...])
    kr = rope(zc[:, 768:896])
    ones = jnp.ones((MLA_SUM_ROWS, zc.shape[0]), BF16)
    for pair in range(MLA_HEADS // 2):
        qr = rope(q[:, nq + pair * LANES:nq + (pair + 1) * LANES]).T.astype(BF16)
        for h in (2 * pair, 2 * pair + 1):
            qc_ref[h, 0:MLA_NOPE, :] = q[:, h * MLA_NOPE:(h + 1) * MLA_NOPE].T.astype(BF16)
            qc_ref[h, MLA_NOPE:, :] = qr
            kc_ref[h, :, 0:MLA_NOPE] = kv[:, h * MLA_NOPE:(h + 1) * MLA_NOPE].astype(BF16)
            keep = low if h % 2 == 0 else jnp.logical_not(low)
            kc_ref[h, :, MLA_NOPE:] = jnp.where(keep, kr, 0.0).astype(BF16)
            vc_ref[h, 0:MLA_V, :] = kv[:, nq + h * MLA_V:nq + (h + 1) * MLA_V].T.astype(BF16)
            vc_ref[h, MLA_V:, :] = ones

    for pair in range(SWA_HEADS // 2):
        qd_ref[pair] = (rope(zd[:, pair * LANES:(pair + 1) * LANES]) * SWA_SCALE).T.astype(BF16)
    kd_ref[...] = rope(zd[:, 512:640]).astype(BF16)
    vd_ref[...] = zd[:, 640:768].T.astype(BF16)


def _mla_kernel(*refs, has_lat, groups):
    if has_lat:
        qt_ref, kl_ref, vlt_ref, kx_ref, vxt_ref, o_ref = refs
    else:
        qt_ref, kx_ref, vxt_ref, o_ref = refs
    cols = qt_ref.shape[1] // groups
    scores = []
    for g in range(groups):
        qt = qt_ref[:, g * cols:(g + 1) * cols]
        scores.append((_dot(kx_ref[...], qt), _dot(kl_ref[...], qt) if has_lat else None))
    for g, (sx, sl) in enumerate(scores):
        m = jnp.max(sx, axis=0, keepdims=True)
        if has_lat:
            m = jnp.maximum(m, jnp.max(sl, axis=0, keepdims=True))
        ot = _dot(vxt_ref[...], jnp.exp2(sx - m).astype(BF16))
        if has_lat:
            ot = ot + _dot(vlt_ref[...], jnp.exp2(sl - m).astype(BF16))
        o_ref[g * cols:(g + 1) * cols, :] = (ot[:MLA_V] / ot[MLA_V:MLA_V + 1]).T.astype(BF16)


def _mla_attention(qc, kc, vc, batch, seq_lat, seq_ctx, lat_queries, out_rows, into=None):
    n_lat_rows = batch * seq_lat
    ctx_blk0 = n_lat_rows // seq_ctx
    tq = math.gcd(MLA_GROUPS * MLA_GROUP_Q, seq_lat if lat_queries else seq_ctx)
    if lat_queries:
        nq, q0 = seq_lat // tq, 0
    else:
        nq, q0 = seq_ctx // tq, n_lat_rows // tq
    vrows = vc.shape[1]
    qspec = pl.BlockSpec((None, MLA_QK, tq), lambda b, h, i: (h, 0, q0 + b * nq + i))
    kx = pl.BlockSpec((None, seq_ctx, MLA_QK), lambda b, h, i: (h, ctx_blk0 + b, 0))
    vx = pl.BlockSpec((None, vrows, seq_ctx), lambda b, h, i: (h, 0, ctx_blk0 + b))
    if lat_queries:
        kl = pl.BlockSpec((None, seq_lat, MLA_QK), lambda b, h, i: (h, b, 0))
        vl = pl.BlockSpec((None, vrows, seq_lat), lambda b, h, i: (h, 0, b))
        in_specs, args = [qspec, kl, vl, kx, vx], (qc, kc, vc, kc, vc)
    else:
        in_specs, args = [qspec, kx, vx], (qc, kc, vc)
    kern = functools.partial(_mla_kernel, has_lat=lat_queries, groups=max(1, tq // MLA_GROUP_Q))
    kern, in_specs, args, aliases = _written_into(kern, in_specs, args, into)
    return pl.pallas_call(
        kern,
        grid=(batch, MLA_HEADS, nq),
        in_specs=in_specs,
        out_specs=pl.BlockSpec((tq, MLA_V), lambda b, h, i: (q0 + b * nq + i, h)),
        out_shape=jax.ShapeDtypeStruct((out_rows, MLA_HEADS * MLA_V), BF16),
        input_output_aliases=aliases,
        compiler_params=_cparams(("parallel", "parallel", "parallel")),
    )(*args)


def _swa_kernel(*refs, has_lat):
    if has_lat:
        qt_ref, kx_ref, vxt_ref, sink_ref, bias_ref = refs[:5]
        kw_refs, vw_refs, o_ref = refs[5:5 + SWA_WINDOW_BLOCKS], refs[5 + SWA_WINDOW_BLOCKS:-1], refs[-1]
    else:
        qt_ref, kx_ref, vxt_ref, sink_ref, o_ref = refs
    blocks = qt_ref.shape[2] // SWA_BLOCK
    half = LANES // 2
    ncol = blocks * SWA_GROUP * SWA_BLOCK
    zeros = jnp.zeros((half, ncol), BF16)
    if has_lat:
        k_win = jnp.concatenate([r[...] for r in kw_refs], axis=0)
        v_win = jnp.concatenate([r[...] for r in vw_refs], axis=1)
    scores = []
    for kv in range(SWA_KV):
        parts = []
        for blk in range(blocks):
            for g in range(SWA_GROUP):
                h = kv * SWA_GROUP + g
                parts.append(qt_ref[h // 2, (h % 2) * half:(h % 2 + 1) * half, blk * SWA_BLOCK:(blk + 1) * SWA_BLOCK])
        qt = jnp.concatenate(parts, axis=1)
        qt = jnp.concatenate([qt, zeros] if kv == 0 else [zeros, qt], axis=0)
        scores.append((_dot(kx_ref[...], qt), _dot(k_win, qt) if has_lat else None))
    for kv, (sx, sl) in enumerate(scores):
        rows = slice(kv * half, (kv + 1) * half)
        sink = sink_ref[kv] * LOG2_E
        m = jnp.maximum(sink, jnp.max(sx, axis=0, keepdims=True))
        if has_lat:
            sl = sl + bias_ref[...]
            m = jnp.maximum(m, jnp.max(sl, axis=0, keepdims=True))
        ones = jnp.ones((SWA_SUM_ROWS, 1), BF16)
        vx = vxt_ref[rows, :]
        ot = _dot(jnp.concatenate([vx, jnp.broadcast_to(ones, (SWA_SUM_ROWS, vx.shape[1]))], axis=0),
                  jnp.exp2(sx - m).astype(BF16))
        if has_lat:
            vl = v_win[rows, :]
            ot = ot + _dot(jnp.concatenate([vl, jnp.broadcast_to(ones, (SWA_SUM_ROWS, vl.shape[1]))], axis=0),
                           jnp.exp2(sl - m).astype(BF16))
        o = ot[:half] / (ot[half:half + 1] + jnp.exp2(sink - m))
        for blk in range(blocks):
            for pair in range(SWA_GROUP // 2):
                c0 = (blk * SWA_GROUP + 2 * pair) * SWA_BLOCK
                slab_t = jnp.concatenate([o[:, c0:c0 + SWA_BLOCK], o[:, c0 + SWA_BLOCK:c0 + 2 * SWA_BLOCK]], axis=0)
                lane0 = (kv * (SWA_GROUP // 2) + pair) * LANES
                o_ref[blk * SWA_BLOCK:(blk + 1) * SWA_BLOCK, lane0:lane0 + LANES] = slab_t.T.astype(BF16)


def _swa_window_bias(blocks):
    nkeys = SWA_WINDOW_BLOCKS * SWA_BLOCK
    ncol = blocks * SWA_GROUP * SWA_BLOCK
    key = jnp.arange(nkeys)[:, None] - SWA_BLOCK
    col = jnp.arange(ncol)[None, :]
    qry = (col // (SWA_GROUP * SWA_BLOCK)) * SWA_BLOCK + (col % SWA_BLOCK)
    inside = jnp.abs(key - qry) <= WINDOW
    after_start = key >= 0
    before_end = key < blocks * SWA_BLOCK
    variants = [inside, inside & after_start, inside & before_end, inside & after_start & before_end]
    return jnp.where(jnp.stack(variants), 0.0, -jnp.inf).astype(F32)


def _swa_attention(qd, kd, vd, sink, batch, seq_lat, seq_ctx, lat_queries, out_rows, into=None):
    n_lat_rows = batch * seq_lat
    ctx_blk0 = n_lat_rows // seq_ctx
    seq_q = seq_lat if lat_queries else seq_ctx
    blocks = 2 if seq_q % (2 * SWA_BLOCK) == 0 and n_lat_rows % (2 * SWA_BLOCK) == 0 else 1
    tq = blocks * SWA_BLOCK
    nb = seq_q // tq
    q0 = 0 if lat_queries else n_lat_rows // tq
    kblocks = seq_lat // SWA_BLOCK
    sink_row = jnp.tile(jnp.repeat(sink.reshape(SWA_KV, SWA_GROUP), SWA_BLOCK, axis=1), (1, blocks))[:, None, :]
    in_specs = [pl.BlockSpec((SWA_HEADS // 2, LANES, tq), lambda b, i: (0, 0, q0 + b * nb + i)),
                pl.BlockSpec((seq_ctx, LANES), lambda b, i: (ctx_blk0 + b, 0)),
                pl.BlockSpec((LANES, seq_ctx), lambda b, i: (0, ctx_blk0 + b)),
                pl.BlockSpec(sink_row.shape, lambda b, i: (0, 0, 0))]
    args = [qd, kd, vd, sink_row]
    if lat_queries:
        assert blocks + 2 == SWA_WINDOW_BLOCKS
        bias = _swa_window_bias(blocks)
        in_specs.append(pl.BlockSpec((None,) + bias.shape[1:],
                                     lambda b, i: ((i == 0).astype(jnp.int32) + 2 * (i == nb - 1).astype(jnp.int32), 0, 0)))
        args.append(bias)

        def win(j):
            return lambda b, i: b * kblocks + jnp.clip(i * blocks - 1 + j, 0, kblocks - 1)

        in_specs += [pl.BlockSpec((SWA_BLOCK, LANES), lambda b, i, j=j: (win(j)(b, i), 0))
                     for j in range(SWA_WINDOW_BLOCKS)]
        in_specs += [pl.BlockSpec((LANES, SWA_BLOCK), lambda b, i, j=j: (0, win(j)(b, i)))
                     for j in range(SWA_WINDOW_BLOCKS)]
        args += [kd] * SWA_WINDOW_BLOCKS + [vd] * SWA_WINDOW_BLOCKS
    kern, in_specs, args, aliases = _written_into(functools.partial(_swa_kernel, has_lat=lat_queries),
                                                  in_specs, args, into)
    return pl.pallas_call(
        kern,
        grid=(batch, nb),
        in_specs=in_specs,
        out_specs=pl.BlockSpec((tq, BRANCH_W), lambda b, i: (q0 + b * nb + i, 0)),
        out_shape=jax.ShapeDtypeStruct((out_rows, BRANCH_W), BF16),
        input_output_aliases=aliases,
        compiler_params=_cparams(("parallel", "parallel")),
    )(*args)


def _gate_kernel(h_ref, ya_ref, yb_ref, yc_ref, yd_ref, wg0, wg1, wg2, wg3, bg_ref,
                 wb0, wb1, wb2, wb3, o_ref):
    h = h_ref[...]
    acc = None
    for b, (y_ref, wg_ref, wb_ref) in enumerate(zip((ya_ref, yb_ref, yc_ref, yd_ref),
                                                    (wg0, wg1, wg2, wg3), (wb0, wb1, wb2, wb3))):
        gate = jax.nn.sigmoid(_dot(h, wg_ref[...]) + bg_ref[b])
        term = gate * _dot(y_ref[...], wb_ref[...])
        acc = term if acc is None else acc + term
    o_ref[...] = acc.astype(BF16)


def _gated_merge(h, ys, wg, bg, wbr, rows, tm, tn):
    d = h.shape[1]
    ncol = d // tn
    wg_specs = [pl.BlockSpec((d, tn), lambda j, i, b=b: (0, b * ncol + j)) for b in range(4)]
    wb_specs = [pl.BlockSpec((None, BRANCH_W, tn), lambda j, i, b=b: (b, 0, j)) for b in range(4)]
    yspec = pl.BlockSpec((tm, BRANCH_W), lambda j, i: (i, 0))
    return pl.pallas_call(
        _gate_kernel,
        grid=(ncol, rows // tm),
        in_specs=[pl.BlockSpec((tm, d), lambda j, i: (i, 0)), yspec, yspec, yspec, yspec,
                  *wg_specs, pl.BlockSpec((4, 1, tn), lambda j, i: (0, 0, j)), *wb_specs],
        out_specs=pl.BlockSpec((tm, tn), lambda j, i: (i, j)),
        out_shape=jax.ShapeDtypeStruct((rows, d), BF16),
        compiler_params=_cparams(("parallel", "parallel")),
    )(h, *ys, wg, wg, wg, wg, bg, wbr, wbr, wbr, wbr)


def _outproj_kernel(a_ref, x_ref, gate_ref, gpost_ref, w_ref, o_ref):
    y = _dot(a_ref[...], w_ref[...])
    o_ref[...] = x_ref[...] + gate_ref[...] * (_rms(y) * gpost_ref[...])


def _outproj(a, x, mod, row_of_tile, gpost, w, tm):
    rows, d = a.shape
    return pl.pallas_call(
        _outproj_kernel,
        grid=(rows // tm,),
        in_specs=[
            pl.BlockSpec((tm, d), lambda i: (i, 0)),
            pl.BlockSpec((tm, d), lambda i: (i, 0)),
            pl.BlockSpec((None, None, 1, d), lambda i: (row_of_tile(i), 5, 0, 0)),
            pl.BlockSpec((1, d), lambda i: (0, 0)),
            pl.BlockSpec((d, d), lambda i: (0, 0), pipeline_mode=pl.Buffered(1)),
        ],
        out_specs=pl.BlockSpec((tm, d), lambda i: (i, 0)),
        out_shape=jax.ShapeDtypeStruct((rows, d), F32),
        compiler_params=_cparams(("parallel",)),
    )(a, x, mod, gpost, w)


def _rope_tables(seq_lat, n_ctx_rows, batch):
    half = ROT_DIM // 2
    freqs = 1.0 / (ROPE_BASE ** (jnp.arange(0, half, 2, dtype=F32) / half))
    t = jnp.arange(seq_lat)
    ang_r = (t // GRID_W).astype(F32)[:, None] * freqs[None, :]
    ang_c = (t % GRID_W).astype(F32)[:, None] * freqs[None, :]
    ang = jnp.concatenate([ang_r, ang_r, ang_c, ang_c], axis=-1)
    ang = jnp.tile(ang, (batch, LANES // ROT_DIM))
    cos, sin = jnp.cos(ang), jnp.sin(ang)
    first = (jnp.arange(LANES) % half) < ROT_QUARTER
    sina = jnp.where(first, -sin, 0.0)
    sinb = jnp.where(first, 0.0, sin)
    ones = jnp.ones((n_ctx_rows, LANES), F32)
    zeros = jnp.zeros((n_ctx_rows, LANES), F32)
    return (jnp.concatenate([cos, ones]), jnp.concatenate([sina, zeros]), jnp.concatenate([sinb, zeros]))


def _pack_layer(l, w_in, b_in, gm_ws, gm_bs, mla_w_uq, mla_w_ukv, swa_sink, w_branch, w_out):
    d = w_in.shape[1]
    p = {}
    wi, bi = w_in[l], b_in[l]
    c_hi = 2048 + 832
    kr = slice(c_hi - ROT_DIM, c_hi)
    p['w_abcd'] = jnp.concatenate([wi[:, :c_hi], wi[:, kr], wi[:, c_hi:c_hi + 768]], axis=1).astype(BF16)
    p['b_abcd'] = jnp.concatenate([bi[:c_hi], bi[kr], bi[c_hi:c_hi + 768]])[None, :]
    p['w_gates'] = wi[:, c_hi + 768:].astype(BF16)
    p['b_gates'] = bi[c_hi + 768:].reshape(4, 1, d)
    p['gm_ws'] = gm_ws[l].astype(BF16)
    p['gm_bs'] = gm_bs[l][:, :, None]
    uq = mla_w_uq[l].reshape(-1, MLA_HEADS, MLA_NOPE + ROT_DIM)
    p['wuq'] = jnp.concatenate([uq[:, :, :MLA_NOPE].reshape(-1, MLA_HEADS * MLA_NOPE),
                                uq[:, :, MLA_NOPE:].reshape(-1, MLA_HEADS * ROT_DIM)], axis=1).astype(BF16)
    ukv = mla_w_ukv[l].reshape(-1, MLA_HEADS, MLA_NOPE + MLA_V)
    p['wukv'] = jnp.concatenate([ukv[:, :, :MLA_NOPE].reshape(-1, MLA_HEADS * MLA_NOPE),
                                 ukv[:, :, MLA_NOPE:].reshape(-1, MLA_HEADS * MLA_V)], axis=1).astype(BF16)
    p['sink'] = swa_sink[l]
    p['w_branch'] = w_branch[l].astype(BF16)
    p['w_out'] = w_out[l].astype(BF16)
    return p


def kernel(x, c, ctx, c_ctx, w_mod, b_mod, norm_pre, norm_post, w_ff_gate, w_ff_up, w_ff_down, w_in, b_in, gm_ln_g, gm_ln_b, gm_ws, gm_bs, cv_dw, cv_db, cv_ln_g, cv_ln_b, mla_q_norm, mla_kv_norm, mla_w_uq, mla_w_ukv, swa_sink, w_branch, w_out):
    batch, seq_lat, d = x.shape
    seq_ctx = ctx.shape[1]
    depth = w_mod.shape[0]
    n_lat = batch * seq_lat
    n_ctx = batch * seq_ctx
    assert batch + 1 <= COND_ROWS and seq_lat % GRID_W == 0
    tm = math.gcd(512, math.gcd(seq_lat, n_ctx))
    ts = math.gcd(256, math.gcd(seq_lat, seq_ctx))
    assert tm % SWA_BLOCK == 0 and ts % CHUNK == 0 and seq_ctx % SWA_BLOCK == 0 and seq_lat >= 3 * SWA_BLOCK

    tm_ffn = math.gcd(1024, math.gcd(seq_lat, n_ctx))

    def cond_row(tile):
        return lambda i: jnp.where(i < n_lat // tile, i // (seq_lat // tile), batch)

    row_of_tile, row_of_ffn_tile = cond_row(tm), cond_row(tm_ffn)

    cond = jnp.zeros((COND_ROWS, d), F32).at[:batch].set(c).at[batch].set(c_ctx)
    mod = _modulation(cond, w_mod, b_mod).reshape(depth, COND_ROWS, N_MOD, 1, d)
    cos, sina, sinb = _rope_tables(seq_lat, n_ctx, batch)

    ff32 = (w_ff_gate, w_ff_up, w_ff_down)
    wts = (_cast_pad_cols(w_ff_gate, 0, 0, D_FF_PAD), _cast_pad_cols(w_ff_up, 0, 0, D_FF_PAD),
           _cast_pad_rows(w_ff_down, 0, 0, D_FF_PAD))

    n_all = n_lat + n_ctx
    xs = None
    for l in range(depth):
        last = l == depth - 1
        p = _pack_layer(l, w_in, b_in, gm_ws, gm_bs, mla_w_uq, mla_w_ukv, swa_sink, w_branch, w_out)
        m = mod[l]
        npre, npost = norm_pre[l][:, None, :], norm_post[l][:, None, :]

        ffn1 = (m, 0, row_of_ffn_tile, npre[0], npost[0])
        if xs is None:
            xs, wts2 = _ffn(x.reshape(n_lat, d), n_lat, *ffn1, wts, tm_ffn, out_rows=n_all, cast_next=ff32 + (l, 1))
            xs = _ffn(ctx.reshape(n_ctx, d), n_ctx, m, 0, lambda i: batch, npre[0], npost[0], wts, tm_ffn,
                      out_rows=n_all, out_tile0=n_lat // tm_ffn, into=xs)
        else:
            xs, wts2 = _ffn(xs, n_all, *ffn1, wts, tm_ffn, cast_next=ff32 + (l, 1))

        h, ya, hb, qc, kc, vc, qd, kd, vd = _inproj(
            xs, m, row_of_tile, npre[1], p['w_abcd'], p['b_abcd'], cos, sina, sinb,
            mla_q_norm[l][None], mla_kv_norm[l][None], p['wuq'], p['wukv'],
            gm_ln_g[l][None], gm_ln_b[l][None], p['gm_ws'], p['gm_bs'], tm)
        yb = _conv_mixer(hb, cv_dw[l], cv_db[l][None], cv_ln_g[l][None], cv_ln_b[l][None],
                         n_lat, seq_lat, seq_ctx, ts)
        rows_mix = n_lat if last else n_all
        attn = (batch, seq_lat, seq_ctx)
        yc = _mla_attention(qc, kc, vc, *attn, True, rows_mix)
        yd = _swa_attention(qd, kd, vd, p['sink'], *attn, True, rows_mix)
        if not last:
            yc = _mla_attention(qc, kc, vc, *attn, False, rows_mix, into=yc)
            yd = _swa_attention(qd, kd, vd, p['sink'], *attn, False, rows_mix, into=yd)
        merged = _gated_merge(h, (ya, yb, yc, yd), p['w_gates'], p['b_gates'], p['w_branch'], rows_mix, tm_ffn, 512)
        xs = _outproj(merged, xs, m, row_of_tile, npost[1], p['w_out'], tm)

        ffn2 = (xs, rows_mix, m, 2, row_of_ffn_tile, npre[2], npost[2], wts2, tm_ffn)
        if last:
            xs = _ffn(*ffn2)
        else:
            xs, wts = _ffn(*ffn2, cast_next=ff32 + (l + 1, 0))
    return xs[:n_lat].reshape(batch, seq_lat, d)
```

```python
import functools
import math

import jax
import jax.numpy as jnp
from jax import lax
from jax.experimental import pallas as pl
from jax.experimental.pallas import tpu as pltpu

F32 = jnp.float32
BF16 = jnp.bfloat16

EPS = 1e-6
ROPE_BASE = 10000.0
GRID_W = 64
N_MOD = 9
COND_ROWS = 8
MOD_STREAMS = 4
LANES = 128
SUBLANES = 8
BF16_ROWS = 16
ROT_DIM = 64
ROT_QUARTER = ROT_DIM // 4
CONV_W = 31
CONV_HALO = 16
CHUNK = 128
GM_GROUPS = 4
BRANCH_W = 512
MLA_HEADS = 4
MLA_NOPE = 128
MLA_V = 128
MLA_QK = 256
MLA_SUM_ROWS = 16
MLA_GROUP_Q = 512
MLA_GROUPS = 4
LOG2_E = math.log2(math.e)
MLA_SCALE = (MLA_NOPE + ROT_DIM) ** -0.5 * LOG2_E
SWA_HEADS = 8
SWA_KV = 2
SWA_GROUP = SWA_HEADS // SWA_KV
SWA_BLOCK = 128
SWA_WINDOW_BLOCKS = 4
SWA_SUM_ROWS = 16
WINDOW = 128
SWA_SCALE = ROT_DIM ** -0.5 * LOG2_E
IN_WIDTHS = (1024, 1024, 896, 768)
D_FF_PAD = 5632
FF_CHUNK = 512
ROW_BLOCK = 256
VMEM_LIMIT = 56 * 1024 * 1024
VMEM_LIMIT_FFN = 62 * 1024 * 1024


def _cparams(sem, vmem_limit=VMEM_LIMIT):
    return pltpu.CompilerParams(dimension_semantics=sem, vmem_limit_bytes=vmem_limit)


def _inv_rms(x):
    return lax.rsqrt(jnp.mean(x * x, axis=-1, keepdims=True) + EPS)


def _rms(x):
    return x * _inv_rms(x)


def _layernorm(x, g, b):
    mu = jnp.mean(x, axis=-1, keepdims=True)
    xc = x - mu
    var = jnp.mean(xc * xc, axis=-1, keepdims=True)
    return xc * lax.rsqrt(var + EPS) * g + b


def _silu(x):
    return x * jax.nn.sigmoid(x)


def _part_specs(parts, tm, tile_of):
    specs, firsts, first = [], [], 0
    for part in parts:
        n = part.shape[0] // tm
        specs.append(pl.BlockSpec((tm, part.shape[1]),
                                  lambda *g, first=first, n=n: (jnp.clip(tile_of(*g) - first, 0, n - 1), 0)))
        firsts.append(first)
        first += n
    return specs, tuple(firsts)


def _load_parts(refs, firsts, tile):
    x = refs[0][...]
    for ref, first in zip(refs[1:], firsts[1:]):
        x = jnp.where(tile >= first, ref[...], x)
    return x


def _for_row_blocks(n_rows, body):
    def step(r, carry):
        body(pl.ds(pl.multiple_of(r * ROW_BLOCK, ROW_BLOCK), ROW_BLOCK))
        return carry

    lax.fori_loop(0, n_rows // ROW_BLOCK, step, 0)


def _dot(a, b):
    return jnp.dot(a, b, preferred_element_type=F32)


def _dot_nt(a, b):
    return lax.dot_general(a, b, (((1,), (1,)), ((), ())), preferred_element_type=F32)


def _cast_pad_cols_kernel(w_ref, o_ref):
    n = w_ref.shape[1]
    o_ref[:, :n] = w_ref[...].astype(BF16)
    o_ref[:, n:] = jnp.zeros((o_ref.shape[0], o_ref.shape[1] - n), BF16)


def _cast_pad_cols(w, l, s, n_pad, tr=256):
    m, n = w.shape[2:]
    return pl.pallas_call(
        _cast_pad_cols_kernel,
        grid=(m // tr,),
        in_specs=[pl.BlockSpec((None, None, tr, n), lambda i: (l, s, i, 0))],
        out_specs=pl.BlockSpec((tr, n_pad), lambda i: (i, 0)),
        out_shape=jax.ShapeDtypeStruct((m, n_pad), BF16),
        compiler_params=_cparams(("parallel",)),
    )(w)


def _cast_pad_rows_kernel(w_ref, o_ref, *, rows_valid):
    tr = w_ref.shape[0]
    row = pl.program_id(0) * tr + lax.broadcasted_iota(jnp.int32, w_ref.shape, 0)
    o_ref[...] = jnp.where(row < rows_valid, w_ref[...], 0.0).astype(BF16)


def _cast_pad_rows(w, l, s, m_pad, tr=512):
    m, n = w.shape[2:]
    return pl.pallas_call(
        functools.partial(_cast_pad_rows_kernel, rows_valid=m),
        grid=(m_pad // tr,),
        in_specs=[pl.BlockSpec((None, None, tr, n), lambda j: (l, s, j, 0))],
        out_specs=pl.BlockSpec((tr, n), lambda j: (j, 0)),
        out_shape=jax.ShapeDtypeStruct((m_pad, n), BF16),
        compiler_params=_cparams(("parallel",)),
    )(w)


def _mod_kernel(cond_ref, b_ref, *refs):
    w_refs, o_ref = refs[:-1], refs[-1]
    k = pl.program_id(1)

    @pl.when(k == 0)
    def _():
        o_ref[...] = jnp.broadcast_to(b_ref[...], o_ref.shape)

    s = _silu(cond_ref[...]).astype(BF16)
    part = o_ref.shape[1] // len(w_refs)
    for j, w_ref in enumerate(w_refs):
        o_ref[:, j * part:(j + 1) * part] += _dot(s, w_ref[...].astype(BF16))


def _modulation(cond, w_mod, b_mod, tk=256):
    n_layers, d, n = w_mod.shape
    part = n // MOD_STREAMS
    w_specs = [pl.BlockSpec((None, tk, part), lambda l, k, j=j: (l, k, j)) for j in range(MOD_STREAMS)]
    return pl.pallas_call(
        _mod_kernel,
        grid=(n_layers, d // tk),
        in_specs=[
            pl.BlockSpec((COND_ROWS, tk), lambda l, k: (0, k)),
            pl.BlockSpec((None, 1, n), lambda l, k: (l, 0, 0)),
            *w_specs,
        ],
        out_specs=pl.BlockSpec((None, COND_ROWS, n), lambda l, k: (l, 0, 0)),
        out_shape=jax.ShapeDtypeStruct((n_layers, COND_ROWS, n), F32),
        compiler_params=_cparams(("parallel", "arbitrary")),
    )(cond, b_mod.reshape(n_layers, 1, n), *([w_mod] * MOD_STREAMS))


def _ffn_kernel(*refs, cast_next, d_ff):
    x_ref, shift_ref, scale_ref, gate_ref, gpre_ref, gpost_ref, wg_ref, wu_ref, wd_ref = refs[:9]
    if cast_next:
        wgn_ref, wun_ref, wdn_ref, o_ref, wgo_ref, wuo_ref, wdo_ref, h_ref, r_ref = refs[9:]
    else:
        o_ref, h_ref, r_ref = refs[9:]
    f = pl.program_id(1)

    @pl.when(f == 0)
    def _():
        gain = gpre_ref[...] * (1.0 + scale_ref[...])
        shift = shift_ref[...]

        def norms(rows):
            r_ref[rows, :] = _inv_rms(x_ref[rows, :])

        def block(rows):
            h_ref[rows, :] = (x_ref[rows, :] * r_ref[rows, :] * gain + shift).astype(BF16)
            o_ref[rows, :] = jnp.zeros((ROW_BLOCK, o_ref.shape[1]), F32)

        _for_row_blocks(x_ref.shape[0], norms)
        _for_row_blocks(x_ref.shape[0], block)

    h = h_ref[...]
    g = _dot(h, wg_ref[...])
    u = _dot(h, wu_ref[...])
    a = (_silu(g) * u).astype(BF16)
    o_ref[...] += _dot(a, wd_ref[...])

    if cast_next:
        for src_ref, dst_ref in ((wgn_ref, wgo_ref), (wun_ref, wuo_ref)):
            dst_ref[:, :d_ff] = src_ref[...].astype(BF16)
            dst_ref[:, d_ff:] = jnp.zeros((dst_ref.shape[0], dst_ref.shape[1] - d_ff), BF16)
        rd = wdo_ref.shape[0]
        step = pl.program_id(0) * pl.num_programs(1) + f
        row = jnp.minimum(step, cast_next - 1) * rd + lax.broadcasted_iota(jnp.int32, wdo_ref.shape, 0)
        wdo_ref[...] = jnp.where(row < d_ff, wdn_ref[...], 0.0).astype(BF16)

    @pl.when(f == pl.num_programs(1) - 1)
    def _():
        gain = 0.5 * gate_ref[...] * gpost_ref[...]

        def norms(rows):
            r_ref[rows, :] = _inv_rms(o_ref[rows, :])

        def block(rows):
            o_ref[rows, :] = x_ref[rows, :] + o_ref[rows, :] * r_ref[rows, :] * gain

        _for_row_blocks(x_ref.shape[0], norms)
        _for_row_blocks(x_ref.shape[0], block)


def _slab_rows(total, n_steps):
    slab = BF16_ROWS
    while total % slab or total // slab > n_steps:
        slab *= 2
        assert slab <= total
    return slab


def _ffn(x, rows, mod, j, row_of_tile, gpre, gpost, wts, tm, cast_next=None):
    wg, wu, wd = wts
    d = x.shape[1]
    f_pad = wg.shape[-1]
    tf = FF_CHUNK
    nf = f_pad // tf

    def mspec(k):
        return pl.BlockSpec((None, None, 1, d), lambda i, f: (row_of_tile(i), k, 0, 0))

    vec = pl.BlockSpec((1, d), lambda i, f: (0, 0))
    in_specs = [
        pl.BlockSpec((tm, d), lambda i, f: (i, 0)),
        mspec(3 * j), mspec(3 * j + 1), mspec(3 * j + 2),
        vec, vec,
        pl.BlockSpec((d, tf), lambda i, f: (0, f)),
        pl.BlockSpec((d, tf), lambda i, f: (0, f)),
        pl.BlockSpec((tf, d), lambda i, f: (f, 0)),
    ]
    args = [x, mod, mod, mod, gpre, gpost, wg, wu, wd]
    out_specs = [pl.BlockSpec((tm, d), lambda i, f: (i, 0))]
    out_shape = [jax.ShapeDtypeStruct((rows, d), F32)]
    down_slabs, d_ff = 0, f_pad
    if cast_next is not None:
        wg32, wu32, wd32, l2, s2 = cast_next
        d_ff = wg32.shape[-1]
        n_steps = (rows // tm) * nf
        rg, rd = _slab_rows(d, n_steps), _slab_rows(f_pad, n_steps)
        up_slabs, down_slabs, down_src_slabs = d // rg, f_pad // rd, pl.cdiv(d_ff, rd)

        def slab(count):
            return lambda i, f: jnp.minimum(i * nf + f, count - 1)

        in_specs += [pl.BlockSpec((None, None, rg, d_ff), lambda i, f: (l2, s2, slab(up_slabs)(i, f), 0))] * 2
        in_specs += [pl.BlockSpec((None, None, rd, d), lambda i, f: (l2, s2, slab(down_src_slabs)(i, f), 0))]
        args += [wg32, wu32, wd32]
        out_specs += [pl.BlockSpec((rg, f_pad), lambda i, f: (slab(up_slabs)(i, f), 0))] * 2
        out_specs += [pl.BlockSpec((rd, d), lambda i, f: (slab(down_slabs)(i, f), 0))]
        out_shape += [jax.ShapeDtypeStruct((d, f_pad), BF16)] * 2 + [jax.ShapeDtypeStruct((f_pad, d), BF16)]
    outs = pl.pallas_call(
        functools.partial(_ffn_kernel, cast_next=down_slabs, d_ff=d_ff),
        grid=(rows // tm, nf),
        in_specs=in_specs,
        out_specs=out_specs,
        out_shape=out_shape,
        scratch_shapes=[pltpu.VMEM((tm, d), BF16), pltpu.VMEM((tm, 1), F32)],
        compiler_params=_cparams(("parallel", "arbitrary"), VMEM_LIMIT_FFN),
    )(*args)
    return outs[0] if cast_next is None else (outs[0], tuple(outs[1:]))


def _inproj_kernel(*refs, x_firsts):
    x_refs = refs[:len(x_firsts)]
    (shift_ref, scale_ref, gpre_ref, w_ref, b_ref, cos_ref, sina_ref, sinb_ref, qn_ref, kvn_ref, wuq_ref, wukv_ref,
     lng_ref, lnb_ref, ws_ref, bs_ref,
     h_ref, ya_ref, hb_ref, qc_ref, kc_ref, vc_ref, qd_ref, kd_ref, vd_ref) = refs[len(x_firsts):]
    y = _rms(_load_parts(x_refs, x_firsts, pl.program_id(0))) * gpre_ref[...]
    h = (y * (1.0 + scale_ref[...]) + shift_ref[...]).astype(BF16)
    h_ref[...] = h

    def proj(lo, hi):
        return _dot(h, w_ref[:, lo:hi]) + b_ref[:, lo:hi]

    na, nb, nc, nd = IN_WIDTHS
    _qkv_prep(proj(na + nb, na + nb + nc), proj(na + nb + nc, na + nb + nc + nd),
              cos_ref, sina_ref, sinb_ref, qn_ref, kvn_ref, wuq_ref, wukv_ref,
              qc_ref, kc_ref, vc_ref, qd_ref, kd_ref, vd_ref)
    w = BRANCH_W
    zb = proj(na, na + nb)
    hb_ref[...] = zb[:, :w] * jax.nn.sigmoid(zb[:, w:])
    za = proj(0, na)
    g = 0.5 * za * (1.0 + lax.erf(za * (2.0 ** -0.5)))
    vln = _layernorm(g[:, w:], lng_ref[...], lnb_ref[...]).astype(BF16)
    for c in range(za.shape[0] // CHUNK):
        rows = slice(c * CHUNK, (c + 1) * CHUNK)
        for gi in range(GM_GROUPS):
            cols = slice(gi * CHUNK, (gi + 1) * CHUNK)
            sv = _dot(ws_ref[gi], vln[rows, cols]) + bs_ref[gi]
            ya_ref[rows, cols] = (g[rows, cols] * sv).astype(BF16)


def _inproj(x_parts, mod, row_of_tile, gpre, w, b, cos, sina, sinb, qn, kvn, wuq, wukv, lng, lnb, ws, bs, tm):
    rows, d = sum(part.shape[0] for part in x_parts), x_parts[0].shape[1]
    x_specs, x_firsts = _part_specs(x_parts, tm, lambda i: i)
    n = w.shape[1]

    def mspec(k):
        return pl.BlockSpec((None, None, 1, d), lambda i: (row_of_tile(i), k, 0, 0))

    def rowspec(wd):
        return pl.BlockSpec((tm, wd), lambda i: (i, 0))

    def full(a):
        return pl.BlockSpec(a.shape, lambda i: (0,) * a.ndim)

    def headspec(nh, wd):
        return pl.BlockSpec((nh, tm, wd), lambda i: (0, i, 0))

    def headspec_t(nh, wd):
        return pl.BlockSpec((nh, wd, tm), lambda i: (0, 0, i))

    kc = jax.ShapeDtypeStruct((MLA_HEADS, rows, MLA_QK), BF16)
    qc = jax.ShapeDtypeStruct((MLA_HEADS, MLA_QK, rows), BF16)
    vc = jax.ShapeDtypeStruct((MLA_HEADS, MLA_V + MLA_SUM_ROWS, rows), BF16)
    qd = jax.ShapeDtypeStruct((SWA_HEADS // 2, LANES, rows), BF16)
    kd = jax.ShapeDtypeStruct((rows, LANES), BF16)
    vd = jax.ShapeDtypeStruct((LANES, rows), BF16)
    outs = [jax.ShapeDtypeStruct((rows, d), BF16), jax.ShapeDtypeStruct((rows, BRANCH_W), BF16),
            jax.ShapeDtypeStruct((rows, BRANCH_W), F32)]
    ospecs = [rowspec(d), rowspec(BRANCH_W), rowspec(BRANCH_W)]
    return pl.pallas_call(
        functools.partial(_inproj_kernel, x_firsts=x_firsts),
        grid=(rows // tm,),
        in_specs=[
            *x_specs,
            mspec(3), mspec(4),
            pl.BlockSpec((1, d), lambda i: (0, 0)),
            pl.BlockSpec((d, n), lambda i: (0, 0), pipeline_mode=pl.Buffered(1)),
            pl.BlockSpec((1, n), lambda i: (0, 0)),
            rowspec(LANES), rowspec(LANES), rowspec(LANES), full(qn), full(kvn), full(wuq), full(wukv),
            full(lng), full(lnb), full(ws), full(bs),
        ],
        out_specs=ospecs + [headspec_t(MLA_HEADS, MLA_QK), headspec(MLA_HEADS, MLA_QK),
                            headspec_t(MLA_HEADS, MLA_V + MLA_SUM_ROWS), headspec_t(SWA_HEADS // 2, LANES),
                            rowspec(LANES), pl.BlockSpec((LANES, tm), lambda i: (0, i))],
        out_shape=outs + [qc, kc, vc, qd, kd, vd],
        compiler_params=_cparams(("parallel",)),
    )(*x_parts, mod, mod, gpre, w, b, cos, sina, sinb, qn, kvn, wuq, wukv, lng, lnb, ws, bs)


def _conv_kernel(h_ref, hprev_ref, hnext_ref, dw_ref, db_ref, lng_ref, lnb_ref, yb_ref, ext_ref, shift_ref,
                 *, ts, n_lat_tiles, lat_tiles_per_seq, ctx_tiles_per_seq):
    i = pl.program_id(0)
    in_lat = i < n_lat_tiles
    pos = jnp.where(in_lat, i % lat_tiles_per_seq, (i - n_lat_tiles) % ctx_tiles_per_seq)
    last = jnp.where(in_lat, lat_tiles_per_seq - 1, ctx_tiles_per_seq - 1)
    ext_ref[0:CONV_HALO, :] = jnp.where(pos != 0, hprev_ref[...], 0.0)
    ext_ref[CONV_HALO:CONV_HALO + ts, :] = h_ref[...]
    ext_ref[CONV_HALO + ts:, :] = jnp.where(pos != last, hnext_ref[...], 0.0)
    span = ts + 2 * CONV_HALO - SUBLANES
    for r in range(1, SUBLANES):
        shift_ref[r - 1, 0:span, :] = ext_ref[r:r + span, :]
    first_tap = CONV_HALO - CONV_W // 2
    acc = jnp.zeros((ts, BRANCH_W), F32) + db_ref[...]
    for k in range(CONV_W):
        a, r = divmod(first_tap + k, SUBLANES)
        win = ext_ref[a * SUBLANES:a * SUBLANES + ts, :] if r == 0 else shift_ref[r - 1, a * SUBLANES:a * SUBLANES + ts, :]
        acc = acc + dw_ref[k:k + 1, :] * win
    yb_ref[...] = _silu(_layernorm(acc, lng_ref[...], lnb_ref[...])).astype(BF16)


def _conv_mixer(hb, dw, db, lng, lnb, n_lat_rows, seq_lat, seq_ctx, ts):
    rows, w = hb.shape
    halo_blocks = ts // CONV_HALO
    n_halo = rows // CONV_HALO
    kern = functools.partial(_conv_kernel, ts=ts, n_lat_tiles=n_lat_rows // ts,
                             lat_tiles_per_seq=seq_lat // ts, ctx_tiles_per_seq=seq_ctx // ts)
    vec = pl.BlockSpec((1, w), lambda i: (0, 0))
    return pl.pallas_call(
        kern,
        grid=(rows // ts,),
        in_specs=[
            pl.BlockSpec((ts, w), lambda i: (i, 0)),
            pl.BlockSpec((CONV_HALO, w), lambda i: (jnp.maximum(i * halo_blocks - 1, 0), 0)),
            pl.BlockSpec((CONV_HALO, w), lambda i: (jnp.minimum((i + 1) * halo_blocks, n_halo - 1), 0)),
            pl.BlockSpec((CONV_W, w), lambda i: (0, 0)),
            vec, vec, vec,
        ],
        out_specs=pl.BlockSpec((ts, w), lambda i: (i, 0)),
        out_shape=jax.ShapeDtypeStruct((rows, w), BF16),
        scratch_shapes=[pltpu.VMEM((ts + 2 * CONV_HALO, w), F32),
                        pltpu.VMEM((SUBLANES - 1, ts + 2 * CONV_HALO - SUBLANES, w), F32)],
        compiler_params=_cparams(("parallel",)),
    )(hb, hb, hb, dw, db, lng, lnb)


def _qkv_prep(zc, zd, cos_ref, sina_ref, sinb_ref, qn_ref, kvn_ref, wuq_ref, wukv_ref,
              qc_ref, kc_ref, vc_ref, qd_ref, kd_ref, vd_ref):
    cos, sina, sinb = cos_ref[...], sina_ref[...], sinb_ref[...]
    low = lax.broadcasted_iota(jnp.int32, cos.shape, 1) < ROT_DIM

    def rope(slab):
        return (slab * cos + pltpu.roll(slab, LANES - ROT_QUARTER, 1) * sina
                + pltpu.roll(slab, ROT_QUARTER, 1) * sinb)

    nq = MLA_HEADS * MLA_NOPE
    q = _dot((_rms(zc[:, :512]) * qn_ref[...]).astype(BF16), wuq_ref[...]) * MLA_SCALE
    kv = _dot((_rms(zc[:, 512:768]) * kvn_ref[...]).astype(BF16), wukv_ref[...])
    kr = rope(zc[:, 768:896])
    ones = jnp.ones((MLA_SUM_ROWS, zc.shape[0]), BF16)
    for pair in range(MLA_HEADS // 2):
        qr = rope(q[:, nq + pair * LANES:nq + (pair + 1) * LANES]).T.astype(BF16)
        for h in (2 * pair, 2 * pair + 1):
            qc_ref[h, 0:MLA_NOPE, :] = q[:, h * MLA_NOPE:(h + 1) * MLA_NOPE].T.astype(BF16)
            qc_ref[h, MLA_NOPE:, :] = qr
            kc_ref[h, :, 0:MLA_NOPE] = kv[:, h * MLA_NOPE:(h + 1) * MLA_NOPE].astype(BF16)
            keep = low if h % 2 == 0 else jnp.logical_not(low)
            kc_ref[h, :, MLA_NOPE:] = jnp.where(keep, kr, 0.0).astype(BF16)
            vc_ref[h, 0:MLA_V, :] = kv[:, nq + h * MLA_V:nq + (h + 1) * MLA_V].T.astype(BF16)
            vc_ref[h, MLA_V:, :] = ones

    for pair in range(SWA_HEADS // 2):
        qd_ref[pair] = (rope(zd[:, pair * LANES:(pair + 1) * LANES]) * SWA_SCALE).T.astype(BF16)
    kd_ref[...] = rope(zd[:, 512:640]).astype(BF16)
    vd_ref[...] = zd[:, 640:768].T.astype(BF16)


def _mla_kernel(*refs, has_lat, groups):
    if has_lat:
        qt_ref, kl_ref, vlt_ref, kx_ref, vxt_ref, o_ref = refs
    else:
        qt_ref, kx_ref, vxt_ref, o_ref = refs
    cols = qt_ref.shape[1] // groups
    scores = []
    for g in range(groups):
        qt = qt_ref[:, g * cols:(g + 1) * cols]
        scores.append((_dot(kx_ref[...], qt), _dot(kl_ref[...], qt) if has_lat else None))
    for g, (sx, sl) in enumerate(scores):
        m = jnp.max(sx, axis=0, keepdims=True)
        if has_lat:
            m = jnp.maximum(m, jnp.max(sl, axis=0, keepdims=True))
        ot = _dot(vxt_ref[...], jnp.exp2(sx - m).astype(BF16))
        if has_lat:
            ot = ot + _dot(vlt_ref[...], jnp.exp2(sl - m).astype(BF16))
        o_ref[g * cols:(g + 1) * cols, :] = (ot[:MLA_V] / ot[MLA_V:MLA_V + 1]).T.astype(BF16)


def _mla_attention(qc, kc, vc, batch, seq_lat, seq_ctx, lat_queries):
    n_lat_rows = batch * seq_lat
    ctx_blk0 = n_lat_rows // seq_ctx
    tq = math.gcd(MLA_GROUPS * MLA_GROUP_Q, seq_lat if lat_queries else seq_ctx)
    if lat_queries:
        nq, q0 = seq_lat // tq, 0
    else:
        nq, q0 = seq_ctx // tq, n_lat_rows // tq
    vrows = vc.shape[1]
    qspec = pl.BlockSpec((None, MLA_QK, tq), lambda b, h, i: (h, 0, q0 + b * nq + i))
    kx = pl.BlockSpec((None, seq_ctx, MLA_QK), lambda b, h, i: (h, ctx_blk0 + b, 0))
    vx = pl.BlockSpec((None, vrows, seq_ctx), lambda b, h, i: (h, 0, ctx_blk0 + b))
    if lat_queries:
        kl = pl.BlockSpec((None, seq_lat, MLA_QK), lambda b, h, i: (h, b, 0))
        vl = pl.BlockSpec((None, vrows, seq_lat), lambda b, h, i: (h, 0, b))
        in_specs, args = [qspec, kl, vl, kx, vx], (qc, kc, vc, kc, vc)
    else:
        in_specs, args = [qspec, kx, vx], (qc, kc, vc)
    return pl.pallas_call(
        functools.partial(_mla_kernel, has_lat=lat_queries, groups=max(1, tq // MLA_GROUP_Q)),
        grid=(batch, MLA_HEADS, nq),
        in_specs=in_specs,
        out_specs=pl.BlockSpec((tq, MLA_V), lambda b, h, i: (b * nq + i, h)),
        out_shape=jax.ShapeDtypeStruct((batch * nq * tq, MLA_HEADS * MLA_V), BF16),
        compiler_params=_cparams(("parallel", "parallel", "parallel")),
    )(*args)


def _swa_kernel(*refs, has_lat):
    if has_lat:
        qt_ref, kx_ref, vxt_ref, sink_ref, bias_ref = refs[:5]
        kw_refs, vw_refs, o_ref = refs[5:5 + SWA_WINDOW_BLOCKS], refs[5 + SWA_WINDOW_BLOCKS:-1], refs[-1]
    else:
        qt_ref, kx_ref, vxt_ref, sink_ref, o_ref = refs
    blocks = qt_ref.shape[2] // SWA_BLOCK
    half = LANES // 2
    ncol = blocks * SWA_GROUP * SWA_BLOCK
    zeros = jnp.zeros((half, ncol), BF16)
    if has_lat:
        k_win = jnp.concatenate([r[...] for r in kw_refs], axis=0)
        v_win = jnp.concatenate([r[...] for r in vw_refs], axis=1)
    scores = []
    for kv in range(SWA_KV):
        parts = []
        for blk in range(blocks):
            for g in range(SWA_GROUP):
                h = kv * SWA_GROUP + g
                parts.append(qt_ref[h // 2, (h % 2) * half:(h % 2 + 1) * half, blk * SWA_BLOCK:(blk + 1) * SWA_BLOCK])
        qt = jnp.concatenate(parts, axis=1)
        qt = jnp.concatenate([qt, zeros] if kv == 0 else [zeros, qt], axis=0)
        scores.append((_dot(kx_ref[...], qt), _dot(k_win, qt) if has_lat else None))
    for kv, (sx, sl) in enumerate(scores):
        rows = slice(kv * half, (kv + 1) * half)
        sink = sink_ref[kv] * LOG2_E
        m = jnp.maximum(sink, jnp.max(sx, axis=0, keepdims=True))
        if has_lat:
            sl = sl + bias_ref[...]
            m = jnp.maximum(m, jnp.max(sl, axis=0, keepdims=True))
        ones = jnp.ones((SWA_SUM_ROWS, 1), BF16)
        vx = vxt_ref[rows, :]
        ot = _dot(jnp.concatenate([vx, jnp.broadcast_to(ones, (SWA_SUM_ROWS, vx.shape[1]))], axis=0),
                  jnp.exp2(sx - m).astype(BF16))
        if has_lat:
            vl = v_win[rows, :]
            ot = ot + _dot(jnp.concatenate([vl, jnp.broadcast_to(ones, (SWA_SUM_ROWS, vl.shape[1]))], axis=0),
                           jnp.exp2(sl - m).astype(BF16))
        o = ot[:half] / (ot[half:half + 1] + jnp.exp2(sink - m))
        for blk in range(blocks):
            for pair in range(SWA_GROUP // 2):
                c0 = (blk * SWA_GROUP + 2 * pair) * SWA_BLOCK
                slab_t = jnp.concatenate([o[:, c0:c0 + SWA_BLOCK], o[:, c0 + SWA_BLOCK:c0 + 2 * SWA_BLOCK]], axis=0)
                lane0 = (kv * (SWA_GROUP // 2) + pair) * LANES
                o_ref[blk * SWA_BLOCK:(blk + 1) * SWA_BLOCK, lane0:lane0 + LANES] = slab_t.T.astype(BF16)


def _swa_window_bias(blocks):
    nkeys = SWA_WINDOW_BLOCKS * SWA_BLOCK
    ncol = blocks * SWA_GROUP * SWA_BLOCK
    key = jnp.arange(nkeys)[:, None] - SWA_BLOCK
    col = jnp.arange(ncol)[None, :]
    qry = (col // (SWA_GROUP * SWA_BLOCK)) * SWA_BLOCK + (col % SWA_BLOCK)
    inside = jnp.abs(key - qry) <= WINDOW
    after_start = key >= 0
    before_end = key < blocks * SWA_BLOCK
    variants = [inside, inside & after_start, inside & before_end, inside & after_start & before_end]
    return jnp.where(jnp.stack(variants), 0.0, -jnp.inf).astype(F32)


def _swa_attention(qd, kd, vd, sink, batch, seq_lat, seq_ctx, lat_queries):
    n_lat_rows = batch * seq_lat
    ctx_blk0 = n_lat_rows // seq_ctx
    seq_q = seq_lat if lat_queries else seq_ctx
    blocks = 2 if seq_q % (2 * SWA_BLOCK) == 0 and n_lat_rows % (2 * SWA_BLOCK) == 0 else 1
    tq = blocks * SWA_BLOCK
    nb = seq_q // tq
    q0 = 0 if lat_queries else n_lat_rows // tq
    kblocks = seq_lat // SWA_BLOCK
    sink_row = jnp.tile(jnp.repeat(sink.reshape(SWA_KV, SWA_GROUP), SWA_BLOCK, axis=1), (1, blocks))[:, None, :]
    in_specs = [pl.BlockSpec((SWA_HEADS // 2, LANES, tq), lambda b, i: (0, 0, q0 + b * nb + i)),
                pl.BlockSpec((seq_ctx, LANES), lambda b, i: (ctx_blk0 + b, 0)),
                pl.BlockSpec((LANES, seq_ctx), lambda b, i: (0, ctx_blk0 + b)),
                pl.BlockSpec(sink_row.shape, lambda b, i: (0, 0, 0))]
    args = [qd, kd, vd, sink_row]
    if lat_queries:
        assert blocks + 2 == SWA_WINDOW_BLOCKS
        bias = _swa_window_bias(blocks)
        in_specs.append(pl.BlockSpec((None,) + bias.shape[1:],
                                     lambda b, i: ((i == 0).astype(jnp.int32) + 2 * (i == nb - 1).astype(jnp.int32), 0, 0)))
        args.append(bias)

        def win(j):
            return lambda b, i: b * kblocks + jnp.clip(i * blocks - 1 + j, 0, kblocks - 1)

        in_specs += [pl.BlockSpec((SWA_BLOCK, LANES), lambda b, i, j=j: (win(j)(b, i), 0))
                     for j in range(SWA_WINDOW_BLOCKS)]
        in_specs += [pl.BlockSpec((LANES, SWA_BLOCK), lambda b, i, j=j: (0, win(j)(b, i)))
                     for j in range(SWA_WINDOW_BLOCKS)]
        args += [kd] * SWA_WINDOW_BLOCKS + [vd] * SWA_WINDOW_BLOCKS
    return pl.pallas_call(
        functools.partial(_swa_kernel, has_lat=lat_queries),
        grid=(batch, nb),
        in_specs=in_specs,
        out_specs=pl.BlockSpec((tq, BRANCH_W), lambda b, i: (b * nb + i, 0)),
        out_shape=jax.ShapeDtypeStruct((batch * nb * tq, BRANCH_W), BF16),
        compiler_params=_cparams(("parallel", "parallel")),
    )(*args)


def _gate_kernel(*refs, y_firsts):
    n = len(y_firsts)
    h_ref, ya_ref, yb_ref = refs[:3]
    yc_refs, yd_refs = refs[3:3 + n], refs[3 + n:3 + 2 * n]
    wg0, wg1, wg2, wg3, bg_ref, wb0, wb1, wb2, wb3, o_ref = refs[3 + 2 * n:]
    tile = pl.program_id(1)
    ys = (ya_ref[...], yb_ref[...], _load_parts(yc_refs, y_firsts, tile), _load_parts(yd_refs, y_firsts, tile))
    h = h_ref[...]
    acc = None
    for b, (y, wg_ref, wb_ref) in enumerate(zip(ys, (wg0, wg1, wg2, wg3), (wb0, wb1, wb2, wb3))):
        gate = jax.nn.sigmoid(_dot(h, wg_ref[...]) + bg_ref[b])
        term = gate * _dot(y, wb_ref[...])
        acc = term if acc is None else acc + term
    o_ref[...] = acc.astype(BF16)


def _gated_merge(h, ya, yb, yc_parts, yd_parts, wg, bg, wbr, rows, tm, tn):
    d = h.shape[1]
    ncol = d // tn
    wg_specs = [pl.BlockSpec((d, tn), lambda j, i, b=b: (0, b * ncol + j)) for b in range(4)]
    wb_specs = [pl.BlockSpec((None, BRANCH_W, tn), lambda j, i, b=b: (b, 0, j)) for b in range(4)]
    yspec = pl.BlockSpec((tm, BRANCH_W), lambda j, i: (i, 0))
    yc_specs, y_firsts = _part_specs(yc_parts, tm, lambda j, i: i)
    yd_specs, _ = _part_specs(yd_parts, tm, lambda j, i: i)
    return pl.pallas_call(
        functools.partial(_gate_kernel, y_firsts=y_firsts),
        grid=(ncol, rows // tm),
        in_specs=[pl.BlockSpec((tm, d), lambda j, i: (i, 0)), yspec, yspec, *yc_specs, *yd_specs,
                  *wg_specs, pl.BlockSpec((4, 1, tn), lambda j, i: (0, 0, j)), *wb_specs],
        out_specs=pl.BlockSpec((tm, tn), lambda j, i: (i, j)),
        out_shape=jax.ShapeDtypeStruct((rows, d), BF16),
        compiler_params=_cparams(("parallel", "parallel")),
    )(h, ya, yb, *yc_parts, *yd_parts, wg, wg, wg, wg, bg, wbr, wbr, wbr, wbr)


def _outproj_kernel(*refs, x_firsts):
    a_ref, x_refs = refs[0], refs[1:1 + len(x_firsts)]
    gate_ref, gpost_ref, w_ref, o_ref = refs[1 + len(x_firsts):]
    y = _dot(a_ref[...], w_ref[...])
    o_ref[...] = _load_parts(x_refs, x_firsts, pl.program_id(0)) + gate_ref[...] * (_rms(y) * gpost_ref[...])


def _outproj(a, x_parts, mod, row_of_tile, gpost, w, tm):
    rows, d = a.shape
    x_specs, x_firsts = _part_specs(x_parts, tm, lambda i: i)
    return pl.pallas_call(
        functools.partial(_outproj_kernel, x_firsts=x_firsts),
        grid=(rows // tm,),
        in_specs=[
            pl.BlockSpec((tm, d), lambda i: (i, 0)),
            *x_specs,
            pl.BlockSpec((None, None, 1, d), lambda i: (row_of_tile(i), 5, 0, 0)),
            pl.BlockSpec((1, d), lambda i: (0, 0)),
            pl.BlockSpec((d, d), lambda i: (0, 0), pipeline_mode=pl.Buffered(1)),
        ],
        out_specs=pl.BlockSpec((tm, d), lambda i: (i, 0)),
        out_shape=jax.ShapeDtypeStruct((rows, d), F32),
        compiler_params=_cparams(("parallel",)),
    )(a, *x_parts, mod, gpost, w)


def _rope_tables(seq_lat, n_ctx_rows, batch):
    half = ROT_DIM // 2
    freqs = 1.0 / (ROPE_BASE ** (jnp.arange(0, half, 2, dtype=F32) / half))
    t = jnp.arange(seq_lat)
    ang_r = (t // GRID_W).astype(F32)[:, None] * freqs[None, :]
    ang_c = (t % GRID_W).astype(F32)[:, None] * freqs[None, :]
    ang = jnp.concatenate([ang_r, ang_r, ang_c, ang_c], axis=-1)
    ang = jnp.tile(ang, (batch, LANES // ROT_DIM))
    cos, sin = jnp.cos(ang), jnp.sin(ang)
    first = (jnp.arange(LANES) % half) < ROT_QUARTER
    sina = jnp.where(first, -sin, 0.0)
    sinb = jnp.where(first, 0.0, sin)
    ones = jnp.ones((n_ctx_rows, LANES), F32)
    zeros = jnp.zeros((n_ctx_rows, LANES), F32)
    return (jnp.concatenate([cos, ones]), jnp.concatenate([sina, zeros]), jnp.concatenate([sinb, zeros]))


def _pack_layer(l, w_in, b_in, gm_ws, gm_bs, mla_w_uq, mla_w_ukv, swa_sink, w_branch, w_out):
    d = w_in.shape[1]
    p = {}
    wi, bi = w_in[l], b_in[l]
    c_hi = 2048 + 832
    kr = slice(c_hi - ROT_DIM, c_hi)
    p['w_abcd'] = jnp.concatenate([wi[:, :c_hi], wi[:, kr], wi[:, c_hi:c_hi + 768]], axis=1).astype(BF16)
    p['b_abcd'] = jnp.concatenate([bi[:c_hi], bi[kr], bi[c_hi:c_hi + 768]])[None, :]
    p['w_gates'] = wi[:, c_hi + 768:].astype(BF16)
    p['b_gates'] = bi[c_hi + 768:].reshape(4, 1, d)
    p['gm_ws'] = gm_ws[l].astype(BF16)
    p['gm_bs'] = gm_bs[l][:, :, None]
    uq = mla_w_uq[l].reshape(-1, MLA_HEADS, MLA_NOPE + ROT_DIM)
    p['wuq'] = jnp.concatenate([uq[:, :, :MLA_NOPE].reshape(-1, MLA_HEADS * MLA_NOPE),
                                uq[:, :, MLA_NOPE:].reshape(-1, MLA_HEADS * ROT_DIM)], axis=1).astype(BF16)
    ukv = mla_w_ukv[l].reshape(-1, MLA_HEADS, MLA_NOPE + MLA_V)
    p['wukv'] = jnp.concatenate([ukv[:, :, :MLA_NOPE].reshape(-1, MLA_HEADS * MLA_NOPE),
                                 ukv[:, :, MLA_NOPE:].reshape(-1, MLA_HEADS * MLA_V)], axis=1).astype(BF16)
    p['sink'] = swa_sink[l]
    p['w_branch'] = w_branch[l].astype(BF16)
    p['w_out'] = w_out[l].astype(BF16)
    return p


def kernel(x, c, ctx, c_ctx, w_mod, b_mod, norm_pre, norm_post, w_ff_gate, w_ff_up, w_ff_down, w_in, b_in, gm_ln_g, gm_ln_b, gm_ws, gm_bs, cv_dw, cv_db, cv_ln_g, cv_ln_b, mla_q_norm, mla_kv_norm, mla_w_uq, mla_w_ukv, swa_sink, w_branch, w_out):
    batch, seq_lat, d = x.shape
    seq_ctx = ctx.shape[1]
    depth = w_mod.shape[0]
    n_lat = batch * seq_lat
    n_ctx = batch * seq_ctx
    assert batch + 1 <= COND_ROWS and seq_lat % GRID_W == 0
    tm = math.gcd(512, math.gcd(seq_lat, n_ctx))
    ts = math.gcd(256, math.gcd(seq_lat, seq_ctx))
    assert tm % SWA_BLOCK == 0 and ts % CHUNK == 0 and seq_ctx % SWA_BLOCK == 0 and seq_lat >= 3 * SWA_BLOCK

    tm_ffn = math.gcd(1024, math.gcd(seq_lat, n_ctx))

    def cond_row(tile):
        return lambda i: jnp.where(i < n_lat // tile, i // (seq_lat // tile), batch)

    row_of_tile, row_of_ffn_tile = cond_row(tm), cond_row(tm_ffn)

    cond = jnp.zeros((COND_ROWS, d), F32).at[:batch].set(c).at[batch].set(c_ctx)
    mod = _modulation(cond, w_mod, b_mod).reshape(depth, COND_ROWS, N_MOD, 1, d)
    cos, sina, sinb = _rope_tables(seq_lat, n_ctx, batch)

    ff32 = (w_ff_gate, w_ff_up, w_ff_down)
    wts = (_cast_pad_cols(w_ff_gate, 0, 0, D_FF_PAD), _cast_pad_cols(w_ff_up, 0, 0, D_FF_PAD),
           _cast_pad_rows(w_ff_down, 0, 0, D_FF_PAD))

    n_all = n_lat + n_ctx
    xs = None
    for l in range(depth):
        last = l == depth - 1
        p = _pack_layer(l, w_in, b_in, gm_ws, gm_bs, mla_w_uq, mla_w_ukv, swa_sink, w_branch, w_out)
        m = mod[l]
        npre, npost = norm_pre[l][:, None, :], norm_post[l][:, None, :]

        ffn1 = (m, 0, row_of_ffn_tile, npre[0], npost[0])
        if xs is None:
            x_lat, wts2 = _ffn(x.reshape(n_lat, d), n_lat, *ffn1, wts, tm_ffn, cast_next=ff32 + (l, 1))
            x_ctx = _ffn(ctx.reshape(n_ctx, d), n_ctx, m, 0, lambda i: batch, npre[0], npost[0], wts, tm_ffn)
            x_parts = (x_lat, x_ctx)
        else:
            xs, wts2 = _ffn(xs, n_all, *ffn1, wts, tm_ffn, cast_next=ff32 + (l, 1))
            x_parts = (xs,)

        h, ya, hb, qc, kc, vc, qd, kd, vd = _inproj(
            x_parts, m, row_of_tile, npre[1], p['w_abcd'], p['b_abcd'], cos, sina, sinb,
            mla_q_norm[l][None], mla_kv_norm[l][None], p['wuq'], p['wukv'],
            gm_ln_g[l][None], gm_ln_b[l][None], p['gm_ws'], p['gm_bs'], tm)
        yb = _conv_mixer(hb, cv_dw[l], cv_db[l][None], cv_ln_g[l][None], cv_ln_b[l][None],
                         n_lat, seq_lat, seq_ctx, ts)
        rows_mix = n_lat if last else n_all
        attn = (batch, seq_lat, seq_ctx)
        yc_parts = (_mla_attention(qc, kc, vc, *attn, True),)
        yd_parts = (_swa_attention(qd, kd, vd, p['sink'], *attn, True),)
        if not last:
            yc_parts += (_mla_attention(qc, kc, vc, *attn, False),)
            yd_parts += (_swa_attention(qd, kd, vd, p['sink'], *attn, False),)
        merged = _gated_merge(h, ya, yb, yc_parts, yd_parts, p['w_gates'], p['b_gates'], p['w_branch'],
                              rows_mix, tm_ffn, 512)
        xs = _outproj(merged, x_parts, m, row_of_tile, npost[1], p['w_out'], tm)

        ffn2 = (xs, rows_mix, m, 2, row_of_ffn_tile, npre[2], npost[2], wts2, tm_ffn)
        if last:
            xs = _ffn(*ffn2)
        else:
            xs, wts = _ffn(*ffn2, cast_next=ff32 + (l + 1, 0))
    return xs[:n_lat].reshape(batch, seq_lat, d)
```

```python
import functools
import math

import jax
import jax.numpy as jnp
from jax import lax
from jax.experimental import pallas as pl
from jax.experimental.pallas import tpu as pltpu

F32 = jnp.float32
BF16 = jnp.bfloat16

EPS = 1e-6
ROPE_BASE = 10000.0
GRID_W = 64
N_MOD = 9
COND_ROWS = 8
MOD_STREAMS = 4
LANES = 128
SUBLANES = 8
BF16_ROWS = 16
ROT_DIM = 64
ROT_QUARTER = ROT_DIM // 4
CONV_W = 31
CONV_HALO = 16
CHUNK = 128
GM_GROUPS = 4
BRANCH_W = 512
MLA_HEADS = 4
MLA_NOPE = 128
MLA_V = 128
MLA_Q_RANK = 512
MLA_KV_RANK = 256
MLA_QK = 256
MLA_SUM_ROWS = 16
MLA_GROUP_Q = 512
MLA_GROUPS = 4
LOG2_E = math.log2(math.e)
MLA_SCALE = (MLA_NOPE + ROT_DIM) ** -0.5 * LOG2_E
SWA_HEADS = 8
SWA_KV = 2
SWA_GROUP = SWA_HEADS // SWA_KV
SWA_BLOCK = 128
SWA_WINDOW_BLOCKS = 4
SWA_SUM_ROWS = 16
WINDOW = 128
SWA_SCALE = ROT_DIM ** -0.5 * LOG2_E
IN_WIDTHS = (1024, 1024, 896, 768)
D_FF_PAD = 5632
FF_CHUNK = 512
GATE_COLS = 512
ROW_BLOCK = 1024
VMEM_LIMIT = 56 * 1024 * 1024
VMEM_LIMIT_FFN = 62 * 1024 * 1024


def _cparams(sem, vmem_limit=VMEM_LIMIT):
    return pltpu.CompilerParams(dimension_semantics=sem, vmem_limit_bytes=vmem_limit)


def _inv_rms(x):
    return lax.rsqrt(jnp.mean(x * x, axis=-1, keepdims=True) + EPS)


def _rms(x):
    return x * _inv_rms(x)


def _layernorm(x, g, b):
    mu = jnp.mean(x, axis=-1, keepdims=True)
    xc = x - mu
    var = jnp.mean(xc * xc, axis=-1, keepdims=True)
    return xc * lax.rsqrt(var + EPS) * g + b


def _silu(x):
    return x * jax.nn.sigmoid(x)


def _part_specs(parts, tm, tile_of):
    specs, firsts, first = [], [], 0
    for part in parts:
        n = part.shape[0] // tm
        specs.append(pl.BlockSpec((tm, part.shape[1]),
                                  lambda *g, first=first, n=n: (jnp.clip(tile_of(*g) - first, 0, n - 1), 0)))
        firsts.append(first)
        first += n
    return specs, tuple(firsts)


def _load_parts(refs, firsts, tile):
    x = refs[0][...]
    for ref, first in zip(refs[1:], firsts[1:]):
        x = jnp.where(tile >= first, ref[...], x)
    return x


def _for_row_blocks(n_rows, body):
    block = min(ROW_BLOCK, n_rows)

    def step(r, carry):
        body(pl.ds(pl.multiple_of(r * block, block), block))
        return carry

    lax.fori_loop(0, n_rows // block, step, 0)


def _dot(a, b):
    return jnp.dot(a, b, preferred_element_type=F32)


def _cast_pad_cols_kernel(w_ref, o_ref):
    n = w_ref.shape[1]
    o_ref[:, :n] = w_ref[...].astype(BF16)
    o_ref[:, n:] = jnp.zeros((o_ref.shape[0], o_ref.shape[1] - n), BF16)


def _cast_pad_cols(w, l, s, n_pad, tr=256):
    m, n = w.shape[2:]
    return pl.pallas_call(
        _cast_pad_cols_kernel,
        grid=(m // tr,),
        in_specs=[pl.BlockSpec((None, None, tr, n), lambda i: (l, s, i, 0))],
        out_specs=pl.BlockSpec((tr, n_pad), lambda i: (i, 0)),
        out_shape=jax.ShapeDtypeStruct((m, n_pad), BF16),
        compiler_params=_cparams(("parallel",)),
    )(w)


def _cast_pad_rows_kernel(w_ref, o_ref, *, rows_valid):
    tr = w_ref.shape[0]
    row = pl.program_id(0) * tr + lax.broadcasted_iota(jnp.int32, w_ref.shape, 0)
    o_ref[...] = jnp.where(row < rows_valid, w_ref[...], 0.0).astype(BF16)


def _cast_pad_rows(w, l, s, m_pad, tr=512):
    m, n = w.shape[2:]
    return pl.pallas_call(
        functools.partial(_cast_pad_rows_kernel, rows_valid=m),
        grid=(m_pad // tr,),
        in_specs=[pl.BlockSpec((None, None, tr, n), lambda j: (l, s, j, 0))],
        out_specs=pl.BlockSpec((tr, n), lambda j: (j, 0)),
        out_shape=jax.ShapeDtypeStruct((m_pad, n), BF16),
        compiler_params=_cparams(("parallel",)),
    )(w)


def _mod_kernel(cond_ref, b_ref, *refs):
    w_refs, o_ref = refs[:-1], refs[-1]
    k = pl.program_id(1)

    @pl.when(k == 0)
    def _():
        o_ref[...] = jnp.broadcast_to(b_ref[...], o_ref.shape)

    s = _silu(cond_ref[...]).astype(BF16)
    part = o_ref.shape[1] // len(w_refs)
    for j, w_ref in enumerate(w_refs):
        o_ref[:, j * part:(j + 1) * part] += _dot(s, w_ref[...].astype(BF16))


def _modulation(cond, w_mod, b_mod, tk=256):
    n_layers, d, n = w_mod.shape
    part = n // MOD_STREAMS
    w_specs = [pl.BlockSpec((None, tk, part), lambda l, k, j=j: (l, k, j)) for j in range(MOD_STREAMS)]
    return pl.pallas_call(
        _mod_kernel,
        grid=(n_layers, d // tk),
        in_specs=[
            pl.BlockSpec((COND_ROWS, tk), lambda l, k: (0, k)),
            pl.BlockSpec((None, 1, n), lambda l, k: (l, 0, 0)),
            *w_specs,
        ],
        out_specs=pl.BlockSpec((None, COND_ROWS, n), lambda l, k: (l, 0, 0)),
        out_shape=jax.ShapeDtypeStruct((n_layers, COND_ROWS, n), F32),
        compiler_params=_cparams(("parallel", "arbitrary")),
    )(cond, b_mod.reshape(n_layers, 1, n), *([w_mod] * MOD_STREAMS))


def _ffn_kernel(*refs, cast_next, d_ff):
    x_ref, shift_ref, scale_ref, gate_ref, gpre_ref, gpost_ref, wg_ref, wu_ref, wd_ref = refs[:9]
    if cast_next:
        wgn_ref, wun_ref, wdn_ref, o_ref, wgo_ref, wuo_ref, wdo_ref, h_ref, r_ref = refs[9:]
    else:
        o_ref, h_ref, r_ref = refs[9:]
    f = pl.program_id(1)

    @pl.when(f == 0)
    def _():
        gain = gpre_ref[...] * (1.0 + scale_ref[...])
        shift = shift_ref[...]

        def norms(rows):
            r_ref[rows, :] = _inv_rms(x_ref[rows, :])

        def block(rows):
            h_ref[rows, :] = (x_ref[rows, :] * r_ref[rows, :] * gain + shift).astype(BF16)
            o_ref[rows, :] = jnp.zeros((rows.size, o_ref.shape[1]), F32)

        _for_row_blocks(x_ref.shape[0], norms)
        _for_row_blocks(x_ref.shape[0], block)

    h = h_ref[...]
    g = _dot(h, wg_ref[...])
    u = _dot(h, wu_ref[...])
    a = (_silu(g) * u).astype(BF16)
    o_ref[...] += _dot(a, wd_ref[...])

    if cast_next:
        for src_ref, dst_ref in ((wgn_ref, wgo_ref), (wun_ref, wuo_ref)):
            dst_ref[:, :d_ff] = src_ref[...].astype(BF16)
            dst_ref[:, d_ff:] = jnp.zeros((dst_ref.shape[0], dst_ref.shape[1] - d_ff), BF16)
        rd = wdo_ref.shape[0]
        step = pl.program_id(0) * pl.num_programs(1) + f
        row = jnp.minimum(step, cast_next - 1) * rd + lax.broadcasted_iota(jnp.int32, wdo_ref.shape, 0)
        wdo_ref[...] = jnp.where(row < d_ff, wdn_ref[...], 0.0).astype(BF16)

    @pl.when(f == pl.num_programs(1) - 1)
    def _():
        gain = 0.5 * gate_ref[...] * gpost_ref[...]

        def norms(rows):
            r_ref[rows, :] = _inv_rms(o_ref[rows, :])

        def block(rows):
            o_ref[rows, :] = x_ref[rows, :] + o_ref[rows, :] * r_ref[rows, :] * gain

        _for_row_blocks(x_ref.shape[0], norms)
        _for_row_blocks(x_ref.shape[0], block)


def _slab_rows(total, n_steps):
    slab = BF16_ROWS
    while total % slab or total // slab > n_steps:
        slab *= 2
        assert slab <= total
    return slab


def _ffn(x, rows, mod, j, row_of_tile, gpre, gpost, wts, tm, cast_next=None):
    wg, wu, wd = wts
    d = x.shape[1]
    f_pad = wg.shape[-1]
    tf = FF_CHUNK
    nf = f_pad // tf

    def mspec(k):
        return pl.BlockSpec((None, None, 1, d), lambda i, f: (row_of_tile(i), k, 0, 0))

    vec = pl.BlockSpec((1, d), lambda i, f: (0, 0))
    in_specs = [
        pl.BlockSpec((tm, d), lambda i, f: (i, 0)),
        mspec(3 * j), mspec(3 * j + 1), mspec(3 * j + 2),
        vec, vec,
        pl.BlockSpec((d, tf), lambda i, f: (0, f)),
        pl.BlockSpec((d, tf), lambda i, f: (0, f)),
        pl.BlockSpec((tf, d), lambda i, f: (f, 0)),
    ]
    args = [x, mod, mod, mod, gpre, gpost, wg, wu, wd]
    out_specs = [pl.BlockSpec((tm, d), lambda i, f: (i, 0))]
    out_shape = [jax.ShapeDtypeStruct((rows, d), F32)]
    down_slabs, d_ff = 0, f_pad
    if cast_next is not None:
        wg32, wu32, wd32, l2, s2 = cast_next
        d_ff = wg32.shape[-1]
        n_steps = (rows // tm) * nf
        rg, rd = _slab_rows(d, n_steps), _slab_rows(f_pad, n_steps)
        up_slabs, down_slabs, down_src_slabs = d // rg, f_pad // rd, pl.cdiv(d_ff, rd)

        def slab(count):
            return lambda i, f: jnp.minimum(i * nf + f, count - 1)

        in_specs += [pl.BlockSpec((None, None, rg, d_ff), lambda i, f: (l2, s2, slab(up_slabs)(i, f), 0))] * 2
        in_specs += [pl.BlockSpec((None, None, rd, d), lambda i, f: (l2, s2, slab(down_src_slabs)(i, f), 0))]
        args += [wg32, wu32, wd32]
        out_specs += [pl.BlockSpec((rg, f_pad), lambda i, f: (slab(up_slabs)(i, f), 0))] * 2
        out_specs += [pl.BlockSpec((rd, d), lambda i, f: (slab(down_slabs)(i, f), 0))]
        out_shape += [jax.ShapeDtypeStruct((d, f_pad), BF16)] * 2 + [jax.ShapeDtypeStruct((f_pad, d), BF16)]
    outs = pl.pallas_call(
        functools.partial(_ffn_kernel, cast_next=down_slabs, d_ff=d_ff),
        grid=(rows // tm, nf),
        in_specs=in_specs,
        out_specs=out_specs,
        out_shape=out_shape,
        scratch_shapes=[pltpu.VMEM((tm, d), BF16), pltpu.VMEM((tm, 1), F32)],
        compiler_params=_cparams(("parallel", "arbitrary"), VMEM_LIMIT_FFN),
    )(*args)
    return outs[0] if cast_next is None else (outs[0], tuple(outs[1:]))


def _inproj_kernel(*refs, x_firsts):
    x_refs = refs[:len(x_firsts)]
    (shift_ref, scale_ref, gpre_ref, w_ref, b_ref, cos_ref, sina_ref, sinb_ref, qn_ref, kvn_ref, wuq_ref, wukv_ref,
     lng_ref, lnb_ref, ws_ref, bs_ref,
     h_ref, ya_ref, hb_ref, qc_ref, kc_ref, vc_ref, qd_ref, kd_ref, vd_ref) = refs[len(x_firsts):]
    y = _rms(_load_parts(x_refs, x_firsts, pl.program_id(0))) * gpre_ref[...]
    h = (y * (1.0 + scale_ref[...]) + shift_ref[...]).astype(BF16)
    h_ref[...] = h

    def proj(lo, hi):
        return _dot(h, w_ref[:, lo:hi]) + b_ref[:, lo:hi]

    na, nb, nc, nd = IN_WIDTHS
    _qkv_prep(proj(na + nb, na + nb + nc), proj(na + nb + nc, na + nb + nc + nd),
              cos_ref, sina_ref, sinb_ref, qn_ref, kvn_ref, wuq_ref, wukv_ref,
              qc_ref, kc_ref, vc_ref, qd_ref, kd_ref, vd_ref)
    w = BRANCH_W
    zb = proj(na, na + nb)
    hb_ref[...] = zb[:, :w] * jax.nn.sigmoid(zb[:, w:])
    za = proj(0, na)
    g = 0.5 * za * (1.0 + lax.erf(za * (2.0 ** -0.5)))
    vln = _layernorm(g[:, w:], lng_ref[...], lnb_ref[...]).astype(BF16)
    for c in range(za.shape[0] // CHUNK):
        rows = slice(c * CHUNK, (c + 1) * CHUNK)
        for gi in range(GM_GROUPS):
            cols = slice(gi * CHUNK, (gi + 1) * CHUNK)
            sv = _dot(ws_ref[gi], vln[rows, cols]) + bs_ref[gi]
            ya_ref[rows, cols] = (g[rows, cols] * sv).astype(BF16)


def _inproj(x_parts, mod, row_of_tile, gpre, w, b, cos, sina, sinb, qn, kvn, wuq, wukv, lng, lnb, ws, bs, tm):
    rows, d = sum(part.shape[0] for part in x_parts), x_parts[0].shape[1]
    x_specs, x_firsts = _part_specs(x_parts, tm, lambda i: i)
    n = w.shape[1]

    def mspec(k):
        return pl.BlockSpec((None, None, 1, d), lambda i: (row_of_tile(i), k, 0, 0))

    def rowspec(wd):
        return pl.BlockSpec((tm, wd), lambda i: (i, 0))

    def full(a):
        return pl.BlockSpec(a.shape, lambda i: (0,) * a.ndim)

    def headspec(nh, wd):
        return pl.BlockSpec((nh, tm, wd), lambda i: (0, i, 0))

    def headspec_t(nh, wd):
        return pl.BlockSpec((nh, wd, tm), lambda i: (0, 0, i))

    kc = jax.ShapeDtypeStruct((MLA_HEADS, rows, MLA_QK), BF16)
    qc = jax.ShapeDtypeStruct((MLA_HEADS, MLA_QK, rows), BF16)
    vc = jax.ShapeDtypeStruct((MLA_HEADS, MLA_V + MLA_SUM_ROWS, rows), BF16)
    qd = jax.ShapeDtypeStruct((SWA_HEADS // 2, LANES, rows), BF16)
    kd = jax.ShapeDtypeStruct((rows, LANES), BF16)
    vd = jax.ShapeDtypeStruct((LANES, rows), BF16)
    outs = [jax.ShapeDtypeStruct((rows, d), BF16), jax.ShapeDtypeStruct((rows, BRANCH_W), BF16),
            jax.ShapeDtypeStruct((rows, BRANCH_W), F32)]
    ospecs = [rowspec(d), rowspec(BRANCH_W), rowspec(BRANCH_W)]
    return pl.pallas_call(
        functools.partial(_inproj_kernel, x_firsts=x_firsts),
        grid=(rows // tm,),
        in_specs=[
            *x_specs,
            mspec(3), mspec(4),
            pl.BlockSpec((1, d), lambda i: (0, 0)),
            pl.BlockSpec((d, n), lambda i: (0, 0), pipeline_mode=pl.Buffered(1)),
            pl.BlockSpec((1, n), lambda i: (0, 0)),
            rowspec(LANES), rowspec(LANES), rowspec(LANES), full(qn), full(kvn), full(wuq), full(wukv),
            full(lng), full(lnb), full(ws), full(bs),
        ],
        out_specs=ospecs + [headspec_t(MLA_HEADS, MLA_QK), headspec(MLA_HEADS, MLA_QK),
                            headspec_t(MLA_HEADS, MLA_V + MLA_SUM_ROWS), headspec_t(SWA_HEADS // 2, LANES),
                            rowspec(LANES), pl.BlockSpec((LANES, tm), lambda i: (0, i))],
        out_shape=outs + [qc, kc, vc, qd, kd, vd],
        compiler_params=_cparams(("parallel",)),
    )(*x_parts, mod, mod, gpre, w, b, cos, sina, sinb, qn, kvn, wuq, wukv, lng, lnb, ws, bs)


def _conv_kernel(h_ref, hprev_ref, hnext_ref, dw_ref, db_ref, lng_ref, lnb_ref, yb_ref, ext_ref, shift_ref,
                 *, ts, n_lat_tiles, lat_tiles_per_seq, ctx_tiles_per_seq):
    i = pl.program_id(0)
    in_lat = i < n_lat_tiles
    pos = jnp.where(in_lat, i % lat_tiles_per_seq, (i - n_lat_tiles) % ctx_tiles_per_seq)
    last = jnp.where(in_lat, lat_tiles_per_seq - 1, ctx_tiles_per_seq - 1)
    ext_ref[0:CONV_HALO, :] = jnp.where(pos != 0, hprev_ref[...], 0.0)
    ext_ref[CONV_HALO:CONV_HALO + ts, :] = h_ref[...]
    ext_ref[CONV_HALO + ts:, :] = jnp.where(pos != last, hnext_ref[...], 0.0)
    span = ts + 2 * CONV_HALO - SUBLANES
    for r in range(1, SUBLANES):
        shift_ref[r - 1, 0:span, :] = ext_ref[r:r + span, :]
    first_tap = CONV_HALO - CONV_W // 2
    acc = jnp.zeros((ts, BRANCH_W), F32) + db_ref[...]
    for k in range(CONV_W):
        a, r = divmod(first_tap + k, SUBLANES)
        win = ext_ref[a * SUBLANES:a * SUBLANES + ts, :] if r == 0 else shift_ref[r - 1, a * SUBLANES:a * SUBLANES + ts, :]
        acc = acc + dw_ref[k:k + 1, :] * win
    yb_ref[...] = _silu(_layernorm(acc, lng_ref[...], lnb_ref[...])).astype(BF16)


def _conv_mixer(hb, dw, db, lng, lnb, n_lat_rows, seq_lat, seq_ctx, ts):
    rows, w = hb.shape
    halo_blocks = ts // CONV_HALO
    n_halo = rows // CONV_HALO
    kern = functools.partial(_conv_kernel, ts=ts, n_lat_tiles=n_lat_rows // ts,
                             lat_tiles_per_seq=seq_lat // ts, ctx_tiles_per_seq=seq_ctx // ts)
    vec = pl.BlockSpec((1, w), lambda i: (0, 0))
    return pl.pallas_call(
        kern,
        grid=(rows // ts,),
        in_specs=[
            pl.BlockSpec((ts, w), lambda i: (i, 0)),
            pl.BlockSpec((CONV_HALO, w), lambda i: (jnp.maximum(i * halo_blocks - 1, 0), 0)),
            pl.BlockSpec((CONV_HALO, w), lambda i: (jnp.minimum((i + 1) * halo_blocks, n_halo - 1), 0)),
            pl.BlockSpec((CONV_W, w), lambda i: (0, 0)),
            vec, vec, vec,
        ],
        out_specs=pl.BlockSpec((ts, w), lambda i: (i, 0)),
        out_shape=jax.ShapeDtypeStruct((rows, w), BF16),
        scratch_shapes=[pltpu.VMEM((ts + 2 * CONV_HALO, w), F32),
                        pltpu.VMEM((SUBLANES - 1, ts + 2 * CONV_HALO - SUBLANES, w), F32)],
        compiler_params=_cparams(("parallel",)),
    )(hb, hb, hb, dw, db, lng, lnb)


def _qkv_prep(zc, zd, cos_ref, sina_ref, sinb_ref, qn_ref, kvn_ref, wuq_ref, wukv_ref,
              qc_ref, kc_ref, vc_ref, qd_ref, kd_ref, vd_ref):
    cos, sina, sinb = cos_ref[...], sina_ref[...], sinb_ref[...]
    low = lax.broadcasted_iota(jnp.int32, cos.shape, 1) < ROT_DIM

    def rope(slab):
        return (slab * cos + pltpu.roll(slab, LANES - ROT_QUARTER, 1) * sina
                + pltpu.roll(slab, ROT_QUARTER, 1) * sinb)

    nq = MLA_HEADS * MLA_NOPE
    kv0 = MLA_Q_RANK + MLA_KV_RANK
    q = _dot((_rms(zc[:, :MLA_Q_RANK]) * qn_ref[...]).astype(BF16), wuq_ref[...]) * MLA_SCALE
    kv = _dot((_rms(zc[:, MLA_Q_RANK:kv0]) * kvn_ref[...]).astype(BF16), wukv_ref[...])
    kr = rope(zc[:, kv0:kv0 + LANES])
    ones = jnp.ones((MLA_SUM_ROWS, zc.shape[0]), BF16)
    for pair in range(MLA_HEADS // 2):
        qr = rope(q[:, nq + pair * LANES:nq + (pair + 1) * LANES]).T.astype(BF16)
        for h in (2 * pair, 2 * pair + 1):
            qc_ref[h, 0:MLA_NOPE, :] = q[:, h * MLA_NOPE:(h + 1) * MLA_NOPE].T.astype(BF16)
            qc_ref[h, MLA_NOPE:, :] = qr
            kc_ref[h, :, 0:MLA_NOPE] = kv[:, h * MLA_NOPE:(h + 1) * MLA_NOPE].astype(BF16)
            keep = low if h % 2 == 0 else jnp.logical_not(low)
            kc_ref[h, :, MLA_NOPE:] = jnp.where(keep, kr, 0.0).astype(BF16)
            vc_ref[h, 0:MLA_V, :] = kv[:, nq + h * MLA_V:nq + (h + 1) * MLA_V].T.astype(BF16)
            vc_ref[h, MLA_V:, :] = ones

    for pair in range(SWA_HEADS // 2):
        qd_ref[pair] = (rope(zd[:, pair * LANES:(pair + 1) * LANES]) * SWA_SCALE).T.astype(BF16)
    nqd = SWA_HEADS * ROT_DIM
    kd_ref[...] = rope(zd[:, nqd:nqd + LANES]).astype(BF16)
    vd_ref[...] = zd[:, nqd + LANES:nqd + 2 * LANES].T.astype(BF16)


def _mla_kernel(*refs, has_lat, groups):
    if has_lat:
        qt_ref, kl_ref, vlt_ref, kx_ref, vxt_ref, o_ref = refs
    else:
        qt_ref, kx_ref, vxt_ref, o_ref = refs
    cols = qt_ref.shape[1] // groups
    scores = []
    for g in range(groups):
        qt = qt_ref[:, g * cols:(g + 1) * cols]
        scores.append((_dot(kx_ref[...], qt), _dot(kl_ref[...], qt) if has_lat else None))
    for g, (sx, sl) in enumerate(scores):
        m = jnp.max(sx, axis=0, keepdims=True)
        if has_lat:
            m = jnp.maximum(m, jnp.max(sl, axis=0, keepdims=True))
        ot = _dot(vxt_ref[...], jnp.exp2(sx - m).astype(BF16))
        if has_lat:
            ot = ot + _dot(vlt_ref[...], jnp.exp2(sl - m).astype(BF16))
        o_ref[g * cols:(g + 1) * cols, :] = (ot[:MLA_V] / ot[MLA_V:MLA_V + 1]).T.astype(BF16)


def _mla_attention(qc, kc, vc, batch, seq_lat, seq_ctx, lat_queries):
    n_lat_rows = batch * seq_lat
    ctx_blk0 = n_lat_rows // seq_ctx
    tq = math.gcd(MLA_GROUPS * MLA_GROUP_Q, seq_lat if lat_queries else seq_ctx)
    if lat_queries:
        nq, q0 = seq_lat // tq, 0
    else:
        nq, q0 = seq_ctx // tq, n_lat_rows // tq
    vrows = vc.shape[1]
    qspec = pl.BlockSpec((None, MLA_QK, tq), lambda b, h, i: (h, 0, q0 + b * nq + i))
    kx = pl.BlockSpec((None, seq_ctx, MLA_QK), lambda b, h, i: (h, ctx_blk0 + b, 0))
    vx = pl.BlockSpec((None, vrows, seq_ctx), lambda b, h, i: (h, 0, ctx_blk0 + b))
    if lat_queries:
        kl = pl.BlockSpec((None, seq_lat, MLA_QK), lambda b, h, i: (h, b, 0))
        vl = pl.BlockSpec((None, vrows, seq_lat), lambda b, h, i: (h, 0, b))
        in_specs, args = [qspec, kl, vl, kx, vx], (qc, kc, vc, kc, vc)
    else:
        in_specs, args = [qspec, kx, vx], (qc, kc, vc)
    return pl.pallas_call(
        functools.partial(_mla_kernel, has_lat=lat_queries, groups=max(1, tq // MLA_GROUP_Q)),
        grid=(batch, MLA_HEADS, nq),
        in_specs=in_specs,
        out_specs=pl.BlockSpec((tq, MLA_V), lambda b, h, i: (b * nq + i, h)),
        out_shape=jax.ShapeDtypeStruct((batch * nq * tq, MLA_HEADS * MLA_V), BF16),
        compiler_params=_cparams(("parallel", "parallel", "parallel")),
    )(*args)


def _swa_kernel(*refs, has_lat):
    if has_lat:
        qt_ref, kx_ref, vxt_ref, sink_ref, bias_ref = refs[:5]
        kw_refs, vw_refs, o_ref = refs[5:5 + SWA_WINDOW_BLOCKS], refs[5 + SWA_WINDOW_BLOCKS:-1], refs[-1]
    else:
        qt_ref, kx_ref, vxt_ref, sink_ref, o_ref = refs
    blocks = qt_ref.shape[2] // SWA_BLOCK
    half = LANES // 2
    span = 3 * SWA_BLOCK
    gcol = SWA_GROUP * SWA_BLOCK
    zeros = jnp.zeros((half, gcol), BF16)
    ones = jnp.ones((SWA_SUM_ROWS, 1), BF16)
    if has_lat:
        k_win = jnp.concatenate([r[...] for r in kw_refs], axis=0)
        v_win = jnp.concatenate([r[...] for r in vw_refs], axis=1)
    chains = []
    for kv in range(SWA_KV):
        for blk in range(blocks):
            heads = range(kv * SWA_GROUP, (kv + 1) * SWA_GROUP)
            qt = jnp.concatenate([qt_ref[h // 2, (h % 2) * half:(h % 2 + 1) * half,
                                         blk * SWA_BLOCK:(blk + 1) * SWA_BLOCK] for h in heads], axis=1)
            qt = jnp.concatenate([qt, zeros] if kv == 0 else [zeros, qt], axis=0)
            k0 = blk * SWA_BLOCK
            chains.append((kv, blk, _dot(kx_ref[...], qt), _dot(k_win[k0:k0 + span], qt) if has_lat else None))
    for kv, blk, sx, sl in chains:
        rows = slice(kv * half, (kv + 1) * half)
        cols = slice(blk * gcol, (blk + 1) * gcol)
        k0 = blk * SWA_BLOCK
        sink = sink_ref[kv][:, cols] * LOG2_E
        m = jnp.maximum(sink, jnp.max(sx, axis=0, keepdims=True))
        if has_lat:
            sl = sl + bias_ref[k0:k0 + span, cols]
            m = jnp.maximum(m, jnp.max(sl, axis=0, keepdims=True))
        vx = vxt_ref[rows, :]
        ot = _dot(jnp.concatenate([vx, jnp.broadcast_to(ones, (SWA_SUM_ROWS, vx.shape[1]))], axis=0),
                  jnp.exp2(sx - m).astype(BF16))
        if has_lat:
            vl = v_win[rows, k0:k0 + span]
            ot = ot + _dot(jnp.concatenate([vl, jnp.broadcast_to(ones, (SWA_SUM_ROWS, span))], axis=0),
                           jnp.exp2(sl - m).astype(BF16))
        o = ot[:half] / (ot[half:half + 1] + jnp.exp2(sink - m))
        for pair in range(SWA_GROUP // 2):
            c0 = 2 * pair * SWA_BLOCK
            slab_t = jnp.concatenate([o[:, c0:c0 + SWA_BLOCK], o[:, c0 + SWA_BLOCK:c0 + 2 * SWA_BLOCK]], axis=0)
            lane0 = (kv * (SWA_GROUP // 2) + pair) * LANES
            o_ref[blk * SWA_BLOCK:(blk + 1) * SWA_BLOCK, lane0:lane0 + LANES] = slab_t.T.astype(BF16)


def _swa_window_bias(blocks):
    nkeys = SWA_WINDOW_BLOCKS * SWA_BLOCK
    ncol = blocks * SWA_GROUP * SWA_BLOCK
    key = jnp.arange(nkeys)[:, None] - SWA_BLOCK
    col = jnp.arange(ncol)[None, :]
    qry = (col // (SWA_GROUP * SWA_BLOCK)) * SWA_BLOCK + (col % SWA_BLOCK)
    inside = jnp.abs(key - qry) <= WINDOW
    after_start = key >= 0
    before_end = key < blocks * SWA_BLOCK
    variants = [inside, inside & after_start, inside & before_end, inside & after_start & before_end]
    return jnp.where(jnp.stack(variants), 0.0, -jnp.inf).astype(F32)


def _swa_attention(qd, kd, vd, sink, batch, seq_lat, seq_ctx, lat_queries):
    n_lat_rows = batch * seq_lat
    ctx_blk0 = n_lat_rows // seq_ctx
    seq_q = seq_lat if lat_queries else seq_ctx
    blocks = 2 if seq_q % (2 * SWA_BLOCK) == 0 and n_lat_rows % (2 * SWA_BLOCK) == 0 else 1
    tq = blocks * SWA_BLOCK
    nb = seq_q // tq
    q0 = 0 if lat_queries else n_lat_rows // tq
    kblocks = seq_lat // SWA_BLOCK
    sink_row = jnp.tile(jnp.repeat(sink.reshape(SWA_KV, SWA_GROUP), SWA_BLOCK, axis=1), (1, blocks))[:, None, :]
    in_specs = [pl.BlockSpec((SWA_HEADS // 2, LANES, tq), lambda b, i: (0, 0, q0 + b * nb + i)),
                pl.BlockSpec((seq_ctx, LANES), lambda b, i: (ctx_blk0 + b, 0)),
                pl.BlockSpec((LANES, seq_ctx), lambda b, i: (0, ctx_blk0 + b)),
                pl.BlockSpec(sink_row.shape, lambda b, i: (0, 0, 0))]
    args = [qd, kd, vd, sink_row]
    if lat_queries:
        assert blocks + 2 == SWA_WINDOW_BLOCKS
        bias = _swa_window_bias(blocks)
        in_specs.append(pl.BlockSpec((None,) + bias.shape[1:],
                                     lambda b, i: ((i == 0).astype(jnp.int32) + 2 * (i == nb - 1).astype(jnp.int32), 0, 0)))
        args.append(bias)

        def win(j):
            return lambda b, i: b * kblocks + jnp.clip(i * blocks - 1 + j, 0, kblocks - 1)

        in_specs += [pl.BlockSpec((SWA_BLOCK, LANES), lambda b, i, j=j: (win(j)(b, i), 0))
                     for j in range(SWA_WINDOW_BLOCKS)]
        in_specs += [pl.BlockSpec((LANES, SWA_BLOCK), lambda b, i, j=j: (0, win(j)(b, i)))
                     for j in range(SWA_WINDOW_BLOCKS)]
        args += [kd] * SWA_WINDOW_BLOCKS + [vd] * SWA_WINDOW_BLOCKS
    return pl.pallas_call(
        functools.partial(_swa_kernel, has_lat=lat_queries),
        grid=(batch, nb),
        in_specs=in_specs,
        out_specs=pl.BlockSpec((tq, BRANCH_W), lambda b, i: (b * nb + i, 0)),
        out_shape=jax.ShapeDtypeStruct((batch * nb * tq, BRANCH_W), BF16),
        compiler_params=_cparams(("parallel", "parallel")),
    )(*args)


def _gate_kernel(*refs, y_firsts):
    n = len(y_firsts)
    h_ref, ya_ref, yb_ref = refs[:3]
    yc_refs, yd_refs = refs[3:3 + n], refs[3 + n:3 + 2 * n]
    wg0, wg1, wg2, wg3, bg_ref, wb0, wb1, wb2, wb3, o_ref = refs[3 + 2 * n:]
    tile = pl.program_id(1)
    ys = (ya_ref[...], yb_ref[...], _load_parts(yc_refs, y_firsts, tile), _load_parts(yd_refs, y_firsts, tile))
    h = h_ref[...]
    acc = None
    for b, (y, wg_ref, wb_ref) in enumerate(zip(ys, (wg0, wg1, wg2, wg3), (wb0, wb1, wb2, wb3))):
        gate = jax.nn.sigmoid(_dot(h, wg_ref[...]) + bg_ref[b])
        term = gate * _dot(y, wb_ref[...])
        acc = term if acc is None else acc + term
    o_ref[...] = acc.astype(BF16)


def _gated_merge(h, ya, yb, yc_parts, yd_parts, wg, bg, wbr, rows, tm, tn):
    d = h.shape[1]
    ncol = d // tn
    wg_specs = [pl.BlockSpec((d, tn), lambda j, i, b=b: (0, b * ncol + j)) for b in range(4)]
    wb_specs = [pl.BlockSpec((None, BRANCH_W, tn), lambda j, i, b=b: (b, 0, j)) for b in range(4)]
    yspec = pl.BlockSpec((tm, BRANCH_W), lambda j, i: (i, 0))
    yc_specs, y_firsts = _part_specs(yc_parts, tm, lambda j, i: i)
    yd_specs, _ = _part_specs(yd_parts, tm, lambda j, i: i)
    return pl.pallas_call(
        functools.partial(_gate_kernel, y_firsts=y_firsts),
        grid=(ncol, rows // tm),
        in_specs=[pl.BlockSpec((tm, d), lambda j, i: (i, 0)), yspec, yspec, *yc_specs, *yd_specs,
                  *wg_specs, pl.BlockSpec((4, 1, tn), lambda j, i: (0, 0, j)), *wb_specs],
        out_specs=pl.BlockSpec((tm, tn), lambda j, i: (i, j)),
        out_shape=jax.ShapeDtypeStruct((rows, d), BF16),
        compiler_params=_cparams(("parallel", "parallel")),
    )(h, ya, yb, *yc_parts, *yd_parts, wg, wg, wg, wg, bg, wbr, wbr, wbr, wbr)


def _outproj_kernel(*refs, x_firsts):
    a_ref, x_refs = refs[0], refs[1:1 + len(x_firsts)]
    gate_ref, gpost_ref, w_ref, o_ref = refs[1 + len(x_firsts):]
    y = _dot(a_ref[...], w_ref[...])
    o_ref[...] = _load_parts(x_refs, x_firsts, pl.program_id(0)) + gate_ref[...] * (_rms(y) * gpost_ref[...])


def _outproj(a, x_parts, mod, row_of_tile, gpost, w, tm):
    rows, d = a.shape
    x_specs, x_firsts = _part_specs(x_parts, tm, lambda i: i)
    return pl.pallas_call(
        functools.partial(_outproj_kernel, x_firsts=x_firsts),
        grid=(rows // tm,),
        in_specs=[
            pl.BlockSpec((tm, d), lambda i: (i, 0)),
            *x_specs,
            pl.BlockSpec((None, None, 1, d), lambda i: (row_of_tile(i), 5, 0, 0)),
            pl.BlockSpec((1, d), lambda i: (0, 0)),
            pl.BlockSpec((d, d), lambda i: (0, 0), pipeline_mode=pl.Buffered(1)),
        ],
        out_specs=pl.BlockSpec((tm, d), lambda i: (i, 0)),
        out_shape=jax.ShapeDtypeStruct((rows, d), F32),
        compiler_params=_cparams(("parallel",)),
    )(a, *x_parts, mod, gpost, w)


def _rope_tables(seq_lat, n_ctx_rows, batch):
    half = ROT_DIM // 2
    freqs = 1.0 / (ROPE_BASE ** (jnp.arange(0, half, 2, dtype=F32) / half))
    t = jnp.arange(seq_lat)
    ang_r = (t // GRID_W).astype(F32)[:, None] * freqs[None, :]
    ang_c = (t % GRID_W).astype(F32)[:, None] * freqs[None, :]
    ang = jnp.concatenate([ang_r, ang_r, ang_c, ang_c], axis=-1)
    ang = jnp.tile(ang, (batch, LANES // ROT_DIM))
    cos, sin = jnp.cos(ang), jnp.sin(ang)
    first = (jnp.arange(LANES) % half) < ROT_QUARTER
    sina = jnp.where(first, -sin, 0.0)
    sinb = jnp.where(first, 0.0, sin)
    ones = jnp.ones((n_ctx_rows, LANES), F32)
    zeros = jnp.zeros((n_ctx_rows, LANES), F32)
    return (jnp.concatenate([cos, ones]), jnp.concatenate([sina, zeros]), jnp.concatenate([sinb, zeros]))


def _pack_layer(l, w_in, b_in, gm_ws, gm_bs, mla_w_uq, mla_w_ukv, swa_sink, w_branch, w_out):
    d = w_in.shape[1]
    p = {}
    wi, bi = w_in[l], b_in[l]
    na, nb, nc, nd = IN_WIDTHS
    c_hi = na + nb + nc - ROT_DIM
    kr = slice(c_hi - ROT_DIM, c_hi)
    p['w_abcd'] = jnp.concatenate([wi[:, :c_hi], wi[:, kr], wi[:, c_hi:c_hi + nd]], axis=1).astype(BF16)
    p['b_abcd'] = jnp.concatenate([bi[:c_hi], bi[kr], bi[c_hi:c_hi + nd]])[None, :]
    p['w_gates'] = wi[:, c_hi + nd:].astype(BF16)
    p['b_gates'] = bi[c_hi + nd:].reshape(4, 1, d)
    p['gm_ws'] = gm_ws[l].astype(BF16)
    p['gm_bs'] = gm_bs[l][:, :, None]
    uq = mla_w_uq[l].reshape(-1, MLA_HEADS, MLA_NOPE + ROT_DIM)
    p['wuq'] = jnp.concatenate([uq[:, :, :MLA_NOPE].reshape(-1, MLA_HEADS * MLA_NOPE),
                                uq[:, :, MLA_NOPE:].reshape(-1, MLA_HEADS * ROT_DIM)], axis=1).astype(BF16)
    ukv = mla_w_ukv[l].reshape(-1, MLA_HEADS, MLA_NOPE + MLA_V)
    p['wukv'] = jnp.concatenate([ukv[:, :, :MLA_NOPE].reshape(-1, MLA_HEADS * MLA_NOPE),
                                 ukv[:, :, MLA_NOPE:].reshape(-1, MLA_HEADS * MLA_V)], axis=1).astype(BF16)
    p['sink'] = swa_sink[l]
    p['w_branch'] = w_branch[l].astype(BF16)
    p['w_out'] = w_out[l].astype(BF16)
    return p


def kernel(x, c, ctx, c_ctx, w_mod, b_mod, norm_pre, norm_post, w_ff_gate, w_ff_up, w_ff_down, w_in, b_in, gm_ln_g, gm_ln_b, gm_ws, gm_bs, cv_dw, cv_db, cv_ln_g, cv_ln_b, mla_q_norm, mla_kv_norm, mla_w_uq, mla_w_ukv, swa_sink, w_branch, w_out):
    batch, seq_lat, d = x.shape
    seq_ctx = ctx.shape[1]
    depth = w_mod.shape[0]
    n_lat = batch * seq_lat
    n_ctx = batch * seq_ctx
    assert batch + 1 <= COND_ROWS and seq_lat % GRID_W == 0
    tm = math.gcd(512, math.gcd(seq_lat, n_ctx))
    ts = math.gcd(256, math.gcd(seq_lat, seq_ctx))
    assert tm % SWA_BLOCK == 0 and ts % CHUNK == 0 and seq_ctx % SWA_BLOCK == 0 and seq_lat >= 3 * SWA_BLOCK

    tm_ffn = math.gcd(1024, math.gcd(seq_lat, n_ctx))

    def cond_row(tile):
        return lambda i: jnp.where(i < n_lat // tile, i // (seq_lat // tile), batch)

    row_of_tile, row_of_ffn_tile = cond_row(tm), cond_row(tm_ffn)

    cond = jnp.zeros((COND_ROWS, d), F32).at[:batch].set(c).at[batch].set(c_ctx)
    mod = _modulation(cond, w_mod, b_mod).reshape(depth, COND_ROWS, N_MOD, 1, d)
    cos, sina, sinb = _rope_tables(seq_lat, n_ctx, batch)

    ff32 = (w_ff_gate, w_ff_up, w_ff_down)
    wts = (_cast_pad_cols(w_ff_gate, 0, 0, D_FF_PAD), _cast_pad_cols(w_ff_up, 0, 0, D_FF_PAD),
           _cast_pad_rows(w_ff_down, 0, 0, D_FF_PAD))

    n_all = n_lat + n_ctx
    xs = None
    for l in range(depth):
        last = l == depth - 1
        p = _pack_layer(l, w_in, b_in, gm_ws, gm_bs, mla_w_uq, mla_w_ukv, swa_sink, w_branch, w_out)
        m = mod[l]
        npre, npost = norm_pre[l][:, None, :], norm_post[l][:, None, :]

        ffn1 = (m, 0, row_of_ffn_tile, npre[0], npost[0])
        if xs is None:
            x_lat, wts2 = _ffn(x.reshape(n_lat, d), n_lat, *ffn1, wts, tm_ffn, cast_next=ff32 + (l, 1))
            x_ctx = _ffn(ctx.reshape(n_ctx, d), n_ctx, m, 0, lambda i: batch, npre[0], npost[0], wts, tm_ffn)
            x_parts = (x_lat, x_ctx)
        else:
            xs, wts2 = _ffn(xs, n_all, *ffn1, wts, tm_ffn, cast_next=ff32 + (l, 1))
            x_parts = (xs,)

        h, ya, hb, qc, kc, vc, qd, kd, vd = _inproj(
            x_parts, m, row_of_tile, npre[1], p['w_abcd'], p['b_abcd'], cos, sina, sinb,
            mla_q_norm[l][None], mla_kv_norm[l][None], p['wuq'], p['wukv'],
            gm_ln_g[l][None], gm_ln_b[l][None], p['gm_ws'], p['gm_bs'], tm)
        yb = _conv_mixer(hb, cv_dw[l], cv_db[l][None], cv_ln_g[l][None], cv_ln_b[l][None],
                         n_lat, seq_lat, seq_ctx, ts)
        rows_mix = n_lat if last else n_all
        attn = (batch, seq_lat, seq_ctx)
        yc_parts = (_mla_attention(qc, kc, vc, *attn, True),)
        yd_parts = (_swa_attention(qd, kd, vd, p['sink'], *attn, True),)
        if not last:
            yc_parts += (_mla_attention(qc, kc, vc, *attn, False),)
            yd_parts += (_swa_attention(qd, kd, vd, p['sink'], *attn, False),)
        merged = _gated_merge(h, ya, yb, yc_parts, yd_parts, p['w_gates'], p['b_gates'], p['w_branch'],
                              rows_mix, tm_ffn, GATE_COLS)
        xs = _outproj(merged, x_parts, m, row_of_tile, npost[1], p['w_out'], tm)

        ffn2 = (xs, rows_mix, m, 2, row_of_ffn_tile, npre[2], npost[2], wts2, tm_ffn)
        if last:
            xs = _ffn(*ffn2)
        else:
            xs, wts = _ffn(*ffn2, cast_next=ff32 + (l + 1, 0))
    return xs[:n_lat].reshape(batch, seq_lat, d)
```

```python
import functools
import math

import jax
import jax.numpy as jnp
from jax import lax
from jax.experimental import pallas as pl
from jax.experimental.pallas import tpu as pltpu

F32 = jnp.float32
BF16 = jnp.bfloat16

EPS = 1e-6
ROPE_BASE = 10000.0
GRID_W = 64
N_MOD = 9
COND_ROWS = 8
MOD_STREAMS = 4
LANES = 128
SUBLANES = 8
BF16_ROWS = 16
ROT_DIM = 64
ROT_QUARTER = ROT_DIM // 4
CONV_W = 31
CONV_HALO = 16
CHUNK = 128
GM_GROUPS = 4
BRANCH_W = 512
MLA_HEADS = 4
MLA_NOPE = 128
MLA_V = 128
MLA_Q_RANK = 512
MLA_KV_RANK = 256
MLA_QK = 256
MLA_SUM_ROWS = 16
MLA_GROUP_Q = 512
MLA_GROUPS = 4
LOG2_E = math.log2(math.e)
MLA_SCALE = (MLA_NOPE + ROT_DIM) ** -0.5 * LOG2_E
SWA_HEADS = 8
SWA_KV = 2
SWA_GROUP = SWA_HEADS // SWA_KV
SWA_BLOCK = 128
SWA_WINDOW_BLOCKS = 4
SWA_SUM_ROWS = 16
WINDOW = 128
SWA_SCALE = ROT_DIM ** -0.5 * LOG2_E
IN_WIDTHS = (1024, 1024, 896, 768)
D_FF_PAD = 5632
FF_CHUNK = 512
GATE_COLS = 512
ROW_BLOCK = 1024
V7X_VMEM_BYTES = 64 * 1024 * 1024
VMEM_LIMIT = V7X_VMEM_BYTES - 8 * 1024 * 1024
VMEM_LIMIT_FFN = V7X_VMEM_BYTES - 2 * 1024 * 1024


def _cparams(sem, vmem_limit=VMEM_LIMIT):
    return pltpu.CompilerParams(dimension_semantics=sem, vmem_limit_bytes=vmem_limit)


def _inv_rms(x):
    return lax.rsqrt(jnp.mean(x * x, axis=-1, keepdims=True) + EPS)


def _rms(x):
    return x * _inv_rms(x)


def _layernorm(x, g, b):
    mu = jnp.mean(x, axis=-1, keepdims=True)
    xc = x - mu
    var = jnp.mean(xc * xc, axis=-1, keepdims=True)
    return xc * lax.rsqrt(var + EPS) * g + b


def _silu(x):
    return x * jax.nn.sigmoid(x)


def _part_specs(parts, tm, tile_of):
    specs, firsts, first = [], [], 0
    for part in parts:
        n = part.shape[0] // tm
        specs.append(pl.BlockSpec((tm, part.shape[1]),
                                  lambda *g, first=first, n=n: (jnp.clip(tile_of(*g) - first, 0, n - 1), 0)))
        firsts.append(first)
        first += n
    return specs, tuple(firsts)


def _load_parts(refs, firsts, tile):
    x = refs[0][...]
    for ref, first in zip(refs[1:], firsts[1:]):
        x = jnp.where(tile >= first, ref[...], x)
    return x


def _for_row_blocks(n_rows, body):
    block = min(ROW_BLOCK, n_rows)

    def step(r, carry):
        body(pl.ds(pl.multiple_of(r * block, block), block))
        return carry

    lax.fori_loop(0, n_rows // block, step, 0)


def _dot(a, b):
    return jnp.dot(a, b, preferred_element_type=F32)


def _cast_pad_cols_kernel(w_ref, o_ref):
    n = w_ref.shape[1]
    o_ref[:, :n] = w_ref[...].astype(BF16)
    o_ref[:, n:] = jnp.zeros((o_ref.shape[0], o_ref.shape[1] - n), BF16)


def _cast_pad_cols(w, l, s, n_pad, tr=256):
    m, n = w.shape[2:]
    return pl.pallas_call(
        _cast_pad_cols_kernel,
        grid=(m // tr,),
        in_specs=[pl.BlockSpec((None, None, tr, n), lambda i: (l, s, i, 0))],
        out_specs=pl.BlockSpec((tr, n_pad), lambda i: (i, 0)),
        out_shape=jax.ShapeDtypeStruct((m, n_pad), BF16),
        compiler_params=_cparams(("parallel",)),
    )(w)


def _cast_pad_rows_kernel(w_ref, o_ref, *, rows_valid):
    tr = w_ref.shape[0]
    row = pl.program_id(0) * tr + lax.broadcasted_iota(jnp.int32, w_ref.shape, 0)
    o_ref[...] = jnp.where(row < rows_valid, w_ref[...], 0.0).astype(BF16)


def _cast_pad_rows(w, l, s, m_pad, tr=512):
    m, n = w.shape[2:]
    return pl.pallas_call(
        functools.partial(_cast_pad_rows_kernel, rows_valid=m),
        grid=(m_pad // tr,),
        in_specs=[pl.BlockSpec((None, None, tr, n), lambda j: (l, s, j, 0))],
        out_specs=pl.BlockSpec((tr, n), lambda j: (j, 0)),
        out_shape=jax.ShapeDtypeStruct((m_pad, n), BF16),
        compiler_params=_cparams(("parallel",)),
    )(w)


def _mod_kernel(cond_ref, b_ref, *refs):
    w_refs, o_ref = refs[:-1], refs[-1]
    k = pl.program_id(1)

    @pl.when(k == 0)
    def _():
        o_ref[...] = jnp.broadcast_to(b_ref[...], o_ref.shape)

    s = _silu(cond_ref[...]).astype(BF16)
    part = o_ref.shape[1] // len(w_refs)
    for j, w_ref in enumerate(w_refs):
        o_ref[:, j * part:(j + 1) * part] += _dot(s, w_ref[...].astype(BF16))


def _modulation(cond, w_mod, b_mod, tk=256):
    n_layers, d, n = w_mod.shape
    part = n // MOD_STREAMS
    w_specs = [pl.BlockSpec((None, tk, part), lambda l, k, j=j: (l, k, j)) for j in range(MOD_STREAMS)]
    return pl.pallas_call(
        _mod_kernel,
        grid=(n_layers, d // tk),
        in_specs=[
            pl.BlockSpec((COND_ROWS, tk), lambda l, k: (0, k)),
            pl.BlockSpec((None, 1, n), lambda l, k: (l, 0, 0)),
            *w_specs,
        ],
        out_specs=pl.BlockSpec((None, COND_ROWS, n), lambda l, k: (l, 0, 0)),
        out_shape=jax.ShapeDtypeStruct((n_layers, COND_ROWS, n), F32),
        compiler_params=_cparams(("parallel", "arbitrary")),
    )(cond, b_mod.reshape(n_layers, 1, n), *([w_mod] * MOD_STREAMS))


def _ffn_kernel(*refs, cast_next, d_ff):
    x_ref, shift_ref, scale_ref, gate_ref, gpre_ref, gpost_ref, wg_ref, wu_ref, wd_ref = refs[:9]
    if cast_next:
        wgn_ref, wun_ref, wdn_ref, o_ref, wgo_ref, wuo_ref, wdo_ref, h_ref, r_ref = refs[9:]
    else:
        o_ref, h_ref, r_ref = refs[9:]
    f = pl.program_id(1)

    @pl.when(f == 0)
    def _():
        gain = gpre_ref[...] * (1.0 + scale_ref[...])
        shift = shift_ref[...]

        def norms(rows):
            r_ref[rows, :] = _inv_rms(x_ref[rows, :])

        def block(rows):
            h_ref[rows, :] = (x_ref[rows, :] * r_ref[rows, :] * gain + shift).astype(BF16)
            o_ref[rows, :] = jnp.zeros((rows.size, o_ref.shape[1]), F32)

        _for_row_blocks(x_ref.shape[0], norms)
        _for_row_blocks(x_ref.shape[0], block)

    h = h_ref[...]
    g = _dot(h, wg_ref[...])
    u = _dot(h, wu_ref[...])
    a = (_silu(g) * u).astype(BF16)
    o_ref[...] += _dot(a, wd_ref[...])

    if cast_next:
        for src_ref, dst_ref in ((wgn_ref, wgo_ref), (wun_ref, wuo_ref)):
            dst_ref[:, :d_ff] = src_ref[...].astype(BF16)
            dst_ref[:, d_ff:] = jnp.zeros((dst_ref.shape[0], dst_ref.shape[1] - d_ff), BF16)
        rd = wdo_ref.shape[0]
        step = pl.program_id(0) * pl.num_programs(1) + f
        row = jnp.minimum(step, cast_next - 1) * rd + lax.broadcasted_iota(jnp.int32, wdo_ref.shape, 0)
        wdo_ref[...] = jnp.where(row < d_ff, wdn_ref[...], 0.0).astype(BF16)

    @pl.when(f == pl.num_programs(1) - 1)
    def _():
        gain = 0.5 * gate_ref[...] * gpost_ref[...]

        def norms(rows):
            r_ref[rows, :] = _inv_rms(o_ref[rows, :])

        def block(rows):
            o_ref[rows, :] = x_ref[rows, :] + o_ref[rows, :] * r_ref[rows, :] * gain

        _for_row_blocks(x_ref.shape[0], norms)
        _for_row_blocks(x_ref.shape[0], block)


def _slab_rows(total, n_steps):
    slab = BF16_ROWS
    while total % slab or total // slab > n_steps:
        slab *= 2
        assert slab <= total
    return slab


def _ffn(x, rows, mod, j, row_of_tile, gpre, gpost, wts, tm, cast_next=None):
    wg, wu, wd = wts
    d = x.shape[1]
    f_pad = wg.shape[-1]
    tf = FF_CHUNK
    nf = f_pad // tf

    def mspec(k):
        return pl.BlockSpec((None, None, 1, d), lambda i, f: (row_of_tile(i), k, 0, 0))

    vec = pl.BlockSpec((1, d), lambda i, f: (0, 0))
    in_specs = [
        pl.BlockSpec((tm, d), lambda i, f: (i, 0)),
        mspec(3 * j), mspec(3 * j + 1), mspec(3 * j + 2),
        vec, vec,
        pl.BlockSpec((d, tf), lambda i, f: (0, f)),
        pl.BlockSpec((d, tf), lambda i, f: (0, f)),
        pl.BlockSpec((tf, d), lambda i, f: (f, 0)),
    ]
    args = [x, mod, mod, mod, gpre, gpost, wg, wu, wd]
    out_specs = [pl.BlockSpec((tm, d), lambda i, f: (i, 0))]
    out_shape = [jax.ShapeDtypeStruct((rows, d), F32)]
    down_slabs, d_ff = 0, f_pad
    if cast_next is not None:
        wg32, wu32, wd32, l2, s2 = cast_next
        d_ff = wg32.shape[-1]
        n_steps = (rows // tm) * nf
        rg, rd = _slab_rows(d, n_steps), _slab_rows(f_pad, n_steps)
        up_slabs, down_slabs, down_src_slabs = d // rg, f_pad // rd, pl.cdiv(d_ff, rd)

        def slab(count):
            return lambda i, f: jnp.minimum(i * nf + f, count - 1)

        in_specs += [pl.BlockSpec((None, None, rg, d_ff), lambda i, f: (l2, s2, slab(up_slabs)(i, f), 0))] * 2
        in_specs += [pl.BlockSpec((None, None, rd, d), lambda i, f: (l2, s2, slab(down_src_slabs)(i, f), 0))]
        args += [wg32, wu32, wd32]
        out_specs += [pl.BlockSpec((rg, f_pad), lambda i, f: (slab(up_slabs)(i, f), 0))] * 2
        out_specs += [pl.BlockSpec((rd, d), lambda i, f: (slab(down_slabs)(i, f), 0))]
        out_shape += [jax.ShapeDtypeStruct((d, f_pad), BF16)] * 2 + [jax.ShapeDtypeStruct((f_pad, d), BF16)]
    outs = pl.pallas_call(
        functools.partial(_ffn_kernel, cast_next=down_slabs, d_ff=d_ff),
        grid=(rows // tm, nf),
        in_specs=in_specs,
        out_specs=out_specs,
        out_shape=out_shape,
        scratch_shapes=[pltpu.VMEM((tm, d), BF16), pltpu.VMEM((tm, 1), F32)],
        compiler_params=_cparams(("parallel", "arbitrary"), VMEM_LIMIT_FFN),
    )(*args)
    return outs[0] if cast_next is None else (outs[0], tuple(outs[1:]))


def _inproj_kernel(*refs, x_firsts):
    x_refs = refs[:len(x_firsts)]
    (shift_ref, scale_ref, gpre_ref, w_ref, b_ref, cos_ref, sina_ref, sinb_ref, qn_ref, kvn_ref, wuq_ref, wukv_ref,
     lng_ref, lnb_ref, ws_ref, bs_ref,
     h_ref, ya_ref, hb_ref, qc_ref, kc_ref, vc_ref, qd_ref, kd_ref, vd_ref) = refs[len(x_firsts):]
    y = _rms(_load_parts(x_refs, x_firsts, pl.program_id(0))) * gpre_ref[...]
    h = (y * (1.0 + scale_ref[...]) + shift_ref[...]).astype(BF16)
    h_ref[...] = h

    def proj(lo, hi):
        return _dot(h, w_ref[:, lo:hi]) + b_ref[:, lo:hi]

    na, nb, nc, nd = IN_WIDTHS
    _qkv_prep(proj(na + nb, na + nb + nc), proj(na + nb + nc, na + nb + nc + nd),
              cos_ref, sina_ref, sinb_ref, qn_ref, kvn_ref, wuq_ref, wukv_ref,
              qc_ref, kc_ref, vc_ref, qd_ref, kd_ref, vd_ref)
    w = BRANCH_W
    zb = proj(na, na + nb)
    hb_ref[...] = zb[:, :w] * jax.nn.sigmoid(zb[:, w:])
    za = proj(0, na)
    g = 0.5 * za * (1.0 + lax.erf(za * (2.0 ** -0.5)))
    vln = _layernorm(g[:, w:], lng_ref[...], lnb_ref[...]).astype(BF16)
    for c in range(za.shape[0] // CHUNK):
        rows = slice(c * CHUNK, (c + 1) * CHUNK)
        for gi in range(GM_GROUPS):
            cols = slice(gi * CHUNK, (gi + 1) * CHUNK)
            sv = _dot(ws_ref[gi], vln[rows, cols]) + bs_ref[gi]
            ya_ref[rows, cols] = (g[rows, cols] * sv).astype(BF16)


def _inproj(x_parts, mod, row_of_tile, gpre, w, b, cos, sina, sinb, qn, kvn, wuq, wukv, lng, lnb, ws, bs, tm):
    rows, d = sum(part.shape[0] for part in x_parts), x_parts[0].shape[1]
    x_specs, x_firsts = _part_specs(x_parts, tm, lambda i: i)
    n = w.shape[1]

    def mspec(k):
        return pl.BlockSpec((None, None, 1, d), lambda i: (row_of_tile(i), k, 0, 0))

    def rowspec(wd):
        return pl.BlockSpec((tm, wd), lambda i: (i, 0))

    def full(a):
        return pl.BlockSpec(a.shape, lambda i: (0,) * a.ndim)

    def headspec(nh, wd):
        return pl.BlockSpec((nh, tm, wd), lambda i: (0, i, 0))

    def headspec_t(nh, wd):
        return pl.BlockSpec((nh, wd, tm), lambda i: (0, 0, i))

    kc = jax.ShapeDtypeStruct((MLA_HEADS, rows, MLA_QK), BF16)
    qc = jax.ShapeDtypeStruct((MLA_HEADS, MLA_QK, rows), BF16)
    vc = jax.ShapeDtypeStruct((MLA_HEADS, MLA_V + MLA_SUM_ROWS, rows), BF16)
    qd = jax.ShapeDtypeStruct((SWA_HEADS // 2, LANES, rows), BF16)
    kd = jax.ShapeDtypeStruct((rows, LANES), BF16)
    vd = jax.ShapeDtypeStruct((LANES, rows), BF16)
    outs = [jax.ShapeDtypeStruct((rows, d), BF16), jax.ShapeDtypeStruct((rows, BRANCH_W), BF16),
            jax.ShapeDtypeStruct((rows, BRANCH_W), F32)]
    ospecs = [rowspec(d), rowspec(BRANCH_W), rowspec(BRANCH_W)]
    return pl.pallas_call(
        functools.partial(_inproj_kernel, x_firsts=x_firsts),
        grid=(rows // tm,),
        in_specs=[
            *x_specs,
            mspec(3), mspec(4),
            pl.BlockSpec((1, d), lambda i: (0, 0)),
            pl.BlockSpec((d, n), lambda i: (0, 0), pipeline_mode=pl.Buffered(1)),
            pl.BlockSpec((1, n), lambda i: (0, 0)),
            rowspec(LANES), rowspec(LANES), rowspec(LANES), full(qn), full(kvn), full(wuq), full(wukv),
            full(lng), full(lnb), full(ws), full(bs),
        ],
        out_specs=ospecs + [headspec_t(MLA_HEADS, MLA_QK), headspec(MLA_HEADS, MLA_QK),
                            headspec_t(MLA_HEADS, MLA_V + MLA_SUM_ROWS), headspec_t(SWA_HEADS // 2, LANES),
                            rowspec(LANES), pl.BlockSpec((LANES, tm), lambda i: (0, i))],
        out_shape=outs + [qc, kc, vc, qd, kd, vd],
        compiler_params=_cparams(("parallel",)),
    )(*x_parts, mod, mod, gpre, w, b, cos, sina, sinb, qn, kvn, wuq, wukv, lng, lnb, ws, bs)


def _conv_kernel(h_ref, hprev_ref, hnext_ref, dw_ref, db_ref, lng_ref, lnb_ref, yb_ref, ext_ref, shift_ref,
                 *, ts, n_lat_tiles, lat_tiles_per_seq, ctx_tiles_per_seq):
    i = pl.program_id(0)
    in_lat = i < n_lat_tiles
    pos = jnp.where(in_lat, i % lat_tiles_per_seq, (i - n_lat_tiles) % ctx_tiles_per_seq)
    last = jnp.where(in_lat, lat_tiles_per_seq - 1, ctx_tiles_per_seq - 1)
    ext_ref[0:CONV_HALO, :] = jnp.where(pos != 0, hprev_ref[...], 0.0)
    ext_ref[CONV_HALO:CONV_HALO + ts, :] = h_ref[...]
    ext_ref[CONV_HALO + ts:, :] = jnp.where(pos != last, hnext_ref[...], 0.0)
    span = ts + 2 * CONV_HALO - SUBLANES
    for r in range(1, SUBLANES):
        shift_ref[r - 1, 0:span, :] = ext_ref[r:r + span, :]
    first_tap = CONV_HALO - CONV_W // 2
    acc = jnp.zeros((ts, BRANCH_W), F32) + db_ref[...]
    for k in range(CONV_W):
        a, r = divmod(first_tap + k, SUBLANES)
        win = ext_ref[a * SUBLANES:a * SUBLANES + ts, :] if r == 0 else shift_ref[r - 1, a * SUBLANES:a * SUBLANES + ts, :]
        acc = acc + dw_ref[k:k + 1, :] * win
    yb_ref[...] = _silu(_layernorm(acc, lng_ref[...], lnb_ref[...])).astype(BF16)


def _conv_mixer(hb, dw, db, lng, lnb, n_lat_rows, seq_lat, seq_ctx, ts):
    rows, w = hb.shape
    halo_blocks = ts // CONV_HALO
    n_halo = rows // CONV_HALO
    kern = functools.partial(_conv_kernel, ts=ts, n_lat_tiles=n_lat_rows // ts,
                             lat_tiles_per_seq=seq_lat // ts, ctx_tiles_per_seq=seq_ctx // ts)
    vec = pl.BlockSpec((1, w), lambda i: (0, 0))
    return pl.pallas_call(
        kern,
        grid=(rows // ts,),
        in_specs=[
            pl.BlockSpec((ts, w), lambda i: (i, 0)),
            pl.BlockSpec((CONV_HALO, w), lambda i: (jnp.maximum(i * halo_blocks - 1, 0), 0)),
            pl.BlockSpec((CONV_HALO, w), lambda i: (jnp.minimum((i + 1) * halo_blocks, n_halo - 1), 0)),
            pl.BlockSpec((CONV_W, w), lambda i: (0, 0)),
            vec, vec, vec,
        ],
        out_specs=pl.BlockSpec((ts, w), lambda i: (i, 0)),
        out_shape=jax.ShapeDtypeStruct((rows, w), BF16),
        scratch_shapes=[pltpu.VMEM((ts + 2 * CONV_HALO, w), F32),
                        pltpu.VMEM((SUBLANES - 1, ts + 2 * CONV_HALO - SUBLANES, w), F32)],
        compiler_params=_cparams(("parallel",)),
    )(hb, hb, hb, dw, db, lng, lnb)


def _qkv_prep(zc, zd, cos_ref, sina_ref, sinb_ref, qn_ref, kvn_ref, wuq_ref, wukv_ref,
              qc_ref, kc_ref, vc_ref, qd_ref, kd_ref, vd_ref):
    cos, sina, sinb = cos_ref[...], sina_ref[...], sinb_ref[...]
    low = lax.broadcasted_iota(jnp.int32, cos.shape, 1) < ROT_DIM

    def rope(slab):
        return (slab * cos + pltpu.roll(slab, LANES - ROT_QUARTER, 1) * sina
                + pltpu.roll(slab, ROT_QUARTER, 1) * sinb)

    nq = MLA_HEADS * MLA_NOPE
    kv0 = MLA_Q_RANK + MLA_KV_RANK
    q = _dot((_rms(zc[:, :MLA_Q_RANK]) * qn_ref[...]).astype(BF16), wuq_ref[...]) * MLA_SCALE
    kv = _dot((_rms(zc[:, MLA_Q_RANK:kv0]) * kvn_ref[...]).astype(BF16), wukv_ref[...])
    kr = rope(zc[:, kv0:kv0 + LANES])
    ones = jnp.ones((MLA_SUM_ROWS, zc.shape[0]), BF16)
    for pair in range(MLA_HEADS // 2):
        qr = rope(q[:, nq + pair * LANES:nq + (pair + 1) * LANES]).T.astype(BF16)
        for h in (2 * pair, 2 * pair + 1):
            qc_ref[h, 0:MLA_NOPE, :] = q[:, h * MLA_NOPE:(h + 1) * MLA_NOPE].T.astype(BF16)
            qc_ref[h, MLA_NOPE:, :] = qr
            kc_ref[h, :, 0:MLA_NOPE] = kv[:, h * MLA_NOPE:(h + 1) * MLA_NOPE].astype(BF16)
            keep = low if h % 2 == 0 else jnp.logical_not(low)
            kc_ref[h, :, MLA_NOPE:] = jnp.where(keep, kr, 0.0).astype(BF16)
            vc_ref[h, 0:MLA_V, :] = kv[:, nq + h * MLA_V:nq + (h + 1) * MLA_V].T.astype(BF16)
            vc_ref[h, MLA_V:, :] = ones

    for pair in range(SWA_HEADS // 2):
        qd_ref[pair] = (rope(zd[:, pair * LANES:(pair + 1) * LANES]) * SWA_SCALE).T.astype(BF16)
    nqd = SWA_HEADS * ROT_DIM
    kd_ref[...] = rope(zd[:, nqd:nqd + LANES]).astype(BF16)
    vd_ref[...] = zd[:, nqd + LANES:nqd + 2 * LANES].T.astype(BF16)


def _mla_kernel(*refs, has_lat, groups):
    if has_lat:
        qt_ref, kl_ref, vlt_ref, kx_ref, vxt_ref, o_ref = refs
    else:
        qt_ref, kx_ref, vxt_ref, o_ref = refs
    cols = qt_ref.shape[1] // groups
    scores = []
    for g in range(groups):
        qt = qt_ref[:, g * cols:(g + 1) * cols]
        scores.append((_dot(kx_ref[...], qt), _dot(kl_ref[...], qt) if has_lat else None))
    for g, (sx, sl) in enumerate(scores):
        m = jnp.max(sx, axis=0, keepdims=True)
        if has_lat:
            m = jnp.maximum(m, jnp.max(sl, axis=0, keepdims=True))
        ot = _dot(vxt_ref[...], jnp.exp2(sx - m).astype(BF16))
        if has_lat:
            ot = ot + _dot(vlt_ref[...], jnp.exp2(sl - m).astype(BF16))
        o_ref[g * cols:(g + 1) * cols, :] = (ot[:MLA_V] / ot[MLA_V:MLA_V + 1]).T.astype(BF16)


def _mla_attention(qc, kc, vc, batch, seq_lat, seq_ctx, lat_queries):
    n_lat_rows = batch * seq_lat
    ctx_blk0 = n_lat_rows // seq_ctx
    tq = math.gcd(MLA_GROUPS * MLA_GROUP_Q, seq_lat if lat_queries else seq_ctx)
    if lat_queries:
        nq, q0 = seq_lat // tq, 0
    else:
        nq, q0 = seq_ctx // tq, n_lat_rows // tq
    vrows = vc.shape[1]
    qspec = pl.BlockSpec((None, MLA_QK, tq), lambda b, h, i: (h, 0, q0 + b * nq + i))
    kx = pl.BlockSpec((None, seq_ctx, MLA_QK), lambda b, h, i: (h, ctx_blk0 + b, 0))
    vx = pl.BlockSpec((None, vrows, seq_ctx), lambda b, h, i: (h, 0, ctx_blk0 + b))
    if lat_queries:
        kl = pl.BlockSpec((None, seq_lat, MLA_QK), lambda b, h, i: (h, b, 0))
        vl = pl.BlockSpec((None, vrows, seq_lat), lambda b, h, i: (h, 0, b))
        in_specs, args = [qspec, kl, vl, kx, vx], (qc, kc, vc, kc, vc)
    else:
        in_specs, args = [qspec, kx, vx], (qc, kc, vc)
    return pl.pallas_call(
        functools.partial(_mla_kernel, has_lat=lat_queries, groups=max(1, tq // MLA_GROUP_Q)),
        grid=(batch, MLA_HEADS, nq),
        in_specs=in_specs,
        out_specs=pl.BlockSpec((tq, MLA_V), lambda b, h, i: (b * nq + i, h)),
        out_shape=jax.ShapeDtypeStruct((batch * nq * tq, MLA_HEADS * MLA_V), BF16),
        compiler_params=_cparams(("parallel", "parallel", "parallel")),
    )(*args)


def _swa_kernel(*refs, has_lat):
    if has_lat:
        qt_ref, kx_ref, vxt_ref, sink_ref, bias_ref = refs[:5]
        kw_refs, vw_refs, o_ref = refs[5:5 + SWA_WINDOW_BLOCKS], refs[5 + SWA_WINDOW_BLOCKS:-1], refs[-1]
    else:
        qt_ref, kx_ref, vxt_ref, sink_ref, o_ref = refs
    blocks = qt_ref.shape[2] // SWA_BLOCK
    half = LANES // 2
    span = 3 * SWA_BLOCK
    gcol = SWA_GROUP * SWA_BLOCK
    zeros = jnp.zeros((half, gcol), BF16)
    ones = jnp.ones((SWA_SUM_ROWS, 1), BF16)
    if has_lat:
        k_win = jnp.concatenate([r[...] for r in kw_refs], axis=0)
        v_win = jnp.concatenate([r[...] for r in vw_refs], axis=1)
    chains = []
    for kv in range(SWA_KV):
        for blk in range(blocks):
            heads = range(kv * SWA_GROUP, (kv + 1) * SWA_GROUP)
            qt = jnp.concatenate([qt_ref[h // 2, (h % 2) * half:(h % 2 + 1) * half,
                                         blk * SWA_BLOCK:(blk + 1) * SWA_BLOCK] for h in heads], axis=1)
            qt = jnp.concatenate([qt, zeros] if kv == 0 else [zeros, qt], axis=0)
            k0 = blk * SWA_BLOCK
            chains.append((kv, blk, _dot(kx_ref[...], qt), _dot(k_win[k0:k0 + span], qt) if has_lat else None))
    for kv, blk, sx, sl in chains:
        rows = slice(kv * half, (kv + 1) * half)
        cols = slice(blk * gcol, (blk + 1) * gcol)
        k0 = blk * SWA_BLOCK
        sink = sink_ref[kv][:, cols] * LOG2_E
        m = jnp.maximum(sink, jnp.max(sx, axis=0, keepdims=True))
        if has_lat:
            sl = sl + bias_ref[k0:k0 + span, cols]
            m = jnp.maximum(m, jnp.max(sl, axis=0, keepdims=True))
        vx = vxt_ref[rows, :]
        ot = _dot(jnp.concatenate([vx, jnp.broadcast_to(ones, (SWA_SUM_ROWS, vx.shape[1]))], axis=0),
                  jnp.exp2(sx - m).astype(BF16))
        if has_lat:
            vl = v_win[rows, k0:k0 + span]
            ot = ot + _dot(jnp.concatenate([vl, jnp.broadcast_to(ones, (SWA_SUM_ROWS, span))], axis=0),
                           jnp.exp2(sl - m).astype(BF16))
        o = ot[:half] / (ot[half:half + 1] + jnp.exp2(sink - m))
        for pair in range(SWA_GROUP // 2):
            c0 = 2 * pair * SWA_BLOCK
            slab_t = jnp.concatenate([o[:, c0:c0 + SWA_BLOCK], o[:, c0 + SWA_BLOCK:c0 + 2 * SWA_BLOCK]], axis=0)
            lane0 = (kv * (SWA_GROUP // 2) + pair) * LANES
            o_ref[blk * SWA_BLOCK:(blk + 1) * SWA_BLOCK, lane0:lane0 + LANES] = slab_t.T.astype(BF16)


def _swa_window_bias(blocks):
    nkeys = SWA_WINDOW_BLOCKS * SWA_BLOCK
    ncol = blocks * SWA_GROUP * SWA_BLOCK
    key = jnp.arange(nkeys)[:, None] - SWA_BLOCK
    col = jnp.arange(ncol)[None, :]
    qry = (col // (SWA_GROUP * SWA_BLOCK)) * SWA_BLOCK + (col % SWA_BLOCK)
    inside = jnp.abs(key - qry) <= WINDOW
    after_start = key >= 0
    before_end = key < blocks * SWA_BLOCK
    variants = [inside, inside & after_start, inside & before_end, inside & after_start & before_end]
    return jnp.where(jnp.stack(variants), 0.0, -jnp.inf).astype(F32)


def _swa_attention(qd, kd, vd, sink, batch, seq_lat, seq_ctx, lat_queries):
    n_lat_rows = batch * seq_lat
    ctx_blk0 = n_lat_rows // seq_ctx
    seq_q = seq_lat if lat_queries else seq_ctx
    blocks = 2 if seq_q % (2 * SWA_BLOCK) == 0 and n_lat_rows % (2 * SWA_BLOCK) == 0 else 1
    tq = blocks * SWA_BLOCK
    nb = seq_q // tq
    q0 = 0 if lat_queries else n_lat_rows // tq
    kblocks = seq_lat // SWA_BLOCK
    sink_row = jnp.tile(jnp.repeat(sink.reshape(SWA_KV, SWA_GROUP), SWA_BLOCK, axis=1), (1, blocks))[:, None, :]
    in_specs = [pl.BlockSpec((SWA_HEADS // 2, LANES, tq), lambda b, i: (0, 0, q0 + b * nb + i)),
                pl.BlockSpec((seq_ctx, LANES), lambda b, i: (ctx_blk0 + b, 0)),
                pl.BlockSpec((LANES, seq_ctx), lambda b, i: (0, ctx_blk0 + b)),
                pl.BlockSpec(sink_row.shape, lambda b, i: (0, 0, 0))]
    args = [qd, kd, vd, sink_row]
    if lat_queries:
        assert blocks + 2 == SWA_WINDOW_BLOCKS
        bias = _swa_window_bias(blocks)
        in_specs.append(pl.BlockSpec((None,) + bias.shape[1:],
                                     lambda b, i: ((i == 0).astype(jnp.int32) + 2 * (i == nb - 1).astype(jnp.int32), 0, 0)))
        args.append(bias)

        def win(j):
            return lambda b, i: b * kblocks + jnp.clip(i * blocks - 1 + j, 0, kblocks - 1)

        in_specs += [pl.BlockSpec((SWA_BLOCK, LANES), lambda b, i, j=j: (win(j)(b, i), 0))
                     for j in range(SWA_WINDOW_BLOCKS)]
        in_specs += [pl.BlockSpec((LANES, SWA_BLOCK), lambda b, i, j=j: (0, win(j)(b, i)))
                     for j in range(SWA_WINDOW_BLOCKS)]
        args += [kd] * SWA_WINDOW_BLOCKS + [vd] * SWA_WINDOW_BLOCKS
    return pl.pallas_call(
        functools.partial(_swa_kernel, has_lat=lat_queries),
        grid=(batch, nb),
        in_specs=in_specs,
        out_specs=pl.BlockSpec((tq, BRANCH_W), lambda b, i: (b * nb + i, 0)),
        out_shape=jax.ShapeDtypeStruct((batch * nb * tq, BRANCH_W), BF16),
        compiler_params=_cparams(("parallel", "parallel")),
    )(*args)


def _gate_kernel(*refs, y_firsts):
    n = len(y_firsts)
    h_ref, ya_ref, yb_ref = refs[:3]
    yc_refs, yd_refs = refs[3:3 + n], refs[3 + n:3 + 2 * n]
    wg0, wg1, wg2, wg3, bg_ref, wb0, wb1, wb2, wb3, o_ref = refs[3 + 2 * n:]
    tile = pl.program_id(1)
    ys = (ya_ref[...], yb_ref[...], _load_parts(yc_refs, y_firsts, tile), _load_parts(yd_refs, y_firsts, tile))
    h = h_ref[...]
    acc = None
    for b, (y, wg_ref, wb_ref) in enumerate(zip(ys, (wg0, wg1, wg2, wg3), (wb0, wb1, wb2, wb3))):
        gate = jax.nn.sigmoid(_dot(h, wg_ref[...]) + bg_ref[b])
        term = gate * _dot(y, wb_ref[...])
        acc = term if acc is None else acc + term
    o_ref[...] = acc.astype(BF16)


def _gated_merge(h, ya, yb, yc_parts, yd_parts, wg, bg, wbr, rows, tm, tn):
    d = h.shape[1]
    ncol = d // tn
    wg_specs = [pl.BlockSpec((d, tn), lambda j, i, b=b: (0, b * ncol + j)) for b in range(4)]
    wb_specs = [pl.BlockSpec((None, BRANCH_W, tn), lambda j, i, b=b: (b, 0, j)) for b in range(4)]
    yspec = pl.BlockSpec((tm, BRANCH_W), lambda j, i: (i, 0))
    yc_specs, y_firsts = _part_specs(yc_parts, tm, lambda j, i: i)
    yd_specs, _ = _part_specs(yd_parts, tm, lambda j, i: i)
    return pl.pallas_call(
        functools.partial(_gate_kernel, y_firsts=y_firsts),
        grid=(ncol, rows // tm),
        in_specs=[pl.BlockSpec((tm, d), lambda j, i: (i, 0)), yspec, yspec, *yc_specs, *yd_specs,
                  *wg_specs, pl.BlockSpec((4, 1, tn), lambda j, i: (0, 0, j)), *wb_specs],
        out_specs=pl.BlockSpec((tm, tn), lambda j, i: (i, j)),
        out_shape=jax.ShapeDtypeStruct((rows, d), BF16),
        compiler_params=_cparams(("parallel", "parallel")),
    )(h, ya, yb, *yc_parts, *yd_parts, wg, wg, wg, wg, bg, wbr, wbr, wbr, wbr)


def _outproj_kernel(*refs, x_firsts):
    a_ref, x_refs = refs[0], refs[1:1 + len(x_firsts)]
    gate_ref, gpost_ref, w_ref, o_ref = refs[1 + len(x_firsts):]
    y = _dot(a_ref[...], w_ref[...])
    o_ref[...] = _load_parts(x_refs, x_firsts, pl.program_id(0)) + gate_ref[...] * (_rms(y) * gpost_ref[...])


def _outproj(a, x_parts, mod, row_of_tile, gpost, w, tm):
    rows, d = a.shape
    x_specs, x_firsts = _part_specs(x_parts, tm, lambda i: i)
    return pl.pallas_call(
        functools.partial(_outproj_kernel, x_firsts=x_firsts),
        grid=(rows // tm,),
        in_specs=[
            pl.BlockSpec((tm, d), lambda i: (i, 0)),
            *x_specs,
            pl.BlockSpec((None, None, 1, d), lambda i: (row_of_tile(i), 5, 0, 0)),
            pl.BlockSpec((1, d), lambda i: (0, 0)),
            pl.BlockSpec((d, d), lambda i: (0, 0), pipeline_mode=pl.Buffered(1)),
        ],
        out_specs=pl.BlockSpec((tm, d), lambda i: (i, 0)),
        out_shape=jax.ShapeDtypeStruct((rows, d), F32),
        compiler_params=_cparams(("parallel",)),
    )(a, *x_parts, mod, gpost, w)


def _rope_tables(seq_lat, n_ctx_rows, batch):
    half = ROT_DIM // 2
    freqs = 1.0 / (ROPE_BASE ** (jnp.arange(0, half, 2, dtype=F32) / half))
    t = jnp.arange(seq_lat)
    ang_r = (t // GRID_W).astype(F32)[:, None] * freqs[None, :]
    ang_c = (t % GRID_W).astype(F32)[:, None] * freqs[None, :]
    ang = jnp.concatenate([ang_r, ang_r, ang_c, ang_c], axis=-1)
    ang = jnp.tile(ang, (batch, LANES // ROT_DIM))
    cos, sin = jnp.cos(ang), jnp.sin(ang)
    first = (jnp.arange(LANES) % half) < ROT_QUARTER
    sina = jnp.where(first, -sin, 0.0)
    sinb = jnp.where(first, 0.0, sin)
    ones = jnp.ones((n_ctx_rows, LANES), F32)
    zeros = jnp.zeros((n_ctx_rows, LANES), F32)
    return (jnp.concatenate([cos, ones]), jnp.concatenate([sina, zeros]), jnp.concatenate([sinb, zeros]))


def _pack_layer(l, w_in, b_in, gm_ws, gm_bs, mla_w_uq, mla_w_ukv, swa_sink, w_branch, w_out):
    d = w_in.shape[1]
    p = {}
    wi, bi = w_in[l], b_in[l]
    na, nb, nc, nd = IN_WIDTHS
    c_hi = na + nb + nc - ROT_DIM
    kr = slice(c_hi - ROT_DIM, c_hi)
    p['w_abcd'] = jnp.concatenate([wi[:, :c_hi], wi[:, kr], wi[:, c_hi:c_hi + nd]], axis=1).astype(BF16)
    p['b_abcd'] = jnp.concatenate([bi[:c_hi], bi[kr], bi[c_hi:c_hi + nd]])[None, :]
    p['w_gates'] = wi[:, c_hi + nd:].astype(BF16)
    p['b_gates'] = bi[c_hi + nd:].reshape(4, 1, d)
    p['gm_ws'] = gm_ws[l].astype(BF16)
    p['gm_bs'] = gm_bs[l][:, :, None]
    uq = mla_w_uq[l].reshape(-1, MLA_HEADS, MLA_NOPE + ROT_DIM)
    p['wuq'] = jnp.concatenate([uq[:, :, :MLA_NOPE].reshape(-1, MLA_HEADS * MLA_NOPE),
                                uq[:, :, MLA_NOPE:].reshape(-1, MLA_HEADS * ROT_DIM)], axis=1).astype(BF16)
    ukv = mla_w_ukv[l].reshape(-1, MLA_HEADS, MLA_NOPE + MLA_V)
    p['wukv'] = jnp.concatenate([ukv[:, :, :MLA_NOPE].reshape(-1, MLA_HEADS * MLA_NOPE),
                                 ukv[:, :, MLA_NOPE:].reshape(-1, MLA_HEADS * MLA_V)], axis=1).astype(BF16)
    p['sink'] = swa_sink[l]
    p['w_branch'] = w_branch[l].astype(BF16)
    p['w_out'] = w_out[l].astype(BF16)
    return p


def kernel(x, c, ctx, c_ctx, w_mod, b_mod, norm_pre, norm_post, w_ff_gate, w_ff_up, w_ff_down, w_in, b_in, gm_ln_g, gm_ln_b, gm_ws, gm_bs, cv_dw, cv_db, cv_ln_g, cv_ln_b, mla_q_norm, mla_kv_norm, mla_w_uq, mla_w_ukv, swa_sink, w_branch, w_out):
    batch, seq_lat, d = x.shape
    seq_ctx = ctx.shape[1]
    depth = w_mod.shape[0]
    n_lat = batch * seq_lat
    n_ctx = batch * seq_ctx
    assert batch + 1 <= COND_ROWS and seq_lat % GRID_W == 0
    tm = math.gcd(512, math.gcd(seq_lat, n_ctx))
    ts = math.gcd(256, math.gcd(seq_lat, seq_ctx))
    assert tm % SWA_BLOCK == 0 and ts % CHUNK == 0 and seq_ctx % SWA_BLOCK == 0 and seq_lat >= 3 * SWA_BLOCK

    tm_ffn = math.gcd(1024, math.gcd(seq_lat, n_ctx))

    def cond_row(tile):
        return lambda i: jnp.where(i < n_lat // tile, i // (seq_lat // tile), batch)

    row_of_tile, row_of_ffn_tile = cond_row(tm), cond_row(tm_ffn)

    cond = jnp.zeros((COND_ROWS, d), F32).at[:batch].set(c).at[batch].set(c_ctx)
    mod = _modulation(cond, w_mod, b_mod).reshape(depth, COND_ROWS, N_MOD, 1, d)
    cos, sina, sinb = _rope_tables(seq_lat, n_ctx, batch)

    ff32 = (w_ff_gate, w_ff_up, w_ff_down)
    wts = (_cast_pad_cols(w_ff_gate, 0, 0, D_FF_PAD), _cast_pad_cols(w_ff_up, 0, 0, D_FF_PAD),
           _cast_pad_rows(w_ff_down, 0, 0, D_FF_PAD))

    n_all = n_lat + n_ctx
    xs = None
    for l in range(depth):
        last = l == depth - 1
        p = _pack_layer(l, w_in, b_in, gm_ws, gm_bs, mla_w_uq, mla_w_ukv, swa_sink, w_branch, w_out)
        m = mod[l]
        npre, npost = norm_pre[l][:, None, :], norm_post[l][:, None, :]

        ffn1 = (m, 0, row_of_ffn_tile, npre[0], npost[0])
        if xs is None:
            x_lat, wts2 = _ffn(x.reshape(n_lat, d), n_lat, *ffn1, wts, tm_ffn, cast_next=ff32 + (l, 1))
            x_ctx = _ffn(ctx.reshape(n_ctx, d), n_ctx, m, 0, lambda i: batch, npre[0], npost[0], wts, tm_ffn)
            x_parts = (x_lat, x_ctx)
        else:
            xs, wts2 = _ffn(xs, n_all, *ffn1, wts, tm_ffn, cast_next=ff32 + (l, 1))
            x_parts = (xs,)

        h, ya, hb, qc, kc, vc, qd, kd, vd = _inproj(
            x_parts, m, row_of_tile, npre[1], p['w_abcd'], p['b_abcd'], cos, sina, sinb,
            mla_q_norm[l][None], mla_kv_norm[l][None], p['wuq'], p['wukv'],
            gm_ln_g[l][None], gm_ln_b[l][None], p['gm_ws'], p['gm_bs'], tm)
        yb = _conv_mixer(hb, cv_dw[l], cv_db[l][None], cv_ln_g[l][None], cv_ln_b[l][None],
                         n_lat, seq_lat, seq_ctx, ts)
        rows_mix = n_lat if last else n_all
        attn = (batch, seq_lat, seq_ctx)
        yc_parts = (_mla_attention(qc, kc, vc, *attn, True),)
        yd_parts = (_swa_attention(qd, kd, vd, p['sink'], *attn, True),)
        if not last:
            yc_parts += (_mla_attention(qc, kc, vc, *attn, False),)
            yd_parts += (_swa_attention(qd, kd, vd, p['sink'], *attn, False),)
        merged = _gated_merge(h, ya, yb, yc_parts, yd_parts, p['w_gates'], p['b_gates'], p['w_branch'],
                              rows_mix, tm_ffn, GATE_COLS)
        xs = _outproj(merged, x_parts, m, row_of_tile, npost[1], p['w_out'], tm)

        ffn2 = (xs, rows_mix, m, 2, row_of_ffn_tile, npre[2], npost[2], wts2, tm_ffn)
        if last:
            xs = _ffn(*ffn2)
        else:
            xs, wts = _ffn(*ffn2, cast_next=ff32 + (l + 1, 0))
    return xs[:n_lat].reshape(batch, seq_lat, d)
```

```python
import functools
import math

import jax
import jax.numpy as jnp
from jax import lax
from jax.experimental import pallas as pl
from jax.experimental.pallas import tpu as pltpu

F32 = jnp.float32
BF16 = jnp.bfloat16

EPS = 1e-6
ROPE_BASE = 10000.0
GRID_W = 64
N_MOD = 9
COND_ROWS = 8
MOD_STREAMS = 4
LANES = 128
SUBLANES = 8
BF16_ROWS = 16
ROT_DIM = 64
ROT_QUARTER = ROT_DIM // 4
CONV_W = 31
CONV_HALO = 16
CHUNK = 128
GM_GROUPS = 4
BRANCH_W = 512
MLA_HEADS = 4
MLA_NOPE = 128
MLA_V = 128
MLA_Q_RANK = 512
MLA_KV_RANK = 256
MLA_QK = 256
MLA_SUM_ROWS = 16
MLA_GROUP_Q = 512
MLA_GROUPS = 4
LOG2_E = math.log2(math.e)
MLA_SCALE = (MLA_NOPE + ROT_DIM) ** -0.5 * LOG2_E
SWA_HEADS = 8
SWA_KV = 2
SWA_GROUP = SWA_HEADS // SWA_KV
SWA_BLOCK = 128
SWA_WINDOW_BLOCKS = 4
SWA_SUM_ROWS = 16
WINDOW = 128
SWA_SCALE = ROT_DIM ** -0.5 * LOG2_E
IN_WIDTHS = (1024, 1024, 896, 768)
D_FF_PAD = 5632
FF_CHUNK = 512
GATE_COLS = 512
ROW_BLOCK = 1024
V7X_VMEM_BYTES = 64 * 1024 * 1024
VMEM_LIMIT = V7X_VMEM_BYTES - 8 * 1024 * 1024
VMEM_LIMIT_FFN = V7X_VMEM_BYTES - 2 * 1024 * 1024


def _cparams(sem, vmem_limit=VMEM_LIMIT):
    return pltpu.CompilerParams(dimension_semantics=sem, vmem_limit_bytes=vmem_limit)


def _inv_rms(x):
    return lax.rsqrt(jnp.mean(x * x, axis=-1, keepdims=True) + EPS)


def _rms(x):
    return x * _inv_rms(x)


def _layernorm(x, g, b):
    mu = jnp.mean(x, axis=-1, keepdims=True)
    xc = x - mu
    var = jnp.mean(xc * xc, axis=-1, keepdims=True)
    return xc * lax.rsqrt(var + EPS) * g + b


def _silu(x):
    return x * jax.nn.sigmoid(x)


def _part_specs(parts, tm, tile_of):
    specs, firsts, first = [], [], 0
    for part in parts:
        n = part.shape[0] // tm
        specs.append(pl.BlockSpec((tm, part.shape[1]),
                                  lambda *g, first=first, n=n: (jnp.clip(tile_of(*g) - first, 0, n - 1), 0)))
        firsts.append(first)
        first += n
    return specs, tuple(firsts)


def _load_parts(refs, firsts, tile):
    x = refs[0][...]
    for ref, first in zip(refs[1:], firsts[1:]):
        x = jnp.where(tile >= first, ref[...], x)
    return x


def _for_row_blocks(n_rows, body):
    block = min(ROW_BLOCK, n_rows)

    def step(r, carry):
        body(pl.ds(pl.multiple_of(r * block, block), block))
        return carry

    lax.fori_loop(0, n_rows // block, step, 0)


def _dot(a, b):
    return jnp.dot(a, b, preferred_element_type=F32)


def _cast_pad_cols_kernel(w_ref, o_ref):
    n = w_ref.shape[1]
    o_ref[:, :n] = w_ref[...].astype(BF16)
    o_ref[:, n:] = jnp.zeros((o_ref.shape[0], o_ref.shape[1] - n), BF16)


def _cast_pad_cols(w, l, s, n_pad, tr=256):
    m, n = w.shape[2:]
    return pl.pallas_call(
        _cast_pad_cols_kernel,
        grid=(m // tr,),
        in_specs=[pl.BlockSpec((None, None, tr, n), lambda i: (l, s, i, 0))],
        out_specs=pl.BlockSpec((tr, n_pad), lambda i: (i, 0)),
        out_shape=jax.ShapeDtypeStruct((m, n_pad), BF16),
        compiler_params=_cparams(("parallel",)),
    )(w)


def _cast_pad_rows_kernel(w_ref, o_ref, *, rows_valid):
    tr = w_ref.shape[0]
    row = pl.program_id(0) * tr + lax.broadcasted_iota(jnp.int32, w_ref.shape, 0)
    o_ref[...] = jnp.where(row < rows_valid, w_ref[...], 0.0).astype(BF16)


def _cast_pad_rows(w, l, s, m_pad, tr=512):
    m, n = w.shape[2:]
    return pl.pallas_call(
        functools.partial(_cast_pad_rows_kernel, rows_valid=m),
        grid=(m_pad // tr,),
        in_specs=[pl.BlockSpec((None, None, tr, n), lambda j: (l, s, j, 0))],
        out_specs=pl.BlockSpec((tr, n), lambda j: (j, 0)),
        out_shape=jax.ShapeDtypeStruct((m_pad, n), BF16),
        compiler_params=_cparams(("parallel",)),
    )(w)


def _mod_kernel(cond_ref, b_ref, *refs):
    w_refs, o_ref = refs[:-1], refs[-1]
    k = pl.program_id(1)

    @pl.when(k == 0)
    def _():
        o_ref[...] = jnp.broadcast_to(b_ref[...], o_ref.shape)

    s = _silu(cond_ref[...]).astype(BF16)
    part = o_ref.shape[1] // len(w_refs)
    for j, w_ref in enumerate(w_refs):
        o_ref[:, j * part:(j + 1) * part] += _dot(s, w_ref[...].astype(BF16))


def _modulation(cond, w_mod, b_mod, tk=256):
    n_layers, d, n = w_mod.shape
    part = n // MOD_STREAMS
    w_specs = [pl.BlockSpec((None, tk, part), lambda l, k, j=j: (l, k, j)) for j in range(MOD_STREAMS)]
    return pl.pallas_call(
        _mod_kernel,
        grid=(n_layers, d // tk),
        in_specs=[
            pl.BlockSpec((COND_ROWS, tk), lambda l, k: (0, k)),
            pl.BlockSpec((None, 1, n), lambda l, k: (l, 0, 0)),
            *w_specs,
        ],
        out_specs=pl.BlockSpec((None, COND_ROWS, n), lambda l, k: (l, 0, 0)),
        out_shape=jax.ShapeDtypeStruct((n_layers, COND_ROWS, n), F32),
        compiler_params=_cparams(("parallel", "arbitrary")),
    )(cond, b_mod.reshape(n_layers, 1, n), *([w_mod] * MOD_STREAMS))


def _ffn_kernel(*refs, cast_next, d_ff, pack_in):
    x_ref, shift_ref, scale_ref, gate_ref, gpre_ref, gpost_ref, wg_ref, wu_ref, wd_ref = refs[:9]
    if cast_next and pack_in:
        (wgn_ref, wun_ref, wdn_ref, win_ref, o_ref, wgo_ref, wuo_ref, wdo_ref, wabcd_ref, wgates_ref,
         h_ref, r_ref) = refs[9:]
    elif cast_next:
        wgn_ref, wun_ref, wdn_ref, o_ref, wgo_ref, wuo_ref, wdo_ref, h_ref, r_ref = refs[9:]
    else:
        o_ref, h_ref, r_ref = refs[9:]
    f = pl.program_id(1)

    @pl.when(f == 0)
    def _():
        gain = gpre_ref[...] * (1.0 + scale_ref[...])
        shift = shift_ref[...]

        def norms(rows):
            r_ref[rows, :] = _inv_rms(x_ref[rows, :])

        def block(rows):
            h_ref[rows, :] = (x_ref[rows, :] * r_ref[rows, :] * gain + shift).astype(BF16)
            o_ref[rows, :] = jnp.zeros((rows.size, o_ref.shape[1]), F32)

        _for_row_blocks(x_ref.shape[0], norms)
        _for_row_blocks(x_ref.shape[0], block)

    h = h_ref[...]
    g = _dot(h, wg_ref[...])
    u = _dot(h, wu_ref[...])
    a = (_silu(g) * u).astype(BF16)
    o_ref[...] += _dot(a, wd_ref[...])

    if cast_next:
        for src_ref, dst_ref in ((wgn_ref, wgo_ref), (wun_ref, wuo_ref)):
            dst_ref[:, :d_ff] = src_ref[...].astype(BF16)
            dst_ref[:, d_ff:] = jnp.zeros((dst_ref.shape[0], dst_ref.shape[1] - d_ff), BF16)
        rd = wdo_ref.shape[0]
        step = pl.program_id(0) * pl.num_programs(1) + f
        row = jnp.minimum(step, cast_next - 1) * rd + lax.broadcasted_iota(jnp.int32, wdo_ref.shape, 0)
        wdo_ref[...] = jnp.where(row < d_ff, wdn_ref[...], 0.0).astype(BF16)
    if pack_in:
        w_in = win_ref[...]
        na, nb, nc, nd = IN_WIDTHS
        c_hi = na + nb + nc - ROT_DIM
        wabcd_ref[...] = jnp.concatenate([w_in[:, :c_hi], w_in[:, c_hi - ROT_DIM:c_hi], w_in[:, c_hi:c_hi + nd]],
                                         axis=1).astype(BF16)
        wgates_ref[...] = w_in[:, c_hi + nd:].astype(BF16)

    @pl.when(f == pl.num_programs(1) - 1)
    def _():
        gain = 0.5 * gate_ref[...] * gpost_ref[...]

        def norms(rows):
            r_ref[rows, :] = _inv_rms(o_ref[rows, :])

        def block(rows):
            o_ref[rows, :] = x_ref[rows, :] + o_ref[rows, :] * r_ref[rows, :] * gain

        _for_row_blocks(x_ref.shape[0], norms)
        _for_row_blocks(x_ref.shape[0], block)


def _slab_rows(total, n_steps):
    slab = BF16_ROWS
    while total % slab or total // slab > n_steps:
        slab *= 2
        assert slab <= total
    return slab


def _ffn(x, rows, mod, j, row_of_tile, gpre, gpost, wts, tm, cast_next=None, pack_in=None):
    wg, wu, wd = wts
    d = x.shape[1]
    f_pad = wg.shape[-1]
    tf = FF_CHUNK
    nf = f_pad // tf

    def mspec(k):
        return pl.BlockSpec((None, None, 1, d), lambda i, f: (row_of_tile(i), k, 0, 0))

    vec = pl.BlockSpec((1, d), lambda i, f: (0, 0))
    in_specs = [
        pl.BlockSpec((tm, d), lambda i, f: (i, 0)),
        mspec(3 * j), mspec(3 * j + 1), mspec(3 * j + 2),
        vec, vec,
        pl.BlockSpec((d, tf), lambda i, f: (0, f)),
        pl.BlockSpec((d, tf), lambda i, f: (0, f)),
        pl.BlockSpec((tf, d), lambda i, f: (f, 0)),
    ]
    args = [x, mod, mod, mod, gpre, gpost, wg, wu, wd]
    out_specs = [pl.BlockSpec((tm, d), lambda i, f: (i, 0))]
    out_shape = [jax.ShapeDtypeStruct((rows, d), F32)]
    down_slabs, d_ff = 0, f_pad
    if cast_next is not None:
        wg32, wu32, wd32, l2, s2 = cast_next
        d_ff = wg32.shape[-1]
        n_steps = (rows // tm) * nf
        rg, rd = _slab_rows(d, n_steps), _slab_rows(f_pad, n_steps)
        up_slabs, down_slabs, down_src_slabs = d // rg, f_pad // rd, pl.cdiv(d_ff, rd)

        def slab(count):
            return lambda i, f: jnp.minimum(i * nf + f, count - 1)

        in_specs += [pl.BlockSpec((None, None, rg, d_ff), lambda i, f: (l2, s2, slab(up_slabs)(i, f), 0))] * 2
        in_specs += [pl.BlockSpec((None, None, rd, d), lambda i, f: (l2, s2, slab(down_src_slabs)(i, f), 0))]
        args += [wg32, wu32, wd32]
        out_specs += [pl.BlockSpec((rg, f_pad), lambda i, f: (slab(up_slabs)(i, f), 0))] * 2
        out_specs += [pl.BlockSpec((rd, d), lambda i, f: (slab(down_slabs)(i, f), 0))]
        out_shape += [jax.ShapeDtypeStruct((d, f_pad), BF16)] * 2 + [jax.ShapeDtypeStruct((f_pad, d), BF16)]
        if pack_in is not None:
            w_in, l_in = pack_in
            n_in, n_abcd = w_in.shape[-1], sum(IN_WIDTHS)
            n_gates = n_in - (n_abcd - ROT_DIM)
            in_specs += [pl.BlockSpec((None, rg, n_in), lambda i, f: (l_in, slab(up_slabs)(i, f), 0))]
            args += [w_in]
            out_specs += [pl.BlockSpec((rg, n_abcd), lambda i, f: (slab(up_slabs)(i, f), 0)),
                          pl.BlockSpec((rg, n_gates), lambda i, f: (slab(up_slabs)(i, f), 0))]
            out_shape += [jax.ShapeDtypeStruct((d, n_abcd), BF16), jax.ShapeDtypeStruct((d, n_gates), BF16)]
    outs = pl.pallas_call(
        functools.partial(_ffn_kernel, cast_next=down_slabs, d_ff=d_ff, pack_in=pack_in is not None),
        grid=(rows // tm, nf),
        in_specs=in_specs,
        out_specs=out_specs,
        out_shape=out_shape,
        scratch_shapes=[pltpu.VMEM((tm, d), BF16), pltpu.VMEM((tm, 1), F32)],
        compiler_params=_cparams(("parallel", "arbitrary"), VMEM_LIMIT_FFN),
    )(*args)
    if cast_next is None:
        return outs[0]
    return (outs[0], tuple(outs[1:4])) + ((tuple(outs[4:]),) if pack_in is not None else ())


def _inproj_kernel(*refs, x_firsts):
    x_refs = refs[:len(x_firsts)]
    (shift_ref, scale_ref, gpre_ref, w_ref, b_ref, cos_ref, sina_ref, sinb_ref, qn_ref, kvn_ref, wuq_ref, wukv_ref,
     lng_ref, lnb_ref, ws_ref, bs_ref,
     h_ref, ya_ref, hb_ref, qc_ref, kc_ref, vc_ref, qd_ref, kd_ref, vd_ref) = refs[len(x_firsts):]
    y = _rms(_load_parts(x_refs, x_firsts, pl.program_id(0))) * gpre_ref[...]
    h = (y * (1.0 + scale_ref[...]) + shift_ref[...]).astype(BF16)
    h_ref[...] = h

    def proj(lo, hi):
        return _dot(h, w_ref[:, lo:hi]) + b_ref[:, lo:hi]

    na, nb, nc, nd = IN_WIDTHS
    _qkv_prep(proj(na + nb, na + nb + nc), proj(na + nb + nc, na + nb + nc + nd),
              cos_ref, sina_ref, sinb_ref, qn_ref, kvn_ref, wuq_ref, wukv_ref,
              qc_ref, kc_ref, vc_ref, qd_ref, kd_ref, vd_ref)
    w = BRANCH_W
    zb = proj(na, na + nb)
    hb_ref[...] = zb[:, :w] * jax.nn.sigmoid(zb[:, w:])
    za = proj(0, na)
    g = 0.5 * za * (1.0 + lax.erf(za * (2.0 ** -0.5)))
    vln = _layernorm(g[:, w:], lng_ref[...], lnb_ref[...]).astype(BF16)
    for c in range(za.shape[0] // CHUNK):
        rows = slice(c * CHUNK, (c + 1) * CHUNK)
        for gi in range(GM_GROUPS):
            cols = slice(gi * CHUNK, (gi + 1) * CHUNK)
            sv = _dot(ws_ref[gi], vln[rows, cols]) + bs_ref[gi]
            ya_ref[rows, cols] = (g[rows, cols] * sv).astype(BF16)


def _inproj(x_parts, mod, row_of_tile, gpre, w, b, cos, sina, sinb, qn, kvn, wuq, wukv, lng, lnb, ws, bs, tm):
    rows, d = sum(part.shape[0] for part in x_parts), x_parts[0].shape[1]
    x_specs, x_firsts = _part_specs(x_parts, tm, lambda i: i)
    n = w.shape[1]

    def mspec(k):
        return pl.BlockSpec((None, None, 1, d), lambda i: (row_of_tile(i), k, 0, 0))

    def rowspec(wd):
        return pl.BlockSpec((tm, wd), lambda i: (i, 0))

    def full(a):
        return pl.BlockSpec(a.shape, lambda i: (0,) * a.ndim)

    def headspec(nh, wd):
        return pl.BlockSpec((nh, tm, wd), lambda i: (0, i, 0))

    def headspec_t(nh, wd):
        return pl.BlockSpec((nh, wd, tm), lambda i: (0, 0, i))

    kc = jax.ShapeDtypeStruct((MLA_HEADS, rows, MLA_QK), BF16)
    qc = jax.ShapeDtypeStruct((MLA_HEADS, MLA_QK, rows), BF16)
    vc = jax.ShapeDtypeStruct((MLA_HEADS, MLA_V + MLA_SUM_ROWS, rows), BF16)
    qd = jax.ShapeDtypeStruct((SWA_HEADS // 2, LANES, rows), BF16)
    kd = jax.ShapeDtypeStruct((rows, LANES), BF16)
    vd = jax.ShapeDtypeStruct((LANES, rows), BF16)
    outs = [jax.ShapeDtypeStruct((rows, d), BF16), jax.ShapeDtypeStruct((rows, BRANCH_W), BF16),
            jax.ShapeDtypeStruct((rows, BRANCH_W), F32)]
    ospecs = [rowspec(d), rowspec(BRANCH_W), rowspec(BRANCH_W)]
    return pl.pallas_call(
        functools.partial(_inproj_kernel, x_firsts=x_firsts),
        grid=(rows // tm,),
        in_specs=[
            *x_specs,
            mspec(3), mspec(4),
            pl.BlockSpec((1, d), lambda i: (0, 0)),
            pl.BlockSpec((d, n), lambda i: (0, 0), pipeline_mode=pl.Buffered(1)),
            pl.BlockSpec((1, n), lambda i: (0, 0)),
            rowspec(LANES), rowspec(LANES), rowspec(LANES), full(qn), full(kvn), full(wuq), full(wukv),
            full(lng), full(lnb), full(ws), full(bs),
        ],
        out_specs=ospecs + [headspec_t(MLA_HEADS, MLA_QK), headspec(MLA_HEADS, MLA_QK),
                            headspec_t(MLA_HEADS, MLA_V + MLA_SUM_ROWS), headspec_t(SWA_HEADS // 2, LANES),
                            rowspec(LANES), pl.BlockSpec((LANES, tm), lambda i: (0, i))],
        out_shape=outs + [qc, kc, vc, qd, kd, vd],
        compiler_params=_cparams(("parallel",)),
    )(*x_parts, mod, mod, gpre, w, b, cos, sina, sinb, qn, kvn, wuq, wukv, lng, lnb, ws, bs)


def _conv_kernel(h_ref, hprev_ref, hnext_ref, dw_ref, db_ref, lng_ref, lnb_ref, yb_ref, ext_ref, shift_ref,
                 *, ts, n_lat_tiles, lat_tiles_per_seq, ctx_tiles_per_seq):
    i = pl.program_id(0)
    in_lat = i < n_lat_tiles
    pos = jnp.where(in_lat, i % lat_tiles_per_seq, (i - n_lat_tiles) % ctx_tiles_per_seq)
    last = jnp.where(in_lat, lat_tiles_per_seq - 1, ctx_tiles_per_seq - 1)
    ext_ref[0:CONV_HALO, :] = jnp.where(pos != 0, hprev_ref[...], 0.0)
    ext_ref[CONV_HALO:CONV_HALO + ts, :] = h_ref[...]
    ext_ref[CONV_HALO + ts:, :] = jnp.where(pos != last, hnext_ref[...], 0.0)
    span = ts + 2 * CONV_HALO - SUBLANES
    for r in range(1, SUBLANES):
        shift_ref[r - 1, 0:span, :] = ext_ref[r:r + span, :]
    first_tap = CONV_HALO - CONV_W // 2
    acc = jnp.zeros((ts, BRANCH_W), F32) + db_ref[...]
    for k in range(CONV_W):
        a, r = divmod(first_tap + k, SUBLANES)
        win = ext_ref[a * SUBLANES:a * SUBLANES + ts, :] if r == 0 else shift_ref[r - 1, a * SUBLANES:a * SUBLANES + ts, :]
        acc = acc + dw_ref[k:k + 1, :] * win
    yb_ref[...] = _silu(_layernorm(acc, lng_ref[...], lnb_ref[...])).astype(BF16)


def _conv_mixer(hb, dw, db, lng, lnb, n_lat_rows, seq_lat, seq_ctx, ts):
    rows, w = hb.shape
    halo_blocks = ts // CONV_HALO
    n_halo = rows // CONV_HALO
    kern = functools.partial(_conv_kernel, ts=ts, n_lat_tiles=n_lat_rows // ts,
                             lat_tiles_per_seq=seq_lat // ts, ctx_tiles_per_seq=seq_ctx // ts)
    vec = pl.BlockSpec((1, w), lambda i: (0, 0))
    return pl.pallas_call(
        kern,
        grid=(rows // ts,),
        in_specs=[
            pl.BlockSpec((ts, w), lambda i: (i, 0)),
            pl.BlockSpec((CONV_HALO, w), lambda i: (jnp.maximum(i * halo_blocks - 1, 0), 0)),
            pl.BlockSpec((CONV_HALO, w), lambda i: (jnp.minimum((i + 1) * halo_blocks, n_halo - 1), 0)),
            pl.BlockSpec((CONV_W, w), lambda i: (0, 0)),
            vec, vec, vec,
        ],
        out_specs=pl.BlockSpec((ts, w), lambda i: (i, 0)),
        out_shape=jax.ShapeDtypeStruct((rows, w), BF16),
        scratch_shapes=[pltpu.VMEM((ts + 2 * CONV_HALO, w), F32),
                        pltpu.VMEM((SUBLANES - 1, ts + 2 * CONV_HALO - SUBLANES, w), F32)],
        compiler_params=_cparams(("parallel",)),
    )(hb, hb, hb, dw, db, lng, lnb)


def _qkv_prep(zc, zd, cos_ref, sina_ref, sinb_ref, qn_ref, kvn_ref, wuq_ref, wukv_ref,
              qc_ref, kc_ref, vc_ref, qd_ref, kd_ref, vd_ref):
    cos, sina, sinb = cos_ref[...], sina_ref[...], sinb_ref[...]
    low = lax.broadcasted_iota(jnp.int32, cos.shape, 1) < ROT_DIM

    def rope(slab):
        return (slab * cos + pltpu.roll(slab, LANES - ROT_QUARTER, 1) * sina
                + pltpu.roll(slab, ROT_QUARTER, 1) * sinb)

    nq = MLA_HEADS * MLA_NOPE
    kv0 = MLA_Q_RANK + MLA_KV_RANK
    q = _dot((_rms(zc[:, :MLA_Q_RANK]) * qn_ref[...]).astype(BF16), wuq_ref[...]) * MLA_SCALE
    kv = _dot((_rms(zc[:, MLA_Q_RANK:kv0]) * kvn_ref[...]).astype(BF16), wukv_ref[...])
    kr = rope(zc[:, kv0:kv0 + LANES])
    ones = jnp.ones((MLA_SUM_ROWS, zc.shape[0]), BF16)
    for pair in range(MLA_HEADS // 2):
        qr = rope(q[:, nq + pair * LANES:nq + (pair + 1) * LANES]).T.astype(BF16)
        for h in (2 * pair, 2 * pair + 1):
            qc_ref[h, 0:MLA_NOPE, :] = q[:, h * MLA_NOPE:(h + 1) * MLA_NOPE].T.astype(BF16)
            qc_ref[h, MLA_NOPE:, :] = qr
            kc_ref[h, :, 0:MLA_NOPE] = kv[:, h * MLA_NOPE:(h + 1) * MLA_NOPE].astype(BF16)
            keep = low if h % 2 == 0 else jnp.logical_not(low)
            kc_ref[h, :, MLA_NOPE:] = jnp.where(keep, kr, 0.0).astype(BF16)
            vc_ref[h, 0:MLA_V, :] = kv[:, nq + h * MLA_V:nq + (h + 1) * MLA_V].T.astype(BF16)
            vc_ref[h, MLA_V:, :] = ones

    for pair in range(SWA_HEADS // 2):
        qd_ref[pair] = (rope(zd[:, pair * LANES:(pair + 1) * LANES]) * SWA_SCALE).T.astype(BF16)
    nqd = SWA_HEADS * ROT_DIM
    kd_ref[...] = rope(zd[:, nqd:nqd + LANES]).astype(BF16)
    vd_ref[...] = zd[:, nqd + LANES:nqd + 2 * LANES].T.astype(BF16)


def _mla_kernel(*refs, has_lat, groups):
    if has_lat:
        qt_ref, kl_ref, vlt_ref, kx_ref, vxt_ref, o_ref = refs
    else:
        qt_ref, kx_ref, vxt_ref, o_ref = refs
    cols = qt_ref.shape[1] // groups
    scores = []
    for g in range(groups):
        qt = qt_ref[:, g * cols:(g + 1) * cols]
        scores.append((_dot(kx_ref[...], qt), _dot(kl_ref[...], qt) if has_lat else None))
    for g, (sx, sl) in enumerate(scores):
        m = jnp.max(sx, axis=0, keepdims=True)
        if has_lat:
            m = jnp.maximum(m, jnp.max(sl, axis=0, keepdims=True))
        ot = _dot(vxt_ref[...], jnp.exp2(sx - m).astype(BF16))
        if has_lat:
            ot = ot + _dot(vlt_ref[...], jnp.exp2(sl - m).astype(BF16))
        o_ref[g * cols:(g + 1) * cols, :] = (ot[:MLA_V] / ot[MLA_V:MLA_V + 1]).T.astype(BF16)


def _mla_attention(qc, kc, vc, batch, seq_lat, seq_ctx, lat_queries):
    n_lat_rows = batch * seq_lat
    ctx_blk0 = n_lat_rows // seq_ctx
    tq = math.gcd(MLA_GROUPS * MLA_GROUP_Q, seq_lat if lat_queries else seq_ctx)
    if lat_queries:
        nq, q0 = seq_lat // tq, 0
    else:
        nq, q0 = seq_ctx // tq, n_lat_rows // tq
    vrows = vc.shape[1]
    qspec = pl.BlockSpec((None, MLA_QK, tq), lambda b, h, i: (h, 0, q0 + b * nq + i))
    kx = pl.BlockSpec((None, seq_ctx, MLA_QK), lambda b, h, i: (h, ctx_blk0 + b, 0))
    vx = pl.BlockSpec((None, vrows, seq_ctx), lambda b, h, i: (h, 0, ctx_blk0 + b))
    if lat_queries:
        kl = pl.BlockSpec((None, seq_lat, MLA_QK), lambda b, h, i: (h, b, 0))
        vl = pl.BlockSpec((None, vrows, seq_lat), lambda b, h, i: (h, 0, b))
        in_specs, args = [qspec, kl, vl, kx, vx], (qc, kc, vc, kc, vc)
    else:
        in_specs, args = [qspec, kx, vx], (qc, kc, vc)
    return pl.pallas_call(
        functools.partial(_mla_kernel, has_lat=lat_queries, groups=max(1, tq // MLA_GROUP_Q)),
        grid=(batch, MLA_HEADS, nq),
        in_specs=in_specs,
        out_specs=pl.BlockSpec((tq, MLA_V), lambda b, h, i: (b * nq + i, h)),
        out_shape=jax.ShapeDtypeStruct((batch * nq * tq, MLA_HEADS * MLA_V), BF16),
        compiler_params=_cparams(("parallel", "parallel", "parallel")),
    )(*args)


def _swa_kernel(*refs, has_lat):
    if has_lat:
        qt_ref, kx_ref, vxt_ref, sink_ref, bias_ref = refs[:5]
        kw_refs, vw_refs, o_ref = refs[5:5 + SWA_WINDOW_BLOCKS], refs[5 + SWA_WINDOW_BLOCKS:-1], refs[-1]
    else:
        qt_ref, kx_ref, vxt_ref, sink_ref, o_ref = refs
    blocks = qt_ref.shape[2] // SWA_BLOCK
    half = LANES // 2
    span = 3 * SWA_BLOCK
    gcol = SWA_GROUP * SWA_BLOCK
    zeros = jnp.zeros((half, gcol), BF16)
    ones = jnp.ones((SWA_SUM_ROWS, 1), BF16)
    if has_lat:
        k_win = jnp.concatenate([r[...] for r in kw_refs], axis=0)
        v_win = jnp.concatenate([r[...] for r in vw_refs], axis=1)
    chains = []
    for kv in range(SWA_KV):
        for blk in range(blocks):
            heads = range(kv * SWA_GROUP, (kv + 1) * SWA_GROUP)
            qt = jnp.concatenate([qt_ref[h // 2, (h % 2) * half:(h % 2 + 1) * half,
                                         blk * SWA_BLOCK:(blk + 1) * SWA_BLOCK] for h in heads], axis=1)
            qt = jnp.concatenate([qt, zeros] if kv == 0 else [zeros, qt], axis=0)
            k0 = blk * SWA_BLOCK
            chains.append((kv, blk, _dot(kx_ref[...], qt), _dot(k_win[k0:k0 + span], qt) if has_lat else None))
    for kv, blk, sx, sl in chains:
        rows = slice(kv * half, (kv + 1) * half)
        cols = slice(blk * gcol, (blk + 1) * gcol)
        k0 = blk * SWA_BLOCK
        sink = sink_ref[kv][:, cols] * LOG2_E
        m = jnp.maximum(sink, jnp.max(sx, axis=0, keepdims=True))
        if has_lat:
            sl = sl + bias_ref[k0:k0 + span, cols]
            m = jnp.maximum(m, jnp.max(sl, axis=0, keepdims=True))
        vx = vxt_ref[rows, :]
        ot = _dot(jnp.concatenate([vx, jnp.broadcast_to(ones, (SWA_SUM_ROWS, vx.shape[1]))], axis=0),
                  jnp.exp2(sx - m).astype(BF16))
        if has_lat:
            vl = v_win[rows, k0:k0 + span]
            ot = ot + _dot(jnp.concatenate([vl, jnp.broadcast_to(ones, (SWA_SUM_ROWS, span))], axis=0),
                           jnp.exp2(sl - m).astype(BF16))
        o = ot[:half] / (ot[half:half + 1] + jnp.exp2(sink - m))
        for pair in range(SWA_GROUP // 2):
            c0 = 2 * pair * SWA_BLOCK
            slab_t = jnp.concatenate([o[:, c0:c0 + SWA_BLOCK], o[:, c0 + SWA_BLOCK:c0 + 2 * SWA_BLOCK]], axis=0)
            lane0 = (kv * (SWA_GROUP // 2) + pair) * LANES
            o_ref[blk * SWA_BLOCK:(blk + 1) * SWA_BLOCK, lane0:lane0 + LANES] = slab_t.T.astype(BF16)


def _swa_window_bias(blocks):
    nkeys = SWA_WINDOW_BLOCKS * SWA_BLOCK
    ncol = blocks * SWA_GROUP * SWA_BLOCK
    key = jnp.arange(nkeys)[:, None] - SWA_BLOCK
    col = jnp.arange(ncol)[None, :]
    qry = (col // (SWA_GROUP * SWA_BLOCK)) * SWA_BLOCK + (col % SWA_BLOCK)
    inside = jnp.abs(key - qry) <= WINDOW
    after_start = key >= 0
    before_end = key < blocks * SWA_BLOCK
    variants = [inside, inside & after_start, inside & before_end, inside & after_start & before_end]
    return jnp.where(jnp.stack(variants), 0.0, -jnp.inf).astype(F32)


def _swa_attention(qd, kd, vd, sink, batch, seq_lat, seq_ctx, lat_queries):
    n_lat_rows = batch * seq_lat
    ctx_blk0 = n_lat_rows // seq_ctx
    seq_q = seq_lat if lat_queries else seq_ctx
    blocks = 2 if seq_q % (2 * SWA_BLOCK) == 0 and n_lat_rows % (2 * SWA_BLOCK) == 0 else 1
    tq = blocks * SWA_BLOCK
    nb = seq_q // tq
    q0 = 0 if lat_queries else n_lat_rows // tq
    kblocks = seq_lat // SWA_BLOCK
    sink_row = jnp.tile(jnp.repeat(sink.reshape(SWA_KV, SWA_GROUP), SWA_BLOCK, axis=1), (1, blocks))[:, None, :]
    in_specs = [pl.BlockSpec((SWA_HEADS // 2, LANES, tq), lambda b, i: (0, 0, q0 + b * nb + i)),
                pl.BlockSpec((seq_ctx, LANES), lambda b, i: (ctx_blk0 + b, 0)),
                pl.BlockSpec((LANES, seq_ctx), lambda b, i: (0, ctx_blk0 + b)),
                pl.BlockSpec(sink_row.shape, lambda b, i: (0, 0, 0))]
    args = [qd, kd, vd, sink_row]
    if lat_queries:
        assert blocks + 2 == SWA_WINDOW_BLOCKS
        bias = _swa_window_bias(blocks)
        in_specs.append(pl.BlockSpec((None,) + bias.shape[1:],
                                     lambda b, i: ((i == 0).astype(jnp.int32) + 2 * (i == nb - 1).astype(jnp.int32), 0, 0)))
        args.append(bias)

        def win(j):
            return lambda b, i: b * kblocks + jnp.clip(i * blocks - 1 + j, 0, kblocks - 1)

        in_specs += [pl.BlockSpec((SWA_BLOCK, LANES), lambda b, i, j=j: (win(j)(b, i), 0))
                     for j in range(SWA_WINDOW_BLOCKS)]
        in_specs += [pl.BlockSpec((LANES, SWA_BLOCK), lambda b, i, j=j: (0, win(j)(b, i)))
                     for j in range(SWA_WINDOW_BLOCKS)]
        args += [kd] * SWA_WINDOW_BLOCKS + [vd] * SWA_WINDOW_BLOCKS
    return pl.pallas_call(
        functools.partial(_swa_kernel, has_lat=lat_queries),
        grid=(batch, nb),
        in_specs=in_specs,
        out_specs=pl.BlockSpec((tq, BRANCH_W), lambda b, i: (b * nb + i, 0)),
        out_shape=jax.ShapeDtypeStruct((batch * nb * tq, BRANCH_W), BF16),
        compiler_params=_cparams(("parallel", "parallel")),
    )(*args)


def _gate_kernel(*refs, y_firsts):
    n = len(y_firsts)
    h_ref, ya_ref, yb_ref = refs[:3]
    yc_refs, yd_refs = refs[3:3 + n], refs[3 + n:3 + 2 * n]
    wg0, wg1, wg2, wg3, bg_ref, wb0, wb1, wb2, wb3, o_ref = refs[3 + 2 * n:]
    tile = pl.program_id(1)
    ys = (ya_ref[...], yb_ref[...], _load_parts(yc_refs, y_firsts, tile), _load_parts(yd_refs, y_firsts, tile))
    h = h_ref[...]
    acc = None
    for b, (y, wg_ref, wb_ref) in enumerate(zip(ys, (wg0, wg1, wg2, wg3), (wb0, wb1, wb2, wb3))):
        gate = jax.nn.sigmoid(_dot(h, wg_ref[...]) + bg_ref[b])
        term = gate * _dot(y, wb_ref[...])
        acc = term if acc is None else acc + term
    o_ref[...] = acc.astype(BF16)


def _gated_merge(h, ya, yb, yc_parts, yd_parts, wg, bg, wbr, rows, tm, tn):
    d = h.shape[1]
    ncol = d // tn
    wg_specs = [pl.BlockSpec((d, tn), lambda j, i, b=b: (0, b * ncol + j)) for b in range(4)]
    wb_specs = [pl.BlockSpec((None, BRANCH_W, tn), lambda j, i, b=b: (b, 0, j)) for b in range(4)]
    yspec = pl.BlockSpec((tm, BRANCH_W), lambda j, i: (i, 0))
    yc_specs, y_firsts = _part_specs(yc_parts, tm, lambda j, i: i)
    yd_specs, _ = _part_specs(yd_parts, tm, lambda j, i: i)
    return pl.pallas_call(
        functools.partial(_gate_kernel, y_firsts=y_firsts),
        grid=(ncol, rows // tm),
        in_specs=[pl.BlockSpec((tm, d), lambda j, i: (i, 0)), yspec, yspec, *yc_specs, *yd_specs,
                  *wg_specs, pl.BlockSpec((4, 1, tn), lambda j, i: (0, 0, j)), *wb_specs],
        out_specs=pl.BlockSpec((tm, tn), lambda j, i: (i, j)),
        out_shape=jax.ShapeDtypeStruct((rows, d), BF16),
        compiler_params=_cparams(("parallel", "parallel")),
    )(h, ya, yb, *yc_parts, *yd_parts, wg, wg, wg, wg, bg, wbr, wbr, wbr, wbr)


def _outproj_kernel(*refs, x_firsts):
    a_ref, x_refs = refs[0], refs[1:1 + len(x_firsts)]
    gate_ref, gpost_ref, w_ref, o_ref = refs[1 + len(x_firsts):]
    y = _dot(a_ref[...], w_ref[...])
    o_ref[...] = _load_parts(x_refs, x_firsts, pl.program_id(0)) + gate_ref[...] * (_rms(y) * gpost_ref[...])


def _outproj(a, x_parts, mod, row_of_tile, gpost, w, tm):
    rows, d = a.shape
    x_specs, x_firsts = _part_specs(x_parts, tm, lambda i: i)
    return pl.pallas_call(
        functools.partial(_outproj_kernel, x_firsts=x_firsts),
        grid=(rows // tm,),
        in_specs=[
            pl.BlockSpec((tm, d), lambda i: (i, 0)),
            *x_specs,
            pl.BlockSpec((None, None, 1, d), lambda i: (row_of_tile(i), 5, 0, 0)),
            pl.BlockSpec((1, d), lambda i: (0, 0)),
            pl.BlockSpec((d, d), lambda i: (0, 0), pipeline_mode=pl.Buffered(1)),
        ],
        out_specs=pl.BlockSpec((tm, d), lambda i: (i, 0)),
        out_shape=jax.ShapeDtypeStruct((rows, d), F32),
        compiler_params=_cparams(("parallel",)),
    )(a, *x_parts, mod, gpost, w)


def _rope_tables(seq_lat, n_ctx_rows, batch):
    half = ROT_DIM // 2
    freqs = 1.0 / (ROPE_BASE ** (jnp.arange(0, half, 2, dtype=F32) / half))
    t = jnp.arange(seq_lat)
    ang_r = (t // GRID_W).astype(F32)[:, None] * freqs[None, :]
    ang_c = (t % GRID_W).astype(F32)[:, None] * freqs[None, :]
    ang = jnp.concatenate([ang_r, ang_r, ang_c, ang_c], axis=-1)
    ang = jnp.tile(ang, (batch, LANES // ROT_DIM))
    cos, sin = jnp.cos(ang), jnp.sin(ang)
    first = (jnp.arange(LANES) % half) < ROT_QUARTER
    sina = jnp.where(first, -sin, 0.0)
    sinb = jnp.where(first, 0.0, sin)
    ones = jnp.ones((n_ctx_rows, LANES), F32)
    zeros = jnp.zeros((n_ctx_rows, LANES), F32)
    return (jnp.concatenate([cos, ones]), jnp.concatenate([sina, zeros]), jnp.concatenate([sinb, zeros]))


def _pack_layer(l, b_in, gm_ws, gm_bs, mla_w_uq, mla_w_ukv, swa_sink, w_branch, w_out):
    d = w_branch.shape[-1]
    p = {}
    bi = b_in[l]
    na, nb, nc, nd = IN_WIDTHS
    c_hi = na + nb + nc - ROT_DIM
    kr = slice(c_hi - ROT_DIM, c_hi)
    p['b_abcd'] = jnp.concatenate([bi[:c_hi], bi[kr], bi[c_hi:c_hi + nd]])[None, :]
    p['b_gates'] = bi[c_hi + nd:].reshape(4, 1, d)
    p['gm_ws'] = gm_ws[l].astype(BF16)
    p['gm_bs'] = gm_bs[l][:, :, None]
    uq = mla_w_uq[l].reshape(-1, MLA_HEADS, MLA_NOPE + ROT_DIM)
    p['wuq'] = jnp.concatenate([uq[:, :, :MLA_NOPE].reshape(-1, MLA_HEADS * MLA_NOPE),
                                uq[:, :, MLA_NOPE:].reshape(-1, MLA_HEADS * ROT_DIM)], axis=1).astype(BF16)
    ukv = mla_w_ukv[l].reshape(-1, MLA_HEADS, MLA_NOPE + MLA_V)
    p['wukv'] = jnp.concatenate([ukv[:, :, :MLA_NOPE].reshape(-1, MLA_HEADS * MLA_NOPE),
                                 ukv[:, :, MLA_NOPE:].reshape(-1, MLA_HEADS * MLA_V)], axis=1).astype(BF16)
    p['sink'] = swa_sink[l]
    p['w_branch'] = w_branch[l].astype(BF16)
    p['w_out'] = w_out[l].astype(BF16)
    return p


def kernel(x, c, ctx, c_ctx, w_mod, b_mod, norm_pre, norm_post, w_ff_gate, w_ff_up, w_ff_down, w_in, b_in, gm_ln_g, gm_ln_b, gm_ws, gm_bs, cv_dw, cv_db, cv_ln_g, cv_ln_b, mla_q_norm, mla_kv_norm, mla_w_uq, mla_w_ukv, swa_sink, w_branch, w_out):
    batch, seq_lat, d = x.shape
    seq_ctx = ctx.shape[1]
    depth = w_mod.shape[0]
    n_lat = batch * seq_lat
    n_ctx = batch * seq_ctx
    assert batch + 1 <= COND_ROWS and seq_lat % GRID_W == 0
    tm = math.gcd(512, math.gcd(seq_lat, n_ctx))
    ts = math.gcd(256, math.gcd(seq_lat, seq_ctx))
    assert tm % SWA_BLOCK == 0 and ts % CHUNK == 0 and seq_ctx % SWA_BLOCK == 0 and seq_lat >= 3 * SWA_BLOCK

    tm_ffn = math.gcd(1024, math.gcd(seq_lat, n_ctx))

    def cond_row(tile):
        return lambda i: jnp.where(i < n_lat // tile, i // (seq_lat // tile), batch)

    row_of_tile, row_of_ffn_tile = cond_row(tm), cond_row(tm_ffn)

    cond = jnp.zeros((COND_ROWS, d), F32).at[:batch].set(c).at[batch].set(c_ctx)
    mod = _modulation(cond, w_mod, b_mod).reshape(depth, COND_ROWS, N_MOD, 1, d)
    cos, sina, sinb = _rope_tables(seq_lat, n_ctx, batch)

    ff32 = (w_ff_gate, w_ff_up, w_ff_down)
    wts = (_cast_pad_cols(w_ff_gate, 0, 0, D_FF_PAD), _cast_pad_cols(w_ff_up, 0, 0, D_FF_PAD),
           _cast_pad_rows(w_ff_down, 0, 0, D_FF_PAD))

    n_all = n_lat + n_ctx
    xs = None
    for l in range(depth):
        last = l == depth - 1
        p = _pack_layer(l, b_in, gm_ws, gm_bs, mla_w_uq, mla_w_ukv, swa_sink, w_branch, w_out)
        m = mod[l]
        npre, npost = norm_pre[l][:, None, :], norm_post[l][:, None, :]

        ffn1 = (m, 0, row_of_ffn_tile, npre[0], npost[0])
        if xs is None:
            x_lat, wts2, w_proj = _ffn(x.reshape(n_lat, d), n_lat, *ffn1, wts, tm_ffn, cast_next=ff32 + (l, 1),
                                       pack_in=(w_in, l))
            x_ctx = _ffn(ctx.reshape(n_ctx, d), n_ctx, m, 0, lambda i: batch, npre[0], npost[0], wts, tm_ffn)
            x_parts = (x_lat, x_ctx)
        else:
            xs, wts2 = _ffn(xs, n_all, *ffn1, wts, tm_ffn, cast_next=ff32 + (l, 1))
            x_parts, w_proj = (xs,), w_proj_next

        h, ya, hb, qc, kc, vc, qd, kd, vd = _inproj(
            x_parts, m, row_of_tile, npre[1], w_proj[0], p['b_abcd'], cos, sina, sinb,
            mla_q_norm[l][None], mla_kv_norm[l][None], p['wuq'], p['wukv'],
            gm_ln_g[l][None], gm_ln_b[l][None], p['gm_ws'], p['gm_bs'], tm)
        yb = _conv_mixer(hb, cv_dw[l], cv_db[l][None], cv_ln_g[l][None], cv_ln_b[l][None],
                         n_lat, seq_lat, seq_ctx, ts)
        rows_mix = n_lat if last else n_all
        attn = (batch, seq_lat, seq_ctx)
        yc_parts = (_mla_attention(qc, kc, vc, *attn, True),)
        yd_parts = (_swa_attention(qd, kd, vd, p['sink'], *attn, True),)
        if not last:
            yc_parts += (_mla_attention(qc, kc, vc, *attn, False),)
            yd_parts += (_swa_attention(qd, kd, vd, p['sink'], *attn, False),)
        merged = _gated_merge(h, ya, yb, yc_parts, yd_parts, w_proj[1], p['b_gates'], p['w_branch'],
                              rows_mix, tm_ffn, GATE_COLS)
        xs = _outproj(merged, x_parts, m, row_of_tile, npost[1], p['w_out'], tm)

        ffn2 = (xs, rows_mix, m, 2, row_of_ffn_tile, npre[2], npost[2], wts2, tm_ffn)
        if last:
            xs = _ffn(*ffn2)
        else:
            xs, wts, w_proj_next = _ffn(*ffn2, cast_next=ff32 + (l + 1, 0), pack_in=(w_in, l + 1))
    return xs[:n_lat].reshape(batch, seq_lat, d)
```

```python
import functools
import math

import jax
import jax.numpy as jnp
from jax import lax
from jax.experimental import pallas as pl
from jax.experimental.pallas import tpu as pltpu

F32 = jnp.float32
BF16 = jnp.bfloat16

EPS = 1e-6
ROPE_BASE = 10000.0
GRID_W = 64
N_MOD = 9
COND_ROWS = 8
MOD_STREAMS = 4
LANES = 128
SUBLANES = 8
BF16_ROWS = 16
ROT_DIM = 64
ROT_QUARTER = ROT_DIM // 4
CONV_W = 31
CONV_HALO = 16
CHUNK = 128
GM_GROUPS = 4
BRANCH_W = 512
MLA_HEADS = 4
MLA_NOPE = 128
MLA_V = 128
MLA_Q_RANK = 512
MLA_KV_RANK = 256
MLA_QK = 256
MLA_SUM_ROWS = 16
MLA_GROUP_Q = 512
MLA_GROUPS = 4
LOG2_E = math.log2(math.e)
MLA_SCALE = (MLA_NOPE + ROT_DIM) ** -0.5 * LOG2_E
SWA_HEADS = 8
SWA_KV = 2
SWA_GROUP = SWA_HEADS // SWA_KV
SWA_BLOCK = 128
SWA_WINDOW_BLOCKS = 4
SWA_SUM_ROWS = 16
WINDOW = 128
SWA_SCALE = ROT_DIM ** -0.5 * LOG2_E
IN_WIDTHS = (1024, 1024, 896, 768)
D_FF_PAD = 5632
FF_CHUNK = 512
GATE_COLS = 512
ROW_BLOCK = 1024
V7X_VMEM_BYTES = 64 * 1024 * 1024
VMEM_LIMIT = V7X_VMEM_BYTES - 8 * 1024 * 1024
VMEM_LIMIT_FFN = V7X_VMEM_BYTES - 512 * 1024


def _cparams(sem, vmem_limit=VMEM_LIMIT):
    return pltpu.CompilerParams(dimension_semantics=sem, vmem_limit_bytes=vmem_limit)


def _inv_rms(x):
    return lax.rsqrt(jnp.mean(x * x, axis=-1, keepdims=True) + EPS)


def _rms(x):
    return x * _inv_rms(x)


def _layernorm(x, g, b):
    mu = jnp.mean(x, axis=-1, keepdims=True)
    xc = x - mu
    var = jnp.mean(xc * xc, axis=-1, keepdims=True)
    return xc * lax.rsqrt(var + EPS) * g + b


def _silu(x):
    return x * jax.nn.sigmoid(x)


def _part_specs(parts, tm, tile_of):
    specs, firsts, first = [], [], 0
    for part in parts:
        n = part.shape[0] // tm
        specs.append(pl.BlockSpec((tm, part.shape[1]),
                                  lambda *g, first=first, n=n: (jnp.clip(tile_of(*g) - first, 0, n - 1), 0)))
        firsts.append(first)
        first += n
    return specs, tuple(firsts)


def _load_parts(refs, firsts, tile):
    x = refs[0][...]
    for ref, first in zip(refs[1:], firsts[1:]):
        x = jnp.where(tile >= first, ref[...], x)
    return x


def _for_row_blocks(n_rows, body):
    block = min(ROW_BLOCK, n_rows)

    def step(r, carry):
        body(pl.ds(pl.multiple_of(r * block, block), block))
        return carry

    lax.fori_loop(0, n_rows // block, step, 0)


def _dot(a, b):
    return jnp.dot(a, b, preferred_element_type=F32)


def _cast_pad_cols_kernel(w_ref, o_ref):
    n = w_ref.shape[1]
    o_ref[:, :n] = w_ref[...].astype(BF16)
    o_ref[:, n:] = jnp.zeros((o_ref.shape[0], o_ref.shape[1] - n), BF16)


def _cast_pad_cols(w, l, s, n_pad, tr=256):
    m, n = w.shape[2:]
    return pl.pallas_call(
        _cast_pad_cols_kernel,
        grid=(m // tr,),
        in_specs=[pl.BlockSpec((None, None, tr, n), lambda i: (l, s, i, 0))],
        out_specs=pl.BlockSpec((tr, n_pad), lambda i: (i, 0)),
        out_shape=jax.ShapeDtypeStruct((m, n_pad), BF16),
        compiler_params=_cparams(("parallel",)),
    )(w)


def _cast_pad_rows_kernel(w_ref, o_ref, *, rows_valid):
    tr = w_ref.shape[0]
    row = pl.program_id(0) * tr + lax.broadcasted_iota(jnp.int32, w_ref.shape, 0)
    o_ref[...] = jnp.where(row < rows_valid, w_ref[...], 0.0).astype(BF16)


def _cast_pad_rows(w, l, s, m_pad, tr=512):
    m, n = w.shape[2:]
    return pl.pallas_call(
        functools.partial(_cast_pad_rows_kernel, rows_valid=m),
        grid=(m_pad // tr,),
        in_specs=[pl.BlockSpec((None, None, tr, n), lambda j: (l, s, j, 0))],
        out_specs=pl.BlockSpec((tr, n), lambda j: (j, 0)),
        out_shape=jax.ShapeDtypeStruct((m_pad, n), BF16),
        compiler_params=_cparams(("parallel",)),
    )(w)


def _mod_kernel(cond_ref, b_ref, *refs):
    w_refs, o_ref = refs[:-1], refs[-1]
    k = pl.program_id(1)

    @pl.when(k == 0)
    def _():
        o_ref[...] = jnp.broadcast_to(b_ref[...], o_ref.shape)

    s = _silu(cond_ref[...]).astype(BF16)
    part = o_ref.shape[1] // len(w_refs)
    for j, w_ref in enumerate(w_refs):
        o_ref[:, j * part:(j + 1) * part] += _dot(s, w_ref[...].astype(BF16))


def _modulation(cond, w_mod, b_mod, tk=256):
    n_layers, d, n = w_mod.shape
    part = n // MOD_STREAMS
    w_specs = [pl.BlockSpec((None, tk, part), lambda l, k, j=j: (l, k, j)) for j in range(MOD_STREAMS)]
    return pl.pallas_call(
        _mod_kernel,
        grid=(n_layers, d // tk),
        in_specs=[
            pl.BlockSpec((COND_ROWS, tk), lambda l, k: (0, k)),
            pl.BlockSpec((None, 1, n), lambda l, k: (l, 0, 0)),
            *w_specs,
        ],
        out_specs=pl.BlockSpec((None, COND_ROWS, n), lambda l, k: (l, 0, 0)),
        out_shape=jax.ShapeDtypeStruct((n_layers, COND_ROWS, n), F32),
        compiler_params=_cparams(("parallel", "arbitrary")),
    )(cond, b_mod.reshape(n_layers, 1, n), *([w_mod] * MOD_STREAMS))


def _ffn_kernel(*refs, cast_next, d_ff, pack_in):
    x_ref, shift_ref, scale_ref, gate_ref, gpre_ref, gpost_ref, wg_ref, wu_ref, wd_ref = refs[:9]
    if cast_next and pack_in:
        (wgn_ref, wun_ref, wdn_ref, pa_ref, pb_ref, ga_ref, gb_ref,
         o_ref, wgo_ref, wuo_ref, wdo_ref, wabcd_ref, wgates_ref, h_ref, r_ref) = refs[9:]
    elif cast_next:
        wgn_ref, wun_ref, wdn_ref, o_ref, wgo_ref, wuo_ref, wdo_ref, h_ref, r_ref = refs[9:]
    else:
        o_ref, h_ref, r_ref = refs[9:]
    f = pl.program_id(1)

    @pl.when(f == 0)
    def _():
        gain = gpre_ref[...] * (1.0 + scale_ref[...])
        shift = shift_ref[...]

        def norms(rows):
            r_ref[rows, :] = _inv_rms(x_ref[rows, :])

        def block(rows):
            h_ref[rows, :] = (x_ref[rows, :] * r_ref[rows, :] * gain + shift).astype(BF16)
            o_ref[rows, :] = jnp.zeros((rows.size, o_ref.shape[1]), F32)

        _for_row_blocks(x_ref.shape[0], norms)
        _for_row_blocks(x_ref.shape[0], block)

    h = h_ref[...]
    g = _dot(h, wg_ref[...])
    u = _dot(h, wu_ref[...])
    a = (_silu(g) * u).astype(BF16)
    o_ref[...] += _dot(a, wd_ref[...])

    if cast_next:
        for src_ref, dst_ref in ((wgn_ref, wgo_ref), (wun_ref, wuo_ref)):
            dst_ref[:, :d_ff] = src_ref[...].astype(BF16)
            dst_ref[:, d_ff:] = jnp.zeros((dst_ref.shape[0], dst_ref.shape[1] - d_ff), BF16)
        rd = wdo_ref.shape[0]
        step = pl.program_id(0) * pl.num_programs(1) + f
        row = jnp.minimum(step, cast_next - 1) * rd + lax.broadcasted_iota(jnp.int32, wdo_ref.shape, 0)
        wdo_ref[...] = jnp.where(row < d_ff, wdn_ref[...], 0.0).astype(BF16)
    if pack_in:
        wabcd_ref[...] = jnp.concatenate([pa_ref[...], pb_ref[...]], axis=0).T.astype(BF16)
        wgates_ref[...] = jnp.concatenate([ga_ref[...], gb_ref[...]], axis=0).T.astype(BF16)

    @pl.when(f == pl.num_programs(1) - 1)
    def _():
        gain = 0.5 * gate_ref[...] * gpost_ref[...]

        def norms(rows):
            r_ref[rows, :] = _inv_rms(o_ref[rows, :])

        def block(rows):
            o_ref[rows, :] = x_ref[rows, :] + o_ref[rows, :] * r_ref[rows, :] * gain

        _for_row_blocks(x_ref.shape[0], norms)
        _for_row_blocks(x_ref.shape[0], block)


def _slab_rows(total, n_steps):
    slab = BF16_ROWS
    while total % slab or total // slab > n_steps:
        slab *= 2
        assert slab <= total
    return slab


def _ffn(x, rows, mod, j, row_of_tile, gpre, gpost, wts, tm, cast_next=None, pack_in=None):
    wg, wu, wd = wts
    d = x.shape[1]
    f_pad = wg.shape[-1]
    tf = FF_CHUNK
    nf = f_pad // tf

    def mspec(k):
        return pl.BlockSpec((None, None, 1, d), lambda i, f: (row_of_tile(i), k, 0, 0))

    vec = pl.BlockSpec((1, d), lambda i, f: (0, 0))
    in_specs = [
        pl.BlockSpec((tm, d), lambda i, f: (i, 0)),
        mspec(3 * j), mspec(3 * j + 1), mspec(3 * j + 2),
        vec, vec,
        pl.BlockSpec((d, tf), lambda i, f: (0, f)),
        pl.BlockSpec((d, tf), lambda i, f: (0, f)),
        pl.BlockSpec((tf, d), lambda i, f: (f, 0)),
    ]
    args = [x, mod, mod, mod, gpre, gpost, wg, wu, wd]
    out_specs = [pl.BlockSpec((tm, d), lambda i, f: (i, 0))]
    out_shape = [jax.ShapeDtypeStruct((rows, d), F32)]
    down_slabs, d_ff = 0, f_pad
    if cast_next is not None:
        wg32, wu32, wd32, l2, s2 = cast_next
        d_ff = wg32.shape[-1]
        n_steps = (rows // tm) * nf
        rg, rd = _slab_rows(d, n_steps), _slab_rows(f_pad, n_steps)
        up_slabs, down_slabs, down_src_slabs = d // rg, f_pad // rd, pl.cdiv(d_ff, rd)

        def slab(count):
            return lambda i, f: jnp.minimum(i * nf + f, count - 1)

        in_specs += [pl.BlockSpec((None, None, rg, d_ff), lambda i, f: (l2, s2, slab(up_slabs)(i, f), 0))] * 2
        in_specs += [pl.BlockSpec((None, None, rd, d), lambda i, f: (l2, s2, slab(down_src_slabs)(i, f), 0))]
        args += [wg32, wu32, wd32]
        out_specs += [pl.BlockSpec((rg, f_pad), lambda i, f: (slab(up_slabs)(i, f), 0))] * 2
        out_specs += [pl.BlockSpec((rd, d), lambda i, f: (slab(down_slabs)(i, f), 0))]
        out_shape += [jax.ShapeDtypeStruct((d, f_pad), BF16)] * 2 + [jax.ShapeDtypeStruct((f_pad, d), BF16)]
        if pack_in is not None:
            w_in_t, l_in = pack_in
            half = ROT_DIM
            n_abcd = sum(IN_WIDTHS)
            n_gates = w_in_t.shape[1] - (n_abcd - half)
            dup = (n_abcd - IN_WIDTHS[3]) // half - 1
            abcd_blocks, gate_blocks = n_abcd // LANES, n_gates // LANES
            if n_steps < max(abcd_blocks, gate_blocks):
                return _ffn(x, rows, mod, j, row_of_tile, gpre, gpost, wts, tm, cast_next=cast_next) + (None,)

            def src_abcd(k):
                return jnp.where(k < dup, k, k - 1)

            def half_spec(block_of_step, src, part):
                return pl.BlockSpec((None, half, d),
                                    lambda i, f: (l_in, src(2 * block_of_step(i, f) + part), 0))

            gate0 = (n_abcd - half) // half
            in_specs += [half_spec(slab(abcd_blocks), src_abcd, 0), half_spec(slab(abcd_blocks), src_abcd, 1),
                         half_spec(slab(gate_blocks), lambda k: gate0 + k, 0),
                         half_spec(slab(gate_blocks), lambda k: gate0 + k, 1)]
            args += [w_in_t] * 4
            out_specs += [pl.BlockSpec((d, LANES), lambda i, f: (0, slab(abcd_blocks)(i, f))),
                          pl.BlockSpec((d, LANES), lambda i, f: (0, slab(gate_blocks)(i, f)))]
            out_shape += [jax.ShapeDtypeStruct((d, n_abcd), BF16), jax.ShapeDtypeStruct((d, n_gates), BF16)]
    row_axis = "parallel" if cast_next is None else "arbitrary"
    outs = pl.pallas_call(
        functools.partial(_ffn_kernel, cast_next=down_slabs, d_ff=d_ff, pack_in=pack_in is not None),
        grid=(rows // tm, nf),
        in_specs=in_specs,
        out_specs=out_specs,
        out_shape=out_shape,
        scratch_shapes=[pltpu.VMEM((tm, d), BF16), pltpu.VMEM((tm, 1), F32)],
        compiler_params=_cparams((row_axis, "arbitrary"), VMEM_LIMIT_FFN),
    )(*args)
    if cast_next is None:
        return outs[0]
    return (outs[0], tuple(outs[1:4])) + ((tuple(outs[4:]),) if pack_in is not None else ())


def _inproj_kernel(*refs, x_firsts):
    x_refs = refs[:len(x_firsts)]
    (shift_ref, scale_ref, gpre_ref, w_ref, b_ref, cos_ref, sina_ref, sinb_ref, qn_ref, kvn_ref, wuq_ref, wukv_ref,
     lng_ref, lnb_ref, ws_ref, bs_ref,
     h_ref, ya_ref, hb_ref, qc_ref, kc_ref, vc_ref, qd_ref, kd_ref, vd_ref) = refs[len(x_firsts):]
    y = _rms(_load_parts(x_refs, x_firsts, pl.program_id(0))) * gpre_ref[...]
    h = (y * (1.0 + scale_ref[...]) + shift_ref[...]).astype(BF16)
    h_ref[...] = h

    def proj(lo, hi):
        return _dot(h, w_ref[:, lo:hi]) + b_ref[:, lo:hi]

    na, nb, nc, nd = IN_WIDTHS
    _qkv_prep(proj(na + nb, na + nb + nc), proj(na + nb + nc, na + nb + nc + nd),
              cos_ref, sina_ref, sinb_ref, qn_ref, kvn_ref, wuq_ref, wukv_ref,
              qc_ref, kc_ref, vc_ref, qd_ref, kd_ref, vd_ref)
    w = BRANCH_W
    zb = proj(na, na + nb)
    hb_ref[...] = zb[:, :w] * jax.nn.sigmoid(zb[:, w:])
    za = proj(0, na)
    g = 0.5 * za * (1.0 + lax.erf(za * (2.0 ** -0.5)))
    vln = _layernorm(g[:, w:], lng_ref[...], lnb_ref[...]).astype(BF16)
    for c in range(za.shape[0] // CHUNK):
        rows = slice(c * CHUNK, (c + 1) * CHUNK)
        for gi in range(GM_GROUPS):
            cols = slice(gi * CHUNK, (gi + 1) * CHUNK)
            sv = _dot(ws_ref[gi], vln[rows, cols]) + bs_ref[gi]
            ya_ref[rows, cols] = (g[rows, cols] * sv).astype(BF16)


def _inproj(x_parts, mod, row_of_tile, gpre, w, b, cos, sina, sinb, qn, kvn, wuq, wukv, lng, lnb, ws, bs, tm):
    rows, d = sum(part.shape[0] for part in x_parts), x_parts[0].shape[1]
    x_specs, x_firsts = _part_specs(x_parts, tm, lambda i: i)
    n = w.shape[1]

    def mspec(k):
        return pl.BlockSpec((None, None, 1, d), lambda i: (row_of_tile(i), k, 0, 0))

    def rowspec(wd):
        return pl.BlockSpec((tm, wd), lambda i: (i, 0))

    def full(a):
        return pl.BlockSpec(a.shape, lambda i: (0,) * a.ndim)

    def headspec(nh, wd):
        return pl.BlockSpec((nh, tm, wd), lambda i: (0, i, 0))

    def headspec_t(nh, wd):
        return pl.BlockSpec((nh, wd, tm), lambda i: (0, 0, i))

    kc = jax.ShapeDtypeStruct((MLA_HEADS, rows, MLA_QK), BF16)
    qc = jax.ShapeDtypeStruct((MLA_HEADS, MLA_QK, rows), BF16)
    vc = jax.ShapeDtypeStruct((MLA_HEADS, MLA_V + MLA_SUM_ROWS, rows), BF16)
    qd = jax.ShapeDtypeStruct((SWA_HEADS // 2, LANES, rows), BF16)
    kd = jax.ShapeDtypeStruct((rows, LANES), BF16)
    vd = jax.ShapeDtypeStruct((LANES, rows), BF16)
    outs = [jax.ShapeDtypeStruct((rows, d), BF16), jax.ShapeDtypeStruct((rows, BRANCH_W), BF16),
            jax.ShapeDtypeStruct((rows, BRANCH_W), F32)]
    ospecs = [rowspec(d), rowspec(BRANCH_W), rowspec(BRANCH_W)]
    return pl.pallas_call(
        functools.partial(_inproj_kernel, x_firsts=x_firsts),
        grid=(rows // tm,),
        in_specs=[
            *x_specs,
            mspec(3), mspec(4),
            pl.BlockSpec((1, d), lambda i: (0, 0)),
            pl.BlockSpec((d, n), lambda i: (0, 0), pipeline_mode=pl.Buffered(1)),
            pl.BlockSpec((1, n), lambda i: (0, 0)),
            rowspec(LANES), rowspec(LANES), rowspec(LANES), full(qn), full(kvn), full(wuq), full(wukv),
            full(lng), full(lnb), full(ws), full(bs),
        ],
        out_specs=ospecs + [headspec_t(MLA_HEADS, MLA_QK), headspec(MLA_HEADS, MLA_QK),
                            headspec_t(MLA_HEADS, MLA_V + MLA_SUM_ROWS), headspec_t(SWA_HEADS // 2, LANES),
                            rowspec(LANES), pl.BlockSpec((LANES, tm), lambda i: (0, i))],
        out_shape=outs + [qc, kc, vc, qd, kd, vd],
        compiler_params=_cparams(("parallel",)),
    )(*x_parts, mod, mod, gpre, w, b, cos, sina, sinb, qn, kvn, wuq, wukv, lng, lnb, ws, bs)


def _conv_kernel(h_ref, hprev_ref, hnext_ref, dw_ref, db_ref, lng_ref, lnb_ref, yb_ref, ext_ref, shift_ref,
                 *, ts, n_lat_tiles, lat_tiles_per_seq, ctx_tiles_per_seq):
    i = pl.program_id(0)
    in_lat = i < n_lat_tiles
    pos = jnp.where(in_lat, i % lat_tiles_per_seq, (i - n_lat_tiles) % ctx_tiles_per_seq)
    last = jnp.where(in_lat, lat_tiles_per_seq - 1, ctx_tiles_per_seq - 1)
    ext_ref[0:CONV_HALO, :] = jnp.where(pos != 0, hprev_ref[...], 0.0)
    ext_ref[CONV_HALO:CONV_HALO + ts, :] = h_ref[...]
    ext_ref[CONV_HALO + ts:, :] = jnp.where(pos != last, hnext_ref[...], 0.0)
    span = ts + 2 * CONV_HALO - SUBLANES
    for r in range(1, SUBLANES):
        shift_ref[r - 1, 0:span, :] = ext_ref[r:r + span, :]
    first_tap = CONV_HALO - CONV_W // 2
    acc = jnp.zeros((ts, BRANCH_W), F32) + db_ref[...]
    for k in range(CONV_W):
        a, r = divmod(first_tap + k, SUBLANES)
        win = ext_ref[a * SUBLANES:a * SUBLANES + ts, :] if r == 0 else shift_ref[r - 1, a * SUBLANES:a * SUBLANES + ts, :]
        acc = acc + dw_ref[k:k + 1, :] * win
    yb_ref[...] = _silu(_layernorm(acc, lng_ref[...], lnb_ref[...])).astype(BF16)


def _conv_mixer(hb, dw, db, lng, lnb, n_lat_rows, seq_lat, seq_ctx, ts):
    rows, w = hb.shape
    halo_blocks = ts // CONV_HALO
    n_halo = rows // CONV_HALO
    kern = functools.partial(_conv_kernel, ts=ts, n_lat_tiles=n_lat_rows // ts,
                             lat_tiles_per_seq=seq_lat // ts, ctx_tiles_per_seq=seq_ctx // ts)
    vec = pl.BlockSpec((1, w), lambda i: (0, 0))
    return pl.pallas_call(
        kern,
        grid=(rows // ts,),
        in_specs=[
            pl.BlockSpec((ts, w), lambda i: (i, 0)),
            pl.BlockSpec((CONV_HALO, w), lambda i: (jnp.maximum(i * halo_blocks - 1, 0), 0)),
            pl.BlockSpec((CONV_HALO, w), lambda i: (jnp.minimum((i + 1) * halo_blocks, n_halo - 1), 0)),
            pl.BlockSpec((CONV_W, w), lambda i: (0, 0)),
            vec, vec, vec,
        ],
        out_specs=pl.BlockSpec((ts, w), lambda i: (i, 0)),
        out_shape=jax.ShapeDtypeStruct((rows, w), BF16),
        scratch_shapes=[pltpu.VMEM((ts + 2 * CONV_HALO, w), F32),
                        pltpu.VMEM((SUBLANES - 1, ts + 2 * CONV_HALO - SUBLANES, w), F32)],
        compiler_params=_cparams(("parallel",)),
    )(hb, hb, hb, dw, db, lng, lnb)


def _qkv_prep(zc, zd, cos_ref, sina_ref, sinb_ref, qn_ref, kvn_ref, wuq_ref, wukv_ref,
              qc_ref, kc_ref, vc_ref, qd_ref, kd_ref, vd_ref):
    cos, sina, sinb = cos_ref[...], sina_ref[...], sinb_ref[...]
    low = lax.broadcasted_iota(jnp.int32, cos.shape, 1) < ROT_DIM

    def rope(slab):
        return (slab * cos + pltpu.roll(slab, LANES - ROT_QUARTER, 1) * sina
                + pltpu.roll(slab, ROT_QUARTER, 1) * sinb)

    nq = MLA_HEADS * MLA_NOPE
    kv0 = MLA_Q_RANK + MLA_KV_RANK
    q = _dot((_rms(zc[:, :MLA_Q_RANK]) * qn_ref[...]).astype(BF16), wuq_ref[...]) * MLA_SCALE
    kv = _dot((_rms(zc[:, MLA_Q_RANK:kv0]) * kvn_ref[...]).astype(BF16), wukv_ref[...])
    kr = rope(zc[:, kv0:kv0 + LANES])
    ones = jnp.ones((MLA_SUM_ROWS, zc.shape[0]), BF16)
    for pair in range(MLA_HEADS // 2):
        qr = rope(q[:, nq + pair * LANES:nq + (pair + 1) * LANES]).T.astype(BF16)
        for h in (2 * pair, 2 * pair + 1):
            qc_ref[h, 0:MLA_NOPE, :] = q[:, h * MLA_NOPE:(h + 1) * MLA_NOPE].T.astype(BF16)
            qc_ref[h, MLA_NOPE:, :] = qr
            kc_ref[h, :, 0:MLA_NOPE] = kv[:, h * MLA_NOPE:(h + 1) * MLA_NOPE].astype(BF16)
            keep = low if h % 2 == 0 else jnp.logical_not(low)
            kc_ref[h, :, MLA_NOPE:] = jnp.where(keep, kr, 0.0).astype(BF16)
            vc_ref[h, 0:MLA_V, :] = kv[:, nq + h * MLA_V:nq + (h + 1) * MLA_V].T.astype(BF16)
            vc_ref[h, MLA_V:, :] = ones

    for pair in range(SWA_HEADS // 2):
        qd_ref[pair] = (rope(zd[:, pair * LANES:(pair + 1) * LANES]) * SWA_SCALE).T.astype(BF16)
    nqd = SWA_HEADS * ROT_DIM
    kd_ref[...] = rope(zd[:, nqd:nqd + LANES]).astype(BF16)
    vd_ref[...] = zd[:, nqd + LANES:nqd + 2 * LANES].T.astype(BF16)


def _mla_kernel(*refs, has_lat, groups):
    if has_lat:
        qt_ref, kl_ref, vlt_ref, kx_ref, vxt_ref, o_ref = refs
    else:
        qt_ref, kx_ref, vxt_ref, o_ref = refs
    cols = qt_ref.shape[1] // groups
    scores = []
    for g in range(groups):
        qt = qt_ref[:, g * cols:(g + 1) * cols]
        scores.append((_dot(kx_ref[...], qt), _dot(kl_ref[...], qt) if has_lat else None))
    for g, (sx, sl) in enumerate(scores):
        m = jnp.max(sx, axis=0, keepdims=True)
        if has_lat:
            m = jnp.maximum(m, jnp.max(sl, axis=0, keepdims=True))
        ot = _dot(vxt_ref[...], jnp.exp2(sx - m).astype(BF16))
        if has_lat:
            ot = ot + _dot(vlt_ref[...], jnp.exp2(sl - m).astype(BF16))
        o_ref[g * cols:(g + 1) * cols, :] = (ot[:MLA_V] / ot[MLA_V:MLA_V + 1]).T.astype(BF16)


def _mla_attention(qc, kc, vc, batch, seq_lat, seq_ctx, lat_queries):
    n_lat_rows = batch * seq_lat
    ctx_blk0 = n_lat_rows // seq_ctx
    tq = math.gcd(MLA_GROUPS * MLA_GROUP_Q, seq_lat if lat_queries else seq_ctx)
    if lat_queries:
        nq, q0 = seq_lat // tq, 0
    else:
        nq, q0 = seq_ctx // tq, n_lat_rows // tq
    vrows = vc.shape[1]
    qspec = pl.BlockSpec((None, MLA_QK, tq), lambda b, h, i: (h, 0, q0 + b * nq + i))
    kx = pl.BlockSpec((None, seq_ctx, MLA_QK), lambda b, h, i: (h, ctx_blk0 + b, 0))
    vx = pl.BlockSpec((None, vrows, seq_ctx), lambda b, h, i: (h, 0, ctx_blk0 + b))
    if lat_queries:
        kl = pl.BlockSpec((None, seq_lat, MLA_QK), lambda b, h, i: (h, b, 0))
        vl = pl.BlockSpec((None, vrows, seq_lat), lambda b, h, i: (h, 0, b))
        in_specs, args = [qspec, kl, vl, kx, vx], (qc, kc, vc, kc, vc)
    else:
        in_specs, args = [qspec, kx, vx], (qc, kc, vc)
    return pl.pallas_call(
        functools.partial(_mla_kernel, has_lat=lat_queries, groups=max(1, tq // MLA_GROUP_Q)),
        grid=(batch, MLA_HEADS, nq),
        in_specs=in_specs,
        out_specs=pl.BlockSpec((tq, MLA_V), lambda b, h, i: (b * nq + i, h)),
        out_shape=jax.ShapeDtypeStruct((batch * nq * tq, MLA_HEADS * MLA_V), BF16),
        compiler_params=_cparams(("parallel", "parallel", "parallel")),
    )(*args)


def _swa_kernel(*refs, has_lat):
    if has_lat:
        qt_ref, kx_ref, vxt_ref, sink_ref, bias_ref = refs[:5]
        kw_refs, vw_refs, o_ref = refs[5:5 + SWA_WINDOW_BLOCKS], refs[5 + SWA_WINDOW_BLOCKS:-1], refs[-1]
    else:
        qt_ref, kx_ref, vxt_ref, sink_ref, o_ref = refs
    blocks = qt_ref.shape[2] // SWA_BLOCK
    half = LANES // 2
    span = 3 * SWA_BLOCK
    gcol = SWA_GROUP * SWA_BLOCK
    zeros = jnp.zeros((half, gcol), BF16)
    ones = jnp.ones((SWA_SUM_ROWS, 1), BF16)
    if has_lat:
        k_win = jnp.concatenate([r[...] for r in kw_refs], axis=0)
        v_win = jnp.concatenate([r[...] for r in vw_refs], axis=1)
    chains = []
    for kv in range(SWA_KV):
        for blk in range(blocks):
            heads = range(kv * SWA_GROUP, (kv + 1) * SWA_GROUP)
            qt = jnp.concatenate([qt_ref[h // 2, (h % 2) * half:(h % 2 + 1) * half,
                                         blk * SWA_BLOCK:(blk + 1) * SWA_BLOCK] for h in heads], axis=1)
            qt = jnp.concatenate([qt, zeros] if kv == 0 else [zeros, qt], axis=0)
            k0 = blk * SWA_BLOCK
            chains.append((kv, blk, _dot(kx_ref[...], qt), _dot(k_win[k0:k0 + span], qt) if has_lat else None))
    for kv, blk, sx, sl in chains:
        rows = slice(kv * half, (kv + 1) * half)
        cols = slice(blk * gcol, (blk + 1) * gcol)
        k0 = blk * SWA_BLOCK
        sink = sink_ref[kv][:, cols] * LOG2_E
        m = jnp.maximum(sink, jnp.max(sx, axis=0, keepdims=True))
        if has_lat:
            sl = sl + bias_ref[k0:k0 + span, cols]
            m = jnp.maximum(m, jnp.max(sl, axis=0, keepdims=True))
        vx = vxt_ref[rows, :]
        ot = _dot(jnp.concatenate([vx, jnp.broadcast_to(ones, (SWA_SUM_ROWS, vx.shape[1]))], axis=0),
                  jnp.exp2(sx - m).astype(BF16))
        if has_lat:
            vl = v_win[rows, k0:k0 + span]
            ot = ot + _dot(jnp.concatenate([vl, jnp.broadcast_to(ones, (SWA_SUM_ROWS, span))], axis=0),
                           jnp.exp2(sl - m).astype(BF16))
        o = ot[:half] / (ot[half:half + 1] + jnp.exp2(sink - m))
        for pair in range(SWA_GROUP // 2):
            c0 = 2 * pair * SWA_BLOCK
            slab_t = jnp.concatenate([o[:, c0:c0 + SWA_BLOCK], o[:, c0 + SWA_BLOCK:c0 + 2 * SWA_BLOCK]], axis=0)
            lane0 = (kv * (SWA_GROUP // 2) + pair) * LANES
            o_ref[blk * SWA_BLOCK:(blk + 1) * SWA_BLOCK, lane0:lane0 + LANES] = slab_t.T.astype(BF16)


def _swa_window_bias(blocks):
    nkeys = SWA_WINDOW_BLOCKS * SWA_BLOCK
    ncol = blocks * SWA_GROUP * SWA_BLOCK
    key = jnp.arange(nkeys)[:, None] - SWA_BLOCK
    col = jnp.arange(ncol)[None, :]
    qry = (col // (SWA_GROUP * SWA_BLOCK)) * SWA_BLOCK + (col % SWA_BLOCK)
    inside = jnp.abs(key - qry) <= WINDOW
    after_start = key >= 0
    before_end = key < blocks * SWA_BLOCK
    variants = [inside, inside & after_start, inside & before_end, inside & after_start & before_end]
    return jnp.where(jnp.stack(variants), 0.0, -jnp.inf).astype(F32)


def _swa_attention(qd, kd, vd, sink, batch, seq_lat, seq_ctx, lat_queries):
    n_lat_rows = batch * seq_lat
    ctx_blk0 = n_lat_rows // seq_ctx
    seq_q = seq_lat if lat_queries else seq_ctx
    blocks = 2 if seq_q % (2 * SWA_BLOCK) == 0 and n_lat_rows % (2 * SWA_BLOCK) == 0 else 1
    tq = blocks * SWA_BLOCK
    nb = seq_q // tq
    q0 = 0 if lat_queries else n_lat_rows // tq
    kblocks = seq_lat // SWA_BLOCK
    sink_row = jnp.tile(jnp.repeat(sink.reshape(SWA_KV, SWA_GROUP), SWA_BLOCK, axis=1), (1, blocks))[:, None, :]
    in_specs = [pl.BlockSpec((SWA_HEADS // 2, LANES, tq), lambda b, i: (0, 0, q0 + b * nb + i)),
                pl.BlockSpec((seq_ctx, LANES), lambda b, i: (ctx_blk0 + b, 0)),
                pl.BlockSpec((LANES, seq_ctx), lambda b, i: (0, ctx_blk0 + b)),
                pl.BlockSpec(sink_row.shape, lambda b, i: (0, 0, 0))]
    args = [qd, kd, vd, sink_row]
    if lat_queries:
        assert blocks + 2 == SWA_WINDOW_BLOCKS
        bias = _swa_window_bias(blocks)
        in_specs.append(pl.BlockSpec((None,) + bias.shape[1:],
                                     lambda b, i: ((i == 0).astype(jnp.int32) + 2 * (i == nb - 1).astype(jnp.int32), 0, 0)))
        args.append(bias)

        def win(j):
            return lambda b, i: b * kblocks + jnp.clip(i * blocks - 1 + j, 0, kblocks - 1)

        in_specs += [pl.BlockSpec((SWA_BLOCK, LANES), lambda b, i, j=j: (win(j)(b, i), 0))
                     for j in range(SWA_WINDOW_BLOCKS)]
        in_specs += [pl.BlockSpec((LANES, SWA_BLOCK), lambda b, i, j=j: (0, win(j)(b, i)))
                     for j in range(SWA_WINDOW_BLOCKS)]
        args += [kd] * SWA_WINDOW_BLOCKS + [vd] * SWA_WINDOW_BLOCKS
    return pl.pallas_call(
        functools.partial(_swa_kernel, has_lat=lat_queries),
        grid=(batch, nb),
        in_specs=in_specs,
        out_specs=pl.BlockSpec((tq, BRANCH_W), lambda b, i: (b * nb + i, 0)),
        out_shape=jax.ShapeDtypeStruct((batch * nb * tq, BRANCH_W), BF16),
        compiler_params=_cparams(("parallel", "parallel")),
    )(*args)


def _gate_kernel(*refs, y_firsts):
    n = len(y_firsts)
    h_ref, ya_ref, yb_ref = refs[:3]
    yc_refs, yd_refs = refs[3:3 + n], refs[3 + n:3 + 2 * n]
    wg0, wg1, wg2, wg3, bg_ref, wb0, wb1, wb2, wb3, o_ref = refs[3 + 2 * n:]
    tile = pl.program_id(1)
    ys = (ya_ref[...], yb_ref[...], _load_parts(yc_refs, y_firsts, tile), _load_parts(yd_refs, y_firsts, tile))
    h = h_ref[...]
    acc = None
    for b, (y, wg_ref, wb_ref) in enumerate(zip(ys, (wg0, wg1, wg2, wg3), (wb0, wb1, wb2, wb3))):
        gate = jax.nn.sigmoid(_dot(h, wg_ref[...]) + bg_ref[b])
        term = gate * _dot(y, wb_ref[...])
        acc = term if acc is None else acc + term
    o_ref[...] = acc.astype(BF16)


def _gated_merge(h, ya, yb, yc_parts, yd_parts, wg, bg, wbr, rows, tm, tn):
    d = h.shape[1]
    ncol = d // tn
    wg_specs = [pl.BlockSpec((d, tn), lambda j, i, b=b: (0, b * ncol + j)) for b in range(4)]
    wb_specs = [pl.BlockSpec((None, BRANCH_W, tn), lambda j, i, b=b: (b, 0, j)) for b in range(4)]
    yspec = pl.BlockSpec((tm, BRANCH_W), lambda j, i: (i, 0))
    yc_specs, y_firsts = _part_specs(yc_parts, tm, lambda j, i: i)
    yd_specs, _ = _part_specs(yd_parts, tm, lambda j, i: i)
    return pl.pallas_call(
        functools.partial(_gate_kernel, y_firsts=y_firsts),
        grid=(ncol, rows // tm),
        in_specs=[pl.BlockSpec((tm, d), lambda j, i: (i, 0)), yspec, yspec, *yc_specs, *yd_specs,
                  *wg_specs, pl.BlockSpec((4, 1, tn), lambda j, i: (0, 0, j)), *wb_specs],
        out_specs=pl.BlockSpec((tm, tn), lambda j, i: (i, j)),
        out_shape=jax.ShapeDtypeStruct((rows, d), BF16),
        compiler_params=_cparams(("parallel", "parallel")),
    )(h, ya, yb, *yc_parts, *yd_parts, wg, wg, wg, wg, bg, wbr, wbr, wbr, wbr)


def _outproj_kernel(*refs, x_firsts):
    a_ref, x_refs = refs[0], refs[1:1 + len(x_firsts)]
    gate_ref, gpost_ref, w_ref, o_ref = refs[1 + len(x_firsts):]
    y = _dot(a_ref[...], w_ref[...])
    o_ref[...] = _load_parts(x_refs, x_firsts, pl.program_id(0)) + gate_ref[...] * (_rms(y) * gpost_ref[...])


def _outproj(a, x_parts, mod, row_of_tile, gpost, w, tm):
    rows, d = a.shape
    x_specs, x_firsts = _part_specs(x_parts, tm, lambda i: i)
    return pl.pallas_call(
        functools.partial(_outproj_kernel, x_firsts=x_firsts),
        grid=(rows // tm,),
        in_specs=[
            pl.BlockSpec((tm, d), lambda i: (i, 0)),
            *x_specs,
            pl.BlockSpec((None, None, 1, d), lambda i: (row_of_tile(i), 5, 0, 0)),
            pl.BlockSpec((1, d), lambda i: (0, 0)),
            pl.BlockSpec((d, d), lambda i: (0, 0), pipeline_mode=pl.Buffered(1)),
        ],
        out_specs=pl.BlockSpec((tm, d), lambda i: (i, 0)),
        out_shape=jax.ShapeDtypeStruct((rows, d), F32),
        compiler_params=_cparams(("parallel",)),
    )(a, *x_parts, mod, gpost, w)


def _rope_tables(seq_lat, n_ctx_rows, batch):
    half = ROT_DIM // 2
    freqs = 1.0 / (ROPE_BASE ** (jnp.arange(0, half, 2, dtype=F32) / half))
    t = jnp.arange(seq_lat)
    ang_r = (t // GRID_W).astype(F32)[:, None] * freqs[None, :]
    ang_c = (t % GRID_W).astype(F32)[:, None] * freqs[None, :]
    ang = jnp.concatenate([ang_r, ang_r, ang_c, ang_c], axis=-1)
    ang = jnp.tile(ang, (batch, LANES // ROT_DIM))
    cos, sin = jnp.cos(ang), jnp.sin(ang)
    first = (jnp.arange(LANES) % half) < ROT_QUARTER
    sina = jnp.where(first, -sin, 0.0)
    sinb = jnp.where(first, 0.0, sin)
    ones = jnp.ones((n_ctx_rows, LANES), F32)
    zeros = jnp.zeros((n_ctx_rows, LANES), F32)
    return (jnp.concatenate([cos, ones]), jnp.concatenate([sina, zeros]), jnp.concatenate([sinb, zeros]))


def _pack_in_projection(w_in, l):
    na, nb, nc, nd = IN_WIDTHS
    c_hi = na + nb + nc - ROT_DIM
    wi = w_in[l]
    w_abcd = jnp.concatenate([wi[:, :c_hi], wi[:, c_hi - ROT_DIM:c_hi], wi[:, c_hi:c_hi + nd]], axis=1)
    return w_abcd.astype(BF16), wi[:, c_hi + nd:].astype(BF16)


def _pack_layer(l, b_in, gm_ws, gm_bs, mla_w_uq, mla_w_ukv, swa_sink, w_branch, w_out):
    d = w_branch.shape[-1]
    p = {}
    bi = b_in[l]
    na, nb, nc, nd = IN_WIDTHS
    c_hi = na + nb + nc - ROT_DIM
    kr = slice(c_hi - ROT_DIM, c_hi)
    p['b_abcd'] = jnp.concatenate([bi[:c_hi], bi[kr], bi[c_hi:c_hi + nd]])[None, :]
    p['b_gates'] = bi[c_hi + nd:].reshape(4, 1, d)
    p['gm_ws'] = gm_ws[l].astype(BF16)
    p['gm_bs'] = gm_bs[l][:, :, None]
    uq = mla_w_uq[l].reshape(-1, MLA_HEADS, MLA_NOPE + ROT_DIM)
    p['wuq'] = jnp.concatenate([uq[:, :, :MLA_NOPE].reshape(-1, MLA_HEADS * MLA_NOPE),
                                uq[:, :, MLA_NOPE:].reshape(-1, MLA_HEADS * ROT_DIM)], axis=1).astype(BF16)
    ukv = mla_w_ukv[l].reshape(-1, MLA_HEADS, MLA_NOPE + MLA_V)
    p['wukv'] = jnp.concatenate([ukv[:, :, :MLA_NOPE].reshape(-1, MLA_HEADS * MLA_NOPE),
                                 ukv[:, :, MLA_NOPE:].reshape(-1, MLA_HEADS * MLA_V)], axis=1).astype(BF16)
    p['sink'] = swa_sink[l]
    p['w_branch'] = w_branch[l].astype(BF16)
    p['w_out'] = w_out[l].astype(BF16)
    return p


def kernel(x, c, ctx, c_ctx, w_mod, b_mod, norm_pre, norm_post, w_ff_gate, w_ff_up, w_ff_down, w_in, b_in, gm_ln_g, gm_ln_b, gm_ws, gm_bs, cv_dw, cv_db, cv_ln_g, cv_ln_b, mla_q_norm, mla_kv_norm, mla_w_uq, mla_w_ukv, swa_sink, w_branch, w_out):
    batch, seq_lat, d = x.shape
    seq_ctx = ctx.shape[1]
    depth = w_mod.shape[0]
    n_lat = batch * seq_lat
    n_ctx = batch * seq_ctx
    assert batch + 1 <= COND_ROWS and seq_lat % GRID_W == 0
    tm = math.gcd(512, math.gcd(seq_lat, n_ctx))
    ts = math.gcd(256, math.gcd(seq_lat, seq_ctx))
    assert tm % SWA_BLOCK == 0 and ts % CHUNK == 0 and seq_ctx % SWA_BLOCK == 0 and seq_lat >= 3 * SWA_BLOCK

    tm_ffn = math.gcd(1024, math.gcd(seq_lat, n_ctx))

    def cond_row(tile):
        return lambda i: jnp.where(i < n_lat // tile, i // (seq_lat // tile), batch)

    row_of_tile, row_of_ffn_tile = cond_row(tm), cond_row(tm_ffn)

    cond = jnp.zeros((COND_ROWS, d), F32).at[:batch].set(c).at[batch].set(c_ctx)
    mod = _modulation(cond, w_mod, b_mod).reshape(depth, COND_ROWS, N_MOD, 1, d)
    cos, sina, sinb = _rope_tables(seq_lat, n_ctx, batch)

    ff32 = (w_ff_gate, w_ff_up, w_ff_down)
    w_in_t = jnp.swapaxes(w_in, 1, 2)
    wts = (_cast_pad_cols(w_ff_gate, 0, 0, D_FF_PAD), _cast_pad_cols(w_ff_up, 0, 0, D_FF_PAD),
           _cast_pad_rows(w_ff_down, 0, 0, D_FF_PAD))

    n_all = n_lat + n_ctx
    xs = None
    for l in range(depth):
        last = l == depth - 1
        p = _pack_layer(l, b_in, gm_ws, gm_bs, mla_w_uq, mla_w_ukv, swa_sink, w_branch, w_out)
        m = mod[l]
        npre, npost = norm_pre[l][:, None, :], norm_post[l][:, None, :]

        ffn1 = (m, 0, row_of_ffn_tile, npre[0], npost[0])
        if xs is None:
            x_lat, wts2, w_proj = _ffn(x.reshape(n_lat, d), n_lat, *ffn1, wts, tm_ffn, cast_next=ff32 + (l, 1),
                                       pack_in=(w_in_t, l))
            x_ctx = _ffn(ctx.reshape(n_ctx, d), n_ctx, m, 0, lambda i: batch, npre[0], npost[0], wts, tm_ffn)
            x_parts = (x_lat, x_ctx)
        else:
            xs, wts2 = _ffn(xs, n_all, *ffn1, wts, tm_ffn, cast_next=ff32 + (l, 1))
            x_parts, w_proj = (xs,), w_proj_next

        if w_proj is None:
            w_proj = _pack_in_projection(w_in, l)
        h, ya, hb, qc, kc, vc, qd, kd, vd = _inproj(
            x_parts, m, row_of_tile, npre[1], w_proj[0], p['b_abcd'], cos, sina, sinb,
            mla_q_norm[l][None], mla_kv_norm[l][None], p['wuq'], p['wukv'],
            gm_ln_g[l][None], gm_ln_b[l][None], p['gm_ws'], p['gm_bs'], tm)
        yb = _conv_mixer(hb, cv_dw[l], cv_db[l][None], cv_ln_g[l][None], cv_ln_b[l][None],
                         n_lat, seq_lat, seq_ctx, ts)
        rows_mix = n_lat if last else n_all
        attn = (batch, seq_lat, seq_ctx)
        yc_parts = (_mla_attention(qc, kc, vc, *attn, True),)
        yd_parts = (_swa_attention(qd, kd, vd, p['sink'], *attn, True),)
        if not last:
            yc_parts += (_mla_attention(qc, kc, vc, *attn, False),)
            yd_parts += (_swa_attention(qd, kd, vd, p['sink'], *attn, False),)
        merged = _gated_merge(h, ya, yb, yc_parts, yd_parts, w_proj[1], p['b_gates'], p['w_branch'],
                              rows_mix, tm_ffn, GATE_COLS)
        xs = _outproj(merged, x_parts, m, row_of_tile, npost[1], p['w_out'], tm)

        ffn2 = (xs, rows_mix, m, 2, row_of_ffn_tile, npre[2], npost[2], wts2, tm_ffn)
        if last:
            xs = _ffn(*ffn2)
        else:
            xs, wts, w_proj_next = _ffn(*ffn2, cast_next=ff32 + (l + 1, 0), pack_in=(w_in_t, l + 1))
    return xs[:n_lat].reshape(batch, seq_lat, d)
```

```python
import functools
import math

import jax
import jax.numpy as jnp
from jax import lax
from jax.experimental import pallas as pl
from jax.experimental.pallas import tpu as pltpu

F32 = jnp.float32
BF16 = jnp.bfloat16

EPS = 1e-6
ROPE_BASE = 10000.0
GRID_W = 64
N_MOD = 9
COND_ROWS = 8
MOD_STREAMS = 4
LANES = 128
SUBLANES = 8
BF16_ROWS = 16
ROT_DIM = 64
ROT_QUARTER = ROT_DIM // 4
CONV_W = 31
CONV_HALO = 16
CHUNK = 128
GM_GROUPS = 4
BRANCH_W = 512
MLA_HEADS = 4
MLA_NOPE = 128
MLA_V = 128
MLA_Q_RANK = 512
MLA_KV_RANK = 256
MLA_QK = 256
MLA_SUM_ROWS = 16
MLA_GROUP_Q = 512
MLA_GROUPS = 4
LOG2_E = math.log2(math.e)
MLA_SCALE = (MLA_NOPE + ROT_DIM) ** -0.5 * LOG2_E
SWA_HEADS = 8
SWA_KV = 2
SWA_GROUP = SWA_HEADS // SWA_KV
SWA_BLOCK = 128
SWA_QUERY_BLOCKS = 4
SWA_SUM_ROWS = 16
WINDOW = 128
SWA_SCALE = ROT_DIM ** -0.5 * LOG2_E
IN_WIDTHS = (1024, 1024, 896, 768)
D_FF_PAD = 5632
FF_CHUNK = 512
GATE_COLS = 512
ROW_BLOCK = 1024
V7X_VMEM_BYTES = 64 * 1024 * 1024
VMEM_LIMIT = V7X_VMEM_BYTES - 8 * 1024 * 1024
VMEM_LIMIT_FFN = V7X_VMEM_BYTES - 512 * 1024


def _cparams(sem, vmem_limit=VMEM_LIMIT):
    return pltpu.CompilerParams(dimension_semantics=sem, vmem_limit_bytes=vmem_limit)


def _inv_rms(x):
    return lax.rsqrt(jnp.mean(x * x, axis=-1, keepdims=True) + EPS)


def _rms(x):
    return x * _inv_rms(x)


def _layernorm(x, g, b):
    mu = jnp.mean(x, axis=-1, keepdims=True)
    xc = x - mu
    var = jnp.mean(xc * xc, axis=-1, keepdims=True)
    return xc * lax.rsqrt(var + EPS) * g + b


def _silu(x):
    return x * jax.nn.sigmoid(x)


def _part_specs(parts, tm, tile_of):
    specs, firsts, first = [], [], 0
    for part in parts:
        n = part.shape[0] // tm
        specs.append(pl.BlockSpec((tm, part.shape[1]),
                                  lambda *g, first=first, n=n: (jnp.clip(tile_of(*g) - first, 0, n - 1), 0)))
        firsts.append(first)
        first += n
    return specs, tuple(firsts)


def _load_parts(refs, firsts, tile):
    x = refs[0][...]
    for ref, first in zip(refs[1:], firsts[1:]):
        x = jnp.where(tile >= first, ref[...], x)
    return x


def _for_row_blocks(n_rows, body):
    block = min(ROW_BLOCK, n_rows)

    def step(r, carry):
        body(pl.ds(pl.multiple_of(r * block, block), block))
        return carry

    lax.fori_loop(0, n_rows // block, step, 0)


def _dot(a, b):
    return jnp.dot(a, b, preferred_element_type=F32)


def _cast_pad_cols_kernel(w_ref, o_ref):
    n = w_ref.shape[1]
    o_ref[:, :n] = w_ref[...].astype(BF16)
    o_ref[:, n:] = jnp.zeros((o_ref.shape[0], o_ref.shape[1] - n), BF16)


def _cast_pad_cols(w, l, s, n_pad, tr=256):
    m, n = w.shape[2:]
    return pl.pallas_call(
        _cast_pad_cols_kernel,
        grid=(m // tr,),
        in_specs=[pl.BlockSpec((None, None, tr, n), lambda i: (l, s, i, 0))],
        out_specs=pl.BlockSpec((tr, n_pad), lambda i: (i, 0)),
        out_shape=jax.ShapeDtypeStruct((m, n_pad), BF16),
        compiler_params=_cparams(("parallel",)),
    )(w)


def _cast_pad_rows_kernel(w_ref, o_ref, *, rows_valid):
    tr = w_ref.shape[0]
    row = pl.program_id(0) * tr + lax.broadcasted_iota(jnp.int32, w_ref.shape, 0)
    o_ref[...] = jnp.where(row < rows_valid, w_ref[...], 0.0).astype(BF16)


def _cast_pad_rows(w, l, s, m_pad, tr=512):
    m, n = w.shape[2:]
    return pl.pallas_call(
        functools.partial(_cast_pad_rows_kernel, rows_valid=m),
        grid=(m_pad // tr,),
        in_specs=[pl.BlockSpec((None, None, tr, n), lambda j: (l, s, j, 0))],
        out_specs=pl.BlockSpec((tr, n), lambda j: (j, 0)),
        out_shape=jax.ShapeDtypeStruct((m_pad, n), BF16),
        compiler_params=_cparams(("parallel",)),
    )(w)


def _mod_kernel(cond_ref, b_ref, *refs):
    w_refs, o_ref = refs[:-1], refs[-1]
    k = pl.program_id(1)

    @pl.when(k == 0)
    def _():
        o_ref[...] = jnp.broadcast_to(b_ref[...], o_ref.shape)

    s = _silu(cond_ref[...]).astype(BF16)
    part = o_ref.shape[1] // len(w_refs)
    for j, w_ref in enumerate(w_refs):
        o_ref[:, j * part:(j + 1) * part] += _dot(s, w_ref[...].astype(BF16))


def _modulation(cond, w_mod, b_mod, tk=256):
    n_layers, d, n = w_mod.shape
    part = n // MOD_STREAMS
    w_specs = [pl.BlockSpec((None, tk, part), lambda l, k, j=j: (l, k, j)) for j in range(MOD_STREAMS)]
    return pl.pallas_call(
        _mod_kernel,
        grid=(n_layers, d // tk),
        in_specs=[
            pl.BlockSpec((COND_ROWS, tk), lambda l, k: (0, k)),
            pl.BlockSpec((None, 1, n), lambda l, k: (l, 0, 0)),
            *w_specs,
        ],
        out_specs=pl.BlockSpec((None, COND_ROWS, n), lambda l, k: (l, 0, 0)),
        out_shape=jax.ShapeDtypeStruct((n_layers, COND_ROWS, n), F32),
        compiler_params=_cparams(("parallel", "arbitrary")),
    )(cond, b_mod.reshape(n_layers, 1, n), *([w_mod] * MOD_STREAMS))


def _ffn_kernel(*refs, cast_next, d_ff, pack_in):
    x_ref, shift_ref, scale_ref, gate_ref, gpre_ref, gpost_ref, wg_ref, wu_ref, wd_ref = refs[:9]
    if cast_next and pack_in:
        (wgn_ref, wun_ref, wdn_ref, pa_ref, pb_ref, ga_ref, gb_ref,
         o_ref, wgo_ref, wuo_ref, wdo_ref, wabcd_ref, wgates_ref, h_ref, r_ref) = refs[9:]
    elif cast_next:
        wgn_ref, wun_ref, wdn_ref, o_ref, wgo_ref, wuo_ref, wdo_ref, h_ref, r_ref = refs[9:]
    else:
        o_ref, h_ref, r_ref = refs[9:]
    f = pl.program_id(1)

    @pl.when(f == 0)
    def _():
        gain = gpre_ref[...] * (1.0 + scale_ref[...])
        shift = shift_ref[...]

        def norms(rows):
            r_ref[rows, :] = _inv_rms(x_ref[rows, :])

        def block(rows):
            h_ref[rows, :] = (x_ref[rows, :] * r_ref[rows, :] * gain + shift).astype(BF16)
            o_ref[rows, :] = jnp.zeros((rows.size, o_ref.shape[1]), F32)

        _for_row_blocks(x_ref.shape[0], norms)
        _for_row_blocks(x_ref.shape[0], block)

    h = h_ref[...]
    g = _dot(h, wg_ref[...])
    u = _dot(h, wu_ref[...])
    a = (_silu(g) * u).astype(BF16)
    o_ref[...] += _dot(a, wd_ref[...])

    if cast_next:
        for src_ref, dst_ref in ((wgn_ref, wgo_ref), (wun_ref, wuo_ref)):
            dst_ref[:, :d_ff] = src_ref[...].astype(BF16)
            dst_ref[:, d_ff:] = jnp.zeros((dst_ref.shape[0], dst_ref.shape[1] - d_ff), BF16)
        rd = wdo_ref.shape[0]
        step = pl.program_id(0) * pl.num_programs(1) + f
        row = jnp.minimum(step, cast_next - 1) * rd + lax.broadcasted_iota(jnp.int32, wdo_ref.shape, 0)
        wdo_ref[...] = jnp.where(row < d_ff, wdn_ref[...], 0.0).astype(BF16)
    if pack_in:
        wabcd_ref[...] = jnp.concatenate([pa_ref[...], pb_ref[...]], axis=0).T.astype(BF16)
        wgates_ref[...] = jnp.concatenate([ga_ref[...], gb_ref[...]], axis=0).T.astype(BF16)

    @pl.when(f == pl.num_programs(1) - 1)
    def _():
        gain = 0.5 * gate_ref[...] * gpost_ref[...]

        def norms(rows):
            r_ref[rows, :] = _inv_rms(o_ref[rows, :])

        def block(rows):
            o_ref[rows, :] = x_ref[rows, :] + o_ref[rows, :] * r_ref[rows, :] * gain

        _for_row_blocks(x_ref.shape[0], norms)
        _for_row_blocks(x_ref.shape[0], block)


def _slab_rows(total, n_steps):
    slab = BF16_ROWS
    while total % slab or total // slab > n_steps:
        slab *= 2
        assert slab <= total
    return slab


def _ffn(x, rows, mod, j, row_of_tile, gpre, gpost, wts, tm, cast_next=None, pack_in=None):
    wg, wu, wd = wts
    d = x.shape[1]
    f_pad = wg.shape[-1]
    tf = FF_CHUNK
    nf = f_pad // tf

    def mspec(k):
        return pl.BlockSpec((None, None, 1, d), lambda i, f: (row_of_tile(i), k, 0, 0))

    vec = pl.BlockSpec((1, d), lambda i, f: (0, 0))
    in_specs = [
        pl.BlockSpec((tm, d), lambda i, f: (i, 0)),
        mspec(3 * j), mspec(3 * j + 1), mspec(3 * j + 2),
        vec, vec,
        pl.BlockSpec((d, tf), lambda i, f: (0, f)),
        pl.BlockSpec((d, tf), lambda i, f: (0, f)),
        pl.BlockSpec((tf, d), lambda i, f: (f, 0)),
    ]
    args = [x, mod, mod, mod, gpre, gpost, wg, wu, wd]
    out_specs = [pl.BlockSpec((tm, d), lambda i, f: (i, 0))]
    out_shape = [jax.ShapeDtypeStruct((rows, d), F32)]
    down_slabs, d_ff = 0, f_pad
    if cast_next is not None:
        wg32, wu32, wd32, l2, s2 = cast_next
        d_ff = wg32.shape[-1]
        n_steps = (rows // tm) * nf
        rg, rd = _slab_rows(d, n_steps), _slab_rows(f_pad, n_steps)
        up_slabs, down_slabs, down_src_slabs = d // rg, f_pad // rd, pl.cdiv(d_ff, rd)

        def slab(count):
            return lambda i, f: jnp.minimum(i * nf + f, count - 1)

        in_specs += [pl.BlockSpec((None, None, rg, d_ff), lambda i, f: (l2, s2, slab(up_slabs)(i, f), 0))] * 2
        in_specs += [pl.BlockSpec((None, None, rd, d), lambda i, f: (l2, s2, slab(down_src_slabs)(i, f), 0))]
        args += [wg32, wu32, wd32]
        out_specs += [pl.BlockSpec((rg, f_pad), lambda i, f: (slab(up_slabs)(i, f), 0))] * 2
        out_specs += [pl.BlockSpec((rd, d), lambda i, f: (slab(down_slabs)(i, f), 0))]
        out_shape += [jax.ShapeDtypeStruct((d, f_pad), BF16)] * 2 + [jax.ShapeDtypeStruct((f_pad, d), BF16)]
        if pack_in is not None:
            w_in_t, l_in = pack_in
            half = ROT_DIM
            n_abcd = sum(IN_WIDTHS)
            n_gates = w_in_t.shape[1] - (n_abcd - half)
            dup = (n_abcd - IN_WIDTHS[3]) // half - 1
            abcd_blocks, gate_blocks = n_abcd // LANES, n_gates // LANES
            if n_steps < max(abcd_blocks, gate_blocks):
                return _ffn(x, rows, mod, j, row_of_tile, gpre, gpost, wts, tm, cast_next=cast_next) + (None,)

            def src_abcd(k):
                return jnp.where(k < dup, k, k - 1)

            def half_spec(block_of_step, src, part):
                return pl.BlockSpec((None, half, d),
                                    lambda i, f: (l_in, src(2 * block_of_step(i, f) + part), 0))

            gate0 = (n_abcd - half) // half
            in_specs += [half_spec(slab(abcd_blocks), src_abcd, 0), half_spec(slab(abcd_blocks), src_abcd, 1),
                         half_spec(slab(gate_blocks), lambda k: gate0 + k, 0),
                         half_spec(slab(gate_blocks), lambda k: gate0 + k, 1)]
            args += [w_in_t] * 4
            out_specs += [pl.BlockSpec((d, LANES), lambda i, f: (0, slab(abcd_blocks)(i, f))),
                          pl.BlockSpec((d, LANES), lambda i, f: (0, slab(gate_blocks)(i, f)))]
            out_shape += [jax.ShapeDtypeStruct((d, n_abcd), BF16), jax.ShapeDtypeStruct((d, n_gates), BF16)]
    row_axis = "parallel" if cast_next is None else "arbitrary"
    outs = pl.pallas_call(
        functools.partial(_ffn_kernel, cast_next=down_slabs, d_ff=d_ff, pack_in=pack_in is not None),
        grid=(rows // tm, nf),
        in_specs=in_specs,
        out_specs=out_specs,
        out_shape=out_shape,
        scratch_shapes=[pltpu.VMEM((tm, d), BF16), pltpu.VMEM((tm, 1), F32)],
        compiler_params=_cparams((row_axis, "arbitrary"), VMEM_LIMIT_FFN),
    )(*args)
    if cast_next is None:
        return outs[0]
    return (outs[0], tuple(outs[1:4])) + ((tuple(outs[4:]),) if pack_in is not None else ())


def _inproj_kernel(*refs, x_firsts):
    x_refs = refs[:len(x_firsts)]
    (shift_ref, scale_ref, gpre_ref, w_ref, b_ref, cos_ref, sina_ref, sinb_ref, qn_ref, kvn_ref, wuq_ref, wukv_ref,
     lng_ref, lnb_ref, ws_ref, bs_ref,
     h_ref, ya_ref, hb_ref, qc_ref, kc_ref, vc_ref, qd_ref, kd_ref, vd_ref) = refs[len(x_firsts):]
    y = _rms(_load_parts(x_refs, x_firsts, pl.program_id(0))) * gpre_ref[...]
    h = (y * (1.0 + scale_ref[...]) + shift_ref[...]).astype(BF16)
    h_ref[...] = h

    def proj(lo, hi):
        return _dot(h, w_ref[:, lo:hi]) + b_ref[:, lo:hi]

    na, nb, nc, nd = IN_WIDTHS
    _qkv_prep(proj(na + nb, na + nb + nc), proj(na + nb + nc, na + nb + nc + nd),
              cos_ref, sina_ref, sinb_ref, qn_ref, kvn_ref, wuq_ref, wukv_ref,
              qc_ref, kc_ref, vc_ref, qd_ref, kd_ref, vd_ref)
    w = BRANCH_W
    zb = proj(na, na + nb)
    hb_ref[...] = zb[:, :w] * jax.nn.sigmoid(zb[:, w:])
    za = proj(0, na)
    g = 0.5 * za * (1.0 + lax.erf(za * (2.0 ** -0.5)))
    vln = _layernorm(g[:, w:], lng_ref[...], lnb_ref[...]).astype(BF16)
    for c in range(za.shape[0] // CHUNK):
        rows = slice(c * CHUNK, (c + 1) * CHUNK)
        for gi in range(GM_GROUPS):
            cols = slice(gi * CHUNK, (gi + 1) * CHUNK)
            sv = _dot(ws_ref[gi], vln[rows, cols]) + bs_ref[gi]
            ya_ref[rows, cols] = (g[rows, cols] * sv).astype(BF16)


def _inproj(x_parts, mod, row_of_tile, gpre, w, b, cos, sina, sinb, qn, kvn, wuq, wukv, lng, lnb, ws, bs, tm):
    rows, d = sum(part.shape[0] for part in x_parts), x_parts[0].shape[1]
    x_specs, x_firsts = _part_specs(x_parts, tm, lambda i: i)
    n = w.shape[1]

    def mspec(k):
        return pl.BlockSpec((None, None, 1, d), lambda i: (row_of_tile(i), k, 0, 0))

    def rowspec(wd):
        return pl.BlockSpec((tm, wd), lambda i: (i, 0))

    def full(a):
        return pl.BlockSpec(a.shape, lambda i: (0,) * a.ndim)

    def headspec(nh, wd):
        return pl.BlockSpec((nh, tm, wd), lambda i: (0, i, 0))

    def headspec_t(nh, wd):
        return pl.BlockSpec((nh, wd, tm), lambda i: (0, 0, i))

    kc = jax.ShapeDtypeStruct((MLA_HEADS, rows, MLA_QK), BF16)
    qc = jax.ShapeDtypeStruct((MLA_HEADS, MLA_QK, rows), BF16)
    vc = jax.ShapeDtypeStruct((MLA_HEADS, MLA_V + MLA_SUM_ROWS, rows), BF16)
    qd = jax.ShapeDtypeStruct((SWA_HEADS // 2, LANES, rows), BF16)
    kd = jax.ShapeDtypeStruct((rows, LANES), BF16)
    vd = jax.ShapeDtypeStruct((LANES, rows), BF16)
    outs = [jax.ShapeDtypeStruct((rows, d), BF16), jax.ShapeDtypeStruct((rows, BRANCH_W), BF16),
            jax.ShapeDtypeStruct((rows, BRANCH_W), F32)]
    ospecs = [rowspec(d), rowspec(BRANCH_W), rowspec(BRANCH_W)]
    return pl.pallas_call(
        functools.partial(_inproj_kernel, x_firsts=x_firsts),
        grid=(rows // tm,),
        in_specs=[
            *x_specs,
            mspec(3), mspec(4),
            pl.BlockSpec((1, d), lambda i: (0, 0)),
            pl.BlockSpec((d, n), lambda i: (0, 0), pipeline_mode=pl.Buffered(1)),
            pl.BlockSpec((1, n), lambda i: (0, 0)),
            rowspec(LANES), rowspec(LANES), rowspec(LANES), full(qn), full(kvn), full(wuq), full(wukv),
            full(lng), full(lnb), full(ws), full(bs),
        ],
        out_specs=ospecs + [headspec_t(MLA_HEADS, MLA_QK), headspec(MLA_HEADS, MLA_QK),
                            headspec_t(MLA_HEADS, MLA_V + MLA_SUM_ROWS), headspec_t(SWA_HEADS // 2, LANES),
                            rowspec(LANES), pl.BlockSpec((LANES, tm), lambda i: (0, i))],
        out_shape=outs + [qc, kc, vc, qd, kd, vd],
        compiler_params=_cparams(("parallel",)),
    )(*x_parts, mod, mod, gpre, w, b, cos, sina, sinb, qn, kvn, wuq, wukv, lng, lnb, ws, bs)


def _conv_kernel(h_ref, hprev_ref, hnext_ref, dw_ref, db_ref, lng_ref, lnb_ref, yb_ref, ext_ref, shift_ref,
                 *, ts, n_lat_tiles, lat_tiles_per_seq, ctx_tiles_per_seq):
    i = pl.program_id(0)
    in_lat = i < n_lat_tiles
    pos = jnp.where(in_lat, i % lat_tiles_per_seq, (i - n_lat_tiles) % ctx_tiles_per_seq)
    last = jnp.where(in_lat, lat_tiles_per_seq - 1, ctx_tiles_per_seq - 1)
    ext_ref[0:CONV_HALO, :] = jnp.where(pos != 0, hprev_ref[...], 0.0)
    ext_ref[CONV_HALO:CONV_HALO + ts, :] = h_ref[...]
    ext_ref[CONV_HALO + ts:, :] = jnp.where(pos != last, hnext_ref[...], 0.0)
    span = ts + 2 * CONV_HALO - SUBLANES
    for r in range(1, SUBLANES):
        shift_ref[r - 1, 0:span, :] = ext_ref[r:r + span, :]
    first_tap = CONV_HALO - CONV_W // 2
    acc = jnp.zeros((ts, BRANCH_W), F32) + db_ref[...]
    for k in range(CONV_W):
        a, r = divmod(first_tap + k, SUBLANES)
        win = ext_ref[a * SUBLANES:a * SUBLANES + ts, :] if r == 0 else shift_ref[r - 1, a * SUBLANES:a * SUBLANES + ts, :]
        acc = acc + dw_ref[k:k + 1, :] * win
    yb_ref[...] = _silu(_layernorm(acc, lng_ref[...], lnb_ref[...])).astype(BF16)


def _conv_mixer(hb, dw, db, lng, lnb, n_lat_rows, seq_lat, seq_ctx, ts):
    rows, w = hb.shape
    halo_blocks = ts // CONV_HALO
    n_halo = rows // CONV_HALO
    kern = functools.partial(_conv_kernel, ts=ts, n_lat_tiles=n_lat_rows // ts,
                             lat_tiles_per_seq=seq_lat // ts, ctx_tiles_per_seq=seq_ctx // ts)
    vec = pl.BlockSpec((1, w), lambda i: (0, 0))
    return pl.pallas_call(
        kern,
        grid=(rows // ts,),
        in_specs=[
            pl.BlockSpec((ts, w), lambda i: (i, 0)),
            pl.BlockSpec((CONV_HALO, w), lambda i: (jnp.maximum(i * halo_blocks - 1, 0), 0)),
            pl.BlockSpec((CONV_HALO, w), lambda i: (jnp.minimum((i + 1) * halo_blocks, n_halo - 1), 0)),
            pl.BlockSpec((CONV_W, w), lambda i: (0, 0)),
            vec, vec, vec,
        ],
        out_specs=pl.BlockSpec((ts, w), lambda i: (i, 0)),
        out_shape=jax.ShapeDtypeStruct((rows, w), BF16),
        scratch_shapes=[pltpu.VMEM((ts + 2 * CONV_HALO, w), F32),
                        pltpu.VMEM((SUBLANES - 1, ts + 2 * CONV_HALO - SUBLANES, w), F32)],
        compiler_params=_cparams(("parallel",)),
    )(hb, hb, hb, dw, db, lng, lnb)


def _qkv_prep(zc, zd, cos_ref, sina_ref, sinb_ref, qn_ref, kvn_ref, wuq_ref, wukv_ref,
              qc_ref, kc_ref, vc_ref, qd_ref, kd_ref, vd_ref):
    cos, sina, sinb = cos_ref[...], sina_ref[...], sinb_ref[...]
    low = lax.broadcasted_iota(jnp.int32, cos.shape, 1) < ROT_DIM

    def rope(slab):
        return (slab * cos + pltpu.roll(slab, LANES - ROT_QUARTER, 1) * sina
                + pltpu.roll(slab, ROT_QUARTER, 1) * sinb)

    nq = MLA_HEADS * MLA_NOPE
    kv0 = MLA_Q_RANK + MLA_KV_RANK
    q = _dot((_rms(zc[:, :MLA_Q_RANK]) * qn_ref[...]).astype(BF16), wuq_ref[...]) * MLA_SCALE
    kv = _dot((_rms(zc[:, MLA_Q_RANK:kv0]) * kvn_ref[...]).astype(BF16), wukv_ref[...])
    kr = rope(zc[:, kv0:kv0 + LANES])
    ones = jnp.ones((MLA_SUM_ROWS, zc.shape[0]), BF16)
    for pair in range(MLA_HEADS // 2):
        qr = rope(q[:, nq + pair * LANES:nq + (pair + 1) * LANES]).T.astype(BF16)
        for h in (2 * pair, 2 * pair + 1):
            qc_ref[h, 0:MLA_NOPE, :] = q[:, h * MLA_NOPE:(h + 1) * MLA_NOPE].T.astype(BF16)
            qc_ref[h, MLA_NOPE:, :] = qr
            kc_ref[h, :, 0:MLA_NOPE] = kv[:, h * MLA_NOPE:(h + 1) * MLA_NOPE].astype(BF16)
            keep = low if h % 2 == 0 else jnp.logical_not(low)
            kc_ref[h, :, MLA_NOPE:] = jnp.where(keep, kr, 0.0).astype(BF16)
            vc_ref[h, 0:MLA_V, :] = kv[:, nq + h * MLA_V:nq + (h + 1) * MLA_V].T.astype(BF16)
            vc_ref[h, MLA_V:, :] = ones

    for pair in range(SWA_HEADS // 2):
        qd_ref[pair] = (rope(zd[:, pair * LANES:(pair + 1) * LANES]) * SWA_SCALE).T.astype(BF16)
    nqd = SWA_HEADS * ROT_DIM
    kd_ref[...] = rope(zd[:, nqd:nqd + LANES]).astype(BF16)
    vd_ref[...] = zd[:, nqd + LANES:nqd + 2 * LANES].T.astype(BF16)


def _mla_kernel(*refs, has_lat, groups):
    if has_lat:
        qt_ref, kl_ref, vlt_ref, kx_ref, vxt_ref, o_ref = refs
    else:
        qt_ref, kx_ref, vxt_ref, o_ref = refs
    cols = qt_ref.shape[1] // groups
    scores = []
    for g in range(groups):
        qt = qt_ref[:, g * cols:(g + 1) * cols]
        scores.append((_dot(kx_ref[...], qt), _dot(kl_ref[...], qt) if has_lat else None))
    for g, (sx, sl) in enumerate(scores):
        m = jnp.max(sx, axis=0, keepdims=True)
        if has_lat:
            m = jnp.maximum(m, jnp.max(sl, axis=0, keepdims=True))
        ot = _dot(vxt_ref[...], jnp.exp2(sx - m).astype(BF16))
        if has_lat:
            ot = ot + _dot(vlt_ref[...], jnp.exp2(sl - m).astype(BF16))
        o_ref[g * cols:(g + 1) * cols, :] = (ot[:MLA_V] / ot[MLA_V:MLA_V + 1]).T.astype(BF16)


def _mla_attention(qc, kc, vc, batch, seq_lat, seq_ctx, lat_queries):
    n_lat_rows = batch * seq_lat
    ctx_blk0 = n_lat_rows // seq_ctx
    tq = math.gcd(MLA_GROUPS * MLA_GROUP_Q, seq_lat if lat_queries else seq_ctx)
    if lat_queries:
        nq, q0 = seq_lat // tq, 0
    else:
        nq, q0 = seq_ctx // tq, n_lat_rows // tq
    vrows = vc.shape[1]
    qspec = pl.BlockSpec((None, MLA_QK, tq), lambda b, h, i: (h, 0, q0 + b * nq + i))
    kx = pl.BlockSpec((None, seq_ctx, MLA_QK), lambda b, h, i: (h, ctx_blk0 + b, 0))
    vx = pl.BlockSpec((None, vrows, seq_ctx), lambda b, h, i: (h, 0, ctx_blk0 + b))
    if lat_queries:
        kl = pl.BlockSpec((None, seq_lat, MLA_QK), lambda b, h, i: (h, b, 0))
        vl = pl.BlockSpec((None, vrows, seq_lat), lambda b, h, i: (h, 0, b))
        in_specs, args = [qspec, kl, vl, kx, vx], (qc, kc, vc, kc, vc)
    else:
        in_specs, args = [qspec, kx, vx], (qc, kc, vc)
    return pl.pallas_call(
        functools.partial(_mla_kernel, has_lat=lat_queries, groups=max(1, tq // MLA_GROUP_Q)),
        grid=(batch, MLA_HEADS, nq),
        in_specs=in_specs,
        out_specs=pl.BlockSpec((tq, MLA_V), lambda b, h, i: (b * nq + i, h)),
        out_shape=jax.ShapeDtypeStruct((batch * nq * tq, MLA_HEADS * MLA_V), BF16),
        compiler_params=_cparams(("parallel", "parallel", "parallel")),
    )(*args)


def _swa_kernel(*refs, has_lat):
    if has_lat:
        qt_ref, kx_ref, vxt_ref, sink_ref, bias_ref = refs[:5]
        n_win = (len(refs) - 6) // 2
        kw_refs, vw_refs, o_ref = refs[5:5 + n_win], refs[5 + n_win:-1], refs[-1]
    else:
        qt_ref, kx_ref, vxt_ref, sink_ref, o_ref = refs
    blocks = qt_ref.shape[2] // SWA_BLOCK
    half = LANES // 2
    span = 3 * SWA_BLOCK
    gcol = SWA_GROUP * SWA_BLOCK
    zeros = jnp.zeros((half, gcol), BF16)
    ones = jnp.ones((SWA_SUM_ROWS, 1), BF16)
    if has_lat:
        k_win = jnp.concatenate([r[...] for r in kw_refs], axis=0)
        v_win = jnp.concatenate([r[...] for r in vw_refs], axis=1)
    chains = []
    for kv in range(SWA_KV):
        for blk in range(blocks):
            heads = range(kv * SWA_GROUP, (kv + 1) * SWA_GROUP)
            qt = jnp.concatenate([qt_ref[h // 2, (h % 2) * half:(h % 2 + 1) * half,
                                         blk * SWA_BLOCK:(blk + 1) * SWA_BLOCK] for h in heads], axis=1)
            qt = jnp.concatenate([qt, zeros] if kv == 0 else [zeros, qt], axis=0)
            k0 = blk * SWA_BLOCK
            chains.append((kv, blk, _dot(kx_ref[...], qt), _dot(k_win[k0:k0 + span], qt) if has_lat else None))
    for kv, blk, sx, sl in chains:
        rows = slice(kv * half, (kv + 1) * half)
        cols = slice(blk * gcol, (blk + 1) * gcol)
        k0 = blk * SWA_BLOCK
        sink = sink_ref[kv][:, cols] * LOG2_E
        m = jnp.maximum(sink, jnp.max(sx, axis=0, keepdims=True))
        if has_lat:
            sl = sl + bias_ref[k0:k0 + span, cols]
            m = jnp.maximum(m, jnp.max(sl, axis=0, keepdims=True))
        vx = vxt_ref[rows, :]
        ot = _dot(jnp.concatenate([vx, jnp.broadcast_to(ones, (SWA_SUM_ROWS, vx.shape[1]))], axis=0),
                  jnp.exp2(sx - m).astype(BF16))
        if has_lat:
            vl = v_win[rows, k0:k0 + span]
            ot = ot + _dot(jnp.concatenate([vl, jnp.broadcast_to(ones, (SWA_SUM_ROWS, span))], axis=0),
                           jnp.exp2(sl - m).astype(BF16))
        o = ot[:half] / (ot[half:half + 1] + jnp.exp2(sink - m))
        for pair in range(SWA_GROUP // 2):
            c0 = 2 * pair * SWA_BLOCK
            slab_t = jnp.concatenate([o[:, c0:c0 + SWA_BLOCK], o[:, c0 + SWA_BLOCK:c0 + 2 * SWA_BLOCK]], axis=0)
            lane0 = (kv * (SWA_GROUP // 2) + pair) * LANES
            o_ref[blk * SWA_BLOCK:(blk + 1) * SWA_BLOCK, lane0:lane0 + LANES] = slab_t.T.astype(BF16)


def _swa_window_bias(blocks):
    nkeys = (blocks + 2) * SWA_BLOCK
    ncol = blocks * SWA_GROUP * SWA_BLOCK
    key = jnp.arange(nkeys)[:, None] - SWA_BLOCK
    col = jnp.arange(ncol)[None, :]
    qry = (col // (SWA_GROUP * SWA_BLOCK)) * SWA_BLOCK + (col % SWA_BLOCK)
    inside = jnp.abs(key - qry) <= WINDOW
    after_start = key >= 0
    before_end = key < blocks * SWA_BLOCK
    variants = [inside, inside & after_start, inside & before_end, inside & after_start & before_end]
    return jnp.where(jnp.stack(variants), 0.0, -jnp.inf).astype(F32)


def _swa_attention(qd, kd, vd, sink, batch, seq_lat, seq_ctx, lat_queries):
    n_lat_rows = batch * seq_lat
    ctx_blk0 = n_lat_rows // seq_ctx
    seq_q = seq_lat if lat_queries else seq_ctx
    blocks = math.gcd(SWA_QUERY_BLOCKS, math.gcd(seq_q, n_lat_rows) // SWA_BLOCK)
    tq = blocks * SWA_BLOCK
    nb = seq_q // tq
    q0 = 0 if lat_queries else n_lat_rows // tq
    kblocks = seq_lat // SWA_BLOCK
    sink_row = jnp.tile(jnp.repeat(sink.reshape(SWA_KV, SWA_GROUP), SWA_BLOCK, axis=1), (1, blocks))[:, None, :]
    in_specs = [pl.BlockSpec((SWA_HEADS // 2, LANES, tq), lambda b, i: (0, 0, q0 + b * nb + i)),
                pl.BlockSpec((seq_ctx, LANES), lambda b, i: (ctx_blk0 + b, 0)),
                pl.BlockSpec((LANES, seq_ctx), lambda b, i: (0, ctx_blk0 + b)),
                pl.BlockSpec(sink_row.shape, lambda b, i: (0, 0, 0))]
    args = [qd, kd, vd, sink_row]
    if lat_queries:
        n_win = blocks + 2
        bias = _swa_window_bias(blocks)
        in_specs.append(pl.BlockSpec((None,) + bias.shape[1:],
                                     lambda b, i: ((i == 0).astype(jnp.int32) + 2 * (i == nb - 1).astype(jnp.int32), 0, 0)))
        args.append(bias)

        def win(j):
            return lambda b, i: b * kblocks + jnp.clip(i * blocks - 1 + j, 0, kblocks - 1)

        in_specs += [pl.BlockSpec((SWA_BLOCK, LANES), lambda b, i, j=j: (win(j)(b, i), 0))
                     for j in range(n_win)]
        in_specs += [pl.BlockSpec((LANES, SWA_BLOCK), lambda b, i, j=j: (0, win(j)(b, i)))
                     for j in range(n_win)]
        args += [kd] * n_win + [vd] * n_win
    return pl.pallas_call(
        functools.partial(_swa_kernel, has_lat=lat_queries),
        grid=(batch, nb),
        in_specs=in_specs,
        out_specs=pl.BlockSpec((tq, BRANCH_W), lambda b, i: (b * nb + i, 0)),
        out_shape=jax.ShapeDtypeStruct((batch * nb * tq, BRANCH_W), BF16),
        compiler_params=_cparams(("parallel", "parallel")),
    )(*args)


def _gate_kernel(*refs, y_firsts):
    n = len(y_firsts)
    h_ref, ya_ref, yb_ref = refs[:3]
    yc_refs, yd_refs = refs[3:3 + n], refs[3 + n:3 + 2 * n]
    wg0, wg1, wg2, wg3, bg_ref, wb0, wb1, wb2, wb3, o_ref = refs[3 + 2 * n:]
    tile = pl.program_id(1)
    ys = (ya_ref[...], yb_ref[...], _load_parts(yc_refs, y_firsts, tile), _load_parts(yd_refs, y_firsts, tile))
    h = h_ref[...]
    acc = None
    for b, (y, wg_ref, wb_ref) in enumerate(zip(ys, (wg0, wg1, wg2, wg3), (wb0, wb1, wb2, wb3))):
        gate = jax.nn.sigmoid(_dot(h, wg_ref[...]) + bg_ref[b])
        term = gate * _dot(y, wb_ref[...])
        acc = term if acc is None else acc + term
    o_ref[...] = acc.astype(BF16)


def _gated_merge(h, ya, yb, yc_parts, yd_parts, wg, bg, wbr, rows, tm, tn):
    d = h.shape[1]
    ncol = d // tn
    wg_specs = [pl.BlockSpec((d, tn), lambda j, i, b=b: (0, b * ncol + j)) for b in range(4)]
    wb_specs = [pl.BlockSpec((None, BRANCH_W, tn), lambda j, i, b=b: (b, 0, j)) for b in range(4)]
    yspec = pl.BlockSpec((tm, BRANCH_W), lambda j, i: (i, 0))
    yc_specs, y_firsts = _part_specs(yc_parts, tm, lambda j, i: i)
    yd_specs, _ = _part_specs(yd_parts, tm, lambda j, i: i)
    return pl.pallas_call(
        functools.partial(_gate_kernel, y_firsts=y_firsts),
        grid=(ncol, rows // tm),
        in_specs=[pl.BlockSpec((tm, d), lambda j, i: (i, 0)), yspec, yspec, *yc_specs, *yd_specs,
                  *wg_specs, pl.BlockSpec((4, 1, tn), lambda j, i: (0, 0, j)), *wb_specs],
        out_specs=pl.BlockSpec((tm, tn), lambda j, i: (i, j)),
        out_shape=jax.ShapeDtypeStruct((rows, d), BF16),
        compiler_params=_cparams(("parallel", "parallel")),
    )(h, ya, yb, *yc_parts, *yd_parts, wg, wg, wg, wg, bg, wbr, wbr, wbr, wbr)


def _outproj_kernel(*refs, x_firsts):
    a_ref, x_refs = refs[0], refs[1:1 + len(x_firsts)]
    gate_ref, gpost_ref, w_ref, o_ref = refs[1 + len(x_firsts):]
    y = _dot(a_ref[...], w_ref[...])
    o_ref[...] = _load_parts(x_refs, x_firsts, pl.program_id(0)) + gate_ref[...] * (_rms(y) * gpost_ref[...])


def _outproj(a, x_parts, mod, row_of_tile, gpost, w, tm):
    rows, d = a.shape
    x_specs, x_firsts = _part_specs(x_parts, tm, lambda i: i)
    return pl.pallas_call(
        functools.partial(_outproj_kernel, x_firsts=x_firsts),
        grid=(rows // tm,),
        in_specs=[
            pl.BlockSpec((tm, d), lambda i: (i, 0)),
            *x_specs,
            pl.BlockSpec((None, None, 1, d), lambda i: (row_of_tile(i), 5, 0, 0)),
            pl.BlockSpec((1, d), lambda i: (0, 0)),
            pl.BlockSpec((d, d), lambda i: (0, 0), pipeline_mode=pl.Buffered(1)),
        ],
        out_specs=pl.BlockSpec((tm, d), lambda i: (i, 0)),
        out_shape=jax.ShapeDtypeStruct((rows, d), F32),
        compiler_params=_cparams(("parallel",)),
    )(a, *x_parts, mod, gpost, w)


def _rope_tables(seq_lat, n_ctx_rows, batch):
    half = ROT_DIM // 2
    freqs = 1.0 / (ROPE_BASE ** (jnp.arange(0, half, 2, dtype=F32) / half))
    t = jnp.arange(seq_lat)
    ang_r = (t // GRID_W).astype(F32)[:, None] * freqs[None, :]
    ang_c = (t % GRID_W).astype(F32)[:, None] * freqs[None, :]
    ang = jnp.concatenate([ang_r, ang_r, ang_c, ang_c], axis=-1)
    ang = jnp.tile(ang, (batch, LANES // ROT_DIM))
    cos, sin = jnp.cos(ang), jnp.sin(ang)
    first = (jnp.arange(LANES) % half) < ROT_QUARTER
    sina = jnp.where(first, -sin, 0.0)
    sinb = jnp.where(first, 0.0, sin)
    ones = jnp.ones((n_ctx_rows, LANES), F32)
    zeros = jnp.zeros((n_ctx_rows, LANES), F32)
    return (jnp.concatenate([cos, ones]), jnp.concatenate([sina, zeros]), jnp.concatenate([sinb, zeros]))


def _pack_in_projection(w_in, l):
    na, nb, nc, nd = IN_WIDTHS
    c_hi = na + nb + nc - ROT_DIM
    wi = w_in[l]
    w_abcd = jnp.concatenate([wi[:, :c_hi], wi[:, c_hi - ROT_DIM:c_hi], wi[:, c_hi:c_hi + nd]], axis=1)
    return w_abcd.astype(BF16), wi[:, c_hi + nd:].astype(BF16)


def _pack_layer(l, b_in, gm_ws, gm_bs, mla_w_uq, mla_w_ukv, swa_sink, w_branch, w_out):
    d = w_branch.shape[-1]
    p = {}
    bi = b_in[l]
    na, nb, nc, nd = IN_WIDTHS
    c_hi = na + nb + nc - ROT_DIM
    kr = slice(c_hi - ROT_DIM, c_hi)
    p['b_abcd'] = jnp.concatenate([bi[:c_hi], bi[kr], bi[c_hi:c_hi + nd]])[None, :]
    p['b_gates'] = bi[c_hi + nd:].reshape(4, 1, d)
    p['gm_ws'] = gm_ws[l].astype(BF16)
    p['gm_bs'] = gm_bs[l][:, :, None]
    uq = mla_w_uq[l].reshape(-1, MLA_HEADS, MLA_NOPE + ROT_DIM)
    p['wuq'] = jnp.concatenate([uq[:, :, :MLA_NOPE].reshape(-1, MLA_HEADS * MLA_NOPE),
                                uq[:, :, MLA_NOPE:].reshape(-1, MLA_HEADS * ROT_DIM)], axis=1).astype(BF16)
    ukv = mla_w_ukv[l].reshape(-1, MLA_HEADS, MLA_NOPE + MLA_V)
    p['wukv'] = jnp.concatenate([ukv[:, :, :MLA_NOPE].reshape(-1, MLA_HEADS * MLA_NOPE),
                                 ukv[:, :, MLA_NOPE:].reshape(-1, MLA_HEADS * MLA_V)], axis=1).astype(BF16)
    p['sink'] = swa_sink[l]
    p['w_branch'] = w_branch[l].astype(BF16)
    p['w_out'] = w_out[l].astype(BF16)
    return p


def kernel(x, c, ctx, c_ctx, w_mod, b_mod, norm_pre, norm_post, w_ff_gate, w_ff_up, w_ff_down, w_in, b_in, gm_ln_g, gm_ln_b, gm_ws, gm_bs, cv_dw, cv_db, cv_ln_g, cv_ln_b, mla_q_norm, mla_kv_norm, mla_w_uq, mla_w_ukv, swa_sink, w_branch, w_out):
    batch, seq_lat, d = x.shape
    seq_ctx = ctx.shape[1]
    depth = w_mod.shape[0]
    n_lat = batch * seq_lat
    n_ctx = batch * seq_ctx
    assert batch + 1 <= COND_ROWS and seq_lat % GRID_W == 0
    tm = math.gcd(512, math.gcd(seq_lat, n_ctx))
    ts = math.gcd(256, math.gcd(seq_lat, seq_ctx))
    assert tm % SWA_BLOCK == 0 and ts % CHUNK == 0 and seq_ctx % SWA_BLOCK == 0 and seq_lat >= 3 * SWA_BLOCK

    tm_ffn = math.gcd(1024, math.gcd(seq_lat, n_ctx))

    def cond_row(tile):
        return lambda i: jnp.where(i < n_lat // tile, i // (seq_lat // tile), batch)

    row_of_tile, row_of_ffn_tile = cond_row(tm), cond_row(tm_ffn)

    cond = jnp.zeros((COND_ROWS, d), F32).at[:batch].set(c).at[batch].set(c_ctx)
    mod = _modulation(cond, w_mod, b_mod).reshape(depth, COND_ROWS, N_MOD, 1, d)
    cos, sina, sinb = _rope_tables(seq_lat, n_ctx, batch)

    ff32 = (w_ff_gate, w_ff_up, w_ff_down)
    w_in_t = jnp.swapaxes(w_in, 1, 2)
    wts = (_cast_pad_cols(w_ff_gate, 0, 0, D_FF_PAD), _cast_pad_cols(w_ff_up, 0, 0, D_FF_PAD),
           _cast_pad_rows(w_ff_down, 0, 0, D_FF_PAD))

    n_all = n_lat + n_ctx
    xs = None
    for l in range(depth):
        last = l == depth - 1
        p = _pack_layer(l, b_in, gm_ws, gm_bs, mla_w_uq, mla_w_ukv, swa_sink, w_branch, w_out)
        m = mod[l]
        npre, npost = norm_pre[l][:, None, :], norm_post[l][:, None, :]

        ffn1 = (m, 0, row_of_ffn_tile, npre[0], npost[0])
        if xs is None:
            x_lat, wts2, w_proj = _ffn(x.reshape(n_lat, d), n_lat, *ffn1, wts, tm_ffn, cast_next=ff32 + (l, 1),
                                       pack_in=(w_in_t, l))
            x_ctx = _ffn(ctx.reshape(n_ctx, d), n_ctx, m, 0, lambda i: batch, npre[0], npost[0], wts, tm_ffn)
            x_parts = (x_lat, x_ctx)
        else:
            xs, wts2 = _ffn(xs, n_all, *ffn1, wts, tm_ffn, cast_next=ff32 + (l, 1))
            x_parts, w_proj = (xs,), w_proj_next

        if w_proj is None:
            w_proj = _pack_in_projection(w_in, l)
        h, ya, hb, qc, kc, vc, qd, kd, vd = _inproj(
            x_parts, m, row_of_tile, npre[1], w_proj[0], p['b_abcd'], cos, sina, sinb,
            mla_q_norm[l][None], mla_kv_norm[l][None], p['wuq'], p['wukv'],
            gm_ln_g[l][None], gm_ln_b[l][None], p['gm_ws'], p['gm_bs'], tm)
        yb = _conv_mixer(hb, cv_dw[l], cv_db[l][None], cv_ln_g[l][None], cv_ln_b[l][None],
                         n_lat, seq_lat, seq_ctx, ts)
        rows_mix = n_lat if last else n_all
        attn = (batch, seq_lat, seq_ctx)
        yc_parts = (_mla_attention(qc, kc, vc, *attn, True),)
        yd_parts = (_swa_attention(qd, kd, vd, p['sink'], *attn, True),)
        if not last:
            yc_parts += (_mla_attention(qc, kc, vc, *attn, False),)
            yd_parts += (_swa_attention(qd, kd, vd, p['sink'], *attn, False),)
        merged = _gated_merge(h, ya, yb, yc_parts, yd_parts, w_proj[1], p['b_gates'], p['w_branch'],
                              rows_mix, tm_ffn, GATE_COLS)
        xs = _outproj(merged, x_parts, m, row_of_tile, npost[1], p['w_out'], tm)

        ffn2 = (xs, rows_mix, m, 2, row_of_ffn_tile, npre[2], npost[2], wts2, tm_ffn)
        if last:
            xs = _ffn(*ffn2)
        else:
            xs, wts, w_proj_next = _ffn(*ffn2, cast_next=ff32 + (l + 1, 0), pack_in=(w_in_t, l + 1))
    return xs[:n_lat].reshape(batch, seq_lat, d)
```

```python
import functools
import math

import jax
import jax.numpy as jnp
from jax import lax
from jax.experimental import pallas as pl
from jax.experimental.pallas import tpu as pltpu

F32 = jnp.float32
BF16 = jnp.bfloat16

EPS = 1e-6
ROPE_BASE = 10000.0
GRID_W = 64
N_MOD = 9
COND_ROWS = 8
MOD_STREAMS = 4
LANES = 128
SUBLANES = 8
BF16_ROWS = 16
ROT_DIM = 64
ROT_QUARTER = ROT_DIM // 4
CONV_W = 31
CONV_HALO = 16
CHUNK = 128
GM_GROUPS = 4
BRANCH_W = 512
MLA_HEADS = 4
MLA_NOPE = 128
MLA_V = 128
MLA_Q_RANK = 512
MLA_KV_RANK = 256
MLA_QK = 256
MLA_SUM_ROWS = 16
MLA_GROUP_Q = 512
MLA_GROUPS = 4
MLA_KEY_CHUNK = 1024
LOG2_E = math.log2(math.e)
MLA_SCALE = (MLA_NOPE + ROT_DIM) ** -0.5 * LOG2_E
SWA_HEADS = 8
SWA_KV = 2
SWA_GROUP = SWA_HEADS // SWA_KV
SWA_BLOCK = 128
SWA_QUERY_BLOCKS = 4
SWA_SUM_ROWS = 16
WINDOW = 128
SWA_SCALE = ROT_DIM ** -0.5 * LOG2_E
IN_WIDTHS = (1024, 1024, 896, 768)
D_FF_PAD = 5632
FF_CHUNK = 512
GATE_COLS = 512
ROW_BLOCK = 1024
V7X_VMEM_BYTES = 64 * 1024 * 1024
VMEM_LIMIT = V7X_VMEM_BYTES - 8 * 1024 * 1024
VMEM_LIMIT_FFN = V7X_VMEM_BYTES - 512 * 1024


def _cparams(sem, vmem_limit=VMEM_LIMIT):
    return pltpu.CompilerParams(dimension_semantics=sem, vmem_limit_bytes=vmem_limit)


def _inv_rms(x):
    return lax.rsqrt(jnp.mean(x * x, axis=-1, keepdims=True) + EPS)


def _rms(x):
    return x * _inv_rms(x)


def _layernorm(x, g, b):
    mu = jnp.mean(x, axis=-1, keepdims=True)
    xc = x - mu
    var = jnp.mean(xc * xc, axis=-1, keepdims=True)
    return xc * lax.rsqrt(var + EPS) * g + b


def _silu(x):
    return x * jax.nn.sigmoid(x)


def _part_specs(parts, tm, tile_of):
    specs, firsts, first = [], [], 0
    for part in parts:
        n = part.shape[0] // tm
        specs.append(pl.BlockSpec((tm, part.shape[1]),
                                  lambda *g, first=first, n=n: (jnp.clip(tile_of(*g) - first, 0, n - 1), 0)))
        firsts.append(first)
        first += n
    return specs, tuple(firsts)


def _load_parts(refs, firsts, tile):
    x = refs[0][...]
    for ref, first in zip(refs[1:], firsts[1:]):
        x = jnp.where(tile >= first, ref[...], x)
    return x


def _for_row_blocks(n_rows, body):
    block = min(ROW_BLOCK, n_rows)

    def step(r, carry):
        body(pl.ds(pl.multiple_of(r * block, block), block))
        return carry

    lax.fori_loop(0, n_rows // block, step, 0)


def _dot(a, b):
    return jnp.dot(a, b, preferred_element_type=F32)


def _cast_pad_cols_kernel(w_ref, o_ref):
    n = w_ref.shape[1]
    o_ref[:, :n] = w_ref[...].astype(BF16)
    o_ref[:, n:] = jnp.zeros((o_ref.shape[0], o_ref.shape[1] - n), BF16)


def _cast_pad_cols(w, l, s, n_pad, tr=256):
    m, n = w.shape[2:]
    return pl.pallas_call(
        _cast_pad_cols_kernel,
        grid=(m // tr,),
        in_specs=[pl.BlockSpec((None, None, tr, n), lambda i: (l, s, i, 0))],
        out_specs=pl.BlockSpec((tr, n_pad), lambda i: (i, 0)),
        out_shape=jax.ShapeDtypeStruct((m, n_pad), BF16),
        compiler_params=_cparams(("parallel",)),
    )(w)


def _cast_pad_rows_kernel(w_ref, o_ref, *, rows_valid):
    tr = w_ref.shape[0]
    row = pl.program_id(0) * tr + lax.broadcasted_iota(jnp.int32, w_ref.shape, 0)
    o_ref[...] = jnp.where(row < rows_valid, w_ref[...], 0.0).astype(BF16)


def _cast_pad_rows(w, l, s, m_pad, tr=512):
    m, n = w.shape[2:]
    return pl.pallas_call(
        functools.partial(_cast_pad_rows_kernel, rows_valid=m),
        grid=(m_pad // tr,),
        in_specs=[pl.BlockSpec((None, None, tr, n), lambda j: (l, s, j, 0))],
        out_specs=pl.BlockSpec((tr, n), lambda j: (j, 0)),
        out_shape=jax.ShapeDtypeStruct((m_pad, n), BF16),
        compiler_params=_cparams(("parallel",)),
    )(w)


def _mod_kernel(cond_ref, b_ref, *refs):
    w_refs, o_ref = refs[:-1], refs[-1]
    k = pl.program_id(1)

    @pl.when(k == 0)
    def _():
        o_ref[...] = jnp.broadcast_to(b_ref[...], o_ref.shape)

    s = _silu(cond_ref[...]).astype(BF16)
    part = o_ref.shape[1] // len(w_refs)
    for j, w_ref in enumerate(w_refs):
        o_ref[:, j * part:(j + 1) * part] += _dot(s, w_ref[...].astype(BF16))


def _modulation(cond, w_mod, b_mod, tk=256):
    n_layers, d, n = w_mod.shape
    part = n // MOD_STREAMS
    w_specs = [pl.BlockSpec((None, tk, part), lambda l, k, j=j: (l, k, j)) for j in range(MOD_STREAMS)]
    return pl.pallas_call(
        _mod_kernel,
        grid=(n_layers, d // tk),
        in_specs=[
            pl.BlockSpec((COND_ROWS, tk), lambda l, k: (0, k)),
            pl.BlockSpec((None, 1, n), lambda l, k: (l, 0, 0)),
            *w_specs,
        ],
        out_specs=pl.BlockSpec((None, COND_ROWS, n), lambda l, k: (l, 0, 0)),
        out_shape=jax.ShapeDtypeStruct((n_layers, COND_ROWS, n), F32),
        compiler_params=_cparams(("parallel", "arbitrary")),
    )(cond, b_mod.reshape(n_layers, 1, n), *([w_mod] * MOD_STREAMS))


def _ffn_kernel(*refs, cast_next, d_ff, pack_in):
    x_ref, shift_ref, scale_ref, gate_ref, gpre_ref, gpost_ref, wg_ref, wu_ref, wd_ref = refs[:9]
    if cast_next and pack_in:
        (wgn_ref, wun_ref, wdn_ref, pa_ref, pb_ref, ga_ref, gb_ref,
         o_ref, wgo_ref, wuo_ref, wdo_ref, wabcd_ref, wgates_ref, h_ref, r_ref) = refs[9:]
    elif cast_next:
        wgn_ref, wun_ref, wdn_ref, o_ref, wgo_ref, wuo_ref, wdo_ref, h_ref, r_ref = refs[9:]
    else:
        o_ref, h_ref, r_ref = refs[9:]
    f = pl.program_id(1)

    @pl.when(f == 0)
    def _():
        gain = gpre_ref[...] * (1.0 + scale_ref[...])
        shift = shift_ref[...]

        def norms(rows):
            r_ref[rows, :] = _inv_rms(x_ref[rows, :])

        def block(rows):
            h_ref[rows, :] = (x_ref[rows, :] * r_ref[rows, :] * gain + shift).astype(BF16)
            o_ref[rows, :] = jnp.zeros((rows.size, o_ref.shape[1]), F32)

        _for_row_blocks(x_ref.shape[0], norms)
        _for_row_blocks(x_ref.shape[0], block)

    h = h_ref[...]
    g = _dot(h, wg_ref[...])
    u = _dot(h, wu_ref[...])
    a = (_silu(g) * u).astype(BF16)
    o_ref[...] += _dot(a, wd_ref[...])

    if cast_next:
        for src_ref, dst_ref in ((wgn_ref, wgo_ref), (wun_ref, wuo_ref)):
            dst_ref[:, :d_ff] = src_ref[...].astype(BF16)
            dst_ref[:, d_ff:] = jnp.zeros((dst_ref.shape[0], dst_ref.shape[1] - d_ff), BF16)
        rd = wdo_ref.shape[0]
        step = pl.program_id(0) * pl.num_programs(1) + f
        row = jnp.minimum(step, cast_next - 1) * rd + lax.broadcasted_iota(jnp.int32, wdo_ref.shape, 0)
        wdo_ref[...] = jnp.where(row < d_ff, wdn_ref[...], 0.0).astype(BF16)
    if pack_in:
        wabcd_ref[...] = jnp.concatenate([pa_ref[...], pb_ref[...]], axis=0).T.astype(BF16)
        wgates_ref[...] = jnp.concatenate([ga_ref[...], gb_ref[...]], axis=0).T.astype(BF16)

    @pl.when(f == pl.num_programs(1) - 1)
    def _():
        gain = 0.5 * gate_ref[...] * gpost_ref[...]

        def norms(rows):
            r_ref[rows, :] = _inv_rms(o_ref[rows, :])

        def block(rows):
            o_ref[rows, :] = x_ref[rows, :] + o_ref[rows, :] * r_ref[rows, :] * gain

        _for_row_blocks(x_ref.shape[0], norms)
        _for_row_blocks(x_ref.shape[0], block)


def _slab_rows(total, n_steps):
    slab = BF16_ROWS
    while total % slab or total // slab > n_steps:
        slab *= 2
        assert slab <= total
    return slab


def _ffn(x, rows, mod, j, row_of_tile, gpre, gpost, wts, tm, cast_next=None, pack_in=None):
    wg, wu, wd = wts
    d = x.shape[1]
    f_pad = wg.shape[-1]
    tf = FF_CHUNK
    nf = f_pad // tf

    def mspec(k):
        return pl.BlockSpec((None, None, 1, d), lambda i, f: (row_of_tile(i), k, 0, 0))

    vec = pl.BlockSpec((1, d), lambda i, f: (0, 0))
    in_specs = [
        pl.BlockSpec((tm, d), lambda i, f: (i, 0)),
        mspec(3 * j), mspec(3 * j + 1), mspec(3 * j + 2),
        vec, vec,
        pl.BlockSpec((d, tf), lambda i, f: (0, f)),
        pl.BlockSpec((d, tf), lambda i, f: (0, f)),
        pl.BlockSpec((tf, d), lambda i, f: (f, 0)),
    ]
    args = [x, mod, mod, mod, gpre, gpost, wg, wu, wd]
    out_specs = [pl.BlockSpec((tm, d), lambda i, f: (i, 0))]
    out_shape = [jax.ShapeDtypeStruct((rows, d), F32)]
    down_slabs, d_ff = 0, f_pad
    if cast_next is not None:
        wg32, wu32, wd32, l2, s2 = cast_next
        d_ff = wg32.shape[-1]
        n_steps = (rows // tm) * nf
        rg, rd = _slab_rows(d, n_steps), _slab_rows(f_pad, n_steps)
        up_slabs, down_slabs, down_src_slabs = d // rg, f_pad // rd, pl.cdiv(d_ff, rd)

        def slab(count):
            return lambda i, f: jnp.minimum(i * nf + f, count - 1)

        in_specs += [pl.BlockSpec((None, None, rg, d_ff), lambda i, f: (l2, s2, slab(up_slabs)(i, f), 0))] * 2
        in_specs += [pl.BlockSpec((None, None, rd, d), lambda i, f: (l2, s2, slab(down_src_slabs)(i, f), 0))]
        args += [wg32, wu32, wd32]
        out_specs += [pl.BlockSpec((rg, f_pad), lambda i, f: (slab(up_slabs)(i, f), 0))] * 2
        out_specs += [pl.BlockSpec((rd, d), lambda i, f: (slab(down_slabs)(i, f), 0))]
        out_shape += [jax.ShapeDtypeStruct((d, f_pad), BF16)] * 2 + [jax.ShapeDtypeStruct((f_pad, d), BF16)]
        if pack_in is not None:
            w_in_t, l_in = pack_in
            half = ROT_DIM
            n_abcd = sum(IN_WIDTHS)
            n_gates = w_in_t.shape[1] - (n_abcd - half)
            dup = (n_abcd - IN_WIDTHS[3]) // half - 1
            abcd_blocks, gate_blocks = n_abcd // LANES, n_gates // LANES
            if n_steps < max(abcd_blocks, gate_blocks):
                return _ffn(x, rows, mod, j, row_of_tile, gpre, gpost, wts, tm, cast_next=cast_next) + (None,)

            def src_abcd(k):
                return jnp.where(k < dup, k, k - 1)

            def half_spec(block_of_step, src, part):
                return pl.BlockSpec((None, half, d),
                                    lambda i, f: (l_in, src(2 * block_of_step(i, f) + part), 0))

            gate0 = (n_abcd - half) // half
            in_specs += [half_spec(slab(abcd_blocks), src_abcd, 0), half_spec(slab(abcd_blocks), src_abcd, 1),
                         half_spec(slab(gate_blocks), lambda k: gate0 + k, 0),
                         half_spec(slab(gate_blocks), lambda k: gate0 + k, 1)]
            args += [w_in_t] * 4
            out_specs += [pl.BlockSpec((d, LANES), lambda i, f: (0, slab(abcd_blocks)(i, f))),
                          pl.BlockSpec((d, LANES), lambda i, f: (0, slab(gate_blocks)(i, f)))]
            out_shape += [jax.ShapeDtypeStruct((d, n_abcd), BF16), jax.ShapeDtypeStruct((d, n_gates), BF16)]
    row_axis = "parallel" if cast_next is None else "arbitrary"
    outs = pl.pallas_call(
        functools.partial(_ffn_kernel, cast_next=down_slabs, d_ff=d_ff, pack_in=pack_in is not None),
        grid=(rows // tm, nf),
        in_specs=in_specs,
        out_specs=out_specs,
        out_shape=out_shape,
        scratch_shapes=[pltpu.VMEM((tm, d), BF16), pltpu.VMEM((tm, 1), F32)],
        compiler_params=_cparams((row_axis, "arbitrary"), VMEM_LIMIT_FFN),
    )(*args)
    if cast_next is None:
        return outs[0]
    return (outs[0], tuple(outs[1:4])) + ((tuple(outs[4:]),) if pack_in is not None else ())


def _inproj_kernel(*refs, x_firsts):
    x_refs = refs[:len(x_firsts)]
    (shift_ref, scale_ref, gpre_ref, w_ref, b_ref, cos_ref, sina_ref, sinb_ref, qn_ref, kvn_ref, wuq_ref, wukv_ref,
     lng_ref, lnb_ref, ws_ref, bs_ref,
     h_ref, ya_ref, hb_ref, qc_ref, kc_ref, vc_ref, qd_ref, kd_ref, vd_ref) = refs[len(x_firsts):]
    y = _rms(_load_parts(x_refs, x_firsts, pl.program_id(0))) * gpre_ref[...]
    h = (y * (1.0 + scale_ref[...]) + shift_ref[...]).astype(BF16)
    h_ref[...] = h

    def proj(lo, hi):
        return _dot(h, w_ref[:, lo:hi]) + b_ref[:, lo:hi]

    na, nb, nc, nd = IN_WIDTHS
    _qkv_prep(proj(na + nb, na + nb + nc), proj(na + nb + nc, na + nb + nc + nd),
              cos_ref, sina_ref, sinb_ref, qn_ref, kvn_ref, wuq_ref, wukv_ref,
              qc_ref, kc_ref, vc_ref, qd_ref, kd_ref, vd_ref)
    w = BRANCH_W
    zb = proj(na, na + nb)
    hb_ref[...] = zb[:, :w] * jax.nn.sigmoid(zb[:, w:])
    za = proj(0, na)
    g = 0.5 * za * (1.0 + lax.erf(za * (2.0 ** -0.5)))
    vln = _layernorm(g[:, w:], lng_ref[...], lnb_ref[...]).astype(BF16)
    for c in range(za.shape[0] // CHUNK):
        rows = slice(c * CHUNK, (c + 1) * CHUNK)
        for gi in range(GM_GROUPS):
            cols = slice(gi * CHUNK, (gi + 1) * CHUNK)
            sv = _dot(ws_ref[gi], vln[rows, cols]) + bs_ref[gi]
            ya_ref[rows, cols] = (g[rows, cols] * sv).astype(BF16)


def _inproj(x_parts, mod, row_of_tile, gpre, w, b, cos, sina, sinb, qn, kvn, wuq, wukv, lng, lnb, ws, bs, tm):
    rows, d = sum(part.shape[0] for part in x_parts), x_parts[0].shape[1]
    x_specs, x_firsts = _part_specs(x_parts, tm, lambda i: i)
    n = w.shape[1]

    def mspec(k):
        return pl.BlockSpec((None, None, 1, d), lambda i: (row_of_tile(i), k, 0, 0))

    def rowspec(wd):
        return pl.BlockSpec((tm, wd), lambda i: (i, 0))

    def full(a):
        return pl.BlockSpec(a.shape, lambda i: (0,) * a.ndim)

    def headspec(nh, wd):
        return pl.BlockSpec((nh, tm, wd), lambda i: (0, i, 0))

    def headspec_t(nh, wd):
        return pl.BlockSpec((nh, wd, tm), lambda i: (0, 0, i))

    kc = jax.ShapeDtypeStruct((MLA_HEADS, rows, MLA_QK), BF16)
    qc = jax.ShapeDtypeStruct((MLA_HEADS, MLA_QK, rows), BF16)
    vc = jax.ShapeDtypeStruct((MLA_HEADS, MLA_V + MLA_SUM_ROWS, rows), BF16)
    qd = jax.ShapeDtypeStruct((SWA_HEADS // 2, LANES, rows), BF16)
    kd = jax.ShapeDtypeStruct((rows, LANES), BF16)
    vd = jax.ShapeDtypeStruct((LANES, rows), BF16)
    outs = [jax.ShapeDtypeStruct((rows, d), BF16), jax.ShapeDtypeStruct((rows, BRANCH_W), BF16),
            jax.ShapeDtypeStruct((rows, BRANCH_W), F32)]
    ospecs = [rowspec(d), rowspec(BRANCH_W), rowspec(BRANCH_W)]
    return pl.pallas_call(
        functools.partial(_inproj_kernel, x_firsts=x_firsts),
        grid=(rows // tm,),
        in_specs=[
            *x_specs,
            mspec(3), mspec(4),
            pl.BlockSpec((1, d), lambda i: (0, 0)),
            pl.BlockSpec((d, n), lambda i: (0, 0), pipeline_mode=pl.Buffered(1)),
            pl.BlockSpec((1, n), lambda i: (0, 0)),
            rowspec(LANES), rowspec(LANES), rowspec(LANES), full(qn), full(kvn), full(wuq), full(wukv),
            full(lng), full(lnb), full(ws), full(bs),
        ],
        out_specs=ospecs + [headspec_t(MLA_HEADS, MLA_QK), headspec(MLA_HEADS, MLA_QK),
                            headspec_t(MLA_HEADS, MLA_V + MLA_SUM_ROWS), headspec_t(SWA_HEADS // 2, LANES),
                            rowspec(LANES), pl.BlockSpec((LANES, tm), lambda i: (0, i))],
        out_shape=outs + [qc, kc, vc, qd, kd, vd],
        compiler_params=_cparams(("parallel",)),
    )(*x_parts, mod, mod, gpre, w, b, cos, sina, sinb, qn, kvn, wuq, wukv, lng, lnb, ws, bs)


def _conv_kernel(h_ref, hprev_ref, hnext_ref, dw_ref, db_ref, lng_ref, lnb_ref, yb_ref, ext_ref, shift_ref,
                 *, ts, n_lat_tiles, lat_tiles_per_seq, ctx_tiles_per_seq):
    i = pl.program_id(0)
    in_lat = i < n_lat_tiles
    pos = jnp.where(in_lat, i % lat_tiles_per_seq, (i - n_lat_tiles) % ctx_tiles_per_seq)
    last = jnp.where(in_lat, lat_tiles_per_seq - 1, ctx_tiles_per_seq - 1)
    ext_ref[0:CONV_HALO, :] = jnp.where(pos != 0, hprev_ref[...], 0.0)
    ext_ref[CONV_HALO:CONV_HALO + ts, :] = h_ref[...]
    ext_ref[CONV_HALO + ts:, :] = jnp.where(pos != last, hnext_ref[...], 0.0)
    span = ts + 2 * CONV_HALO - SUBLANES
    for r in range(1, SUBLANES):
        shift_ref[r - 1, 0:span, :] = ext_ref[r:r + span, :]
    first_tap = CONV_HALO - CONV_W // 2
    acc = jnp.zeros((ts, BRANCH_W), F32) + db_ref[...]
    for k in range(CONV_W):
        a, r = divmod(first_tap + k, SUBLANES)
        win = ext_ref[a * SUBLANES:a * SUBLANES + ts, :] if r == 0 else shift_ref[r - 1, a * SUBLANES:a * SUBLANES + ts, :]
        acc = acc + dw_ref[k:k + 1, :] * win
    yb_ref[...] = _silu(_layernorm(acc, lng_ref[...], lnb_ref[...])).astype(BF16)


def _conv_mixer(hb, dw, db, lng, lnb, n_lat_rows, seq_lat, seq_ctx, ts):
    rows, w = hb.shape
    halo_blocks = ts // CONV_HALO
    n_halo = rows // CONV_HALO
    kern = functools.partial(_conv_kernel, ts=ts, n_lat_tiles=n_lat_rows // ts,
                             lat_tiles_per_seq=seq_lat // ts, ctx_tiles_per_seq=seq_ctx // ts)
    vec = pl.BlockSpec((1, w), lambda i: (0, 0))
    return pl.pallas_call(
        kern,
        grid=(rows // ts,),
        in_specs=[
            pl.BlockSpec((ts, w), lambda i: (i, 0)),
            pl.BlockSpec((CONV_HALO, w), lambda i: (jnp.maximum(i * halo_blocks - 1, 0), 0)),
            pl.BlockSpec((CONV_HALO, w), lambda i: (jnp.minimum((i + 1) * halo_blocks, n_halo - 1), 0)),
            pl.BlockSpec((CONV_W, w), lambda i: (0, 0)),
            vec, vec, vec,
        ],
        out_specs=pl.BlockSpec((ts, w), lambda i: (i, 0)),
        out_shape=jax.ShapeDtypeStruct((rows, w), BF16),
        scratch_shapes=[pltpu.VMEM((ts + 2 * CONV_HALO, w), F32),
                        pltpu.VMEM((SUBLANES - 1, ts + 2 * CONV_HALO - SUBLANES, w), F32)],
        compiler_params=_cparams(("parallel",)),
    )(hb, hb, hb, dw, db, lng, lnb)


def _qkv_prep(zc, zd, cos_ref, sina_ref, sinb_ref, qn_ref, kvn_ref, wuq_ref, wukv_ref,
              qc_ref, kc_ref, vc_ref, qd_ref, kd_ref, vd_ref):
    cos, sina, sinb = cos_ref[...], sina_ref[...], sinb_ref[...]
    low = lax.broadcasted_iota(jnp.int32, cos.shape, 1) < ROT_DIM

    def rope(slab):
        return (slab * cos + pltpu.roll(slab, LANES - ROT_QUARTER, 1) * sina
                + pltpu.roll(slab, ROT_QUARTER, 1) * sinb)

    nq = MLA_HEADS * MLA_NOPE
    kv0 = MLA_Q_RANK + MLA_KV_RANK
    q = _dot((_rms(zc[:, :MLA_Q_RANK]) * qn_ref[...]).astype(BF16), wuq_ref[...]) * MLA_SCALE
    kv = _dot((_rms(zc[:, MLA_Q_RANK:kv0]) * kvn_ref[...]).astype(BF16), wukv_ref[...])
    kr = rope(zc[:, kv0:kv0 + LANES])
    ones = jnp.ones((MLA_SUM_ROWS, zc.shape[0]), BF16)
    for pair in range(MLA_HEADS // 2):
        qr = rope(q[:, nq + pair * LANES:nq + (pair + 1) * LANES]).T.astype(BF16)
        for h in (2 * pair, 2 * pair + 1):
            qc_ref[h, 0:MLA_NOPE, :] = q[:, h * MLA_NOPE:(h + 1) * MLA_NOPE].T.astype(BF16)
            qc_ref[h, MLA_NOPE:, :] = qr
            kc_ref[h, :, 0:MLA_NOPE] = kv[:, h * MLA_NOPE:(h + 1) * MLA_NOPE].astype(BF16)
            keep = low if h % 2 == 0 else jnp.logical_not(low)
            kc_ref[h, :, MLA_NOPE:] = jnp.where(keep, kr, 0.0).astype(BF16)
            vc_ref[h, 0:MLA_V, :] = kv[:, nq + h * MLA_V:nq + (h + 1) * MLA_V].T.astype(BF16)
            vc_ref[h, MLA_V:, :] = ones

    for pair in range(SWA_HEADS // 2):
        qd_ref[pair] = (rope(zd[:, pair * LANES:(pair + 1) * LANES]) * SWA_SCALE).T.astype(BF16)
    nqd = SWA_HEADS * ROT_DIM
    kd_ref[...] = rope(zd[:, nqd:nqd + LANES]).astype(BF16)
    vd_ref[...] = zd[:, nqd + LANES:nqd + 2 * LANES].T.astype(BF16)


def _mla_kernel(*refs, has_lat, groups):
    if has_lat:
        qt_ref, kl_ref, vlt_ref, kx_ref, vxt_ref, o_ref = refs
    else:
        qt_ref, kx_ref, vxt_ref, o_ref = refs
    cols = qt_ref.shape[1] // groups
    chunks = [(kx_ref, vxt_ref, 0, kx_ref.shape[0])]
    if has_lat:
        step = math.gcd(MLA_KEY_CHUNK, kl_ref.shape[0])
        chunks += [(kl_ref, vlt_ref, k0, step) for k0 in range(0, kl_ref.shape[0], step)]
    state = [(None, None)] * groups
    for k_ref, vt_ref, k0, n in chunks:
        for g in range(groups):
            m, ot = state[g]
            s = _dot(k_ref[k0:k0 + n, :], qt_ref[:, g * cols:(g + 1) * cols])
            m_new = jnp.max(s, axis=0, keepdims=True)
            if m is not None:
                m_new = jnp.maximum(m, m_new)
            part = _dot(vt_ref[:, k0:k0 + n], jnp.exp2(s - m_new).astype(BF16))
            state[g] = (m_new, part if ot is None else ot * jnp.exp2(m - m_new) + part)
    for g, (_, ot) in enumerate(state):
        o_ref[g * cols:(g + 1) * cols, :] = (ot[:MLA_V] / ot[MLA_V:MLA_V + 1]).T.astype(BF16)


def _mla_attention(qc, kc, vc, batch, seq_lat, seq_ctx, lat_queries):
    n_lat_rows = batch * seq_lat
    ctx_blk0 = n_lat_rows // seq_ctx
    tq = math.gcd(MLA_GROUPS * MLA_GROUP_Q, seq_lat if lat_queries else seq_ctx)
    if lat_queries:
        nq, q0 = seq_lat // tq, 0
    else:
        nq, q0 = seq_ctx // tq, n_lat_rows // tq
    vrows = vc.shape[1]
    qspec = pl.BlockSpec((None, MLA_QK, tq), lambda b, h, i: (h, 0, q0 + b * nq + i))
    kx = pl.BlockSpec((None, seq_ctx, MLA_QK), lambda b, h, i: (h, ctx_blk0 + b, 0))
    vx = pl.BlockSpec((None, vrows, seq_ctx), lambda b, h, i: (h, 0, ctx_blk0 + b))
    if lat_queries:
        kl = pl.BlockSpec((None, seq_lat, MLA_QK), lambda b, h, i: (h, b, 0))
        vl = pl.BlockSpec((None, vrows, seq_lat), lambda b, h, i: (h, 0, b))
        in_specs, args = [qspec, kl, vl, kx, vx], (qc, kc, vc, kc, vc)
    else:
        in_specs, args = [qspec, kx, vx], (qc, kc, vc)
    return pl.pallas_call(
        functools.partial(_mla_kernel, has_lat=lat_queries, groups=max(1, tq // MLA_GROUP_Q)),
        grid=(batch, MLA_HEADS, nq),
        in_specs=in_specs,
        out_specs=pl.BlockSpec((tq, MLA_V), lambda b, h, i: (b * nq + i, h)),
        out_shape=jax.ShapeDtypeStruct((batch * nq * tq, MLA_HEADS * MLA_V), BF16),
        compiler_params=_cparams(("parallel", "parallel", "parallel")),
    )(*args)


def _swa_kernel(*refs, has_lat):
    if has_lat:
        qt_ref, kx_ref, vxt_ref, sink_ref, bias_ref = refs[:5]
        n_win = (len(refs) - 6) // 2
        kw_refs, vw_refs, o_ref = refs[5:5 + n_win], refs[5 + n_win:-1], refs[-1]
    else:
        qt_ref, kx_ref, vxt_ref, sink_ref, o_ref = refs
    blocks = qt_ref.shape[2] // SWA_BLOCK
    half = LANES // 2
    span = 3 * SWA_BLOCK
    gcol = SWA_GROUP * SWA_BLOCK
    zeros = jnp.zeros((half, gcol), BF16)
    ones = jnp.ones((SWA_SUM_ROWS, 1), BF16)
    if has_lat:
        k_win = jnp.concatenate([r[...] for r in kw_refs], axis=0)
        v_win = jnp.concatenate([r[...] for r in vw_refs], axis=1)
    chains = []
    for kv in range(SWA_KV):
        for blk in range(blocks):
            heads = range(kv * SWA_GROUP, (kv + 1) * SWA_GROUP)
            qt = jnp.concatenate([qt_ref[h // 2, (h % 2) * half:(h % 2 + 1) * half,
                                         blk * SWA_BLOCK:(blk + 1) * SWA_BLOCK] for h in heads], axis=1)
            qt = jnp.concatenate([qt, zeros] if kv == 0 else [zeros, qt], axis=0)
            k0 = blk * SWA_BLOCK
            chains.append((kv, blk, _dot(kx_ref[...], qt), _dot(k_win[k0:k0 + span], qt) if has_lat else None))
    for kv, blk, sx, sl in chains:
        rows = slice(kv * half, (kv + 1) * half)
        cols = slice(blk * gcol, (blk + 1) * gcol)
        k0 = blk * SWA_BLOCK
        sink = sink_ref[kv][:, cols] * LOG2_E
        m = jnp.maximum(sink, jnp.max(sx, axis=0, keepdims=True))
        if has_lat:
            sl = sl + bias_ref[k0:k0 + span, cols]
            m = jnp.maximum(m, jnp.max(sl, axis=0, keepdims=True))
        vx = vxt_ref[rows, :]
        ot = _dot(jnp.concatenate([vx, jnp.broadcast_to(ones, (SWA_SUM_ROWS, vx.shape[1]))], axis=0),
                  jnp.exp2(sx - m).astype(BF16))
        if has_lat:
            vl = v_win[rows, k0:k0 + span]
            ot = ot + _dot(jnp.concatenate([vl, jnp.broadcast_to(ones, (SWA_SUM_ROWS, span))], axis=0),
                           jnp.exp2(sl - m).astype(BF16))
        o = ot[:half] / (ot[half:half + 1] + jnp.exp2(sink - m))
        for pair in range(SWA_GROUP // 2):
            c0 = 2 * pair * SWA_BLOCK
            slab_t = jnp.concatenate([o[:, c0:c0 + SWA_BLOCK], o[:, c0 + SWA_BLOCK:c0 + 2 * SWA_BLOCK]], axis=0)
            lane0 = (kv * (SWA_GROUP // 2) + pair) * LANES
            o_ref[blk * SWA_BLOCK:(blk + 1) * SWA_BLOCK, lane0:lane0 + LANES] = slab_t.T.astype(BF16)


def _swa_window_bias(blocks):
    nkeys = (blocks + 2) * SWA_BLOCK
    ncol = blocks * SWA_GROUP * SWA_BLOCK
    key = jnp.arange(nkeys)[:, None] - SWA_BLOCK
    col = jnp.arange(ncol)[None, :]
    qry = (col // (SWA_GROUP * SWA_BLOCK)) * SWA_BLOCK + (col % SWA_BLOCK)
    inside = jnp.abs(key - qry) <= WINDOW
    after_start = key >= 0
    before_end = key < blocks * SWA_BLOCK
    variants = [inside, inside & after_start, inside & before_end, inside & after_start & before_end]
    return jnp.where(jnp.stack(variants), 0.0, -jnp.inf).astype(F32)


def _swa_attention(qd, kd, vd, sink, batch, seq_lat, seq_ctx, lat_queries):
    n_lat_rows = batch * seq_lat
    ctx_blk0 = n_lat_rows // seq_ctx
    seq_q = seq_lat if lat_queries else seq_ctx
    blocks = math.gcd(SWA_QUERY_BLOCKS, math.gcd(seq_q, n_lat_rows) // SWA_BLOCK)
    tq = blocks * SWA_BLOCK
    nb = seq_q // tq
    q0 = 0 if lat_queries else n_lat_rows // tq
    kblocks = seq_lat // SWA_BLOCK
    sink_row = jnp.tile(jnp.repeat(sink.reshape(SWA_KV, SWA_GROUP), SWA_BLOCK, axis=1), (1, blocks))[:, None, :]
    in_specs = [pl.BlockSpec((SWA_HEADS // 2, LANES, tq), lambda b, i: (0, 0, q0 + b * nb + i)),
                pl.BlockSpec((seq_ctx, LANES), lambda b, i: (ctx_blk0 + b, 0)),
                pl.BlockSpec((LANES, seq_ctx), lambda b, i: (0, ctx_blk0 + b)),
                pl.BlockSpec(sink_row.shape, lambda b, i: (0, 0, 0))]
    args = [qd, kd, vd, sink_row]
    if lat_queries:
        n_win = blocks + 2
        bias = _swa_window_bias(blocks)
        in_specs.append(pl.BlockSpec((None,) + bias.shape[1:],
                                     lambda b, i: ((i == 0).astype(jnp.int32) + 2 * (i == nb - 1).astype(jnp.int32), 0, 0)))
        args.append(bias)

        def win(j):
            return lambda b, i: b * kblocks + jnp.clip(i * blocks - 1 + j, 0, kblocks - 1)

        in_specs += [pl.BlockSpec((SWA_BLOCK, LANES), lambda b, i, j=j: (win(j)(b, i), 0))
                     for j in range(n_win)]
        in_specs += [pl.BlockSpec((LANES, SWA_BLOCK), lambda b, i, j=j: (0, win(j)(b, i)))
                     for j in range(n_win)]
        args += [kd] * n_win + [vd] * n_win
    return pl.pallas_call(
        functools.partial(_swa_kernel, has_lat=lat_queries),
        grid=(batch, nb),
        in_specs=in_specs,
        out_specs=pl.BlockSpec((tq, BRANCH_W), lambda b, i: (b * nb + i, 0)),
        out_shape=jax.ShapeDtypeStruct((batch * nb * tq, BRANCH_W), BF16),
        compiler_params=_cparams(("parallel", "parallel")),
    )(*args)


def _gate_kernel(*refs, y_firsts):
    n = len(y_firsts)
    h_ref, ya_ref, yb_ref = refs[:3]
    yc_refs, yd_refs = refs[3:3 + n], refs[3 + n:3 + 2 * n]
    wg0, wg1, wg2, wg3, bg_ref, wb0, wb1, wb2, wb3, o_ref = refs[3 + 2 * n:]
    tile = pl.program_id(1)
    ys = (ya_ref[...], yb_ref[...], _load_parts(yc_refs, y_firsts, tile), _load_parts(yd_refs, y_firsts, tile))
    h = h_ref[...]
    acc = None
    for b, (y, wg_ref, wb_ref) in enumerate(zip(ys, (wg0, wg1, wg2, wg3), (wb0, wb1, wb2, wb3))):
        gate = jax.nn.sigmoid(_dot(h, wg_ref[...]) + bg_ref[b])
        term = gate * _dot(y, wb_ref[...])
        acc = term if acc is None else acc + term
    o_ref[...] = acc.astype(BF16)


def _gated_merge(h, ya, yb, yc_parts, yd_parts, wg, bg, wbr, rows, tm, tn):
    d = h.shape[1]
    ncol = d // tn
    wg_specs = [pl.BlockSpec((d, tn), lambda j, i, b=b: (0, b * ncol + j)) for b in range(4)]
    wb_specs = [pl.BlockSpec((None, BRANCH_W, tn), lambda j, i, b=b: (b, 0, j)) for b in range(4)]
    yspec = pl.BlockSpec((tm, BRANCH_W), lambda j, i: (i, 0))
    yc_specs, y_firsts = _part_specs(yc_parts, tm, lambda j, i: i)
    yd_specs, _ = _part_specs(yd_parts, tm, lambda j, i: i)
    return pl.pallas_call(
        functools.partial(_gate_kernel, y_firsts=y_firsts),
        grid=(ncol, rows // tm),
        in_specs=[pl.BlockSpec((tm, d), lambda j, i: (i, 0)), yspec, yspec, *yc_specs, *yd_specs,
                  *wg_specs, pl.BlockSpec((4, 1, tn), lambda j, i: (0, 0, j)), *wb_specs],
        out_specs=pl.BlockSpec((tm, tn), lambda j, i: (i, j)),
        out_shape=jax.ShapeDtypeStruct((rows, d), BF16),
        compiler_params=_cparams(("parallel", "parallel")),
    )(h, ya, yb, *yc_parts, *yd_parts, wg, wg, wg, wg, bg, wbr, wbr, wbr, wbr)


def _outproj_kernel(*refs, x_firsts):
    a_ref, x_refs = refs[0], refs[1:1 + len(x_firsts)]
    gate_ref, gpost_ref, w_ref, o_ref = refs[1 + len(x_firsts):]
    y = _dot(a_ref[...], w_ref[...])
    o_ref[...] = _load_parts(x_refs, x_firsts, pl.program_id(0)) + gate_ref[...] * (_rms(y) * gpost_ref[...])


def _outproj(a, x_parts, mod, row_of_tile, gpost, w, tm):
    rows, d = a.shape
    x_specs, x_firsts = _part_specs(x_parts, tm, lambda i: i)
    return pl.pallas_call(
        functools.partial(_outproj_kernel, x_firsts=x_firsts),
        grid=(rows // tm,),
        in_specs=[
            pl.BlockSpec((tm, d), lambda i: (i, 0)),
            *x_specs,
            pl.BlockSpec((None, None, 1, d), lambda i: (row_of_tile(i), 5, 0, 0)),
            pl.BlockSpec((1, d), lambda i: (0, 0)),
            pl.BlockSpec((d, d), lambda i: (0, 0), pipeline_mode=pl.Buffered(1)),
        ],
        out_specs=pl.BlockSpec((tm, d), lambda i: (i, 0)),
        out_shape=jax.ShapeDtypeStruct((rows, d), F32),
        compiler_params=_cparams(("parallel",)),
    )(a, *x_parts, mod, gpost, w)


def _rope_tables(seq_lat, n_ctx_rows, batch):
    half = ROT_DIM // 2
    freqs = 1.0 / (ROPE_BASE ** (jnp.arange(0, half, 2, dtype=F32) / half))
    t = jnp.arange(seq_lat)
    ang_r = (t // GRID_W).astype(F32)[:, None] * freqs[None, :]
    ang_c = (t % GRID_W).astype(F32)[:, None] * freqs[None, :]
    ang = jnp.concatenate([ang_r, ang_r, ang_c, ang_c], axis=-1)
    ang = jnp.tile(ang, (batch, LANES // ROT_DIM))
    cos, sin = jnp.cos(ang), jnp.sin(ang)
    first = (jnp.arange(LANES) % half) < ROT_QUARTER
    sina = jnp.where(first, -sin, 0.0)
    sinb = jnp.where(first, 0.0, sin)
    ones = jnp.ones((n_ctx_rows, LANES), F32)
    zeros = jnp.zeros((n_ctx_rows, LANES), F32)
    return (jnp.concatenate([cos, ones]), jnp.concatenate([sina, zeros]), jnp.concatenate([sinb, zeros]))


def _pack_in_projection(w_in, l):
    na, nb, nc, nd = IN_WIDTHS
    c_hi = na + nb + nc - ROT_DIM
    wi = w_in[l]
    w_abcd = jnp.concatenate([wi[:, :c_hi], wi[:, c_hi - ROT_DIM:c_hi], wi[:, c_hi:c_hi + nd]], axis=1)
    return w_abcd.astype(BF16), wi[:, c_hi + nd:].astype(BF16)


def _pack_layer(l, b_in, gm_ws, gm_bs, mla_w_uq, mla_w_ukv, swa_sink, w_branch, w_out):
    d = w_branch.shape[-1]
    p = {}
    bi = b_in[l]
    na, nb, nc, nd = IN_WIDTHS
    c_hi = na + nb + nc - ROT_DIM
    kr = slice(c_hi - ROT_DIM, c_hi)
    p['b_abcd'] = jnp.concatenate([bi[:c_hi], bi[kr], bi[c_hi:c_hi + nd]])[None, :]
    p['b_gates'] = bi[c_hi + nd:].reshape(4, 1, d)
    p['gm_ws'] = gm_ws[l].astype(BF16)
    p['gm_bs'] = gm_bs[l][:, :, None]
    uq = mla_w_uq[l].reshape(-1, MLA_HEADS, MLA_NOPE + ROT_DIM)
    p['wuq'] = jnp.concatenate([uq[:, :, :MLA_NOPE].reshape(-1, MLA_HEADS * MLA_NOPE),
                                uq[:, :, MLA_NOPE:].reshape(-1, MLA_HEADS * ROT_DIM)], axis=1).astype(BF16)
    ukv = mla_w_ukv[l].reshape(-1, MLA_HEADS, MLA_NOPE + MLA_V)
    p['wukv'] = jnp.concatenate([ukv[:, :, :MLA_NOPE].reshape(-1, MLA_HEADS * MLA_NOPE),
                                 ukv[:, :, MLA_NOPE:].reshape(-1, MLA_HEADS * MLA_V)], axis=1).astype(BF16)
    p['sink'] = swa_sink[l]
    p['w_branch'] = w_branch[l].astype(BF16)
    p['w_out'] = w_out[l].astype(BF16)
    return p


def kernel(x, c, ctx, c_ctx, w_mod, b_mod, norm_pre, norm_post, w_ff_gate, w_ff_up, w_ff_down, w_in, b_in, gm_ln_g, gm_ln_b, gm_ws, gm_bs, cv_dw, cv_db, cv_ln_g, cv_ln_b, mla_q_norm, mla_kv_norm, mla_w_uq, mla_w_ukv, swa_sink, w_branch, w_out):
    batch, seq_lat, d = x.shape
    seq_ctx = ctx.shape[1]
    depth = w_mod.shape[0]
    n_lat = batch * seq_lat
    n_ctx = batch * seq_ctx
    assert batch + 1 <= COND_ROWS and seq_lat % GRID_W == 0
    tm = math.gcd(512, math.gcd(seq_lat, n_ctx))
    ts = math.gcd(256, math.gcd(seq_lat, seq_ctx))
    assert tm % SWA_BLOCK == 0 and ts % CHUNK == 0 and seq_ctx % SWA_BLOCK == 0 and seq_lat >= 3 * SWA_BLOCK

    tm_ffn = math.gcd(1024, math.gcd(seq_lat, n_ctx))

    def cond_row(tile):
        return lambda i: jnp.where(i < n_lat // tile, i // (seq_lat // tile), batch)

    row_of_tile, row_of_ffn_tile = cond_row(tm), cond_row(tm_ffn)

    cond = jnp.zeros((COND_ROWS, d), F32).at[:batch].set(c).at[batch].set(c_ctx)
    mod = _modulation(cond, w_mod, b_mod).reshape(depth, COND_ROWS, N_MOD, 1, d)
    cos, sina, sinb = _rope_tables(seq_lat, n_ctx, batch)

    ff32 = (w_ff_gate, w_ff_up, w_ff_down)
    w_in_t = jnp.swapaxes(w_in, 1, 2)
    wts = (_cast_pad_cols(w_ff_gate, 0, 0, D_FF_PAD), _cast_pad_cols(w_ff_up, 0, 0, D_FF_PAD),
           _cast_pad_rows(w_ff_down, 0, 0, D_FF_PAD))

    n_all = n_lat + n_ctx
    xs = None
    for l in range(depth):
        last = l == depth - 1
        p = _pack_layer(l, b_in, gm_ws, gm_bs, mla_w_uq, mla_w_ukv, swa_sink, w_branch, w_out)
        m = mod[l]
        npre, npost = norm_pre[l][:, None, :], norm_post[l][:, None, :]

        ffn1 = (m, 0, row_of_ffn_tile, npre[0], npost[0])
        if xs is None:
            x_lat, wts2, w_proj = _ffn(x.reshape(n_lat, d), n_lat, *ffn1, wts, tm_ffn, cast_next=ff32 + (l, 1),
                                       pack_in=(w_in_t, l))
            x_ctx = _ffn(ctx.reshape(n_ctx, d), n_ctx, m, 0, lambda i: batch, npre[0], npost[0], wts, tm_ffn)
            x_parts = (x_lat, x_ctx)
        else:
            xs, wts2 = _ffn(xs, n_all, *ffn1, wts, tm_ffn, cast_next=ff32 + (l, 1))
            x_parts, w_proj = (xs,), w_proj_next

        if w_proj is None:
            w_proj = _pack_in_projection(w_in, l)
        h, ya, hb, qc, kc, vc, qd, kd, vd = _inproj(
            x_parts, m, row_of_tile, npre[1], w_proj[0], p['b_abcd'], cos, sina, sinb,
            mla_q_norm[l][None], mla_kv_norm[l][None], p['wuq'], p['wukv'],
            gm_ln_g[l][None], gm_ln_b[l][None], p['gm_ws'], p['gm_bs'], tm)
        yb = _conv_mixer(hb, cv_dw[l], cv_db[l][None], cv_ln_g[l][None], cv_ln_b[l][None],
                         n_lat, seq_lat, seq_ctx, ts)
        rows_mix = n_lat if last else n_all
        attn = (batch, seq_lat, seq_ctx)
        yc_parts = (_mla_attention(qc, kc, vc, *attn, True),)
        yd_parts = (_swa_attention(qd, kd, vd, p['sink'], *attn, True),)
        if not last:
            yc_parts += (_mla_attention(qc, kc, vc, *attn, False),)
            yd_parts += (_swa_attention(qd, kd, vd, p['sink'], *attn, False),)
        merged = _gated_merge(h, ya, yb, yc_parts, yd_parts, w_proj[1], p['b_gates'], p['w_branch'],
                              rows_mix, tm_ffn, GATE_COLS)
        xs = _outproj(merged, x_parts, m, row_of_tile, npost[1], p['w_out'], tm)

        ffn2 = (xs, rows_mix, m, 2, row_of_ffn_tile, npre[2], npost[2], wts2, tm_ffn)
        if last:
            xs = _ffn(*ffn2)
        else:
            xs, wts, w_proj_next = _ffn(*ffn2, cast_next=ff32 + (l + 1, 0), pack_in=(w_in_t, l + 1))
    return xs[:n_lat].reshape(batch, seq_lat, d)
```
